```python
import jax, jax.numpy as jnp
from jax import lax
import numpy as np

D_MODEL = 2048
BATCH = 4
SEQ = 2048
DEPTH = 1

D_MIX = D_MODEL
ATTN_WIDTH = D_MIX // 2
HGRN_WIDTH = D_MIX - ATTN_WIDTH
ATTN_HEAD_DIM = 128
N_ATTN_HEADS = ATTN_WIDTH // ATTN_HEAD_DIM
HGRN_HEAD_DIM = 128
N_HGRN_HEADS = HGRN_WIDTH // HGRN_HEAD_DIM
IN_COLS = 3 * ATTN_WIDTH + 4 * HGRN_WIDTH
MOBA_BLOCK = 256
MOBA_TOPK = 3
MOBA_Q_BLOCK = 64
HGRN_CHUNK = 64
N_GROUPS = 4
EXPERTS_PER_GROUP = 4
N_EXPERTS = N_GROUPS * EXPERTS_PER_GROUP
TOPK_IN_GROUP = 2
D_EXPERT = D_MODEL // 2
RMS_EPS = 1e-6

kernel_name = "hymba_moba_hgrn2_hmoe_block"


def rms_norm(x, g):
    xf = x.astype(jnp.float32)
    y = xf * lax.rsqrt(jnp.mean(xf * xf, axis=-1, keepdims=True) + RMS_EPS)
    return (y * g.astype(jnp.float32)).astype(x.dtype)


def alibi_slopes(n_heads):
    return jnp.asarray(2.0 ** (-8.0 * np.arange(1, n_heads + 1) / n_heads), dtype=jnp.float32)


def moba_attention(q, k, v):
    B, H, T, Dh = q.shape
    Tp = -(-T // MOBA_BLOCK) * MOBA_BLOCK
    if Tp != T:
        pad = ((0, 0), (0, 0), (0, Tp - T), (0, 0))
        q, k, v = jnp.pad(q, pad), jnp.pad(k, pad), jnp.pad(v, pad)
    nb = Tp // MOBA_BLOCK
    k_top = min(MOBA_TOPK, nb)
    scale = Dh ** -0.5
    slopes = alibi_slopes(H)
    kb = k.reshape(B, H, nb, MOBA_BLOCK, Dh)
    vb = v.reshape(B, H, nb, MOBA_BLOCK, Dh)
    k_mean = jnp.mean(kb.astype(jnp.float32), axis=3)
    gate = jnp.einsum('bhtd,bhnd->bhtn', q.astype(jnp.float32), k_mean)
    q_blk = jnp.arange(Tp) // MOBA_BLOCK
    past = jnp.arange(nb)[None, :] < q_blk[:, None]
    gate = jnp.where(past[None, None], gate, -jnp.inf)
    _, sel = lax.top_k(gate, k_top)
    sel_valid = sel < q_blk[None, None, :, None]

    n_qb = Tp // MOBA_Q_BLOCK

    def to_blocks(a):
        a = a.reshape(B, H, n_qb, MOBA_Q_BLOCK, *a.shape[3:])
        return jnp.moveaxis(a, 2, 0)

    gather_blocks = jax.vmap(jax.vmap(lambda blocks, idx: blocks[idx]))

    def step(args):
        qc, selc, validc, ci = args
        t_pos = ci * MOBA_Q_BLOCK + jnp.arange(MOBA_Q_BLOCK)
        own = (ci * MOBA_Q_BLOCK) // MOBA_BLOCK
        k_g = gather_blocks(kb, selc)
        v_g = gather_blocks(vb, selc)
        s_sel = jnp.einsum('bhqd,bhqkcd->bhqkc', qc, k_g,
                           preferred_element_type=jnp.float32) * scale
        key_pos = selc[..., None] * MOBA_BLOCK + jnp.arange(MOBA_BLOCK)
        dist_sel = (t_pos[None, None, :, None, None] - key_pos).astype(jnp.float32)
        s_sel = s_sel - slopes[None, :, None, None, None] * dist_sel
        s_sel = jnp.where(validc[..., None], s_sel, -jnp.inf)
        k_own = lax.dynamic_slice_in_dim(k, own * MOBA_BLOCK, MOBA_BLOCK, axis=2)
        v_own = lax.dynamic_slice_in_dim(v, own * MOBA_BLOCK, MOBA_BLOCK, axis=2)
        s_own = jnp.einsum('bhqd,bhcd->bhqc', qc, k_own,
                           preferred_element_type=jnp.float32) * scale
        dist_own = t_pos[:, None] - (own * MOBA_BLOCK + jnp.arange(MOBA_BLOCK))[None, :]
        s_own = s_own - slopes[:, None, None] * dist_own.astype(jnp.float32)
        s_own = jnp.where((dist_own >= 0)[None, None], s_own, -jnp.inf)
        n_sel = k_top * MOBA_BLOCK
        s = jnp.concatenate([s_sel.reshape(B, H, MOBA_Q_BLOCK, n_sel), s_own], axis=-1)
        p = jax.nn.softmax(s, axis=-1).astype(v.dtype)
        p_sel = p[..., :n_sel].reshape(B, H, MOBA_Q_BLOCK, k_top, MOBA_BLOCK)
        p_own = p[..., n_sel:]
        return (jnp.einsum('bhqkc,bhqkcd->bhqd', p_sel, v_g)
                + jnp.einsum('bhqc,bhcd->bhqd', p_own, v_own))

    out = lax.map(step, (to_blocks(q), to_blocks(sel), to_blocks(sel_valid),
                         jnp.arange(n_qb)))
    out = jnp.moveaxis(out, 0, 2).reshape(B, H, Tp, Dh)
    return out[:, :, :T]


def hgrn2(q, f_logit, i, g, lb, out_norm_g):
    B, T, H, Dk = q.shape
    Dv = i.shape[-1]
    dt = q.dtype
    qf = jax.nn.silu(q.astype(jnp.float32))
    f = lb.astype(jnp.float32) + (1.0 - lb.astype(jnp.float32)) * jax.nn.sigmoid(f_logit.astype(jnp.float32))
    kf = 1.0 - f
    log_f = jnp.log(f)
    vf = i.astype(jnp.float32)
    C = HGRN_CHUNK
    n = T // C

    def to_chunks(a):
        return a.reshape(B, n, C, H, a.shape[-1]).transpose(1, 0, 3, 2, 4)

    causal = jnp.tril(jnp.ones((C, C), dtype=bool))

    def chunk_step(S, inp):
        qc, kc, vc, lfc = inp
        b = jnp.cumsum(lfc, axis=2)
        inter = jnp.einsum('bhtk,bhkv->bhtv', qc * jnp.exp(b), S)
        diff = b[:, :, :, None, :] - b[:, :, None, :, :]
        decay = jnp.exp(jnp.where(causal[None, None, :, :, None], diff, -jnp.inf))
        A = jnp.einsum('bhtk,bhsk,bhtsk->bhts', qc, kc, decay)
        intra = jnp.einsum('bhts,bhsv->bhtv', A, vc)
        b_last = b[:, :, -1:, :]
        S_new = (jnp.exp(b_last[:, :, 0, :])[..., None] * S
                 + jnp.einsum('bhsk,bhsv->bhkv', kc * jnp.exp(b_last - b), vc))
        return S_new, inter + intra

    S0 = jnp.zeros((B, H, Dk, Dv), jnp.float32)
    _, o = lax.scan(chunk_step, S0, (to_chunks(qf), to_chunks(kf), to_chunks(vf), to_chunks(log_f)))
    o = o.transpose(1, 0, 3, 2, 4).reshape(B, T, H, Dv)
    o = rms_norm(o, out_norm_g) * jax.nn.silu(g.astype(jnp.float32))
    return o.reshape(B, T, H * Dv).astype(dt)


def hier_moe(h, w_grp, b_grp, w_er, b_er, w_gate, w_up, w_down):
    B, T, D = h.shape
    hf = h.reshape(B * T, D)
    grp_prob = jax.nn.softmax((hf @ w_grp).astype(jnp.float32) + b_grp.astype(jnp.float32), axis=-1)
    grp_w, grp_idx = lax.top_k(grp_prob, 1)
    exp_logits = (jnp.einsum('nd,gde->nge', hf, w_er).astype(jnp.float32)
                  + b_er.astype(jnp.float32)[None])
    exp_logits = jnp.take_along_axis(exp_logits, grp_idx[:, :, None], axis=1)[:, 0]
    exp_prob = jax.nn.softmax(exp_logits, axis=-1)
    top_w, top_idx = lax.top_k(exp_prob, TOPK_IN_GROUP)
    top_w = top_w / jnp.sum(top_w, axis=-1, keepdims=True) * grp_w
    global_idx = grp_idx * EXPERTS_PER_GROUP + top_idx
    combine = jnp.sum(jax.nn.one_hot(global_idx, N_EXPERTS, dtype=jnp.float32)
                      * top_w[..., None], axis=1).astype(h.dtype)
    out = jnp.zeros_like(hf)
    for gi in range(N_GROUPS):
        sl = slice(gi * EXPERTS_PER_GROUP, (gi + 1) * EXPERTS_PER_GROUP)
        a = jnp.einsum('nd,edf->nef', hf, w_gate[sl])
        u = jnp.einsum('nd,edf->nef', hf, w_up[sl])
        hid = jax.nn.silu(a) * u * combine[:, sl, None]
        out = out + jnp.einsum('nef,efd->nd', hid, w_down[sl])
    return out.reshape(B, T, D)


def setup_inputs(seed: int = 0) -> dict:
    key = jax.random.key(seed)
    ks = jax.random.split(key, 16)
    f32 = jnp.float32

    def nrm(k, shape, scale):
        return jax.random.normal(k, shape, f32) * scale

    return {
        "x": nrm(ks[0], (BATCH, SEQ, D_MODEL), 1.0),
        "norm_mix_g": 1.0 + nrm(ks[1], (DEPTH, D_MODEL), 0.02),
        "w_in": nrm(ks[2], (DEPTH, D_MODEL, IN_COLS), D_MODEL ** -0.5),
        "hgrn_lb_logits": nrm(ks[3], (DEPTH + 1, HGRN_WIDTH), 0.5),
        "hgrn_out_norm_g": 1.0 + nrm(ks[4], (DEPTH, HGRN_HEAD_DIM), 0.02),
        "w_out": nrm(ks[5], (DEPTH, D_MIX, D_MODEL), D_MIX ** -0.5),
        "norm_ffn_g": 1.0 + nrm(ks[6], (DEPTH, D_MODEL), 0.02),
        "w_group_router": nrm(ks[7], (DEPTH, D_MODEL, N_GROUPS), D_MODEL ** -0.5),
        "b_group_router": nrm(ks[8], (DEPTH, N_GROUPS), 0.01),
        "w_expert_router": nrm(ks[9], (DEPTH, N_GROUPS, D_MODEL, EXPERTS_PER_GROUP), D_MODEL ** -0.5),
        "b_expert_router": nrm(ks[10], (DEPTH, N_GROUPS, EXPERTS_PER_GROUP), 0.01),
        "w_gate": nrm(ks[11], (DEPTH, N_EXPERTS, D_MODEL, D_EXPERT), D_MODEL ** -0.5),
        "w_up": nrm(ks[12], (DEPTH, N_EXPERTS, D_MODEL, D_EXPERT), D_MODEL ** -0.5),
        "w_down": nrm(ks[13], (DEPTH, N_EXPERTS, D_EXPERT, D_MODEL), D_EXPERT ** -0.5),
        "final_norm_g": 1.0 + nrm(ks[14], (D_MODEL,), 0.02),
    }


def reference(x, norm_mix_g, w_in, hgrn_lb_logits, hgrn_out_norm_g, w_out, norm_ffn_g,
              w_group_router, b_group_router, w_expert_router, b_expert_router,
              w_gate, w_up, w_down, final_norm_g):
    B, T, _ = x.shape
    lb_all = jnp.cumsum(jax.nn.softmax(hgrn_lb_logits.astype(jnp.float32), axis=0), axis=0)[:DEPTH]
    for l in range(DEPTH):
        h = rms_norm(x, norm_mix_g[l])
        proj = h @ w_in[l]
        o0 = 0
        q_a = proj[..., o0:o0 + ATTN_WIDTH]; o0 += ATTN_WIDTH
        k_a = proj[..., o0:o0 + ATTN_WIDTH]; o0 += ATTN_WIDTH
        v_a = proj[..., o0:o0 + ATTN_WIDTH]; o0 += ATTN_WIDTH
        q_r = proj[..., o0:o0 + HGRN_WIDTH]; o0 += HGRN_WIDTH
        f_r = proj[..., o0:o0 + HGRN_WIDTH]; o0 += HGRN_WIDTH
        i_r = proj[..., o0:o0 + HGRN_WIDTH]; o0 += HGRN_WIDTH
        g_r = proj[..., o0:o0 + HGRN_WIDTH]

        def heads(a):
            return a.reshape(B, T, N_ATTN_HEADS, ATTN_HEAD_DIM).transpose(0, 2, 1, 3)

        o_attn = moba_attention(heads(q_a), heads(k_a), heads(v_a))
        o_attn = o_attn.transpose(0, 2, 1, 3).reshape(B, T, ATTN_WIDTH)

        def rheads(a):
            return a.reshape(B, T, N_HGRN_HEADS, HGRN_HEAD_DIM)

        o_rec = hgrn2(rheads(q_r), rheads(f_r), rheads(i_r), rheads(g_r),
                      lb_all[l].reshape(N_HGRN_HEADS, HGRN_HEAD_DIM), hgrn_out_norm_g[l])
        x = x + jnp.concatenate([o_attn, o_rec], axis=-1) @ w_out[l]

        h2 = rms_norm(x, norm_ffn_g[l])
        x = x + hier_moe(h2, w_group_router[l], b_group_router[l], w_expert_router[l],
                         b_expert_router[l], w_gate[l], w_up[l], w_down[l])
    return rms_norm(x, final_norm_g)
```

```python
import functools

import jax
import jax.numpy as jnp
import numpy as np
from jax import lax
from jax.experimental import pallas as pl
from jax.experimental.pallas import tpu as pltpu

F32 = jnp.float32
BF16 = jnp.bfloat16
HIGHEST = lax.Precision.HIGHEST

HEAD_DIM = 128
MOBA_BLOCK = 256
MOBA_TOPK = 3
N_GROUPS = 4
EXPERTS_PER_GROUP = 4
N_EXPERTS = N_GROUPS * EXPERTS_PER_GROUP
RMS_EPS = 1e-6

LANES = 128
VMEM_LIMIT_BYTES = 56 * 1024 * 1024

HGRN_CHUNK = 64
HGRN_SUB = 16
HGRN_STEP = 256
MOE_TILE = 256
NT_DIMS = (((1,), (1,)), ((), ()))
TN_DIMS = (((0,), (0,)), ((), ()))


def _cparams(*sem):
    return pltpu.CompilerParams(dimension_semantics=sem, vmem_limit_bytes=VMEM_LIMIT_BYTES)


def _inproj_kernel(x_ref, g_ref, w_ref, o_ref, hn_ref):
    @pl.when(pl.program_id(1) == 0)
    def _():
        x = x_ref[...]
        ms = jnp.mean(x * x, axis=-1, keepdims=True)
        hn_ref[...] = (x * lax.rsqrt(ms + RMS_EPS) * g_ref[...]).astype(BF16)

    o_ref[...] = jnp.dot(hn_ref[...], w_ref[...], preferred_element_type=F32).astype(o_ref.dtype)


def _inproj(x2, g, w, out_dtype, tm=512, tn=1024):
    n, d = x2.shape
    cols = w.shape[1]
    return pl.pallas_call(
        _inproj_kernel,
        grid=(n // tm, cols // tn),
        in_specs=[
            pl.BlockSpec((tm, d), lambda i, j: (i, 0)),
            pl.BlockSpec((1, d), lambda i, j: (0, 0)),
            pl.BlockSpec((d, tn), lambda i, j: (0, j)),
        ],
        out_specs=pl.BlockSpec((tm, tn), lambda i, j: (i, j)),
        out_shape=jax.ShapeDtypeStruct((n, cols), out_dtype),
        scratch_shapes=[pltpu.VMEM((tm, d), BF16)],
        compiler_params=_cparams("parallel", "arbitrary"),
        name="inproj",
    )(x2, g, w)


def _moba_kernel(slopes_ref, q_ref, k_ref, v_ref, o_ref, kmean_ref, sel_ref, m_ref, l_ref, acc_ref,
                 *, nb, scale):
    h = pl.program_id(1)
    i = pl.program_id(2)
    bs = MOBA_BLOCK

    @pl.when(i == 0)
    def _():
        kmean_ref[...] = jnp.zeros_like(kmean_ref)
        for j in range(nb):
            kb = k_ref[0, j * bs:(j + 1) * bs, :].astype(F32)
            kmean_ref[j:j + 1, :] = jnp.mean(kb, axis=0, keepdims=True)

    q = q_ref[0]
    gate = lax.dot_general(q.astype(F32), kmean_ref[...], NT_DIMS,
                           precision=HIGHEST, preferred_element_type=F32)
    lane = lax.broadcasted_iota(jnp.int32, gate.shape, 1)
    rank = jnp.zeros(gate.shape, F32)
    for jp in range(nb - 1):
        col = gate[:, jp:jp + 1]
        beats = (col > gate) | ((col == gate) & (lane > jp))
        rank = rank + jnp.where(beats, (jp < i).astype(F32), 0.0)
    sel_ref[...] = jnp.where((rank < MOBA_TOPK) & (lane < i), 1.0, 0.0)

    slope = slopes_ref[h]
    colpos = lax.broadcasted_iota(jnp.int32, (1, bs), 1).astype(F32)

    k_own = k_ref[0, pl.ds(pl.multiple_of(i * bs, bs), bs), :]
    v_own = v_ref[0, pl.ds(pl.multiple_of(i * bs, bs), bs), :]
    s = lax.dot_general(q, k_own, NT_DIMS, preferred_element_type=F32) * scale + slope * colpos
    rowi = lax.broadcasted_iota(jnp.int32, (bs, bs), 0)
    coli = lax.broadcasted_iota(jnp.int32, (bs, bs), 1)
    s = jnp.where(rowi >= coli, s, -jnp.inf)
    m0 = jnp.max(s, axis=1, keepdims=True)
    p = jnp.exp(s - m0)
    m_ref[...] = m0
    l_ref[...] = jnp.sum(p, axis=1, keepdims=True)
    acc_ref[...] = jnp.dot(p.astype(BF16), v_own, preferred_element_type=F32)

    for j in range(nb - 1):
        @pl.when(j < i)
        def _(j=j):
            kj = k_ref[0, j * bs:(j + 1) * bs, :]
            vj = v_ref[0, j * bs:(j + 1) * bs, :]
            off = ((j - i) * bs).astype(F32)
            sj = (lax.dot_general(q, kj, NT_DIMS, preferred_element_type=F32) * scale
                  + slope * (colpos + off))
            sj = jnp.where(sel_ref[:, j:j + 1] > 0.5, sj, -jnp.inf)
            m_old = m_ref[...]
            m_new = jnp.maximum(m_old, jnp.max(sj, axis=1, keepdims=True))
            alpha = jnp.exp(m_old - m_new)
            pj = jnp.exp(sj - m_new)
            l_ref[...] = alpha * l_ref[...] + jnp.sum(pj, axis=1, keepdims=True)
            acc_ref[...] = alpha * acc_ref[...] + jnp.dot(pj.astype(BF16), vj,
                                                          preferred_element_type=F32)
            m_ref[...] = m_new

    o_ref[0] = (acc_ref[...] / l_ref[...]).astype(o_ref.dtype)


def _moba(qkv, slopes, n_heads):
    b, t, _ = qkv.shape
    bs = MOBA_BLOCK
    nb = t // bs
    kern = functools.partial(_moba_kernel, nb=nb, scale=HEAD_DIM ** -0.5)
    return pl.pallas_call(
        kern,
        grid=(b, n_heads, nb),
        in_specs=[
            pl.BlockSpec(memory_space=pltpu.SMEM),
            pl.BlockSpec((1, bs, HEAD_DIM), lambda bi, hi, qi: (bi, qi, hi)),
            pl.BlockSpec((1, t, HEAD_DIM), lambda bi, hi, qi: (bi, 0, n_heads + hi)),
            pl.BlockSpec((1, t, HEAD_DIM), lambda bi, hi, qi: (bi, 0, 2 * n_heads + hi)),
        ],
        out_specs=pl.BlockSpec((1, bs, HEAD_DIM), lambda bi, hi, qi: (bi, qi, hi)),
        out_shape=jax.ShapeDtypeStruct((b, t, n_heads * HEAD_DIM), BF16),
        scratch_shapes=[
            pltpu.VMEM((LANES, HEAD_DIM), F32),
            pltpu.VMEM((bs, LANES), F32),
            pltpu.VMEM((bs, 1), F32),
            pltpu.VMEM((bs, 1), F32),
            pltpu.VMEM((bs, HEAD_DIM), F32),
        ],
        compiler_params=_cparams("parallel", "parallel", "arbitrary"),
        name="moba",
    )(slopes, qkv, qkv, qkv)


def _sigmoid(x):
    return 1.0 / (1.0 + jnp.exp(-x))


def _hgrn_chunk(q, fl, iv, g, lb, gn, st_ref):
    c = HGRN_CHUNK
    sub = HGRN_SUB
    nsub = c // sub
    qf = q * _sigmoid(q)
    f = lb + (1.0 - lb) * _sigmoid(fl)
    kf = 1.0 - f
    lf = jnp.log(f)
    r = lax.broadcasted_iota(jnp.int32, (c, c), 0)
    cc = lax.broadcasted_iota(jnp.int32, (c, c), 1)
    tril = jnp.where(r >= cc, 1.0, 0.0).astype(F32)
    bcum = jnp.dot(tril, lf, precision=HIGHEST, preferred_element_type=F32)

    st = st_ref[...]
    inter = lax.dot_general((qf * jnp.exp(bcum)).astype(BF16), st.astype(BF16), NT_DIMS,
                            preferred_element_type=F32)

    rowid = lax.broadcasted_iota(jnp.int32, (c, 1), 0)
    lane = lax.broadcasted_iota(jnp.int32, (sub, LANES), 1)
    tsub = lax.broadcasted_iota(jnp.int32, (sub, 1), 0)
    ones = jnp.ones((LANES, LANES), BF16)
    a_rows = []
    for bi in range(nsub):
        lo = bi * sub
        b_i = bcum[lo:lo + sub]
        q_i = qf[lo:lo + sub]
        k_i = kf[lo:lo + sub]
        pieces = []
        for s in range(sub):
            d = jnp.where(tsub >= s, b_i - b_i[s:s + 1, :], -jnp.inf)
            pieces.append(q_i * k_i[s:s + 1, :] * jnp.exp(d))
        pm = jnp.concatenate(pieces, axis=0).astype(BF16)
        rs = jnp.dot(pm, ones, preferred_element_type=F32)
        a_blk = jnp.zeros((sub, LANES), F32)
        for s in range(sub):
            a_blk = a_blk + jnp.where(lane == lo + s, rs[s * sub:(s + 1) * sub, :], 0.0)
        a_blk = a_blk[:, :c]
        if bi > 0:
            b0 = bcum[lo - 1:lo, :]
            qt = q_i * jnp.exp(b_i - b0)
            kt = kf * jnp.exp(jnp.where(rowid < lo, b0 - bcum, -jnp.inf))
            a_blk = a_blk + lax.dot_general(qt.astype(BF16), kt.astype(BF16), NT_DIMS,
                                            preferred_element_type=F32)
        a_rows.append(a_blk)
    a = jnp.concatenate(a_rows, axis=0)
    intra = jnp.dot(a.astype(BF16), iv.astype(BF16), preferred_element_type=F32)

    b_last = bcum[c - 1:c, :]
    khat = kf * jnp.exp(b_last - bcum)
    st_ref[...] = st * jnp.exp(b_last) + lax.dot_general(
        iv.astype(BF16), khat.astype(BF16), TN_DIMS, preferred_element_type=F32)

    o = inter + intra
    y = o * lax.rsqrt(jnp.mean(o * o, axis=-1, keepdims=True) + RMS_EPS) * gn
    return y * (g * _sigmoid(g))


def _hgrn_kernel(q_ref, f_ref, i_ref, g_ref, lb_ref, gn_ref, o_ref, st_ref):
    @pl.when(pl.program_id(2) == 0)
    def _():
        st_ref[...] = jnp.zeros_like(st_ref)

    lb = lb_ref[...]
    gn = gn_ref[...]
    for ci in range(HGRN_STEP // HGRN_CHUNK):
        sl = slice(ci * HGRN_CHUNK, (ci + 1) * HGRN_CHUNK)
        out = _hgrn_chunk(q_ref[0, sl, :], f_ref[0, sl, :], i_ref[0, sl, :], g_ref[0, sl, :],
                          lb, gn, st_ref)
        o_ref[0, sl, :] = out.astype(o_ref.dtype)


def _hgrn(hp, lb, gn, n_heads):
    b, t, _ = hp.shape
    d = HEAD_DIM
    ts = HGRN_STEP

    def col(group):
        return pl.BlockSpec((1, ts, d), lambda bi, hi, ti: (bi, ti, group * n_heads + hi))

    return pl.pallas_call(
        _hgrn_kernel,
        grid=(b, n_heads, t // ts),
        in_specs=[col(0), col(1), col(2), col(3),
                  pl.BlockSpec((1, d), lambda bi, hi, ti: (0, hi)),
                  pl.BlockSpec((1, d), lambda bi, hi, ti: (0, 0))],
        out_specs=pl.BlockSpec((1, ts, d), lambda bi, hi, ti: (bi, ti, hi)),
        out_shape=jax.ShapeDtypeStruct((b, t, n_heads * d), BF16),
        scratch_shapes=[pltpu.VMEM((d, d), F32)],
        compiler_params=_cparams("parallel", "parallel", "arbitrary"),
        name="hgrn2",
    )(hp, hp, hp, hp, lb, gn)


def _outproj_kernel(oa_ref, or_ref, x_ref, w_ref, g_ref, wr_ref, br_ref, x1_ref, h2_ref, rt_ref):
    wa = oa_ref.shape[1]
    x1 = (x_ref[...]
          + jnp.dot(oa_ref[...], w_ref[0:wa, :], preferred_element_type=F32)
          + jnp.dot(or_ref[...], w_ref[wa:, :], preferred_element_type=F32))
    x1_ref[...] = x1
    h2 = x1 * lax.rsqrt(jnp.mean(x1 * x1, axis=-1, keepdims=True) + RMS_EPS) * g_ref[...]
    h2_ref[...] = h2.astype(h2_ref.dtype)

    logits = jnp.dot(h2, wr_ref[...], precision=HIGHEST, preferred_element_type=F32) + br_ref[...]
    lane = lax.broadcasted_iota(jnp.int32, logits.shape, 1)
    big = jnp.int32(4 * LANES)
    ninf = -jnp.inf

    lg = jnp.where(lane < N_GROUPS, logits, ninf)
    mg = jnp.max(lg, axis=1, keepdims=True)
    gidx = jnp.min(jnp.where(lg == mg, lane, big), axis=1, keepdims=True)
    grp_w = 1.0 / jnp.sum(jnp.exp(lg - mg), axis=1, keepdims=True)

    lo = N_GROUPS + EXPERTS_PER_GROUP * gidx
    le = jnp.where((lane >= lo) & (lane < lo + EXPERTS_PER_GROUP), logits, ninf)
    m1 = jnp.max(le, axis=1, keepdims=True)
    i1 = jnp.min(jnp.where(le == m1, lane, big), axis=1, keepdims=True)
    le2 = jnp.where(lane == i1, ninf, le)
    m2 = jnp.max(le2, axis=1, keepdims=True)
    i2 = jnp.min(jnp.where(le2 == m2, lane, big), axis=1, keepdims=True)
    r21 = jnp.exp(m2 - m1)
    w1 = grp_w / (1.0 + r21)
    w2 = grp_w * r21 / (1.0 + r21)
    e1 = (i1 - N_GROUPS).astype(F32)
    e2 = (i2 - N_GROUPS).astype(F32)
    rt_ref[...] = jnp.where(lane == 0, e1, jnp.where(lane == 1, e2, jnp.where(lane == 2, w1, w2)))


def _outproj(oa, orec, x2, w_out, g, wr, br, tm=256):
    n, d = x2.shape
    wa = oa.shape[1]
    wrc = orec.shape[1]
    row = lambda i: (i, 0)
    const = lambda i: (0, 0)
    return pl.pallas_call(
        _outproj_kernel,
        grid=(n // tm,),
        in_specs=[
            pl.BlockSpec((tm, wa), row),
            pl.BlockSpec((tm, wrc), row),
            pl.BlockSpec((tm, d), row),
            pl.BlockSpec((wa + wrc, d), const),
            pl.BlockSpec((1, d), const),
            pl.BlockSpec((d, LANES), const),
            pl.BlockSpec((1, LANES), const),
        ],
        out_specs=[pl.BlockSpec((tm, d), row), pl.BlockSpec((tm, d), row),
                   pl.BlockSpec((tm, LANES), row)],
        out_shape=[jax.ShapeDtypeStruct((n, d), F32), jax.ShapeDtypeStruct((n, d), BF16),
                   jax.ShapeDtypeStruct((n, LANES), F32)],
        compiler_params=_cparams("parallel"),
        name="outproj_route",
    )(oa, orec, x2, w_out, g, wr, br)


def _moe_kernel(te_ref, nu_ref, x_ref, wg_ref, wu_ref, wd_ref, y_ref):
    n = pl.program_id(0)

    @pl.when(n < nu_ref[0])
    def _():
        x = x_ref[...]
        a = jnp.dot(x, wg_ref[0], preferred_element_type=F32)
        u = jnp.dot(x, wu_ref[0], preferred_element_type=F32)
        hid = (a * _sigmoid(a) * u).astype(BF16)
        y_ref[...] = jnp.dot(hid, wd_ref[0], preferred_element_type=F32)

    @pl.when(n >= nu_ref[0])
    def _():
        y_ref[...] = jnp.zeros_like(y_ref)


def _moe(tile_expert, n_used, xs, wg, wu, wd):
    p, d = xs.shape
    f = wg.shape[2]
    tm = MOE_TILE
    grid_spec = pltpu.PrefetchScalarGridSpec(
        num_scalar_prefetch=2,
        grid=(p // tm,),
        in_specs=[
            pl.BlockSpec((tm, d), lambda n, te, nu: (n, 0)),
            pl.BlockSpec((1, d, f), lambda n, te, nu: (te[n], 0, 0)),
            pl.BlockSpec((1, d, f), lambda n, te, nu: (te[n], 0, 0)),
            pl.BlockSpec((1, f, d), lambda n, te, nu: (te[n], 0, 0)),
        ],
        out_specs=pl.BlockSpec((tm, d), lambda n, te, nu: (n, 0)),
    )
    return pl.pallas_call(
        _moe_kernel,
        grid_spec=grid_spec,
        out_shape=jax.ShapeDtypeStruct((p, d), F32),
        compiler_params=_cparams("arbitrary"),
        name="moe_ffn",
    )(tile_expert, n_used, xs, wg, wu, wd)


def _final_kernel(x1_ref, y1_ref, y2_ref, rt_ref, g_ref, o_ref):
    rt = rt_ref[...]
    x2 = x1_ref[...] + rt[:, 2:3] * y1_ref[...] + rt[:, 3:4] * y2_ref[...]
    o_ref[...] = x2 * lax.rsqrt(jnp.mean(x2 * x2, axis=-1, keepdims=True) + RMS_EPS) * g_ref[...]


def _final(x1, y1, y2, rt, g, tm=256):
    n, d = x1.shape
    row = lambda i: (i, 0)
    return pl.pallas_call(
        _final_kernel,
        grid=(n // tm,),
        in_specs=[pl.BlockSpec((tm, d), row), pl.BlockSpec((tm, d), row), pl.BlockSpec((tm, d), row),
                  pl.BlockSpec((tm, LANES), row), pl.BlockSpec((1, d), lambda i: (0, 0))],
        out_specs=pl.BlockSpec((tm, d), row),
        out_shape=jax.ShapeDtypeStruct((n, d), F32),
        compiler_params=_cparams("parallel"),
        name="combine_norm",
    )(x1, y1, y2, rt, g)


def _dispatch_plan(e1, e2, n_tokens):
    tm = MOE_TILE
    n_rows = 2 * n_tokens + N_EXPERTS * tm
    n_tiles = n_rows // tm
    ef = jnp.concatenate([e1, e2])
    onehot = (ef[:, None] == jnp.arange(N_EXPERTS, dtype=jnp.int32)[None, :]).astype(jnp.int32)
    csum = jnp.cumsum(onehot, axis=0)
    counts = csum[-1]
    rank = jnp.sum((csum - onehot) * onehot, axis=1)
    tiles_per = (counts + tm - 1) // tm
    tile_end = jnp.cumsum(tiles_per)
    row_start = (tile_end - tiles_per) * tm
    pos = row_start[ef] + rank
    tok = jnp.concatenate([jnp.arange(n_tokens, dtype=jnp.int32)] * 2)
    src = jnp.zeros((n_rows,), jnp.int32).at[pos].set(tok)
    n_used = tile_end[-1]
    tile_ids = jnp.arange(n_tiles, dtype=jnp.int32)
    tile_expert = jnp.sum((tile_ids[:, None] >= tile_end[None, :]).astype(jnp.int32), axis=1)
    last_expert = jnp.sum((n_used - 1 >= tile_end).astype(jnp.int32))
    tile_expert = jnp.where(tile_ids < n_used, tile_expert, last_expert).astype(jnp.int32)
    return src, pos[:n_tokens], pos[n_tokens:], tile_expert, n_used.reshape(1).astype(jnp.int32)


def kernel(x, norm_mix_g, w_in, hgrn_lb_logits, hgrn_out_norm_g, w_out, norm_ffn_g, w_group_router,
           b_group_router, w_expert_router, b_expert_router, w_gate, w_up, w_down, final_norm_g):
    b, t, d = x.shape
    n = b * t
    depth = w_in.shape[0]
    assert depth == 1, "the final norm is fused into the combine step of the only layer"
    attn_w = d // 2
    n_heads = attn_w // HEAD_DIM
    x2 = x.reshape(n, d)
    lb_all = jnp.cumsum(jax.nn.softmax(hgrn_lb_logits.astype(F32), axis=0), axis=0)[:depth]
    slopes = jnp.asarray(2.0 ** (-8.0 * np.arange(1, n_heads + 1) / n_heads), dtype=F32)

    for l in range(depth):
        g_mix = norm_mix_g[l].reshape(1, d)
        w_in_l = w_in[l].astype(BF16)
        qkv = _inproj(x2, g_mix, w_in_l[:, :3 * attn_w], BF16)
        hp = _inproj(x2, g_mix, w_in_l[:, 3 * attn_w:], F32)
        o_attn = _moba(qkv.reshape(b, t, 3 * attn_w), slopes, n_heads)
        o_rec = _hgrn(hp.reshape(b, t, -1), lb_all[l].reshape(1, -1),
                      hgrn_out_norm_g[l].reshape(1, HEAD_DIM), n_heads)

        wr = jnp.concatenate(
            [w_group_router[l],
             jnp.transpose(w_expert_router[l], (1, 0, 2)).reshape(d, N_EXPERTS)], axis=1)
        wr = jnp.pad(wr, ((0, 0), (0, LANES - wr.shape[1])))
        br = jnp.concatenate([b_group_router[l], b_expert_router[l].reshape(-1)])
        br = jnp.pad(br, (0, LANES - br.shape[0])).reshape(1, LANES)
        x1, h2, rt = _outproj(o_attn.reshape(n, attn_w), o_rec.reshape(n, -1), x2,
                              w_out[l].astype(BF16), norm_ffn_g[l].reshape(1, d), wr, br)

        e1 = rt[:, 0].astype(jnp.int32)
        e2 = rt[:, 1].astype(jnp.int32)
        src, pos1, pos2, tile_expert, n_used = _dispatch_plan(e1, e2, n)
        xs = jnp.take(h2, src, axis=0)
        ys = _moe(tile_expert, n_used, xs, w_gate[l].astype(BF16), w_up[l].astype(BF16),
                  w_down[l].astype(BF16))
        y1 = jnp.take(ys, pos1, axis=0)
        y2 = jnp.take(ys, pos2, axis=0)
        x2 = _final(x1, y1, y2, rt, final_norm_g.reshape(1, d))
    return x2.reshape(b, t, d)
```

```python
import functools

import jax
import jax.numpy as jnp
import numpy as np
from jax import lax
from jax.experimental import pallas as pl
from jax.experimental.pallas import tpu as pltpu

F32 = jnp.float32
BF16 = jnp.bfloat16
HIGHEST = lax.Precision.HIGHEST

HEAD_DIM = 128
MOBA_BLOCK = 256
MOBA_TOPK = 3
N_GROUPS = 4
EXPERTS_PER_GROUP = 4
N_EXPERTS = N_GROUPS * EXPERTS_PER_GROUP
RMS_EPS = 1e-6

LANES = 128
VMEM_LIMIT_BYTES = 56 * 1024 * 1024

HGRN_CHUNK = 64
HGRN_SUB = 16
HGRN_STEP = 256
MOE_TILE = 256
NT_DIMS = (((1,), (1,)), ((), ()))
TN_DIMS = (((0,), (0,)), ((), ()))


def _cparams(*sem):
    return pltpu.CompilerParams(dimension_semantics=sem, vmem_limit_bytes=VMEM_LIMIT_BYTES)


def _inproj_kernel(x_ref, g_ref, w_ref, cs_ref, o_ref, hn_ref):
    @pl.when(pl.program_id(1) == 0)
    def _():
        x = x_ref[...]
        ms = jnp.mean(x * x, axis=-1, keepdims=True)
        hn_ref[...] = (x * lax.rsqrt(ms + RMS_EPS) * g_ref[...]).astype(BF16)

    acc = jnp.dot(hn_ref[...], w_ref[...], preferred_element_type=F32)
    o_ref[...] = (acc * cs_ref[...]).astype(o_ref.dtype)


def _inproj(x2, g, w, colscale, out_dtype, tm=512, tn=1024):
    n, d = x2.shape
    cols = w.shape[1]
    return pl.pallas_call(
        _inproj_kernel,
        grid=(n // tm, cols // tn),
        in_specs=[
            pl.BlockSpec((tm, d), lambda i, j: (i, 0)),
            pl.BlockSpec((1, d), lambda i, j: (0, 0)),
            pl.BlockSpec((d, tn), lambda i, j: (0, j)),
            pl.BlockSpec((1, tn), lambda i, j: (0, j)),
        ],
        out_specs=pl.BlockSpec((tm, tn), lambda i, j: (i, j)),
        out_shape=jax.ShapeDtypeStruct((n, cols), out_dtype),
        scratch_shapes=[pltpu.VMEM((tm, d), BF16)],
        compiler_params=_cparams("parallel", "arbitrary"),
        name="inproj",
    )(x2, g, w, colscale)


MOBA_EXTRA_POS = 0
MOBA_EXTRA_SEL = 8
MASK_BIG = 2.0 ** 60
LOG2E = 1.4426950408889634


def _moba_key_extras(t):
    nb = t // MOBA_BLOCK
    assert MOBA_EXTRA_SEL + nb <= LANES and MOBA_BLOCK <= 256
    pos = np.arange(t)
    kx = np.zeros((t, LANES), np.float32)
    kx[:, MOBA_EXTRA_POS:MOBA_EXTRA_POS + 3] = ((pos // MOBA_BLOCK) * MOBA_BLOCK)[:, None]
    kx[:, MOBA_EXTRA_POS + 3:MOBA_EXTRA_POS + 6] = (pos % MOBA_BLOCK)[:, None]
    kx[pos, MOBA_EXTRA_SEL + pos // MOBA_BLOCK] = -MASK_BIG
    return jnp.asarray(kx, dtype=BF16)


def _moba_kernel(sl_ref, q_ref, k_ref, v_ref, kx_ref, o_ref, kaug_ref, kmean_ref, *, nb):
    h = pl.program_id(1)
    i = pl.program_id(2)
    bs = MOBA_BLOCK
    nbp = kmean_ref.shape[0]

    @pl.when(i == 0)
    def _():
        kaug_ref[:, 0:HEAD_DIM] = k_ref[0]
        kaug_ref[:, HEAD_DIM:] = kx_ref[...]
        kmean_ref[...] = jnp.zeros_like(kmean_ref)
        for j in range(nb):
            kb = k_ref[0, j * bs:(j + 1) * bs, :].astype(F32)
            kmean_ref[j:j + 1, :] = jnp.mean(kb, axis=0, keepdims=True)

    q = q_ref[0]
    gate_t = lax.dot_general(kmean_ref[...], q.astype(F32), NT_DIMS,
                             precision=HIGHEST, preferred_element_type=F32)
    blk = lax.broadcasted_iota(jnp.int32, gate_t.shape, 0)
    rank = jnp.zeros(gate_t.shape, F32)
    for jp in range(nb - 1):
        other = gate_t[jp:jp + 1, :]
        beats = (other > gate_t) | ((other == gate_t) & (blk > jp))
        rank = rank + jnp.where(beats, (jp < i).astype(F32), 0.0)
    notsel_t = jnp.where((blk < i) & (rank >= MOBA_TOPK), 1.0, 0.0)

    r8 = lax.broadcasted_iota(jnp.int32, (8, bs), 0)
    piece = r8 % 3
    slope_rows = jnp.where(r8 >= 6, 0.0,
                           jnp.where(piece == 0, sl_ref[0, h],
                                     jnp.where(piece == 1, sl_ref[1, h], sl_ref[2, h])))
    qx_t = jnp.concatenate(
        [slope_rows, notsel_t, jnp.zeros((LANES - 8 - nbp, bs), F32)], axis=0)
    qa = jnp.concatenate([q, qx_t.T.astype(BF16)], axis=1)

    rowi = lax.broadcasted_iota(jnp.int32, (bs, bs), 0)
    coli = lax.broadcasted_iota(jnp.int32, (bs, bs), 1)

    for c in range(nb):
        @pl.when(i == c)
        def _(c=c):
            n = (c + 1) * bs
            s = lax.dot_general(qa, kaug_ref[0:n, :], NT_DIMS, preferred_element_type=F32)
            s_own = jnp.where(rowi >= coli, s[:, c * bs:], -jnp.inf)
            m = jnp.max(s_own, axis=1, keepdims=True)
            if c > 0:
                s_past = s[:, :c * bs]
                m = jnp.maximum(m, jnp.max(s_past, axis=1, keepdims=True))
            p_own = jnp.exp2(s_own - m)
            l = jnp.sum(p_own, axis=1, keepdims=True)
            acc = jnp.dot(p_own.astype(BF16), v_ref[0, c * bs:n, :], preferred_element_type=F32)
            if c > 0:
                p_past = jnp.exp2(s_past - m)
                l = l + jnp.sum(p_past, axis=1, keepdims=True)
                acc = acc + jnp.dot(p_past.astype(BF16), v_ref[0, 0:c * bs, :],
                                    preferred_element_type=F32)
            o_ref[0] = (acc / l).astype(o_ref.dtype)


def _moba(qkv, slope_pieces, n_heads):
    b, t, _ = qkv.shape
    bs = MOBA_BLOCK
    nb = t // bs
    nbp = -(-nb // 8) * 8
    kern = functools.partial(_moba_kernel, nb=nb)
    return pl.pallas_call(
        kern,
        grid=(b, n_heads, nb),
        in_specs=[
            pl.BlockSpec(memory_space=pltpu.SMEM),
            pl.BlockSpec((1, bs, HEAD_DIM), lambda bi, hi, qi: (bi, qi, hi)),
            pl.BlockSpec((1, t, HEAD_DIM), lambda bi, hi, qi: (bi, 0, n_heads + hi)),
            pl.BlockSpec((1, t, HEAD_DIM), lambda bi, hi, qi: (bi, 0, 2 * n_heads + hi)),
            pl.BlockSpec((t, LANES), lambda bi, hi, qi: (0, 0)),
        ],
        out_specs=pl.BlockSpec((1, bs, HEAD_DIM), lambda bi, hi, qi: (bi, qi, hi)),
        out_shape=jax.ShapeDtypeStruct((b, t, n_heads * HEAD_DIM), BF16),
        scratch_shapes=[
            pltpu.VMEM((t, HEAD_DIM + LANES), BF16),
            pltpu.VMEM((nbp, HEAD_DIM), F32),
        ],
        compiler_params=_cparams("parallel", "parallel", "arbitrary"),
        name="moba",
    )(slope_pieces, qkv, qkv, qkv, _moba_key_extras(t))


def _sigmoid(x):
    return 1.0 / (1.0 + jnp.exp(-x))


def _hgrn_chunk(q, fl, iv, g, lb, gn, st_ref):
    c = HGRN_CHUNK
    sub = HGRN_SUB
    nsub = c // sub
    qf = q * _sigmoid(q)
    f = lb + (1.0 - lb) * _sigmoid(fl)
    kf = 1.0 - f
    lf = jnp.log(f)
    r = lax.broadcasted_iota(jnp.int32, (c, c), 0)
    cc = lax.broadcasted_iota(jnp.int32, (c, c), 1)
    tril = jnp.where(r >= cc, 1.0, 0.0).astype(F32)
    bcum = jnp.dot(tril, lf, precision=HIGHEST, preferred_element_type=F32)

    st = st_ref[...]
    inter = lax.dot_general((qf * jnp.exp(bcum)).astype(BF16), st.astype(BF16), NT_DIMS,
                            preferred_element_type=F32)

    rowid = lax.broadcasted_iota(jnp.int32, (c, 1), 0)
    lane = lax.broadcasted_iota(jnp.int32, (sub, LANES), 1)
    tsub = lax.broadcasted_iota(jnp.int32, (sub, 1), 0)
    ones = jnp.ones((LANES, LANES), BF16)
    a_rows = []
    for bi in range(nsub):
        lo = bi * sub
        b_i = bcum[lo:lo + sub]
        q_i = qf[lo:lo + sub]
        k_i = kf[lo:lo + sub]
        pieces = []
        for s in range(sub):
            d = jnp.where(tsub >= s, b_i - b_i[s:s + 1, :], -jnp.inf)
            pieces.append(q_i * k_i[s:s + 1, :] * jnp.exp(d))
        pm = jnp.concatenate(pieces, axis=0).astype(BF16)
        rs = jnp.dot(pm, ones, preferred_element_type=F32)
        a_blk = jnp.zeros((sub, LANES), F32)
        for s in range(sub):
            a_blk = a_blk + jnp.where(lane == lo + s, rs[s * sub:(s + 1) * sub, :], 0.0)
        a_blk = a_blk[:, :c]
        if bi > 0:
            b0 = bcum[lo - 1:lo, :]
            qt = q_i * jnp.exp(b_i - b0)
            kt = kf * jnp.exp(jnp.where(rowid < lo, b0 - bcum, -jnp.inf))
            a_blk = a_blk + lax.dot_general(qt.astype(BF16), kt.astype(BF16), NT_DIMS,
                                            preferred_element_type=F32)
        a_rows.append(a_blk)
    a = jnp.concatenate(a_rows, axis=0)
    intra = jnp.dot(a.astype(BF16), iv.astype(BF16), preferred_element_type=F32)

    b_last = bcum[c - 1:c, :]
    khat = kf * jnp.exp(b_last - bcum)
    st_ref[...] = st * jnp.exp(b_last) + lax.dot_general(
        iv.astype(BF16), khat.astype(BF16), TN_DIMS, preferred_element_type=F32)

    o = inter + intra
    y = o * lax.rsqrt(jnp.mean(o * o, axis=-1, keepdims=True) + RMS_EPS) * gn
    return y * (g * _sigmoid(g))


def _hgrn_kernel(q_ref, f_ref, i_ref, g_ref, lb_ref, gn_ref, o_ref, st_ref):
    @pl.when(pl.program_id(2) == 0)
    def _():
        st_ref[...] = jnp.zeros_like(st_ref)

    lb = lb_ref[...]
    gn = gn_ref[...]
    for ci in range(HGRN_STEP // HGRN_CHUNK):
        sl = slice(ci * HGRN_CHUNK, (ci + 1) * HGRN_CHUNK)
        out = _hgrn_chunk(q_ref[0, sl, :], f_ref[0, sl, :], i_ref[0, sl, :], g_ref[0, sl, :],
                          lb, gn, st_ref)
        o_ref[0, sl, :] = out.astype(o_ref.dtype)


def _hgrn(hp, lb, gn, n_heads):
    b, t, _ = hp.shape
    d = HEAD_DIM
    ts = HGRN_STEP

    def col(group):
        return pl.BlockSpec((1, ts, d), lambda bi, hi, ti: (bi, ti, group * n_heads + hi))

    return pl.pallas_call(
        _hgrn_kernel,
        grid=(b, n_heads, t // ts),
        in_specs=[col(0), col(1), col(2), col(3),
                  pl.BlockSpec((1, d), lambda bi, hi, ti: (0, hi)),
                  pl.BlockSpec((1, d), lambda bi, hi, ti: (0, 0))],
        out_specs=pl.BlockSpec((1, ts, d), lambda bi, hi, ti: (bi, ti, hi)),
        out_shape=jax.ShapeDtypeStruct((b, t, n_heads * d), BF16),
        scratch_shapes=[pltpu.VMEM((d, d), F32)],
        compiler_params=_cparams("parallel", "parallel", "arbitrary"),
        name="hgrn2",
    )(hp, hp, hp, hp, lb, gn)


def _outproj_kernel(oa_ref, or_ref, x_ref, w_ref, g_ref, wr_ref, br_ref,
                    x1_ref, h2_ref, rt_ref, cnt_out_ref, cnt_ref):
    wa = oa_ref.shape[1]
    x1 = (x_ref[...]
          + jnp.dot(oa_ref[...], w_ref[0:wa, :], preferred_element_type=F32)
          + jnp.dot(or_ref[...], w_ref[wa:, :], preferred_element_type=F32))
    x1_ref[...] = x1
    h2 = x1 * lax.rsqrt(jnp.mean(x1 * x1, axis=-1, keepdims=True) + RMS_EPS) * g_ref[...]
    h2_ref[...] = h2

    logits = jnp.dot(h2, wr_ref[...], precision=HIGHEST, preferred_element_type=F32) + br_ref[...]
    lane = lax.broadcasted_iota(jnp.int32, logits.shape, 1)
    big = jnp.int32(4 * LANES)
    ninf = -jnp.inf

    lg = jnp.where(lane < N_GROUPS, logits, ninf)
    mg = jnp.max(lg, axis=1, keepdims=True)
    gidx = jnp.min(jnp.where(lg == mg, lane, big), axis=1, keepdims=True)
    grp_w = 1.0 / jnp.sum(jnp.exp(lg - mg), axis=1, keepdims=True)

    lo = N_GROUPS + EXPERTS_PER_GROUP * gidx
    le = jnp.where((lane >= lo) & (lane < lo + EXPERTS_PER_GROUP), logits, ninf)
    m1 = jnp.max(le, axis=1, keepdims=True)
    i1 = jnp.min(jnp.where(le == m1, lane, big), axis=1, keepdims=True)
    le2 = jnp.where(lane == i1, ninf, le)
    m2 = jnp.max(le2, axis=1, keepdims=True)
    i2 = jnp.min(jnp.where(le2 == m2, lane, big), axis=1, keepdims=True)
    r21 = jnp.exp(m2 - m1)
    w1 = grp_w / (1.0 + r21)
    w2 = grp_w * r21 / (1.0 + r21)
    e1 = i1 - N_GROUPS
    e2 = i2 - N_GROUPS

    @pl.when(pl.program_id(0) == 0)
    def _():
        cnt_ref[...] = jnp.zeros_like(cnt_ref)

    tm = logits.shape[0]
    onehot = jnp.where((lane == e1) | (lane == e2), 1.0, 0.0)
    rr = lax.broadcasted_iota(jnp.int32, (tm, tm), 0)
    rc = lax.broadcasted_iota(jnp.int32, (tm, tm), 1)
    before = jnp.where(rr > rc, 1.0, 0.0).astype(BF16)
    prefix = (jnp.dot(before, onehot.astype(BF16), preferred_element_type=F32) + cnt_ref[0:1, :])
    rank1 = jnp.sum(jnp.where(lane == e1, prefix, 0.0), axis=1, keepdims=True)
    rank2 = jnp.sum(jnp.where(lane == e2, prefix, 0.0), axis=1, keepdims=True)
    total = cnt_ref[0:1, :] + jnp.sum(onehot, axis=0, keepdims=True)
    cnt_ref[0:1, :] = total
    cnt_out_ref[...] = jnp.broadcast_to(total, cnt_out_ref.shape)

    cols = [e1.astype(F32), e2.astype(F32), w1, w2, rank1, rank2]
    rt = jnp.zeros(logits.shape, F32)
    for ci, cv in enumerate(cols):
        rt = jnp.where(lane == ci, cv, rt)
    rt_ref[...] = rt


def _outproj(oa, orec, x2, w_out, g, wr, br, tm=256):
    n, d = x2.shape
    wa = oa.shape[1]
    wrc = orec.shape[1]
    row = lambda i: (i, 0)
    const = lambda i: (0, 0)
    return pl.pallas_call(
        _outproj_kernel,
        grid=(n // tm,),
        in_specs=[
            pl.BlockSpec((tm, wa), row),
            pl.BlockSpec((tm, wrc), row),
            pl.BlockSpec((tm, d), row),
            pl.BlockSpec((wa + wrc, d), const),
            pl.BlockSpec((1, d), const),
            pl.BlockSpec((d, LANES), const),
            pl.BlockSpec((1, LANES), const),
        ],
        out_specs=[pl.BlockSpec((tm, d), row), pl.BlockSpec((tm, d), row),
                   pl.BlockSpec((tm, LANES), row), pl.BlockSpec((8, LANES), const)],
        out_shape=[jax.ShapeDtypeStruct((n, d), F32), jax.ShapeDtypeStruct((n, d), F32),
                   jax.ShapeDtypeStruct((n, LANES), F32), jax.ShapeDtypeStruct((8, LANES), F32)],
        scratch_shapes=[pltpu.VMEM((8, LANES), F32)],
        compiler_params=_cparams("arbitrary"),
        name="outproj_route",
    )(oa, orec, x2, w_out, g, wr, br)


def _moe_kernel(te_ref, nu_ref, x_ref, wg_ref, wu_ref, wd_ref, y_ref):
    n = pl.program_id(0)

    @pl.when(n < nu_ref[0])
    def _():
        x = x_ref[...].astype(BF16)
        a = jnp.dot(x, wg_ref[0], preferred_element_type=F32)
        u = jnp.dot(x, wu_ref[0], preferred_element_type=F32)
        hid = (a * _sigmoid(a) * u).astype(BF16)
        y_ref[...] = jnp.dot(hid, wd_ref[0], preferred_element_type=F32)

    @pl.when(n >= nu_ref[0])
    def _():
        y_ref[...] = jnp.zeros_like(y_ref)


def _moe(tile_expert, n_used, xs, wg, wu, wd):
    p = xs.shape[0]
    d, f = wg.shape[1:]
    tm = MOE_TILE
    grid_spec = pltpu.PrefetchScalarGridSpec(
        num_scalar_prefetch=2,
        grid=(p // tm,),
        in_specs=[
            pl.BlockSpec((tm, d), lambda n, te, nu: (n, 0)),
            pl.BlockSpec((1, d, f), lambda n, te, nu: (te[n], 0, 0)),
            pl.BlockSpec((1, d, f), lambda n, te, nu: (te[n], 0, 0)),
            pl.BlockSpec((1, f, d), lambda n, te, nu: (te[n], 0, 0)),
        ],
        out_specs=pl.BlockSpec((tm, d), lambda n, te, nu: (n, 0)),
    )
    return pl.pallas_call(
        _moe_kernel,
        grid_spec=grid_spec,
        out_shape=jax.ShapeDtypeStruct((p, d), F32),
        compiler_params=_cparams("arbitrary"),
        name="moe_ffn",
    )(tile_expert, n_used, xs, wg, wu, wd)


def _final_kernel(x1_ref, y1_ref, y2_ref, rt_ref, g_ref, o_ref):
    rt = rt_ref[...]
    x2 = x1_ref[...] + rt[:, 2:3] * y1_ref[...] + rt[:, 3:4] * y2_ref[...]
    o_ref[...] = x2 * lax.rsqrt(jnp.mean(x2 * x2, axis=-1, keepdims=True) + RMS_EPS) * g_ref[...]


def _final(x1, y1, y2, rt, g, tm=256):
    n, d = x1.shape
    row = lambda i: (i, 0)
    return pl.pallas_call(
        _final_kernel,
        grid=(n // tm,),
        in_specs=[pl.BlockSpec((tm, d), row), pl.BlockSpec((tm, d), row), pl.BlockSpec((tm, d), row),
                  pl.BlockSpec((tm, LANES), row), pl.BlockSpec((1, d), lambda i: (0, 0))],
        out_specs=pl.BlockSpec((tm, d), row),
        out_shape=jax.ShapeDtypeStruct((n, d), F32),
        compiler_params=_cparams("parallel"),
        name="combine_norm",
    )(x1, y1, y2, rt, g)


def _dispatch_plan(rt, counts, n_tokens):
    tm = MOE_TILE
    n_rows = 2 * n_tokens + N_EXPERTS * tm
    n_tiles = n_rows // tm
    ids = rt[:, 0:6].astype(jnp.int32)
    ef = jnp.concatenate([ids[:, 0], ids[:, 1]])
    rank = jnp.concatenate([ids[:, 4], ids[:, 5]])
    counts = counts[0, :N_EXPERTS].astype(jnp.int32)
    tiles_per = (counts + tm - 1) // tm
    tile_end = jnp.cumsum(tiles_per)
    row_start = (tile_end - tiles_per) * tm
    pos = row_start[ef] + rank
    tok = jnp.concatenate([jnp.arange(n_tokens, dtype=jnp.int32)] * 2)
    src = jnp.zeros((n_rows,), jnp.int32).at[pos].set(tok)
    n_used = tile_end[-1]
    tile_ids = jnp.arange(n_tiles, dtype=jnp.int32)
    tile_expert = jnp.sum((tile_ids[:, None] >= tile_end[None, :]).astype(jnp.int32), axis=1)
    last_expert = jnp.sum((n_used - 1 >= tile_end).astype(jnp.int32))
    tile_expert = jnp.where(tile_ids < n_used, tile_expert, last_expert).astype(jnp.int32)
    return src, pos[:n_tokens], pos[n_tokens:], tile_expert, n_used.reshape(1).astype(jnp.int32)


def kernel(x, norm_mix_g, w_in, hgrn_lb_logits, hgrn_out_norm_g, w_out, norm_ffn_g, w_group_router,
           b_group_router, w_expert_router, b_expert_router, w_gate, w_up, w_down, final_norm_g):
    b, t, d = x.shape
    n = b * t
    depth = w_in.shape[0]
    assert depth == 1, "the final norm is fused into the combine step of the only layer"
    attn_w = d // 2
    n_heads = attn_w // HEAD_DIM
    x2 = x.reshape(n, d)
    lb_all = jnp.cumsum(jax.nn.softmax(hgrn_lb_logits.astype(F32), axis=0), axis=0)[:depth]
    slope2 = jnp.asarray(2.0 ** (-8.0 * np.arange(1, n_heads + 1) / n_heads), dtype=F32) * F32(LOG2E)
    s_hi = slope2.astype(BF16).astype(F32)
    s_mid = (slope2 - s_hi).astype(BF16).astype(F32)
    s_lo = (slope2 - s_hi - s_mid).astype(BF16).astype(F32)
    slope_pieces = jnp.stack([s_hi, s_mid, s_lo])
    qkv_scale = jnp.concatenate([jnp.full((attn_w,), HEAD_DIM ** -0.5 * LOG2E, F32),
                                 jnp.ones((2 * attn_w,), F32)]).reshape(1, -1)

    for l in range(depth):
        g_mix = norm_mix_g[l].reshape(1, d)
        w_in_l = w_in[l].astype(BF16)
        qkv = _inproj(x2, g_mix, w_in_l[:, :3 * attn_w], qkv_scale, BF16)
        hp = _inproj(x2, g_mix, w_in_l[:, 3 * attn_w:], jnp.ones((1, 4 * (d - attn_w)), F32), F32)
        o_attn = _moba(qkv.reshape(b, t, 3 * attn_w), slope_pieces, n_heads)
        o_rec = _hgrn(hp.reshape(b, t, -1), lb_all[l].reshape(1, -1),
                      hgrn_out_norm_g[l].reshape(1, HEAD_DIM), n_heads)

        wr = jnp.concatenate(
            [w_group_router[l],
             jnp.transpose(w_expert_router[l], (1, 0, 2)).reshape(d, N_EXPERTS)], axis=1)
        wr = jnp.pad(wr, ((0, 0), (0, LANES - wr.shape[1])))
        br = jnp.concatenate([b_group_router[l], b_expert_router[l].reshape(-1)])
        br = jnp.pad(br, (0, LANES - br.shape[0])).reshape(1, LANES)
        x1, h2, rt, counts = _outproj(o_attn.reshape(n, attn_w), o_rec.reshape(n, -1), x2,
                                      w_out[l].astype(BF16), norm_ffn_g[l].reshape(1, d), wr, br)
        src, pos1, pos2, tile_expert, n_used = _dispatch_plan(rt, counts, n)
        xs = jnp.take(h2, src, axis=0)
        ys = _moe(tile_expert, n_used, xs, w_gate[l].astype(BF16), w_up[l].astype(BF16),
                  w_down[l].astype(BF16))
        y1 = jnp.take(ys, pos1, axis=0)
        y2 = jnp.take(ys, pos2, axis=0)
        x2 = _final(x1, y1, y2, rt, final_norm_g.reshape(1, d))
    return x2.reshape(b, t, d)
```

```python
import functools

import jax
import jax.numpy as jnp
import numpy as np
from jax import lax
from jax.experimental import pallas as pl
from jax.experimental.pallas import tpu as pltpu

F32 = jnp.float32
BF16 = jnp.bfloat16
HIGHEST = lax.Precision.HIGHEST

HEAD_DIM = 128
MOBA_BLOCK = 256
MOBA_TOPK = 3
N_GROUPS = 4
EXPERTS_PER_GROUP = 4
N_EXPERTS = N_GROUPS * EXPERTS_PER_GROUP
RMS_EPS = 1e-6

LANES = 128
VMEM_LIMIT_BYTES = 56 * 1024 * 1024

HGRN_CHUNK = 64
HGRN_SUB = 16
HGRN_STEP = 256
HGRN_HEADS_PER_STEP = 4
MOE_TILE = 256
NT_DIMS = (((1,), (1,)), ((), ()))
TN_DIMS = (((0,), (0,)), ((), ()))


def _cparams(*sem):
    return pltpu.CompilerParams(dimension_semantics=sem, vmem_limit_bytes=VMEM_LIMIT_BYTES)


def _inproj_kernel(x_ref, g_ref, w_ref, cs_ref, o_ref, hn_ref):
    @pl.when(pl.program_id(1) == 0)
    def _():
        x = x_ref[...]
        ms = jnp.mean(x * x, axis=-1, keepdims=True)
        hn_ref[...] = (x * lax.rsqrt(ms + RMS_EPS) * g_ref[...]).astype(BF16)

    acc = jnp.dot(hn_ref[...], w_ref[...], preferred_element_type=F32)
    o_ref[...] = (acc * cs_ref[...]).astype(o_ref.dtype)


def _inproj(x2, g, w, colscale, out_dtype, tm=512, tn=1024):
    n, d = x2.shape
    cols = w.shape[1]
    return pl.pallas_call(
        _inproj_kernel,
        grid=(n // tm, cols // tn),
        in_specs=[
            pl.BlockSpec((tm, d), lambda i, j: (i, 0)),
            pl.BlockSpec((1, d), lambda i, j: (0, 0)),
            pl.BlockSpec((d, tn), lambda i, j: (0, j)),
            pl.BlockSpec((1, tn), lambda i, j: (0, j)),
        ],
        out_specs=pl.BlockSpec((tm, tn), lambda i, j: (i, j)),
        out_shape=jax.ShapeDtypeStruct((n, cols), out_dtype),
        scratch_shapes=[pltpu.VMEM((tm, d), BF16)],
        compiler_params=_cparams("parallel", "arbitrary"),
        name="inproj",
    )(x2, g, w, colscale)


MOBA_EXTRA_POS = 0
MOBA_EXTRA_SEL = 8
MASK_BIG = 2.0 ** 60
LOG2E = 1.4426950408889634


def _moba_key_extras(t):
    nb = t // MOBA_BLOCK
    assert MOBA_EXTRA_SEL + nb <= LANES and MOBA_BLOCK <= 256
    pos = np.arange(t)
    kx = np.zeros((t, LANES), np.float32)
    kx[:, MOBA_EXTRA_POS:MOBA_EXTRA_POS + 3] = ((pos // MOBA_BLOCK) * MOBA_BLOCK)[:, None]
    kx[:, MOBA_EXTRA_POS + 3:MOBA_EXTRA_POS + 6] = (pos % MOBA_BLOCK)[:, None]
    kx[pos, MOBA_EXTRA_SEL + pos // MOBA_BLOCK] = -MASK_BIG
    return jnp.asarray(kx, dtype=BF16)


def _moba_kernel(sl_ref, q_ref, k_ref, v_ref, kx_ref, o_ref, kaug_ref, kmean_ref, *, nb):
    h = pl.program_id(1)
    i = pl.program_id(2)
    bs = MOBA_BLOCK
    nbp = kmean_ref.shape[0]

    @pl.when(i == 0)
    def _():
        kaug_ref[:, 0:HEAD_DIM] = k_ref[0]
        kaug_ref[:, HEAD_DIM:] = kx_ref[...]
        kmean_ref[...] = jnp.zeros_like(kmean_ref)
        for j in range(nb):
            kb = k_ref[0, j * bs:(j + 1) * bs, :].astype(F32)
            kmean_ref[j:j + 1, :] = jnp.mean(kb, axis=0, keepdims=True)

    q = q_ref[0]
    gate_t = lax.dot_general(kmean_ref[...], q.astype(F32), NT_DIMS,
                             precision=HIGHEST, preferred_element_type=F32)
    blk = lax.broadcasted_iota(jnp.int32, gate_t.shape, 0)
    rank = jnp.zeros(gate_t.shape, F32)
    for jp in range(nb - 1):
        other = gate_t[jp:jp + 1, :]
        beats = (other > gate_t) | ((other == gate_t) & (blk > jp))
        rank = rank + jnp.where(beats, (jp < i).astype(F32), 0.0)
    notsel_t = jnp.where((blk < i) & (rank >= MOBA_TOPK), 1.0, 0.0)

    r8 = lax.broadcasted_iota(jnp.int32, (8, bs), 0)
    piece = r8 % 3
    slope_rows = jnp.where(r8 >= 6, 0.0,
                           jnp.where(piece == 0, sl_ref[0, h],
                                     jnp.where(piece == 1, sl_ref[1, h], sl_ref[2, h])))
    qx_t = jnp.concatenate(
        [slope_rows, notsel_t, jnp.zeros((LANES - 8 - nbp, bs), F32)], axis=0)
    qa = jnp.concatenate([q, qx_t.T.astype(BF16)], axis=1)

    rowi = lax.broadcasted_iota(jnp.int32, (bs, bs), 0)
    coli = lax.broadcasted_iota(jnp.int32, (bs, bs), 1)

    for c in range(nb):
        @pl.when(i == c)
        def _(c=c):
            n = (c + 1) * bs
            s = lax.dot_general(qa, kaug_ref[0:n, :], NT_DIMS, preferred_element_type=F32)
            s_own = jnp.where(rowi >= coli, s[:, c * bs:], -jnp.inf)
            m = jnp.max(s_own, axis=1, keepdims=True)
            if c > 0:
                s_past = s[:, :c * bs]
                m = jnp.maximum(m, jnp.max(s_past, axis=1, keepdims=True))
            p_own = jnp.exp2(s_own - m)
            l = jnp.sum(p_own, axis=1, keepdims=True)
            acc = jnp.dot(p_own.astype(BF16), v_ref[0, c * bs:n, :], preferred_element_type=F32)
            if c > 0:
                p_past = jnp.exp2(s_past - m)
                l = l + jnp.sum(p_past, axis=1, keepdims=True)
                acc = acc + jnp.dot(p_past.astype(BF16), v_ref[0, 0:c * bs, :],
                                    preferred_element_type=F32)
            o_ref[0] = (acc / l).astype(o_ref.dtype)


def _moba(qkv, slope_pieces, n_heads):
    b, t, _ = qkv.shape
    bs = MOBA_BLOCK
    nb = t // bs
    nbp = -(-nb // 8) * 8
    kern = functools.partial(_moba_kernel, nb=nb)
    return pl.pallas_call(
        kern,
        grid=(b, n_heads, nb),
        in_specs=[
            pl.BlockSpec(memory_space=pltpu.SMEM),
            pl.BlockSpec((1, bs, HEAD_DIM), lambda bi, hi, qi: (bi, qi, hi)),
            pl.BlockSpec((1, t, HEAD_DIM), lambda bi, hi, qi: (bi, 0, n_heads + hi)),
            pl.BlockSpec((1, t, HEAD_DIM), lambda bi, hi, qi: (bi, 0, 2 * n_heads + hi)),
            pl.BlockSpec((t, LANES), lambda bi, hi, qi: (0, 0)),
        ],
        out_specs=pl.BlockSpec((1, bs, HEAD_DIM), lambda bi, hi, qi: (bi, qi, hi)),
        out_shape=jax.ShapeDtypeStruct((b, t, n_heads * HEAD_DIM), BF16),
        scratch_shapes=[
            pltpu.VMEM((t, HEAD_DIM + LANES), BF16),
            pltpu.VMEM((nbp, HEAD_DIM), F32),
        ],
        compiler_params=_cparams("parallel", "parallel", "arbitrary"),
        name="moba",
    )(slope_pieces, qkv, qkv, qkv, _moba_key_extras(t))


def _sigmoid(x):
    return 1.0 / (1.0 + jnp.exp(-x))


def _hgrn_chunk(q, fl, iv, g, lb, gn, st):
    c = HGRN_CHUNK
    sub = HGRN_SUB
    half = sub // 2
    nsub = c // sub
    qf = q * _sigmoid(q)
    f = lb + (1.0 - lb) * _sigmoid(fl)
    kf = jnp.maximum(1.0 - f, 0.0)
    r = lax.broadcasted_iota(jnp.int32, (c, c), 0)
    cc = lax.broadcasted_iota(jnp.int32, (c, c), 1)
    tril = jnp.where(r >= cc, 1.0, 0.0).astype(F32)
    b2 = jnp.dot(tril, jnp.log2(f), precision=HIGHEST, preferred_element_type=F32)
    c2 = b2 - jnp.log2(kf)

    inter = lax.dot_general((qf * jnp.exp2(b2)).astype(BF16), st.astype(BF16), NT_DIMS,
                            preferred_element_type=F32)

    lane = lax.broadcasted_iota(jnp.int32, (sub, LANES), 1)
    tsub = lax.broadcasted_iota(jnp.int32, (sub, LANES), 0)
    colid = lax.broadcasted_iota(jnp.int32, (sub, c), 1)
    ones = jnp.ones((LANES, LANES), BF16)
    a_rows = []
    for bi in range(nsub):
        lo = bi * sub
        b_i = b2[lo:lo + sub]
        c_i = c2[lo:lo + sub]
        q_i = qf[lo:lo + sub]
        pieces = []
        for s in range(sub):
            if s < half:
                pieces.append(q_i * jnp.exp2(b_i - c_i[s:s + 1, :]))
            else:
                pieces.append(q_i[half:] * jnp.exp2(b_i[half:] - c_i[s:s + 1, :]))
        pm = jnp.concatenate(pieces, axis=0).astype(BF16)
        rs = jnp.dot(pm, ones, preferred_element_type=F32)
        rel = lane - lo
        key = jnp.where((rel >= 0) & (rel <= tsub), rel, -1)
        key_lo, key_hi = key[:half], key[half:]
        a_lo = jnp.zeros((half, LANES), F32)
        a_hi = jnp.zeros((half, LANES), F32)
        off = 0
        for s in range(sub):
            if s < half:
                a_lo = jnp.where(key_lo == s, rs[off:off + half], a_lo)
                a_hi = jnp.where(key_hi == s, rs[off + half:off + sub], a_hi)
                off += sub
            else:
                a_hi = jnp.where(key_hi == s, rs[off:off + half], a_hi)
                off += half
        a_blk = jnp.concatenate([a_lo, a_hi], axis=0)[:, :c]
        if bi > 0:
            b0 = b2[lo - 1:lo, :]
            qt = q_i * jnp.exp2(b_i - b0)
            kt = jnp.exp2(jnp.minimum(b0 - c2, 0.0))
            cross = lax.dot_general(qt.astype(BF16), kt.astype(BF16), NT_DIMS,
                                    preferred_element_type=F32)
            a_blk = a_blk + jnp.where(colid < lo, cross, 0.0)
        a_rows.append(a_blk)
    a = jnp.concatenate(a_rows, axis=0)
    intra = jnp.dot(a.astype(BF16), iv.astype(BF16), preferred_element_type=F32)

    b_last = b2[c - 1:c, :]
    khat = jnp.exp2(b_last - c2)
    st_new = st * jnp.exp2(b_last) + lax.dot_general(
        iv.astype(BF16), khat.astype(BF16), TN_DIMS, preferred_element_type=F32)

    o = inter + intra
    y = o * lax.rsqrt(jnp.mean(o * o, axis=-1, keepdims=True) + RMS_EPS) * gn
    return y * (g * _sigmoid(g)), st_new


def _hgrn_kernel(q_ref, f_ref, i_ref, g_ref, lb_ref, gn_ref, o_ref, st_ref):
    @pl.when(pl.program_id(2) == 0)
    def _():
        st_ref[...] = jnp.zeros_like(st_ref)

    gn = gn_ref[...]
    d = HEAD_DIM
    for hh in range(HGRN_HEADS_PER_STEP):
        cs = slice(hh * d, (hh + 1) * d)
        lb = lb_ref[:, cs]
        st = st_ref[hh]
        for ci in range(HGRN_STEP // HGRN_CHUNK):
            sl = slice(ci * HGRN_CHUNK, (ci + 1) * HGRN_CHUNK)
            out, st = _hgrn_chunk(q_ref[0, sl, cs], f_ref[0, sl, cs], i_ref[0, sl, cs],
                                  g_ref[0, sl, cs], lb, gn, st)
            o_ref[0, sl, cs] = out.astype(o_ref.dtype)
        st_ref[hh] = st


def _hgrn(hp, lb, gn, n_heads):
    b, t, _ = hp.shape
    d = HEAD_DIM
    ts = HGRN_STEP
    hps = HGRN_HEADS_PER_STEP
    ng = n_heads // hps

    def col(group):
        return pl.BlockSpec((1, ts, hps * d), lambda bi, hi, ti: (bi, ti, group * ng + hi))

    return pl.pallas_call(
        _hgrn_kernel,
        grid=(b, ng, t // ts),
        in_specs=[col(0), col(1), col(2), col(3),
                  pl.BlockSpec((1, hps * d), lambda bi, hi, ti: (0, hi)),
                  pl.BlockSpec((1, d), lambda bi, hi, ti: (0, 0))],
        out_specs=pl.BlockSpec((1, ts, hps * d), lambda bi, hi, ti: (bi, ti, hi)),
        out_shape=jax.ShapeDtypeStruct((b, t, n_heads * d), BF16),
        scratch_shapes=[pltpu.VMEM((hps, d, d), F32)],
        compiler_params=_cparams("parallel", "parallel", "arbitrary"),
        name="hgrn2",
    )(hp, hp, hp, hp, lb, gn)


def _outproj_kernel(oa_ref, or_ref, x_ref, w_ref, g_ref, wr_ref, br_ref,
                    x1_ref, h2_ref, rt_ref, cnt_out_ref, cnt_ref):
    wa = oa_ref.shape[1]
    x1 = (x_ref[...]
          + jnp.dot(oa_ref[...], w_ref[0:wa, :], preferred_element_type=F32)
          + jnp.dot(or_ref[...], w_ref[wa:, :], preferred_element_type=F32))
    x1_ref[...] = x1
    h2 = x1 * lax.rsqrt(jnp.mean(x1 * x1, axis=-1, keepdims=True) + RMS_EPS) * g_ref[...]
    h2_ref[...] = h2

    logits = jnp.dot(h2, wr_ref[...], precision=HIGHEST, preferred_element_type=F32) + br_ref[...]
    lane = lax.broadcasted_iota(jnp.int32, logits.shape, 1)
    big = jnp.int32(4 * LANES)
    ninf = -jnp.inf

    lg = jnp.where(lane < N_GROUPS, logits, ninf)
    mg = jnp.max(lg, axis=1, keepdims=True)
    gidx = jnp.min(jnp.where(lg == mg, lane, big), axis=1, keepdims=True)
    grp_w = 1.0 / jnp.sum(jnp.exp(lg - mg), axis=1, keepdims=True)

    lo = N_GROUPS + EXPERTS_PER_GROUP * gidx
    le = jnp.where((lane >= lo) & (lane < lo + EXPERTS_PER_GROUP), logits, ninf)
    m1 = jnp.max(le, axis=1, keepdims=True)
    i1 = jnp.min(jnp.where(le == m1, lane, big), axis=1, keepdims=True)
    le2 = jnp.where(lane == i1, ninf, le)
    m2 = jnp.max(le2, axis=1, keepdims=True)
    i2 = jnp.min(jnp.where(le2 == m2, lane, big), axis=1, keepdims=True)
    r21 = jnp.exp(m2 - m1)
    w1 = grp_w / (1.0 + r21)
    w2 = grp_w * r21 / (1.0 + r21)
    e1 = i1 - N_GROUPS
    e2 = i2 - N_GROUPS

    @pl.when(pl.program_id(0) == 0)
    def _():
        cnt_ref[...] = jnp.zeros_like(cnt_ref)

    tm = logits.shape[0]
    onehot = jnp.where((lane == e1) | (lane == e2), 1.0, 0.0)
    rr = lax.broadcasted_iota(jnp.int32, (tm, tm), 0)
    rc = lax.broadcasted_iota(jnp.int32, (tm, tm), 1)
    before = jnp.where(rr > rc, 1.0, 0.0).astype(BF16)
    prefix = (jnp.dot(before, onehot.astype(BF16), preferred_element_type=F32) + cnt_ref[0:1, :])
    rank1 = jnp.sum(jnp.where(lane == e1, prefix, 0.0), axis=1, keepdims=True)
    rank2 = jnp.sum(jnp.where(lane == e2, prefix, 0.0), axis=1, keepdims=True)
    total = cnt_ref[0:1, :] + jnp.sum(onehot, axis=0, keepdims=True)
    cnt_ref[0:1, :] = total
    cnt_out_ref[...] = jnp.broadcast_to(total, cnt_out_ref.shape)

    cols = [e1.astype(F32), e2.astype(F32), w1, w2, rank1, rank2]
    rt = jnp.zeros(logits.shape, F32)
    for ci, cv in enumerate(cols):
        rt = jnp.where(lane == ci, cv, rt)
    rt_ref[...] = rt


def _outproj(oa, orec, x2, w_out, g, wr, br, tm=256):
    n, d = x2.shape
    wa = oa.shape[1]
    wrc = orec.shape[1]
    row = lambda i: (i, 0)
    const = lambda i: (0, 0)
    return pl.pallas_call(
        _outproj_kernel,
        grid=(n // tm,),
        in_specs=[
            pl.BlockSpec((tm, wa), row),
            pl.BlockSpec((tm, wrc), row),
            pl.BlockSpec((tm, d), row),
            pl.BlockSpec((wa + wrc, d), const),
            pl.BlockSpec((1, d), const),
            pl.BlockSpec((d, LANES), const),
            pl.BlockSpec((1, LANES), const),
        ],
        out_specs=[pl.BlockSpec((tm, d), row), pl.BlockSpec((tm, d), row),
                   pl.BlockSpec((tm, LANES), row), pl.BlockSpec((8, LANES), const)],
        out_shape=[jax.ShapeDtypeStruct((n, d), F32), jax.ShapeDtypeStruct((n, d), F32),
                   jax.ShapeDtypeStruct((n, LANES), F32), jax.ShapeDtypeStruct((8, LANES), F32)],
        scratch_shapes=[pltpu.VMEM((8, LANES), F32)],
        compiler_params=_cparams("arbitrary"),
        name="outproj_route",
    )(oa, orec, x2, w_out, g, wr, br)


def _moe_kernel(te_ref, nu_ref, x_ref, wg_ref, wu_ref, wd_ref, y_ref):
    n = pl.program_id(0)

    @pl.when(n < nu_ref[0])
    def _():
        x = x_ref[...].astype(BF16)
        a = jnp.dot(x, wg_ref[0], preferred_element_type=F32)
        u = jnp.dot(x, wu_ref[0], preferred_element_type=F32)
        hid = (a * _sigmoid(a) * u).astype(BF16)
        y_ref[...] = jnp.dot(hid, wd_ref[0], preferred_element_type=F32)

    @pl.when(n >= nu_ref[0])
    def _():
        y_ref[...] = jnp.zeros_like(y_ref)


def _moe(tile_expert, n_used, xs, wg, wu, wd):
    p = xs.shape[0]
    d, f = wg.shape[1:]
    tm = MOE_TILE
    grid_spec = pltpu.PrefetchScalarGridSpec(
        num_scalar_prefetch=2,
        grid=(p // tm,),
        in_specs=[
            pl.BlockSpec((tm, d), lambda n, te, nu: (n, 0)),
            pl.BlockSpec((1, d, f), lambda n, te, nu: (te[n], 0, 0)),
            pl.BlockSpec((1, d, f), lambda n, te, nu: (te[n], 0, 0)),
            pl.BlockSpec((1, f, d), lambda n, te, nu: (te[n], 0, 0)),
        ],
        out_specs=pl.BlockSpec((tm, d), lambda n, te, nu: (n, 0)),
    )
    return pl.pallas_call(
        _moe_kernel,
        grid_spec=grid_spec,
        out_shape=jax.ShapeDtypeStruct((p, d), F32),
        compiler_params=_cparams("arbitrary"),
        name="moe_ffn",
    )(tile_expert, n_used, xs, wg, wu, wd)


def _final_kernel(x1_ref, y1_ref, y2_ref, rt_ref, g_ref, o_ref):
    rt = rt_ref[...]
    x2 = x1_ref[...] + rt[:, 2:3] * y1_ref[...] + rt[:, 3:4] * y2_ref[...]
    o_ref[...] = x2 * lax.rsqrt(jnp.mean(x2 * x2, axis=-1, keepdims=True) + RMS_EPS) * g_ref[...]


def _final(x1, y1, y2, rt, g, tm=256):
    n, d = x1.shape
    row = lambda i: (i, 0)
    return pl.pallas_call(
        _final_kernel,
        grid=(n // tm,),
        in_specs=[pl.BlockSpec((tm, d), row), pl.BlockSpec((tm, d), row), pl.BlockSpec((tm, d), row),
                  pl.BlockSpec((tm, LANES), row), pl.BlockSpec((1, d), lambda i: (0, 0))],
        out_specs=pl.BlockSpec((tm, d), row),
        out_shape=jax.ShapeDtypeStruct((n, d), F32),
        compiler_params=_cparams("parallel"),
        name="combine_norm",
    )(x1, y1, y2, rt, g)


def _dispatch_plan(rt, counts, n_tokens):
    tm = MOE_TILE
    n_rows = 2 * n_tokens + N_EXPERTS * tm
    n_tiles = n_rows // tm
    ids = rt[:, 0:6].astype(jnp.int32)
    ef = jnp.concatenate([ids[:, 0], ids[:, 1]])
    rank = jnp.concatenate([ids[:, 4], ids[:, 5]])
    counts = counts[0, :N_EXPERTS].astype(jnp.int32)
    tiles_per = (counts + tm - 1) // tm
    tile_end = jnp.cumsum(tiles_per)
    row_start = (tile_end - tiles_per) * tm
    pos = row_start[ef] + rank
    tok = jnp.concatenate([jnp.arange(n_tokens, dtype=jnp.int32)] * 2)
    src = jnp.zeros((n_rows,), jnp.int32).at[pos].set(tok)
    n_used = tile_end[-1]
    tile_ids = jnp.arange(n_tiles, dtype=jnp.int32)
    tile_expert = jnp.sum((tile_ids[:, None] >= tile_end[None, :]).astype(jnp.int32), axis=1)
    last_expert = jnp.sum((n_used - 1 >= tile_end).astype(jnp.int32))
    tile_expert = jnp.where(tile_ids < n_used, tile_expert, last_expert).astype(jnp.int32)
    return src, pos[:n_tokens], pos[n_tokens:], tile_expert, n_used.reshape(1).astype(jnp.int32)


def kernel(x, norm_mix_g, w_in, hgrn_lb_logits, hgrn_out_norm_g, w_out, norm_ffn_g, w_group_router,
           b_group_router, w_expert_router, b_expert_router, w_gate, w_up, w_down, final_norm_g):
    b, t, d = x.shape
    n = b * t
    depth = w_in.shape[0]
    assert depth == 1, "the final norm is fused into the combine step of the only layer"
    attn_w = d // 2
    n_heads = attn_w // HEAD_DIM
    x2 = x.reshape(n, d)
    lb_all = jnp.cumsum(jax.nn.softmax(hgrn_lb_logits.astype(F32), axis=0), axis=0)[:depth]
    slope2 = jnp.asarray(2.0 ** (-8.0 * np.arange(1, n_heads + 1) / n_heads), dtype=F32) * F32(LOG2E)
    s_hi = slope2.astype(BF16).astype(F32)
    s_mid = (slope2 - s_hi).astype(BF16).astype(F32)
    s_lo = (slope2 - s_hi - s_mid).astype(BF16).astype(F32)
    slope_pieces = jnp.stack([s_hi, s_mid, s_lo])
    qkv_scale = jnp.concatenate([jnp.full((attn_w,), HEAD_DIM ** -0.5 * LOG2E, F32),
                                 jnp.ones((2 * attn_w,), F32)]).reshape(1, -1)

    for l in range(depth):
        g_mix = norm_mix_g[l].reshape(1, d)
        w_in_l = w_in[l].astype(BF16)
        qkv = _inproj(x2, g_mix, w_in_l[:, :3 * attn_w], qkv_scale, BF16)
        hp = _inproj(x2, g_mix, w_in_l[:, 3 * attn_w:], jnp.ones((1, 4 * (d - attn_w)), F32), F32)
        o_attn = _moba(qkv.reshape(b, t, 3 * attn_w), slope_pieces, n_heads)
        o_rec = _hgrn(hp.reshape(b, t, -1), lb_all[l].reshape(1, -1),
                      hgrn_out_norm_g[l].reshape(1, HEAD_DIM), n_heads)

        wr = jnp.concatenate(
            [w_group_router[l],
             jnp.transpose(w_expert_router[l], (1, 0, 2)).reshape(d, N_EXPERTS)], axis=1)
        wr = jnp.pad(wr, ((0, 0), (0, LANES - wr.shape[1])))
        br = jnp.concatenate([b_group_router[l], b_expert_router[l].reshape(-1)])
        br = jnp.pad(br, (0, LANES - br.shape[0])).reshape(1, LANES)
        x1, h2, rt, counts = _outproj(o_attn.reshape(n, attn_w), o_rec.reshape(n, -1), x2,
                                      w_out[l].astype(BF16), norm_ffn_g[l].reshape(1, d), wr, br)
        src, pos1, pos2, tile_expert, n_used = _dispatch_plan(rt, counts, n)
        xs = h2.at[src].get(mode="promise_in_bounds")
        ys = _moe(tile_expert, n_used, xs, w_gate[l].astype(BF16), w_up[l].astype(BF16),
                  w_down[l].astype(BF16))
        y1 = ys.at[pos1].get(mode="promise_in_bounds")
        y2 = ys.at[pos2].get(mode="promise_in_bounds")
        x2 = _final(x1, y1, y2, rt, final_norm_g.reshape(1, d))
    return x2.reshape(b, t, d)
```

```python
import functools

import jax
import jax.numpy as jnp
import numpy as np
from jax import lax
from jax.experimental import pallas as pl
from jax.experimental.pallas import tpu as pltpu

F32 = jnp.float32
BF16 = jnp.bfloat16
HIGHEST = lax.Precision.HIGHEST

HEAD_DIM = 128
MOBA_BLOCK = 256
MOBA_TOPK = 3
N_GROUPS = 4
EXPERTS_PER_GROUP = 4
N_EXPERTS = N_GROUPS * EXPERTS_PER_GROUP
RMS_EPS = 1e-6

LANES = 128
VMEM_LIMIT_BYTES = 56 * 1024 * 1024

HGRN_CHUNK = 64
HGRN_SUB = 16
HGRN_STEP = 256
HGRN_HEADS_PER_STEP = 4
MOE_TILE = 256
NT_DIMS = (((1,), (1,)), ((), ()))
TN_DIMS = (((0,), (0,)), ((), ()))


def _cparams(*sem):
    return pltpu.CompilerParams(dimension_semantics=sem, vmem_limit_bytes=VMEM_LIMIT_BYTES)


def _inproj_kernel(x_ref, g_ref, w_ref, cs_ref, o_ref, hn_ref):
    @pl.when(pl.program_id(1) == 0)
    def _():
        x = x_ref[...]
        ms = jnp.mean(x * x, axis=-1, keepdims=True)
        hn_ref[...] = (x * lax.rsqrt(ms + RMS_EPS) * g_ref[...]).astype(BF16)

    acc = jnp.dot(hn_ref[...], w_ref[...], preferred_element_type=F32)
    o_ref[...] = (acc * cs_ref[...]).astype(o_ref.dtype)


def _inproj(x2, g, w, col0, colscale, out_dtype, tm=1024, tn=1024):
    n, d = x2.shape
    cols = colscale.shape[1]
    j0 = col0 // tn
    return pl.pallas_call(
        _inproj_kernel,
        grid=(n // tm, cols // tn),
        in_specs=[
            pl.BlockSpec((tm, d), lambda i, j: (i, 0)),
            pl.BlockSpec((1, d), lambda i, j: (0, 0)),
            pl.BlockSpec((d, tn), lambda i, j: (0, j0 + j)),
            pl.BlockSpec((1, tn), lambda i, j: (0, j)),
        ],
        out_specs=pl.BlockSpec((tm, tn), lambda i, j: (i, j)),
        out_shape=jax.ShapeDtypeStruct((n, cols), out_dtype),
        scratch_shapes=[pltpu.VMEM((tm, d), BF16)],
        compiler_params=_cparams("parallel", "arbitrary"),
        name="inproj",
    )(x2, g, w, colscale)


MOBA_HEADS_PER_STEP = 4
MOBA_EXTRA_POS = 0
MOBA_EXTRA_SEL = 8
MASK_BIG = 2.0 ** 60
LOG2E = 1.4426950408889634


def _moba_key_extras(t):
    nb = t // MOBA_BLOCK
    assert MOBA_EXTRA_SEL + nb <= LANES and MOBA_BLOCK <= 256
    pos = np.arange(t)
    kx = np.zeros((t, LANES), np.float32)
    kx[:, MOBA_EXTRA_POS:MOBA_EXTRA_POS + 3] = ((pos // MOBA_BLOCK) * MOBA_BLOCK)[:, None]
    kx[:, MOBA_EXTRA_POS + 3:MOBA_EXTRA_POS + 6] = (pos % MOBA_BLOCK)[:, None]
    kx[pos, MOBA_EXTRA_SEL + pos // MOBA_BLOCK] = -MASK_BIG
    return jnp.asarray(kx, dtype=BF16)


def _moba_kernel(sl_ref, q_ref, k_ref, v_ref, kx_ref, o_ref, kaug_ref, kmean_ref, *, nb):
    hg = pl.program_id(1)
    i = pl.program_id(2)
    bs = MOBA_BLOCK
    dh = HEAD_DIM
    nbp = kmean_ref.shape[1] // 4
    nx = 8 + nbp

    @pl.when(i == 0)
    def _():
        for hh in range(MOBA_HEADS_PER_STEP):
            kaug_ref[hh, :, 0:dh] = k_ref[0, :, hh * dh:(hh + 1) * dh]
            kaug_ref[hh, :, dh:] = kx_ref[...]
            rows = [jnp.mean(k_ref[0, j * bs:(j + 1) * bs, hh * dh:(hh + 1) * dh].astype(F32),
                             axis=0, keepdims=True) for j in range(nb)]
            if nbp > nb:
                rows.append(jnp.zeros((nbp - nb, dh), F32))
            km = jnp.concatenate(rows, axis=0)
            hi = km.astype(BF16)
            mid = (km - hi.astype(F32)).astype(BF16)
            lo = (km - hi.astype(F32) - mid.astype(F32)).astype(BF16)
            kmean_ref[hh] = jnp.concatenate([hi, mid, lo, jnp.zeros_like(hi)], axis=0)

    r8 = lax.broadcasted_iota(jnp.int32, (8, bs), 0)
    piece = r8 % 3
    er = lax.broadcasted_iota(jnp.int32, (nx, LANES), 0)
    ec = lax.broadcasted_iota(jnp.int32, (nx, LANES), 1)
    embed = jnp.where(er == ec, 1.0, 0.0).astype(BF16)

    qas = []
    for hh in range(MOBA_HEADS_PER_STEP):
        h = hg * MOBA_HEADS_PER_STEP + hh
        q = q_ref[0, :, hh * dh:(hh + 1) * dh]
        g4 = lax.dot_general(kmean_ref[hh], q, NT_DIMS, preferred_element_type=F32)
        gate_t = g4[0:nbp] + g4[nbp:2 * nbp] + g4[2 * nbp:3 * nbp]
        blk = lax.broadcasted_iota(jnp.int32, gate_t.shape, 0)
        rank = jnp.zeros(gate_t.shape, F32)
        for jp in range(nb - 1):
            other = gate_t[jp:jp + 1, :]
            beats = (other > gate_t) | ((other == gate_t) & (blk > jp))
            rank = rank + jnp.where(beats, (jp < i).astype(F32), 0.0)
        notsel_t = jnp.where((blk < i) & (rank >= MOBA_TOPK), 1.0, 0.0)
        slope_rows = jnp.where(r8 >= 6, 0.0,
                               jnp.where(piece == 0, sl_ref[0, h],
                                         jnp.where(piece == 1, sl_ref[1, h], sl_ref[2, h])))
        qx_t = jnp.concatenate([slope_rows, notsel_t], axis=0).astype(BF16)
        qx = lax.dot_general(qx_t, embed, TN_DIMS, preferred_element_type=F32)
        qas.append(jnp.concatenate([q, qx.astype(BF16)], axis=1))

    rowi = lax.broadcasted_iota(jnp.int32, (bs, bs), 0)
    coli = lax.broadcasted_iota(jnp.int32, (bs, bs), 1)

    for c in range(nb):
        @pl.when(i == c)
        def _(c=c):
            n = (c + 1) * bs
            for hh in range(MOBA_HEADS_PER_STEP):
                cs = slice(hh * dh, (hh + 1) * dh)
                s = lax.dot_general(qas[hh], kaug_ref[hh, 0:n, :], NT_DIMS,
                                    preferred_element_type=F32)
                s_own = jnp.where(rowi >= coli, s[:, c * bs:], -jnp.inf)
                m = jnp.max(s_own, axis=1, keepdims=True)
                if c > 0:
                    s_past = s[:, :c * bs]
                    m = jnp.maximum(m, jnp.max(s_past, axis=1, keepdims=True))
                p_own = jnp.exp2(s_own - m)
                l = jnp.sum(p_own, axis=1, keepdims=True)
                acc = jnp.dot(p_own.astype(BF16), v_ref[0, c * bs:n, cs],
                              preferred_element_type=F32)
                if c > 0:
                    p_past = jnp.exp2(s_past - m)
                    l = l + jnp.sum(p_past, axis=1, keepdims=True)
                    acc = acc + jnp.dot(p_past.astype(BF16), v_ref[0, 0:c * bs, cs],
                                        preferred_element_type=F32)
                o_ref[0, :, cs] = (acc / l).astype(o_ref.dtype)


def _moba(qkv, slope_pieces, n_heads):
    b, t, _ = qkv.shape
    bs = MOBA_BLOCK
    nb = t // bs
    nbp = -(-nb // 8) * 8
    hps = MOBA_HEADS_PER_STEP
    assert n_heads % hps == 0
    ng = n_heads // hps
    w = hps * HEAD_DIM
    kern = functools.partial(_moba_kernel, nb=nb)
    return pl.pallas_call(
        kern,
        grid=(b, ng, nb),
        in_specs=[
            pl.BlockSpec(memory_space=pltpu.SMEM),
            pl.BlockSpec((1, bs, w), lambda bi, hi, qi: (bi, qi, hi)),
            pl.BlockSpec((1, t, w), lambda bi, hi, qi: (bi, 0, ng + hi)),
            pl.BlockSpec((1, t, w), lambda bi, hi, qi: (bi, 0, 2 * ng + hi)),
            pl.BlockSpec((t, LANES), lambda bi, hi, qi: (0, 0)),
        ],
        out_specs=pl.BlockSpec((1, bs, w), lambda bi, hi, qi: (bi, qi, hi)),
        out_shape=jax.ShapeDtypeStruct((b, t, n_heads * HEAD_DIM), BF16),
        scratch_shapes=[
            pltpu.VMEM((hps, t, HEAD_DIM + LANES), BF16),
            pltpu.VMEM((hps, 4 * nbp, HEAD_DIM), BF16),
        ],
        compiler_params=_cparams("parallel", "parallel", "arbitrary"),
        name="moba",
    )(slope_pieces, qkv, qkv, qkv, _moba_key_extras(t))


def _sigmoid(x):
    return 1.0 / (1.0 + jnp.exp(-x))


def _hgrn_chunk(q, fl, iv, g, lb, gn, st):
    c = HGRN_CHUNK
    sub = HGRN_SUB
    half = sub // 2
    nsub = c // sub
    qf = q * _sigmoid(q)
    f = lb + (1.0 - lb) * _sigmoid(fl)
    kf = jnp.maximum(1.0 - f, 0.0)
    r = lax.broadcasted_iota(jnp.int32, (c, c), 0)
    cc = lax.broadcasted_iota(jnp.int32, (c, c), 1)
    tril = jnp.where(r >= cc, 1.0, 0.0).astype(F32)
    b2 = jnp.dot(tril, jnp.log2(f), precision=HIGHEST, preferred_element_type=F32)
    c2 = b2 - jnp.log2(kf)

    inter = lax.dot_general((qf * jnp.exp2(b2)).astype(BF16), st.astype(BF16), NT_DIMS,
                            preferred_element_type=F32)

    lane = lax.broadcasted_iota(jnp.int32, (sub, LANES), 1)
    tsub = lax.broadcasted_iota(jnp.int32, (sub, LANES), 0)
    colid = lax.broadcasted_iota(jnp.int32, (sub, c), 1)
    ones = jnp.ones((LANES, LANES), BF16)
    a_rows = []
    for bi in range(nsub):
        lo = bi * sub
        b_i = b2[lo:lo + sub]
        c_i = c2[lo:lo + sub]
        q_i = qf[lo:lo + sub]
        pieces = []
        for s in range(sub):
            if s < half:
                pieces.append(q_i * jnp.exp2(b_i - c_i[s:s + 1, :]))
            else:
                pieces.append(q_i[half:] * jnp.exp2(b_i[half:] - c_i[s:s + 1, :]))
        pm = jnp.concatenate(pieces, axis=0).astype(BF16)
        rs = jnp.dot(pm, ones, preferred_element_type=F32)
        rel = lane - lo
        key = jnp.where((rel >= 0) & (rel <= tsub), rel, -1)
        key_lo, key_hi = key[:half], key[half:]
        a_lo = jnp.zeros((half, LANES), F32)
        a_hi = jnp.zeros((half, LANES), F32)
        off = 0
        for s in range(sub):
            if s < half:
                a_lo = jnp.where(key_lo == s, rs[off:off + half], a_lo)
                a_hi = jnp.where(key_hi == s, rs[off + half:off + sub], a_hi)
                off += sub
            else:
                a_hi = jnp.where(key_hi == s, rs[off:off + half], a_hi)
                off += half
        a_blk = jnp.concatenate([a_lo, a_hi], axis=0)[:, :c]
        if bi > 0:
            b0 = b2[lo - 1:lo, :]
            qt = q_i * jnp.exp2(b_i - b0)
            kt = jnp.exp2(jnp.minimum(b0 - c2, 0.0))
            cross = lax.dot_general(qt.astype(BF16), kt.astype(BF16), NT_DIMS,
                                    preferred_element_type=F32)
            a_blk = a_blk + jnp.where(colid < lo, cross, 0.0)
        a_rows.append(a_blk)
    a = jnp.concatenate(a_rows, axis=0)
    intra = jnp.dot(a.astype(BF16), iv.astype(BF16), preferred_element_type=F32)

    b_last = b2[c - 1:c, :]
    khat = jnp.exp2(b_last - c2)
    st_new = st * jnp.exp2(b_last) + lax.dot_general(
        iv.astype(BF16), khat.astype(BF16), TN_DIMS, preferred_element_type=F32)

    o = inter + intra
    y = o * lax.rsqrt(jnp.mean(o * o, axis=-1, keepdims=True) + RMS_EPS) * gn
    return y * (g * _sigmoid(g)), st_new


def _hgrn_kernel(q_ref, f_ref, i_ref, g_ref, lb_ref, gn_ref, o_ref, st_ref):
    @pl.when(pl.program_id(2) == 0)
    def _():
        st_ref[...] = jnp.zeros_like(st_ref)

    gn = gn_ref[...]
    d = HEAD_DIM
    for hh in range(HGRN_HEADS_PER_STEP):
        cs = slice(hh * d, (hh + 1) * d)
        lb = lb_ref[:, cs]
        st = st_ref[hh]
        for ci in range(HGRN_STEP // HGRN_CHUNK):
            sl = slice(ci * HGRN_CHUNK, (ci + 1) * HGRN_CHUNK)
            out, st = _hgrn_chunk(q_ref[0, sl, cs], f_ref[0, sl, cs], i_ref[0, sl, cs],
                                  g_ref[0, sl, cs], lb, gn, st)
            o_ref[0, sl, cs] = out.astype(o_ref.dtype)
        st_ref[hh] = st


def _hgrn(hp, lb, gn, n_heads):
    b, t, _ = hp.shape
    d = HEAD_DIM
    ts = HGRN_STEP
    hps = HGRN_HEADS_PER_STEP
    assert n_heads % hps == 0
    ng = n_heads // hps

    def col(group):
        return pl.BlockSpec((1, ts, hps * d), lambda bi, hi, ti: (bi, ti, group * ng + hi))

    return pl.pallas_call(
        _hgrn_kernel,
        grid=(b, ng, t // ts),
        in_specs=[col(0), col(1), col(2), col(3),
                  pl.BlockSpec((1, hps * d), lambda bi, hi, ti: (0, hi)),
                  pl.BlockSpec((1, d), lambda bi, hi, ti: (0, 0))],
        out_specs=pl.BlockSpec((1, ts, hps * d), lambda bi, hi, ti: (bi, ti, hi)),
        out_shape=jax.ShapeDtypeStruct((b, t, n_heads * d), BF16),
        scratch_shapes=[pltpu.VMEM((hps, d, d), F32)],
        compiler_params=_cparams("parallel", "parallel", "arbitrary"),
        name="hgrn2",
    )(hp, hp, hp, hp, lb, gn)


def _outproj_kernel(oa_ref, or_ref, x_ref, w_ref, g_ref, wr_ref, br_ref,
                    x1_ref, h2_ref, rt_ref, cnt_out_ref, cnt_ref):
    wa = oa_ref.shape[1]
    x1 = (x_ref[...]
          + jnp.dot(oa_ref[...], w_ref[0:wa, :], preferred_element_type=F32)
          + jnp.dot(or_ref[...], w_ref[wa:, :], preferred_element_type=F32))
    x1_ref[...] = x1
    h2 = x1 * lax.rsqrt(jnp.mean(x1 * x1, axis=-1, keepdims=True) + RMS_EPS) * g_ref[...]
    h2_ref[...] = h2

    logits = jnp.dot(h2, wr_ref[...], precision=HIGHEST, preferred_element_type=F32) + br_ref[...]
    lane = lax.broadcasted_iota(jnp.int32, logits.shape, 1)
    big = jnp.int32(4 * LANES)
    ninf = -jnp.inf

    lg = jnp.where(lane < N_GROUPS, logits, ninf)
    mg = jnp.max(lg, axis=1, keepdims=True)
    gidx = jnp.min(jnp.where(lg == mg, lane, big), axis=1, keepdims=True)
    grp_w = 1.0 / jnp.sum(jnp.exp(lg - mg), axis=1, keepdims=True)

    lo = N_GROUPS + EXPERTS_PER_GROUP * gidx
    le = jnp.where((lane >= lo) & (lane < lo + EXPERTS_PER_GROUP), logits, ninf)
    m1 = jnp.max(le, axis=1, keepdims=True)
    i1 = jnp.min(jnp.where(le == m1, lane, big), axis=1, keepdims=True)
    le2 = jnp.where(lane == i1, ninf, le)
    m2 = jnp.max(le2, axis=1, keepdims=True)
    i2 = jnp.min(jnp.where(le2 == m2, lane, big), axis=1, keepdims=True)
    r21 = jnp.exp(m2 - m1)
    w1 = grp_w / (1.0 + r21)
    w2 = grp_w * r21 / (1.0 + r21)
    e1 = i1 - N_GROUPS
    e2 = i2 - N_GROUPS

    @pl.when(pl.program_id(0) == 0)
    def _():
        cnt_ref[...] = jnp.zeros_like(cnt_ref)

    tm = logits.shape[0]
    onehot = jnp.where((lane == e1) | (lane == e2), 1.0, 0.0)
    rr = lax.broadcasted_iota(jnp.int32, (tm, tm), 0)
    rc = lax.broadcasted_iota(jnp.int32, (tm, tm), 1)
    before = jnp.where(rr > rc, 1.0, 0.0).astype(BF16)
    prefix = (jnp.dot(before, onehot.astype(BF16), preferred_element_type=F32) + cnt_ref[0:1, :])
    rank1 = jnp.sum(jnp.where(lane == e1, prefix, 0.0), axis=1, keepdims=True)
    rank2 = jnp.sum(jnp.where(lane == e2, prefix, 0.0), axis=1, keepdims=True)
    total = cnt_ref[0:1, :] + jnp.sum(onehot, axis=0, keepdims=True)
    cnt_ref[0:1, :] = total
    cnt_out_ref[...] = jnp.broadcast_to(total, cnt_out_ref.shape)

    cols = [e1.astype(F32), e2.astype(F32), w1, w2, rank1, rank2]
    rt = jnp.zeros(logits.shape, F32)
    for ci, cv in enumerate(cols):
        rt = jnp.where(lane == ci, cv, rt)
    rt_ref[...] = rt


def _outproj(oa, orec, x2, w_out, g, wr, br, tm=256):
    n, d = x2.shape
    wa = oa.shape[1]
    wrc = orec.shape[1]
    row = lambda i: (i, 0)
    const = lambda i: (0, 0)
    return pl.pallas_call(
        _outproj_kernel,
        grid=(n // tm,),
        in_specs=[
            pl.BlockSpec((tm, wa), row),
            pl.BlockSpec((tm, wrc), row),
            pl.BlockSpec((tm, d), row),
            pl.BlockSpec((wa + wrc, d), const),
            pl.BlockSpec((1, d), const),
            pl.BlockSpec((d, LANES), const),
            pl.BlockSpec((1, LANES), const),
        ],
        out_specs=[pl.BlockSpec((tm, d), row), pl.BlockSpec((tm, d), row),
                   pl.BlockSpec((tm, LANES), row), pl.BlockSpec((8, LANES), const)],
        out_shape=[jax.ShapeDtypeStruct((n, d), F32), jax.ShapeDtypeStruct((n, d), F32),
                   jax.ShapeDtypeStruct((n, LANES), F32), jax.ShapeDtypeStruct((8, LANES), F32)],
        scratch_shapes=[pltpu.VMEM((8, LANES), F32)],
        compiler_params=_cparams("arbitrary"),
        name="outproj_route",
    )(oa, orec, x2, w_out, g, wr, br)


def _moe_kernel(te_ref, nu_ref, x_ref, wg_ref, wu_ref, wd_ref, y_ref):
    n = pl.program_id(0)

    @pl.when(n < nu_ref[0])
    def _():
        x = x_ref[...].astype(BF16)
        a = jnp.dot(x, wg_ref[0], preferred_element_type=F32)
        u = jnp.dot(x, wu_ref[0], preferred_element_type=F32)
        hid = (a * _sigmoid(a) * u).astype(BF16)
        y_ref[...] = jnp.dot(hid, wd_ref[0], preferred_element_type=F32)

    @pl.when(n >= nu_ref[0])
    def _():
        y_ref[...] = jnp.zeros_like(y_ref)


def _moe(tile_expert, n_used, xs, wg, wu, wd):
    p = xs.shape[0]
    d, f = wg.shape[1:]
    tm = MOE_TILE
    grid_spec = pltpu.PrefetchScalarGridSpec(
        num_scalar_prefetch=2,
        grid=(p // tm,),
        in_specs=[
            pl.BlockSpec((tm, d), lambda n, te, nu: (n, 0)),
            pl.BlockSpec((1, d, f), lambda n, te, nu: (te[n], 0, 0)),
            pl.BlockSpec((1, d, f), lambda n, te, nu: (te[n], 0, 0)),
            pl.BlockSpec((1, f, d), lambda n, te, nu: (te[n], 0, 0)),
        ],
        out_specs=pl.BlockSpec((tm, d), lambda n, te, nu: (n, 0)),
    )
    return pl.pallas_call(
        _moe_kernel,
        grid_spec=grid_spec,
        out_shape=jax.ShapeDtypeStruct((p, d), F32),
        compiler_params=_cparams("arbitrary"),
        name="moe_ffn",
    )(tile_expert, n_used, xs, wg, wu, wd)


def _final_kernel(x1_ref, y1_ref, y2_ref, rt_ref, g_ref, o_ref):
    rt = rt_ref[...]
    x2 = x1_ref[...] + rt[:, 2:3] * y1_ref[...] + rt[:, 3:4] * y2_ref[...]
    o_ref[...] = x2 * lax.rsqrt(jnp.mean(x2 * x2, axis=-1, keepdims=True) + RMS_EPS) * g_ref[...]


def _final(x1, y1, y2, rt, g, tm=256):
    n, d = x1.shape
    row = lambda i: (i, 0)
    return pl.pallas_call(
        _final_kernel,
        grid=(n // tm,),
        in_specs=[pl.BlockSpec((tm, d), row), pl.BlockSpec((tm, d), row), pl.BlockSpec((tm, d), row),
                  pl.BlockSpec((tm, LANES), row), pl.BlockSpec((1, d), lambda i: (0, 0))],
        out_specs=pl.BlockSpec((tm, d), row),
        out_shape=jax.ShapeDtypeStruct((n, d), F32),
        compiler_params=_cparams("parallel"),
        name="combine_norm",
    )(x1, y1, y2, rt, g)


def _dispatch_plan(rt, counts, n_tokens):
    tm = MOE_TILE
    n_rows = 2 * n_tokens + N_EXPERTS * tm
    n_tiles = n_rows // tm
    ids = rt[:, 0:6].astype(jnp.int32)
    ef = jnp.concatenate([ids[:, 0], ids[:, 1]])
    rank = jnp.concatenate([ids[:, 4], ids[:, 5]])
    counts = counts[0, :N_EXPERTS].astype(jnp.int32)
    tiles_per = (counts + tm - 1) // tm
    tile_end = jnp.cumsum(tiles_per)
    row_start = (tile_end - tiles_per) * tm
    pos = row_start[ef] + rank
    tok = jnp.concatenate([jnp.arange(n_tokens, dtype=jnp.int32)] * 2)
    src = jnp.zeros((n_rows,), jnp.int32).at[pos].set(tok)
    n_used = tile_end[-1]
    tile_ids = jnp.arange(n_tiles, dtype=jnp.int32)
    tile_expert = jnp.sum((tile_ids[:, None] >= tile_end[None, :]).astype(jnp.int32), axis=1)
    last_expert = jnp.sum((n_used - 1 >= tile_end).astype(jnp.int32))
    tile_expert = jnp.where(tile_ids < n_used, tile_expert, last_expert).astype(jnp.int32)
    return src, pos[:n_tokens], pos[n_tokens:], tile_expert, n_used.reshape(1).astype(jnp.int32)


def kernel(x, norm_mix_g, w_in, hgrn_lb_logits, hgrn_out_norm_g, w_out, norm_ffn_g, w_group_router,
           b_group_router, w_expert_router, b_expert_router, w_gate, w_up, w_down, final_norm_g):
    b, t, d = x.shape
    n = b * t
    depth = w_in.shape[0]
    assert depth == 1, "the final norm is fused into the combine step of the only layer"
    attn_w = d // 2
    n_heads = attn_w // HEAD_DIM
    x2 = x.reshape(n, d)
    lb_all = jnp.cumsum(jax.nn.softmax(hgrn_lb_logits.astype(F32), axis=0), axis=0)[:depth]
    slope2 = jnp.asarray(2.0 ** (-8.0 * np.arange(1, n_heads + 1) / n_heads), dtype=F32) * F32(LOG2E)
    s_hi = slope2.astype(BF16).astype(F32)
    s_mid = (slope2 - s_hi).astype(BF16).astype(F32)
    s_lo = (slope2 - s_hi - s_mid).astype(BF16).astype(F32)
    slope_pieces = jnp.stack([s_hi, s_mid, s_lo])
    qkv_scale = jnp.concatenate([jnp.full((attn_w,), HEAD_DIM ** -0.5 * LOG2E, F32),
                                 jnp.ones((2 * attn_w,), F32)]).reshape(1, -1)

    for l in range(depth):
        g_mix = norm_mix_g[l].reshape(1, d)
        x2, w_in_l, w_out_l, wg_l, wu_l, wd_l = lax.optimization_barrier(
            (x2, w_in[l].astype(BF16), w_out[l].astype(BF16), w_gate[l].astype(BF16),
             w_up[l].astype(BF16), w_down[l].astype(BF16)))
        qkv = _inproj(x2, g_mix, w_in_l, 0, qkv_scale, BF16)
        hp = _inproj(x2, g_mix, w_in_l, 3 * attn_w, jnp.ones((1, 4 * (d - attn_w)), F32), F32)
        o_attn = _moba(qkv.reshape(b, t, 3 * attn_w), slope_pieces, n_heads)
        o_rec = _hgrn(hp.reshape(b, t, -1), lb_all[l].reshape(1, -1),
                      hgrn_out_norm_g[l].reshape(1, HEAD_DIM), n_heads)

        wr = jnp.concatenate(
            [w_group_router[l],
             jnp.transpose(w_expert_router[l], (1, 0, 2)).reshape(d, N_EXPERTS)], axis=1)
        wr = jnp.pad(wr, ((0, 0), (0, LANES - wr.shape[1])))
        br = jnp.concatenate([b_group_router[l], b_expert_router[l].reshape(-1)])
        br = jnp.pad(br, (0, LANES - br.shape[0])).reshape(1, LANES)
        x1, h2, rt, counts = _outproj(o_attn.reshape(n, attn_w), o_rec.reshape(n, -1), x2,
                                      w_out_l, norm_ffn_g[l].reshape(1, d), wr, br)
        src, pos1, pos2, tile_expert, n_used = _dispatch_plan(rt, counts, n)
        xs = h2.at[src].get(mode="promise_in_bounds")
        ys = _moe(tile_expert, n_used, xs, wg_l, wu_l, wd_l)
        y1 = ys.at[pos1].get(mode="promise_in_bounds")
        y2 = ys.at[pos2].get(mode="promise_in_bounds")
        x2 = _final(x1, y1, y2, rt, final_norm_g.reshape(1, d))
    return x2.reshape(b, t, d)
```

```python
import functools

import jax
import jax.numpy as jnp
import numpy as np
from jax import lax
from jax.experimental import pallas as pl
from jax.experimental.pallas import tpu as pltpu

F32 = jnp.float32
BF16 = jnp.bfloat16
HIGHEST = lax.Precision.HIGHEST

HEAD_DIM = 128
MOBA_BLOCK = 256
MOBA_TOPK = 3
N_GROUPS = 4
EXPERTS_PER_GROUP = 4
N_EXPERTS = N_GROUPS * EXPERTS_PER_GROUP
RMS_EPS = 1e-6

LANES = 128
VMEM_LIMIT_BYTES = 56 * 1024 * 1024

HGRN_CHUNK = 64
HGRN_SUB = 16
HGRN_STEP = 256
HGRN_HEADS_PER_STEP = 4
MOE_TILE = 256
NT_DIMS = (((1,), (1,)), ((), ()))
TN_DIMS = (((0,), (0,)), ((), ()))


def _cparams(*sem):
    return pltpu.CompilerParams(dimension_semantics=sem, vmem_limit_bytes=VMEM_LIMIT_BYTES)


def _inproj_kernel(x_ref, g_ref, w_ref, cs_ref, o_ref, hn_ref):
    @pl.when(pl.program_id(1) == 0)
    def _():
        x = x_ref[...]
        ms = jnp.mean(x * x, axis=-1, keepdims=True)
        hn_ref[...] = (x * lax.rsqrt(ms + RMS_EPS) * g_ref[...]).astype(BF16)

    acc = jnp.dot(hn_ref[...], w_ref[...], preferred_element_type=F32)
    o_ref[...] = (acc * cs_ref[...]).astype(o_ref.dtype)


def _inproj(x2, g, w, col0, colscale, out_dtype, tm=1024, tn=1024):
    n, d = x2.shape
    cols = colscale.shape[1]
    j0 = col0 // tn
    return pl.pallas_call(
        _inproj_kernel,
        grid=(n // tm, cols // tn),
        in_specs=[
            pl.BlockSpec((tm, d), lambda i, j: (i, 0)),
            pl.BlockSpec((1, d), lambda i, j: (0, 0)),
            pl.BlockSpec((d, tn), lambda i, j: (0, j0 + j)),
            pl.BlockSpec((1, tn), lambda i, j: (0, j)),
        ],
        out_specs=pl.BlockSpec((tm, tn), lambda i, j: (i, j)),
        out_shape=jax.ShapeDtypeStruct((n, cols), out_dtype),
        scratch_shapes=[pltpu.VMEM((tm, d), BF16)],
        compiler_params=_cparams("parallel", "arbitrary"),
        name="inproj",
    )(x2, g, w, colscale)


MOBA_HEADS_PER_STEP = 4
MOBA_EXTRA_POS = 0
MOBA_EXTRA_SEL = 8
MASK_BIG = 2.0 ** 60
LOG2E = 1.4426950408889634


def _moba_key_extras(t):
    nb = t // MOBA_BLOCK
    assert MOBA_EXTRA_SEL + nb <= LANES and MOBA_BLOCK <= 256
    pos = np.arange(t)
    kx = np.zeros((t, LANES), np.float32)
    kx[:, MOBA_EXTRA_POS:MOBA_EXTRA_POS + 3] = ((pos // MOBA_BLOCK) * MOBA_BLOCK)[:, None]
    kx[:, MOBA_EXTRA_POS + 3:MOBA_EXTRA_POS + 6] = (pos % MOBA_BLOCK)[:, None]
    kx[pos, MOBA_EXTRA_SEL + pos // MOBA_BLOCK] = -MASK_BIG
    return jnp.asarray(kx, dtype=BF16)


def _moba_kernel(sl_ref, q_ref, k_ref, v_ref, kx_ref, o_ref, kaug_ref, kmean_ref, *, nb):
    hg = pl.program_id(1)
    i = pl.program_id(2)
    bs = MOBA_BLOCK
    dh = HEAD_DIM
    nbp = kmean_ref.shape[1] // 4
    nx = 8 + nbp

    @pl.when(i == 0)
    def _():
        for hh in range(MOBA_HEADS_PER_STEP):
            kaug_ref[hh, :, 0:dh] = k_ref[0, :, hh * dh:(hh + 1) * dh]
            kaug_ref[hh, :, dh:] = kx_ref[...]
            rows = [jnp.mean(k_ref[0, j * bs:(j + 1) * bs, hh * dh:(hh + 1) * dh].astype(F32),
                             axis=0, keepdims=True) for j in range(nb)]
            if nbp > nb:
                rows.append(jnp.zeros((nbp - nb, dh), F32))
            km = jnp.concatenate(rows, axis=0)
            hi = km.astype(BF16)
            mid = (km - hi.astype(F32)).astype(BF16)
            lo = (km - hi.astype(F32) - mid.astype(F32)).astype(BF16)
            kmean_ref[hh] = jnp.concatenate([hi, mid, lo, jnp.zeros_like(hi)], axis=0)

    r8 = lax.broadcasted_iota(jnp.int32, (8, bs), 0)
    piece = r8 % 3
    er = lax.broadcasted_iota(jnp.int32, (nx, LANES), 0)
    ec = lax.broadcasted_iota(jnp.int32, (nx, LANES), 1)
    embed = jnp.where(er == ec, 1.0, 0.0).astype(BF16)

    qas = []
    for hh in range(MOBA_HEADS_PER_STEP):
        h = hg * MOBA_HEADS_PER_STEP + hh
        q = q_ref[0, :, hh * dh:(hh + 1) * dh]
        g4 = lax.dot_general(kmean_ref[hh], q, NT_DIMS, preferred_element_type=F32)
        gate_t = g4[0:nbp] + g4[nbp:2 * nbp] + g4[2 * nbp:3 * nbp]
        blk = lax.broadcasted_iota(jnp.int32, gate_t.shape, 0)
        rank = jnp.zeros(gate_t.shape, F32)
        for jp in range(nb - 1):
            other = gate_t[jp:jp + 1, :]
            beats = (other > gate_t) | ((other == gate_t) & (blk > jp))
            rank = rank + jnp.where(beats, (jp < i).astype(F32), 0.0)
        notsel_t = jnp.where((blk < i) & (rank >= MOBA_TOPK), 1.0, 0.0)
        slope_rows = jnp.where(r8 >= 6, 0.0,
                               jnp.where(piece == 0, sl_ref[0, h],
                                         jnp.where(piece == 1, sl_ref[1, h], sl_ref[2, h])))
        qx_t = jnp.concatenate([slope_rows, notsel_t], axis=0).astype(BF16)
        qx = lax.dot_general(qx_t, embed, TN_DIMS, preferred_element_type=F32)
        qas.append(jnp.concatenate([q, qx.astype(BF16)], axis=1))

    rowi = lax.broadcasted_iota(jnp.int32, (bs, bs), 0)
    coli = lax.broadcasted_iota(jnp.int32, (bs, bs), 1)

    for c in range(nb):
        @pl.when(i == c)
        def _(c=c):
            n = (c + 1) * bs
            for hh in range(MOBA_HEADS_PER_STEP):
                cs = slice(hh * dh, (hh + 1) * dh)
                s = lax.dot_general(qas[hh], kaug_ref[hh, 0:n, :], NT_DIMS,
                                    preferred_element_type=F32)
                s_own = jnp.where(rowi >= coli, s[:, c * bs:], -jnp.inf)
                m = jnp.max(s_own, axis=1, keepdims=True)
                if c > 0:
                    s_past = s[:, :c * bs]
                    m = jnp.maximum(m, jnp.max(s_past, axis=1, keepdims=True))
                p_own = jnp.exp2(s_own - m)
                l = jnp.sum(p_own, axis=1, keepdims=True)
                acc = jnp.dot(p_own.astype(BF16), v_ref[0, c * bs:n, cs],
                              preferred_element_type=F32)
                if c > 0:
                    p_past = jnp.exp2(s_past - m)
                    l = l + jnp.sum(p_past, axis=1, keepdims=True)
                    acc = acc + jnp.dot(p_past.astype(BF16), v_ref[0, 0:c * bs, cs],
                                        preferred_element_type=F32)
                o_ref[0, :, cs] = (acc / l).astype(o_ref.dtype)


def _moba(qkv, slope_pieces, n_heads):
    b, t, _ = qkv.shape
    bs = MOBA_BLOCK
    nb = t // bs
    nbp = -(-nb // 8) * 8
    hps = MOBA_HEADS_PER_STEP
    assert n_heads % hps == 0
    ng = n_heads // hps
    w = hps * HEAD_DIM
    kern = functools.partial(_moba_kernel, nb=nb)
    return pl.pallas_call(
        kern,
        grid=(b, ng, nb),
        in_specs=[
            pl.BlockSpec(memory_space=pltpu.SMEM),
            pl.BlockSpec((1, bs, w), lambda bi, hi, qi: (bi, qi, hi)),
            pl.BlockSpec((1, t, w), lambda bi, hi, qi: (bi, 0, ng + hi)),
            pl.BlockSpec((1, t, w), lambda bi, hi, qi: (bi, 0, 2 * ng + hi)),
            pl.BlockSpec((t, LANES), lambda bi, hi, qi: (0, 0)),
        ],
        out_specs=pl.BlockSpec((1, bs, w), lambda bi, hi, qi: (bi, qi, hi)),
        out_shape=jax.ShapeDtypeStruct((b, t, n_heads * HEAD_DIM), BF16),
        scratch_shapes=[
            pltpu.VMEM((hps, t, HEAD_DIM + LANES), BF16),
            pltpu.VMEM((hps, 4 * nbp, HEAD_DIM), BF16),
        ],
        compiler_params=_cparams("parallel", "parallel", "arbitrary"),
        name="moba",
    )(slope_pieces, qkv, qkv, qkv, _moba_key_extras(t))


def _sigmoid(x):
    return 1.0 / (1.0 + jnp.exp(-x))


def _hgrn_chunk(q, fl, iv, g, lb, gn, st):
    c = HGRN_CHUNK
    sub = HGRN_SUB
    half = sub // 2
    nsub = c // sub
    qf = q * _sigmoid(q)
    f = lb + (1.0 - lb) * _sigmoid(fl)
    kf = jnp.maximum(1.0 - f, 0.0)
    r = lax.broadcasted_iota(jnp.int32, (c, c), 0)
    cc = lax.broadcasted_iota(jnp.int32, (c, c), 1)
    tril = jnp.where(r >= cc, 1.0, 0.0).astype(F32)
    b2 = jnp.dot(tril, jnp.log2(f), precision=HIGHEST, preferred_element_type=F32)
    c2 = b2 - jnp.log2(kf)

    inter = lax.dot_general((qf * jnp.exp2(b2)).astype(BF16), st.astype(BF16), NT_DIMS,
                            preferred_element_type=F32)

    lane = lax.broadcasted_iota(jnp.int32, (sub, LANES), 1)
    tsub = lax.broadcasted_iota(jnp.int32, (sub, LANES), 0)
    colid = lax.broadcasted_iota(jnp.int32, (sub, c), 1)
    ones = jnp.ones((LANES, LANES), BF16)
    a_rows = []
    for bi in range(nsub):
        lo = bi * sub
        b_i = b2[lo:lo + sub]
        c_i = c2[lo:lo + sub]
        q_i = qf[lo:lo + sub]
        pieces = []
        for s in range(sub):
            if s < half:
                pieces.append(q_i * jnp.exp2(b_i - c_i[s:s + 1, :]))
            else:
                pieces.append(q_i[half:] * jnp.exp2(b_i[half:] - c_i[s:s + 1, :]))
        pm = jnp.concatenate(pieces, axis=0).astype(BF16)
        rs = jnp.dot(pm, ones, preferred_element_type=F32)
        rel = lane - lo
        key = jnp.where((rel >= 0) & (rel <= tsub), rel, -1)
        key_lo, key_hi = key[:half], key[half:]
        a_lo = jnp.zeros((half, LANES), F32)
        a_hi = jnp.zeros((half, LANES), F32)
        off = 0
        for s in range(sub):
            if s < half:
                a_lo = jnp.where(key_lo == s, rs[off:off + half], a_lo)
                a_hi = jnp.where(key_hi == s, rs[off + half:off + sub], a_hi)
                off += sub
            else:
                a_hi = jnp.where(key_hi == s, rs[off:off + half], a_hi)
                off += half
        a_blk = jnp.concatenate([a_lo, a_hi], axis=0)[:, :c]
        if bi > 0:
            b0 = b2[lo - 1:lo, :]
            qt = q_i * jnp.exp2(b_i - b0)
            kt = jnp.exp2(jnp.minimum(b0 - c2, 0.0))
            cross = lax.dot_general(qt.astype(BF16), kt.astype(BF16), NT_DIMS,
                                    preferred_element_type=F32)
            a_blk = a_blk + jnp.where(colid < lo, cross, 0.0)
        a_rows.append(a_blk)
    a = jnp.concatenate(a_rows, axis=0)
    intra = jnp.dot(a.astype(BF16), iv.astype(BF16), preferred_element_type=F32)

    b_last = b2[c - 1:c, :]
    khat = jnp.exp2(b_last - c2)
    st_new = st * jnp.exp2(b_last) + lax.dot_general(
        iv.astype(BF16), khat.astype(BF16), TN_DIMS, preferred_element_type=F32)

    o = inter + intra
    y = o * lax.rsqrt(jnp.mean(o * o, axis=-1, keepdims=True) + RMS_EPS) * gn
    return y * (g * _sigmoid(g)), st_new


def _hgrn_kernel(q_ref, f_ref, i_ref, g_ref, lb_ref, gn_ref, o_ref, st_ref):
    @pl.when(pl.program_id(2) == 0)
    def _():
        st_ref[...] = jnp.zeros_like(st_ref)

    gn = gn_ref[...]
    d = HEAD_DIM
    for hh in range(HGRN_HEADS_PER_STEP):
        cs = slice(hh * d, (hh + 1) * d)
        lb = lb_ref[:, cs]
        st = st_ref[hh]
        for ci in range(HGRN_STEP // HGRN_CHUNK):
            sl = slice(ci * HGRN_CHUNK, (ci + 1) * HGRN_CHUNK)
            out, st = _hgrn_chunk(q_ref[0, sl, cs], f_ref[0, sl, cs], i_ref[0, sl, cs],
                                  g_ref[0, sl, cs], lb, gn, st)
            o_ref[0, sl, cs] = out.astype(o_ref.dtype)
        st_ref[hh] = st


def _hgrn(hp, lb, gn, n_heads):
    b, t, _ = hp.shape
    d = HEAD_DIM
    ts = HGRN_STEP
    hps = HGRN_HEADS_PER_STEP
    assert n_heads % hps == 0
    ng = n_heads // hps

    def col(group):
        return pl.BlockSpec((1, ts, hps * d), lambda bi, hi, ti: (bi, ti, group * ng + hi))

    return pl.pallas_call(
        _hgrn_kernel,
        grid=(b, ng, t // ts),
        in_specs=[col(0), col(1), col(2), col(3),
                  pl.BlockSpec((1, hps * d), lambda bi, hi, ti: (0, hi)),
                  pl.BlockSpec((1, d), lambda bi, hi, ti: (0, 0))],
        out_specs=pl.BlockSpec((1, ts, hps * d), lambda bi, hi, ti: (bi, ti, hi)),
        out_shape=jax.ShapeDtypeStruct((b, t, n_heads * d), BF16),
        scratch_shapes=[pltpu.VMEM((hps, d, d), F32)],
        compiler_params=_cparams("parallel", "parallel", "arbitrary"),
        name="hgrn2",
    )(hp, hp, hp, hp, lb, gn)


def _outproj_kernel(oa_ref, or_ref, x_ref, w_ref, g_ref, wr_ref, br_ref,
                    x1_ref, h2_ref, rt_ref, cnt_out_ref, cnt_ref):
    wa = oa_ref.shape[1]
    x1 = (x_ref[...]
          + jnp.dot(oa_ref[...], w_ref[0:wa, :], preferred_element_type=F32)
          + jnp.dot(or_ref[...], w_ref[wa:, :], preferred_element_type=F32))
    x1_ref[...] = x1
    h2 = x1 * lax.rsqrt(jnp.mean(x1 * x1, axis=-1, keepdims=True) + RMS_EPS) * g_ref[...]
    h2_ref[...] = h2

    logits = jnp.dot(h2, wr_ref[...], precision=HIGHEST, preferred_element_type=F32) + br_ref[...]
    lane = lax.broadcasted_iota(jnp.int32, logits.shape, 1)
    big = jnp.int32(4 * LANES)
    ninf = -jnp.inf

    lg = jnp.where(lane < N_GROUPS, logits, ninf)
    mg = jnp.max(lg, axis=1, keepdims=True)
    gidx = jnp.min(jnp.where(lg == mg, lane, big), axis=1, keepdims=True)
    grp_w = 1.0 / jnp.sum(jnp.exp(lg - mg), axis=1, keepdims=True)

    lo = N_GROUPS + EXPERTS_PER_GROUP * gidx
    le = jnp.where((lane >= lo) & (lane < lo + EXPERTS_PER_GROUP), logits, ninf)
    m1 = jnp.max(le, axis=1, keepdims=True)
    i1 = jnp.min(jnp.where(le == m1, lane, big), axis=1, keepdims=True)
    le2 = jnp.where(lane == i1, ninf, le)
    m2 = jnp.max(le2, axis=1, keepdims=True)
    i2 = jnp.min(jnp.where(le2 == m2, lane, big), axis=1, keepdims=True)
    r21 = jnp.exp(m2 - m1)
    w1 = grp_w / (1.0 + r21)
    w2 = grp_w * r21 / (1.0 + r21)
    e1 = i1 - N_GROUPS
    e2 = i2 - N_GROUPS

    @pl.when(pl.program_id(0) == 0)
    def _():
        cnt_ref[...] = jnp.zeros_like(cnt_ref)

    tm = logits.shape[0]
    onehot = jnp.where((lane == e1) | (lane == e2), 1.0, 0.0)
    rr = lax.broadcasted_iota(jnp.int32, (tm, tm), 0)
    rc = lax.broadcasted_iota(jnp.int32, (tm, tm), 1)
    before = jnp.where(rr > rc, 1.0, 0.0).astype(BF16)
    prefix = (jnp.dot(before, onehot.astype(BF16), preferred_element_type=F32) + cnt_ref[0:1, :])
    rank1 = jnp.sum(jnp.where(lane == e1, prefix, 0.0), axis=1, keepdims=True)
    rank2 = jnp.sum(jnp.where(lane == e2, prefix, 0.0), axis=1, keepdims=True)
    total = cnt_ref[0:1, :] + jnp.sum(onehot, axis=0, keepdims=True)
    cnt_ref[0:1, :] = total
    cnt_out_ref[...] = jnp.broadcast_to(total, cnt_out_ref.shape)

    cols = [e1.astype(F32), e2.astype(F32), w1, w2, rank1, rank2]
    rt = jnp.zeros(logits.shape, F32)
    for ci, cv in enumerate(cols):
        rt = jnp.where(lane == ci, cv, rt)
    rt_ref[...] = rt


def _outproj(oa, orec, x2, w_out, g, wr, br, tm=256):
    n, d = x2.shape
    wa = oa.shape[1]
    wrc = orec.shape[1]
    row = lambda i: (i, 0)
    const = lambda i: (0, 0)
    return pl.pallas_call(
        _outproj_kernel,
        grid=(n // tm,),
        in_specs=[
            pl.BlockSpec((tm, wa), row),
            pl.BlockSpec((tm, wrc), row),
            pl.BlockSpec((tm, d), row),
            pl.BlockSpec((wa + wrc, d), const),
            pl.BlockSpec((1, d), const),
            pl.BlockSpec((d, LANES), const),
            pl.BlockSpec((1, LANES), const),
        ],
        out_specs=[pl.BlockSpec((tm, d), row), pl.BlockSpec((tm, d), row),
                   pl.BlockSpec((tm, LANES), row), pl.BlockSpec((8, LANES), const)],
        out_shape=[jax.ShapeDtypeStruct((n, d), F32), jax.ShapeDtypeStruct((n, d), F32),
                   jax.ShapeDtypeStruct((n, LANES), F32), jax.ShapeDtypeStruct((8, LANES), F32)],
        scratch_shapes=[pltpu.VMEM((8, LANES), F32)],
        compiler_params=_cparams("arbitrary"),
        name="outproj_route",
    )(oa, orec, x2, w_out, g, wr, br)


def _moe_kernel(te_ref, nu_ref, src_ref, h2_hbm, wg_ref, wu_ref, wd_ref, y_ref, xbuf, sems):
    s = pl.program_id(0)
    nu = nu_ref[0]
    tm = MOE_TILE

    def gather_row(tile, slot, r):
        tok = src_ref[tile * tm + r]
        return pltpu.make_async_copy(h2_hbm.at[pl.ds(tok, 1), :], xbuf.at[slot, pl.ds(r, 1), :],
                                     sems.at[slot])

    def start_gather(tile):
        slot = tile % 2
        for r in range(tm):
            gather_row(tile, slot, r).start()

    def wait_gather(tile):
        slot = tile % 2
        pltpu.make_async_copy(h2_hbm.at[pl.ds(0, tm), :], xbuf.at[slot], sems.at[slot]).wait()

    def ffn(tile):
        x = xbuf[tile % 2].astype(BF16)
        a = jnp.dot(x, wg_ref[0], preferred_element_type=F32)
        u = jnp.dot(x, wu_ref[0], preferred_element_type=F32)
        hid = (a * _sigmoid(a) * u).astype(BF16)
        y_ref[...] = jnp.dot(hid, wd_ref[0], preferred_element_type=F32)

    @pl.when(s == 0)
    def _():
        start_gather(s)

    @pl.when((s >= 1) & (s < nu))
    def _():
        start_gather(s)
        wait_gather(s - 1)
        ffn(s - 1)

    @pl.when((s >= 1) & (s == nu))
    def _():
        wait_gather(s - 1)
        ffn(s - 1)

    @pl.when(s > nu)
    def _():
        y_ref[...] = jnp.zeros_like(y_ref)


def _moe(tile_expert, n_used, src, h2, wg, wu, wd):
    p = src.shape[0]
    d, f = wg.shape[1:]
    tm = MOE_TILE
    n_tiles = p // tm
    prev = lambda s: jnp.maximum(s - 1, 0)
    grid_spec = pltpu.PrefetchScalarGridSpec(
        num_scalar_prefetch=3,
        grid=(n_tiles + 1,),
        in_specs=[
            pl.BlockSpec(memory_space=pl.ANY),
            pl.BlockSpec((1, d, f), lambda s, te, nu, sr: (te[prev(s)], 0, 0)),
            pl.BlockSpec((1, d, f), lambda s, te, nu, sr: (te[prev(s)], 0, 0)),
            pl.BlockSpec((1, f, d), lambda s, te, nu, sr: (te[prev(s)], 0, 0)),
        ],
        out_specs=pl.BlockSpec((tm, d), lambda s, te, nu, sr: (prev(s), 0)),
        scratch_shapes=[pltpu.VMEM((2, tm, d), F32), pltpu.SemaphoreType.DMA((2,))],
    )
    return pl.pallas_call(
        _moe_kernel,
        grid_spec=grid_spec,
        out_shape=jax.ShapeDtypeStruct((p, d), F32),
        compiler_params=_cparams("arbitrary"),
        name="moe_ffn",
    )(tile_expert, n_used, src, h2, wg, wu, wd)


def _final_kernel(x1_ref, y1_ref, y2_ref, rt_ref, g_ref, o_ref):
    rt = rt_ref[...]
    x2 = x1_ref[...] + rt[:, 2:3] * y1_ref[...] + rt[:, 3:4] * y2_ref[...]
    o_ref[...] = x2 * lax.rsqrt(jnp.mean(x2 * x2, axis=-1, keepdims=True) + RMS_EPS) * g_ref[...]


def _final(x1, y1, y2, rt, g, tm=256):
    n, d = x1.shape
    row = lambda i: (i, 0)
    return pl.pallas_call(
        _final_kernel,
        grid=(n // tm,),
        in_specs=[pl.BlockSpec((tm, d), row), pl.BlockSpec((tm, d), row), pl.BlockSpec((tm, d), row),
                  pl.BlockSpec((tm, LANES), row), pl.BlockSpec((1, d), lambda i: (0, 0))],
        out_specs=pl.BlockSpec((tm, d), row),
        out_shape=jax.ShapeDtypeStruct((n, d), F32),
        compiler_params=_cparams("parallel"),
        name="combine_norm",
    )(x1, y1, y2, rt, g)


def _dispatch_plan(rt, counts, n_tokens):
    tm = MOE_TILE
    n_rows = 2 * n_tokens + N_EXPERTS * tm
    n_tiles = n_rows // tm
    ids = rt[:, 0:6].astype(jnp.int32)
    ef = jnp.concatenate([ids[:, 0], ids[:, 1]])
    rank = jnp.concatenate([ids[:, 4], ids[:, 5]])
    counts = counts[0, :N_EXPERTS].astype(jnp.int32)
    tiles_per = (counts + tm - 1) // tm
    tile_end = jnp.cumsum(tiles_per)
    row_start = (tile_end - tiles_per) * tm
    pos = row_start[ef] + rank
    tok = jnp.concatenate([jnp.arange(n_tokens, dtype=jnp.int32)] * 2)
    src = jnp.zeros((n_rows,), jnp.int32).at[pos].set(tok)
    n_used = tile_end[-1]
    tile_ids = jnp.arange(n_tiles, dtype=jnp.int32)
    tile_expert = jnp.sum((tile_ids[:, None] >= tile_end[None, :]).astype(jnp.int32), axis=1)
    last_expert = jnp.sum((n_used - 1 >= tile_end).astype(jnp.int32))
    tile_expert = jnp.where(tile_ids < n_used, tile_expert, last_expert).astype(jnp.int32)
    return src, pos[:n_tokens], pos[n_tokens:], tile_expert, n_used.reshape(1).astype(jnp.int32)


def kernel(x, norm_mix_g, w_in, hgrn_lb_logits, hgrn_out_norm_g, w_out, norm_ffn_g, w_group_router,
           b_group_router, w_expert_router, b_expert_router, w_gate, w_up, w_down, final_norm_g):
    b, t, d = x.shape
    n = b * t
    depth = w_in.shape[0]
    assert depth == 1, "the final norm is fused into the combine step of the only layer"
    attn_w = d // 2
    n_heads = attn_w // HEAD_DIM
    x2 = x.reshape(n, d)
    lb_all = jnp.cumsum(jax.nn.softmax(hgrn_lb_logits.astype(F32), axis=0), axis=0)[:depth]
    slope2 = jnp.asarray(2.0 ** (-8.0 * np.arange(1, n_heads + 1) / n_heads), dtype=F32) * F32(LOG2E)
    s_hi = slope2.astype(BF16).astype(F32)
    s_mid = (slope2 - s_hi).astype(BF16).astype(F32)
    s_lo = (slope2 - s_hi - s_mid).astype(BF16).astype(F32)
    slope_pieces = jnp.stack([s_hi, s_mid, s_lo])
    qkv_scale = jnp.concatenate([jnp.full((attn_w,), HEAD_DIM ** -0.5 * LOG2E, F32),
                                 jnp.ones((2 * attn_w,), F32)]).reshape(1, -1)

    for l in range(depth):
        g_mix = norm_mix_g[l].reshape(1, d)
        x2, w_in_l, w_out_l, wg_l, wu_l, wd_l = lax.optimization_barrier(
            (x2, w_in[l].astype(BF16), w_out[l].astype(BF16), w_gate[l].astype(BF16),
             w_up[l].astype(BF16), w_down[l].astype(BF16)))
        qkv = _inproj(x2, g_mix, w_in_l, 0, qkv_scale, BF16)
        hp = _inproj(x2, g_mix, w_in_l, 3 * attn_w, jnp.ones((1, 4 * (d - attn_w)), F32), F32)
        o_attn = _moba(qkv.reshape(b, t, 3 * attn_w), slope_pieces, n_heads)
        o_rec = _hgrn(hp.reshape(b, t, -1), lb_all[l].reshape(1, -1),
                      hgrn_out_norm_g[l].reshape(1, HEAD_DIM), n_heads)

        wr = jnp.concatenate(
            [w_group_router[l],
             jnp.transpose(w_expert_router[l], (1, 0, 2)).reshape(d, N_EXPERTS)], axis=1)
        wr = jnp.pad(wr, ((0, 0), (0, LANES - wr.shape[1])))
        br = jnp.concatenate([b_group_router[l], b_expert_router[l].reshape(-1)])
        br = jnp.pad(br, (0, LANES - br.shape[0])).reshape(1, LANES)
        x1, h2, rt, counts = _outproj(o_attn.reshape(n, attn_w), o_rec.reshape(n, -1), x2,
                                      w_out_l, norm_ffn_g[l].reshape(1, d), wr, br)
        src, pos1, pos2, tile_expert, n_used = _dispatch_plan(rt, counts, n)
        ys = _moe(tile_expert, n_used, src, h2, wg_l, wu_l, wd_l)
        y1 = ys.at[pos1].get(mode="promise_in_bounds")
        y2 = ys.at[pos2].get(mode="promise_in_bounds")
        x2 = _final(x1, y1, y2, rt, final_norm_g.reshape(1, d))
    return x2.reshape(b, t, d)
```

```python
import functools

import jax
import jax.numpy as jnp
import numpy as np
from jax import lax
from jax.experimental import pallas as pl
from jax.experimental.pallas import tpu as pltpu

F32 = jnp.float32
BF16 = jnp.bfloat16
HIGHEST = lax.Precision.HIGHEST

HEAD_DIM = 128
MOBA_BLOCK = 256
MOBA_TOPK = 3
N_GROUPS = 4
EXPERTS_PER_GROUP = 4
N_EXPERTS = N_GROUPS * EXPERTS_PER_GROUP
RMS_EPS = 1e-6

LANES = 128
VMEM_LIMIT_BYTES = 56 * 1024 * 1024

HGRN_CHUNK = 64
HGRN_SUB = 16
HGRN_STEP = 256
HGRN_HEADS_PER_STEP = 8
HGRN_MIN_SAFE_F = 2.0 ** -7
MOE_TILE = 256
NT_DIMS = (((1,), (1,)), ((), ()))
TN_DIMS = (((0,), (0,)), ((), ()))


def _cparams(*sem):
    return pltpu.CompilerParams(dimension_semantics=sem, vmem_limit_bytes=VMEM_LIMIT_BYTES)


def _inproj_kernel(x_ref, g_ref, w_ref, cs_ref, o_ref, hn_ref):
    @pl.when(pl.program_id(1) == 0)
    def _():
        x = x_ref[...]
        ms = jnp.mean(x * x, axis=-1, keepdims=True)
        hn_ref[...] = (x * lax.rsqrt(ms + RMS_EPS) * g_ref[...]).astype(BF16)

    acc = jnp.dot(hn_ref[...], w_ref[...], preferred_element_type=F32)
    o_ref[...] = (acc * cs_ref[...]).astype(o_ref.dtype)


def _inproj(x2, g, w, col0, colscale, out_dtype, tm=1024, tn=1024):
    n, d = x2.shape
    cols = colscale.shape[1]
    j0 = col0 // tn
    return pl.pallas_call(
        _inproj_kernel,
        grid=(n // tm, cols // tn),
        in_specs=[
            pl.BlockSpec((tm, d), lambda i, j: (i, 0)),
            pl.BlockSpec((1, d), lambda i, j: (0, 0)),
            pl.BlockSpec((d, tn), lambda i, j: (0, j0 + j)),
            pl.BlockSpec((1, tn), lambda i, j: (0, j)),
        ],
        out_specs=pl.BlockSpec((tm, tn), lambda i, j: (i, j)),
        out_shape=jax.ShapeDtypeStruct((n, cols), out_dtype),
        scratch_shapes=[pltpu.VMEM((tm, d), BF16)],
        compiler_params=_cparams("parallel", "arbitrary"),
        name="inproj",
    )(x2, g, w, colscale)


MOBA_HEADS_PER_STEP = 4
MOBA_EXTRA_POS = 0
MOBA_EXTRA_SEL = 8
MASK_BIG = 2.0 ** 60
LOG2E = 1.4426950408889634


def _moba_key_extras(t):
    nb = t // MOBA_BLOCK
    assert MOBA_EXTRA_SEL + nb <= LANES and MOBA_BLOCK <= 256
    pos = np.arange(t)
    kx = np.zeros((t, LANES), np.float32)
    kx[:, MOBA_EXTRA_POS:MOBA_EXTRA_POS + 3] = ((pos // MOBA_BLOCK) * MOBA_BLOCK)[:, None]
    kx[:, MOBA_EXTRA_POS + 3:MOBA_EXTRA_POS + 6] = (pos % MOBA_BLOCK)[:, None]
    kx[pos, MOBA_EXTRA_SEL + pos // MOBA_BLOCK] = -MASK_BIG
    return jnp.asarray(kx, dtype=BF16)


def _moba_kernel(sl_ref, q_ref, k_ref, v_ref, kx_ref, o_ref, kaug_ref, kmean_ref, *, nb):
    hg = pl.program_id(1)
    i = pl.program_id(2)
    bs = MOBA_BLOCK
    dh = HEAD_DIM
    nbp = kmean_ref.shape[1] // 4
    nx = 8 + nbp

    @pl.when(i == 0)
    def _():
        for hh in range(MOBA_HEADS_PER_STEP):
            kaug_ref[hh, :, 0:dh] = k_ref[0, :, hh * dh:(hh + 1) * dh]
            kaug_ref[hh, :, dh:] = kx_ref[...]
            rows = [jnp.mean(k_ref[0, j * bs:(j + 1) * bs, hh * dh:(hh + 1) * dh].astype(F32),
                             axis=0, keepdims=True) for j in range(nb)]
            if nbp > nb:
                rows.append(jnp.zeros((nbp - nb, dh), F32))
            km = jnp.concatenate(rows, axis=0)
            hi = km.astype(BF16)
            mid = (km - hi.astype(F32)).astype(BF16)
            lo = (km - hi.astype(F32) - mid.astype(F32)).astype(BF16)
            kmean_ref[hh] = jnp.concatenate([hi, mid, lo, jnp.zeros_like(hi)], axis=0)

    r8 = lax.broadcasted_iota(jnp.int32, (8, bs), 0)
    piece = r8 % 3
    er = lax.broadcasted_iota(jnp.int32, (nx, LANES), 0)
    ec = lax.broadcasted_iota(jnp.int32, (nx, LANES), 1)
    embed = jnp.where(er == ec, 1.0, 0.0).astype(BF16)

    qas = []
    for hh in range(MOBA_HEADS_PER_STEP):
        h = hg * MOBA_HEADS_PER_STEP + hh
        q = q_ref[0, :, hh * dh:(hh + 1) * dh]
        g4 = lax.dot_general(kmean_ref[hh], q, NT_DIMS, preferred_element_type=F32)
        gate_t = g4[0:nbp] + g4[nbp:2 * nbp] + g4[2 * nbp:3 * nbp]
        blk = lax.broadcasted_iota(jnp.int32, gate_t.shape, 0)
        rank = jnp.zeros(gate_t.shape, F32)
        for jp in range(nb - 1):
            other = gate_t[jp:jp + 1, :]
            beats = (other > gate_t) | ((other == gate_t) & (blk > jp))
            rank = rank + jnp.where(beats, (jp < i).astype(F32), 0.0)
        notsel_t = jnp.where((blk < i) & (rank >= MOBA_TOPK), 1.0, 0.0)
        slope_rows = jnp.where(r8 >= 6, 0.0,
                               jnp.where(piece == 0, sl_ref[0, h],
                                         jnp.where(piece == 1, sl_ref[1, h], sl_ref[2, h])))
        qx_t = jnp.concatenate([slope_rows, notsel_t], axis=0).astype(BF16)
        qx = lax.dot_general(qx_t, embed, TN_DIMS, preferred_element_type=F32)
        qas.append(jnp.concatenate([q, qx.astype(BF16)], axis=1))

    rowi = lax.broadcasted_iota(jnp.int32, (bs, bs), 0)
    coli = lax.broadcasted_iota(jnp.int32, (bs, bs), 1)

    for c in range(nb):
        @pl.when(i == c)
        def _(c=c):
            n = (c + 1) * bs
            for hh in range(MOBA_HEADS_PER_STEP):
                cs = slice(hh * dh, (hh + 1) * dh)
                s = lax.dot_general(qas[hh], kaug_ref[hh, 0:n, :], NT_DIMS,
                                    preferred_element_type=F32)
                s_own = jnp.where(rowi >= coli, s[:, c * bs:], -jnp.inf)
                m = jnp.max(s_own, axis=1, keepdims=True)
                if c > 0:
                    s_past = s[:, :c * bs]
                    m = jnp.maximum(m, jnp.max(s_past, axis=1, keepdims=True))
                p_own = jnp.exp2(s_own - m)
                l = jnp.sum(p_own, axis=1, keepdims=True)
                acc = jnp.dot(p_own.astype(BF16), v_ref[0, c * bs:n, cs],
                              preferred_element_type=F32)
                if c > 0:
                    p_past = jnp.exp2(s_past - m)
                    l = l + jnp.sum(p_past, axis=1, keepdims=True)
                    acc = acc + jnp.dot(p_past.astype(BF16), v_ref[0, 0:c * bs, cs],
                                        preferred_element_type=F32)
                o_ref[0, :, cs] = (acc / l).astype(o_ref.dtype)


def _moba(qkv, slope_pieces, n_heads):
    b, t, _ = qkv.shape
    bs = MOBA_BLOCK
    nb = t // bs
    nbp = -(-nb // 8) * 8
    hps = MOBA_HEADS_PER_STEP
    assert n_heads % hps == 0
    ng = n_heads // hps
    w = hps * HEAD_DIM
    kern = functools.partial(_moba_kernel, nb=nb)
    return pl.pallas_call(
        kern,
        grid=(b, ng, nb),
        in_specs=[
            pl.BlockSpec(memory_space=pltpu.SMEM),
            pl.BlockSpec((1, bs, w), lambda bi, hi, qi: (bi, qi, hi)),
            pl.BlockSpec((1, t, w), lambda bi, hi, qi: (bi, 0, ng + hi)),
            pl.BlockSpec((1, t, w), lambda bi, hi, qi: (bi, 0, 2 * ng + hi)),
            pl.BlockSpec((t, LANES), lambda bi, hi, qi: (0, 0)),
        ],
        out_specs=pl.BlockSpec((1, bs, w), lambda bi, hi, qi: (bi, qi, hi)),
        out_shape=jax.ShapeDtypeStruct((b, t, n_heads * HEAD_DIM), BF16),
        scratch_shapes=[
            pltpu.VMEM((hps, t, HEAD_DIM + LANES), BF16),
            pltpu.VMEM((hps, 4 * nbp, HEAD_DIM), BF16),
        ],
        compiler_params=_cparams("parallel", "parallel", "arbitrary"),
        name="moba",
    )(slope_pieces, qkv, qkv, qkv, _moba_key_extras(t))


def _sigmoid(x):
    return 1.0 / (1.0 + jnp.exp(-x))


def _hgrn_chunk(q, f, iv, g, gn, st, bounded_decay):
    c = HGRN_CHUNK
    sub = HGRN_SUB
    half = sub // 2
    nsub = c // sub
    qf = q * _sigmoid(q)
    kf = jnp.maximum(1.0 - f, 0.0)
    r = lax.broadcasted_iota(jnp.int32, (c, c), 0)
    cc = lax.broadcasted_iota(jnp.int32, (c, c), 1)
    tril = jnp.where(r >= cc, 1.0, 0.0).astype(BF16)
    lf = jnp.log2(f)
    dk = lf.shape[1]
    lf_hi = lf.astype(BF16)
    lf_mid = (lf - lf_hi.astype(F32)).astype(BF16)
    lf_lo = (lf - lf_hi.astype(F32) - lf_mid.astype(F32)).astype(BF16)
    b3 = jnp.dot(tril, jnp.concatenate([lf_hi, lf_mid, lf_lo], axis=1),
                 preferred_element_type=F32)
    b2 = b3[:, :dk] + b3[:, dk:2 * dk] + b3[:, 2 * dk:]
    c2 = b2 - jnp.log2(kf)

    inter = lax.dot_general((qf * jnp.exp2(b2)).astype(BF16), st.astype(BF16), NT_DIMS,
                            preferred_element_type=F32)

    lane = lax.broadcasted_iota(jnp.int32, (sub, LANES), 1)
    tsub = lax.broadcasted_iota(jnp.int32, (sub, LANES), 0)
    colid = lax.broadcasted_iota(jnp.int32, (sub, c), 1)
    ones = jnp.ones((LANES, LANES), BF16)
    a_rows = []
    for bi in range(nsub):
        lo = bi * sub
        b_i = b2[lo:lo + sub]
        c_i = c2[lo:lo + sub]
        q_i = qf[lo:lo + sub]
        if bounded_decay:
            qt = q_i * (jnp.exp2(b_i - b2[lo - 1:lo, :]) if bi > 0 else jnp.exp2(b_i))
            kt = jnp.exp2((b2[lo - 1:lo, :] - c2[:lo + sub]) if bi > 0 else -c2[:sub])
            if lo + sub < c:
                kt = jnp.concatenate([kt, jnp.zeros((c - lo - sub, kt.shape[1]), F32)], axis=0)
            cross = lax.dot_general(qt.astype(BF16), kt.astype(BF16), NT_DIMS,
                                    preferred_element_type=F32)
            a_rows.append(jnp.where(colid <= lo + tsub[:, :c], cross, 0.0))
            continue
        pieces = []
        for s in range(sub):
            if s < half:
                pieces.append(q_i * jnp.exp2(b_i - c_i[s:s + 1, :]))
            else:
                pieces.append(q_i[half:] * jnp.exp2(b_i[half:] - c_i[s:s + 1, :]))
        pm = jnp.concatenate(pieces, axis=0).astype(BF16)
        rs = jnp.dot(pm, ones, preferred_element_type=F32)
        rel = lane - lo
        key = jnp.where((rel >= 0) & (rel <= tsub), rel, -1)
        key_lo, key_hi = key[:half], key[half:]
        a_lo = jnp.zeros((half, LANES), F32)
        a_hi = jnp.zeros((half, LANES), F32)
        off = 0
        for s in range(sub):
            if s < half:
                a_lo = jnp.where(key_lo == s, rs[off:off + half], a_lo)
                a_hi = jnp.where(key_hi == s, rs[off + half:off + sub], a_hi)
                off += sub
            else:
                a_hi = jnp.where(key_hi == s, rs[off:off + half], a_hi)
                off += half
        a_blk = jnp.concatenate([a_lo, a_hi], axis=0)[:, :c]
        if bi > 0:
            b0 = b2[lo - 1:lo, :]
            qt = q_i * jnp.exp2(b_i - b0)
            kt = jnp.exp2(jnp.minimum(b0 - c2, 0.0))
            cross = lax.dot_general(qt.astype(BF16), kt.astype(BF16), NT_DIMS,
                                    preferred_element_type=F32)
            a_blk = a_blk + jnp.where(colid < lo, cross, 0.0)
        a_rows.append(a_blk)
    a = jnp.concatenate(a_rows, axis=0)
    intra = jnp.dot(a.astype(BF16), iv.astype(BF16), preferred_element_type=F32)

    b_last = b2[c - 1:c, :]
    khat = jnp.exp2(b_last - c2)
    st_new = st * jnp.exp2(b_last) + lax.dot_general(
        iv.astype(BF16), khat.astype(BF16), TN_DIMS, preferred_element_type=F32)

    o = inter + intra
    y = o * lax.rsqrt(jnp.mean(o * o, axis=-1, keepdims=True) + RMS_EPS) * gn
    return y * (g * _sigmoid(g)), st_new


def _hgrn_kernel(q_ref, f_ref, i_ref, g_ref, lb_ref, gn_ref, o_ref, st_ref):
    @pl.when(pl.program_id(2) == 0)
    def _():
        st_ref[...] = jnp.zeros_like(st_ref)

    gn = gn_ref[...]
    d = HEAD_DIM
    lb = lb_ref[...]
    f_all = lb + (1.0 - lb) * _sigmoid(f_ref[0])
    f_min = jnp.min(f_all)

    def run(bounded_decay):
        for hh in range(HGRN_HEADS_PER_STEP):
            cs = slice(hh * d, (hh + 1) * d)
            st = st_ref[hh]
            for ci in range(HGRN_STEP // HGRN_CHUNK):
                sl = slice(ci * HGRN_CHUNK, (ci + 1) * HGRN_CHUNK)
                out, st = _hgrn_chunk(q_ref[0, sl, cs], f_all[sl, cs], i_ref[0, sl, cs],
                                      g_ref[0, sl, cs], gn, st, bounded_decay)
                o_ref[0, sl, cs] = out.astype(o_ref.dtype)
            st_ref[hh] = st

    @pl.when(f_min >= HGRN_MIN_SAFE_F)
    def _():
        run(True)

    @pl.when(jnp.logical_not(f_min >= HGRN_MIN_SAFE_F))
    def _():
        run(False)


def _hgrn(hp, lb, gn, n_heads):
    b, t, _ = hp.shape
    d = HEAD_DIM
    ts = HGRN_STEP
    hps = HGRN_HEADS_PER_STEP
    assert n_heads % hps == 0
    ng = n_heads // hps

    def col(group):
        return pl.BlockSpec((1, ts, hps * d), lambda bi, hi, ti: (bi, ti, group * ng + hi))

    return pl.pallas_call(
        _hgrn_kernel,
        grid=(b, ng, t // ts),
        in_specs=[col(0), col(1), col(2), col(3),
                  pl.BlockSpec((1, hps * d), lambda bi, hi, ti: (0, hi)),
                  pl.BlockSpec((1, d), lambda bi, hi, ti: (0, 0))],
        out_specs=pl.BlockSpec((1, ts, hps * d), lambda bi, hi, ti: (bi, ti, hi)),
        out_shape=jax.ShapeDtypeStruct((b, t, n_heads * d), BF16),
        scratch_shapes=[pltpu.VMEM((hps, d, d), F32)],
        compiler_params=_cparams("parallel", "parallel", "arbitrary"),
        name="hgrn2",
    )(hp, hp, hp, hp, lb, gn)


def _outproj_kernel(oa_ref, or_ref, x_ref, w_ref, g_ref, wr_ref, br_ref,
                    x1_ref, h2_ref, rt_ref, cnt_out_ref, cnt_ref):
    wa = oa_ref.shape[1]
    x1 = (x_ref[...]
          + jnp.dot(oa_ref[...], w_ref[0:wa, :], preferred_element_type=F32)
          + jnp.dot(or_ref[...], w_ref[wa:, :], preferred_element_type=F32))
    x1_ref[...] = x1
    h2 = x1 * lax.rsqrt(jnp.mean(x1 * x1, axis=-1, keepdims=True) + RMS_EPS) * g_ref[...]
    h2_ref[...] = h2

    h_hi = h2.astype(BF16)
    h_mid = (h2 - h_hi.astype(F32)).astype(BF16)
    part = jnp.dot(h_hi, wr_ref[...], preferred_element_type=F32)
    logits = (part[:, :LANES] + part[:, LANES:] + br_ref[...]
              + jnp.dot(h_mid, wr_ref[:, :LANES], preferred_element_type=F32))
    lane = lax.broadcasted_iota(jnp.int32, logits.shape, 1)
    big = jnp.int32(4 * LANES)
    ninf = -jnp.inf

    lg = jnp.where(lane < N_GROUPS, logits, ninf)
    mg = jnp.max(lg, axis=1, keepdims=True)
    gidx = jnp.min(jnp.where(lg == mg, lane, big), axis=1, keepdims=True)
    grp_w = 1.0 / jnp.sum(jnp.exp(lg - mg), axis=1, keepdims=True)

    lo = N_GROUPS + EXPERTS_PER_GROUP * gidx
    le = jnp.where((lane >= lo) & (lane < lo + EXPERTS_PER_GROUP), logits, ninf)
    m1 = jnp.max(le, axis=1, keepdims=True)
    i1 = jnp.min(jnp.where(le == m1, lane, big), axis=1, keepdims=True)
    le2 = jnp.where(lane == i1, ninf, le)
    m2 = jnp.max(le2, axis=1, keepdims=True)
    i2 = jnp.min(jnp.where(le2 == m2, lane, big), axis=1, keepdims=True)
    r21 = jnp.exp(m2 - m1)
    w1 = grp_w / (1.0 + r21)
    w2 = grp_w * r21 / (1.0 + r21)
    e1 = i1 - N_GROUPS
    e2 = i2 - N_GROUPS

    @pl.when(pl.program_id(0) == 0)
    def _():
        cnt_ref[...] = jnp.zeros_like(cnt_ref)

    tm = logits.shape[0]
    onehot = jnp.where((lane == e1) | (lane == e2), 1.0, 0.0)
    rr = lax.broadcasted_iota(jnp.int32, (tm, tm), 0)
    rc = lax.broadcasted_iota(jnp.int32, (tm, tm), 1)
    before = jnp.where(rr > rc, 1.0, 0.0).astype(BF16)
    prefix = (jnp.dot(before, onehot.astype(BF16), preferred_element_type=F32) + cnt_ref[0:1, :])
    rank1 = jnp.sum(jnp.where(lane == e1, prefix, 0.0), axis=1, keepdims=True)
    rank2 = jnp.sum(jnp.where(lane == e2, prefix, 0.0), axis=1, keepdims=True)
    total = cnt_ref[0:1, :] + jnp.sum(onehot, axis=0, keepdims=True)
    cnt_ref[0:1, :] = total
    cnt_out_ref[...] = jnp.broadcast_to(total, cnt_out_ref.shape)

    cols = [e1.astype(F32), e2.astype(F32), w1, w2, rank1, rank2]
    rt = jnp.zeros(logits.shape, F32)
    for ci, cv in enumerate(cols):
        rt = jnp.where(lane == ci, cv, rt)
    rt_ref[...] = rt


def _outproj(oa, orec, x2, w_out, g, wr, br, tm=256):
    n, d = x2.shape
    wa = oa.shape[1]
    wrc = orec.shape[1]
    row = lambda i: (i, 0)
    const = lambda i: (0, 0)
    return pl.pallas_call(
        _outproj_kernel,
        grid=(n // tm,),
        in_specs=[
            pl.BlockSpec((tm, wa), row),
            pl.BlockSpec((tm, wrc), row),
            pl.BlockSpec((tm, d), row),
            pl.BlockSpec((wa + wrc, d), const),
            pl.BlockSpec((1, d), const),
            pl.BlockSpec((d, 2 * LANES), const),
            pl.BlockSpec((1, LANES), const),
        ],
        out_specs=[pl.BlockSpec((tm, d), row), pl.BlockSpec((tm, d), row),
                   pl.BlockSpec((tm, LANES), row), pl.BlockSpec((8, LANES), const)],
        out_shape=[jax.ShapeDtypeStruct((n, d), F32), jax.ShapeDtypeStruct((n, d), F32),
                   jax.ShapeDtypeStruct((n, LANES), F32), jax.ShapeDtypeStruct((8, LANES), F32)],
        scratch_shapes=[pltpu.VMEM((8, LANES), F32)],
        compiler_params=_cparams("arbitrary"),
        name="outproj_route",
    )(oa, orec, x2, w_out, g, wr, br)


def _moe_kernel(te_ref, nu_ref, src_ref, h2_hbm, wg_ref, wu_ref, wd_ref, y_ref, xbuf, sems):
    s = pl.program_id(0)
    nu = nu_ref[0]
    tm = MOE_TILE

    def gather_row(tile, slot, r):
        tok = src_ref[tile * tm + r]
        return pltpu.make_async_copy(h2_hbm.at[pl.ds(tok, 1), :], xbuf.at[slot, pl.ds(r, 1), :],
                                     sems.at[slot])

    def start_gather(tile):
        slot = tile % 2
        for r in range(tm):
            gather_row(tile, slot, r).start()

    def wait_gather(tile):
        slot = tile % 2
        pltpu.make_async_copy(h2_hbm.at[pl.ds(0, tm), :], xbuf.at[slot], sems.at[slot]).wait()

    def ffn(tile):
        x = xbuf[tile % 2].astype(BF16)
        a = jnp.dot(x, wg_ref[0], preferred_element_type=F32)
        u = jnp.dot(x, wu_ref[0], preferred_element_type=F32)
        hid = (a * _sigmoid(a) * u).astype(BF16)
        y_ref[...] = jnp.dot(hid, wd_ref[0], preferred_element_type=F32)

    @pl.when(s == 0)
    def _():
        start_gather(s)

    @pl.when((s >= 1) & (s < nu))
    def _():
        start_gather(s)
        wait_gather(s - 1)
        ffn(s - 1)

    @pl.when((s >= 1) & (s == nu))
    def _():
        wait_gather(s - 1)
        ffn(s - 1)

    @pl.when(s > nu)
    def _():
        y_ref[...] = jnp.zeros_like(y_ref)


def _moe(tile_expert, n_used, src, h2, wg, wu, wd):
    p = src.shape[0]
    d, f = wg.shape[1:]
    tm = MOE_TILE
    n_tiles = p // tm
    prev = lambda s: jnp.maximum(s - 1, 0)
    grid_spec = pltpu.PrefetchScalarGridSpec(
        num_scalar_prefetch=3,
        grid=(n_tiles + 1,),
        in_specs=[
            pl.BlockSpec(memory_space=pl.ANY),
            pl.BlockSpec((1, d, f), lambda s, te, nu, sr: (te[prev(s)], 0, 0)),
            pl.BlockSpec((1, d, f), lambda s, te, nu, sr: (te[prev(s)], 0, 0)),
            pl.BlockSpec((1, f, d), lambda s, te, nu, sr: (te[prev(s)], 0, 0)),
        ],
        out_specs=pl.BlockSpec((tm, d), lambda s, te, nu, sr: (prev(s), 0)),
        scratch_shapes=[pltpu.VMEM((2, tm, d), F32), pltpu.SemaphoreType.DMA((2,))],
    )
    return pl.pallas_call(
        _moe_kernel,
        grid_spec=grid_spec,
        out_shape=jax.ShapeDtypeStruct((p, d), F32),
        compiler_params=_cparams("arbitrary"),
        name="moe_ffn",
    )(tile_expert, n_used, src, h2, wg, wu, wd)


def _final_kernel(x1_ref, y1_ref, y2_ref, rt_ref, g_ref, o_ref):
    rt = rt_ref[...]
    x2 = x1_ref[...] + rt[:, 2:3] * y1_ref[...] + rt[:, 3:4] * y2_ref[...]
    o_ref[...] = x2 * lax.rsqrt(jnp.mean(x2 * x2, axis=-1, keepdims=True) + RMS_EPS) * g_ref[...]


def _final(x1, y1, y2, rt, g, tm=256):
    n, d = x1.shape
    row = lambda i: (i, 0)
    return pl.pallas_call(
        _final_kernel,
        grid=(n // tm,),
        in_specs=[pl.BlockSpec((tm, d), row), pl.BlockSpec((tm, d), row), pl.BlockSpec((tm, d), row),
                  pl.BlockSpec((tm, LANES), row), pl.BlockSpec((1, d), lambda i: (0, 0))],
        out_specs=pl.BlockSpec((tm, d), row),
        out_shape=jax.ShapeDtypeStruct((n, d), F32),
        compiler_params=_cparams("parallel"),
        name="combine_norm",
    )(x1, y1, y2, rt, g)


def _dispatch_plan(rt, counts, n_tokens):
    tm = MOE_TILE
    n_rows = 2 * n_tokens + N_EXPERTS * tm
    n_tiles = n_rows // tm
    ids = rt[:, 0:6].astype(jnp.int32)
    ef = jnp.concatenate([ids[:, 0], ids[:, 1]])
    rank = jnp.concatenate([ids[:, 4], ids[:, 5]])
    counts = counts[0, :N_EXPERTS].astype(jnp.int32)
    tiles_per = (counts + tm - 1) // tm
    tile_end = jnp.cumsum(tiles_per)
    row_start = (tile_end - tiles_per) * tm
    pos = row_start[ef] + rank
    tok = jnp.concatenate([jnp.arange(n_tokens, dtype=jnp.int32)] * 2)
    src = jnp.zeros((n_rows,), jnp.int32).at[pos].set(tok)
    n_used = tile_end[-1]
    tile_ids = jnp.arange(n_tiles, dtype=jnp.int32)
    tile_expert = jnp.sum((tile_ids[:, None] >= tile_end[None, :]).astype(jnp.int32), axis=1)
    last_expert = jnp.sum((n_used - 1 >= tile_end).astype(jnp.int32))
    tile_expert = jnp.where(tile_ids < n_used, tile_expert, last_expert).astype(jnp.int32)
    return src, pos[:n_tokens], pos[n_tokens:], tile_expert, n_used.reshape(1).astype(jnp.int32)


def kernel(x, norm_mix_g, w_in, hgrn_lb_logits, hgrn_out_norm_g, w_out, norm_ffn_g, w_group_router,
           b_group_router, w_expert_router, b_expert_router, w_gate, w_up, w_down, final_norm_g):
    b, t, d = x.shape
    n = b * t
    depth = w_in.shape[0]
    assert depth == 1, "the final norm is fused into the combine step of the only layer"
    attn_w = d // 2
    n_heads = attn_w // HEAD_DIM
    x2 = x.reshape(n, d)
    lb_all = jnp.cumsum(jax.nn.softmax(hgrn_lb_logits.astype(F32), axis=0), axis=0)[:depth]
    slope2 = jnp.asarray(2.0 ** (-8.0 * np.arange(1, n_heads + 1) / n_heads), dtype=F32) * F32(LOG2E)
    s_hi = slope2.astype(BF16).astype(F32)
    s_mid = (slope2 - s_hi).astype(BF16).astype(F32)
    s_lo = (slope2 - s_hi - s_mid).astype(BF16).astype(F32)
    slope_pieces = jnp.stack([s_hi, s_mid, s_lo])
    qkv_scale = jnp.concatenate([jnp.full((attn_w,), HEAD_DIM ** -0.5 * LOG2E, F32),
                                 jnp.ones((2 * attn_w,), F32)]).reshape(1, -1)

    for l in range(depth):
        g_mix = norm_mix_g[l].reshape(1, d)
        x2, w_in_l, w_out_l, wg_l, wu_l, wd_l = lax.optimization_barrier(
            (x2, w_in[l].astype(BF16), w_out[l].astype(BF16), w_gate[l].astype(BF16),
             w_up[l].astype(BF16), w_down[l].astype(BF16)))
        qkv = _inproj(x2, g_mix, w_in_l, 0, qkv_scale, BF16)
        hp = _inproj(x2, g_mix, w_in_l, 3 * attn_w, jnp.ones((1, 4 * (d - attn_w)), F32), F32)
        o_attn = _moba(qkv.reshape(b, t, 3 * attn_w), slope_pieces, n_heads)
        o_rec = _hgrn(hp.reshape(b, t, -1), lb_all[l].reshape(1, -1),
                      hgrn_out_norm_g[l].reshape(1, HEAD_DIM), n_heads)

        wr = jnp.concatenate(
            [w_group_router[l],
             jnp.transpose(w_expert_router[l], (1, 0, 2)).reshape(d, N_EXPERTS)], axis=1)
        wr = jnp.pad(wr, ((0, 0), (0, LANES - wr.shape[1])))
        wr_hi = wr.astype(BF16)
        wr = jnp.concatenate([wr_hi, (wr - wr_hi.astype(F32)).astype(BF16)], axis=1)
        br =jnp.concatenate([b_group_router[l], b_expert_router[l].reshape(-1)])
        br = jnp.pad(br, (0, LANES - br.shape[0])).reshape(1, LANES)
        x1, h2, rt, counts = _outproj(o_attn.reshape(n, attn_w), o_rec.reshape(n, -1), x2,
                                      w_out_l, norm_ffn_g[l].reshape(1, d), wr, br)
        src, pos1, pos2, tile_expert, n_used = _dispatch_plan(rt, counts, n)
        ys = _moe(tile_expert, n_used, src, h2, wg_l, wu_l, wd_l)
        y1 = ys.at[pos1].get(mode="promise_in_bounds")
        y2 = ys.at[pos2].get(mode="promise_in_bounds")
        x2 = _final(x1, y1, y2, rt, final_norm_g.reshape(1, d))
    return x2.reshape(b, t, d)
```

```python
import functools

import jax
import jax.numpy as jnp
import numpy as np
from jax import lax
from jax.experimental import pallas as pl
from jax.experimental.pallas import tpu as pltpu

F32 = jnp.float32
BF16 = jnp.bfloat16
HIGHEST = lax.Precision.HIGHEST

HEAD_DIM = 128
MOBA_BLOCK = 256
MOBA_TOPK = 3
N_GROUPS = 4
EXPERTS_PER_GROUP = 4
N_EXPERTS = N_GROUPS * EXPERTS_PER_GROUP
RMS_EPS = 1e-6

LANES = 128
VMEM_LIMIT_BYTES = 56 * 1024 * 1024

HGRN_CHUNK = 64
HGRN_SUB = 16
HGRN_STEP = 256
HGRN_HEADS_PER_STEP = 8
HGRN_MIN_SAFE_F = 2.0 ** -7
MOE_TILE = 256
NT_DIMS = (((1,), (1,)), ((), ()))
TN_DIMS = (((0,), (0,)), ((), ()))


def _cparams(*sem):
    return pltpu.CompilerParams(dimension_semantics=sem, vmem_limit_bytes=VMEM_LIMIT_BYTES)


def _inproj_kernel(x_ref, g_ref, w_ref, cs_ref, o_ref, hn_ref):
    @pl.when(pl.program_id(1) == 0)
    def _():
        x = x_ref[...]
        ms = jnp.mean(x * x, axis=-1, keepdims=True)
        hn_ref[...] = (x * lax.rsqrt(ms + RMS_EPS) * g_ref[...]).astype(BF16)

    acc = jnp.dot(hn_ref[...], w_ref[...], preferred_element_type=F32)
    o_ref[...] = (acc * cs_ref[...]).astype(o_ref.dtype)


def _inproj(x2, g, w, col0, colscale, out_dtype, tm=1024, tn=1024):
    n, d = x2.shape
    cols = colscale.shape[1]
    j0 = col0 // tn
    return pl.pallas_call(
        _inproj_kernel,
        grid=(n // tm, cols // tn),
        in_specs=[
            pl.BlockSpec((tm, d), lambda i, j: (i, 0)),
            pl.BlockSpec((1, d), lambda i, j: (0, 0)),
            pl.BlockSpec((d, tn), lambda i, j: (0, j0 + j)),
            pl.BlockSpec((1, tn), lambda i, j: (0, j)),
        ],
        out_specs=pl.BlockSpec((tm, tn), lambda i, j: (i, j)),
        out_shape=jax.ShapeDtypeStruct((n, cols), out_dtype),
        scratch_shapes=[pltpu.VMEM((tm, d), BF16)],
        compiler_params=_cparams("parallel", "arbitrary"),
        name="inproj",
    )(x2, g, w, colscale)


MOBA_HEADS_PER_STEP = 4
MOBA_EXTRA_POS = 0
MOBA_EXTRA_SEL = 8
MASK_BIG = 2.0 ** 60
LOG2E = 1.4426950408889634


def _moba_key_extras(t):
    nb = t // MOBA_BLOCK
    assert MOBA_EXTRA_SEL + nb <= LANES and MOBA_BLOCK <= 256
    pos = np.arange(t)
    kx = np.zeros((t, LANES), np.float32)
    kx[:, MOBA_EXTRA_POS:MOBA_EXTRA_POS + 3] = ((pos // MOBA_BLOCK) * MOBA_BLOCK)[:, None]
    kx[:, MOBA_EXTRA_POS + 3:MOBA_EXTRA_POS + 6] = (pos % MOBA_BLOCK)[:, None]
    kx[pos, MOBA_EXTRA_SEL + pos // MOBA_BLOCK] = -MASK_BIG
    return jnp.asarray(kx, dtype=BF16)


def _moba_kernel(sl_ref, q_ref, k_ref, v_ref, kx_ref, o_ref, kaug_ref, kmean_ref, *, nb):
    hg = pl.program_id(1)
    i = pl.program_id(2)
    bs = MOBA_BLOCK
    dh = HEAD_DIM
    nbp = kmean_ref.shape[1] // 4
    nx = 8 + nbp

    @pl.when(i == 0)
    def _():
        for hh in range(MOBA_HEADS_PER_STEP):
            kaug_ref[hh, :, 0:dh] = k_ref[0, :, hh * dh:(hh + 1) * dh]
            kaug_ref[hh, :, dh:] = kx_ref[...]
            rows = [jnp.mean(k_ref[0, j * bs:(j + 1) * bs, hh * dh:(hh + 1) * dh].astype(F32),
                             axis=0, keepdims=True) for j in range(nb)]
            if nbp > nb:
                rows.append(jnp.zeros((nbp - nb, dh), F32))
            km = jnp.concatenate(rows, axis=0)
            hi = km.astype(BF16)
            mid = (km - hi.astype(F32)).astype(BF16)
            lo = (km - hi.astype(F32) - mid.astype(F32)).astype(BF16)
            kmean_ref[hh] = jnp.concatenate([hi, mid, lo, jnp.zeros_like(hi)], axis=0)

    r8 = lax.broadcasted_iota(jnp.int32, (8, bs), 0)
    piece = r8 % 3
    er = lax.broadcasted_iota(jnp.int32, (nx, LANES), 0)
    ec = lax.broadcasted_iota(jnp.int32, (nx, LANES), 1)
    embed = jnp.where(er == ec, 1.0, 0.0).astype(BF16)

    qas = []
    for hh in range(MOBA_HEADS_PER_STEP):
        h = hg * MOBA_HEADS_PER_STEP + hh
        q = q_ref[0, :, hh * dh:(hh + 1) * dh]
        g4 = lax.dot_general(kmean_ref[hh], q, NT_DIMS, preferred_element_type=F32)
        gate_t = g4[0:nbp] + g4[nbp:2 * nbp] + g4[2 * nbp:3 * nbp]
        blk = lax.broadcasted_iota(jnp.int32, gate_t.shape, 0)
        rank = jnp.zeros(gate_t.shape, F32)
        for jp in range(nb - 1):
            other = gate_t[jp:jp + 1, :]
            beats = (other > gate_t) | ((other == gate_t) & (blk > jp))
            rank = rank + jnp.where(beats, (jp < i).astype(F32), 0.0)
        notsel_t = jnp.where((blk < i) & (rank >= MOBA_TOPK), 1.0, 0.0)
        slope_rows = jnp.where(r8 >= 6, 0.0,
                               jnp.where(piece == 0, sl_ref[0, h],
                                         jnp.where(piece == 1, sl_ref[1, h], sl_ref[2, h])))
        qx_t = jnp.concatenate([slope_rows, notsel_t], axis=0).astype(BF16)
        qx = lax.dot_general(qx_t, embed, TN_DIMS, preferred_element_type=F32)
        qas.append(jnp.concatenate([q, qx.astype(BF16)], axis=1))

    rowi = lax.broadcasted_iota(jnp.int32, (bs, bs), 0)
    coli = lax.broadcasted_iota(jnp.int32, (bs, bs), 1)

    for c in range(nb):
        @pl.when(i == c)
        def _(c=c):
            n = (c + 1) * bs
            for hh in range(MOBA_HEADS_PER_STEP):
                cs = slice(hh * dh, (hh + 1) * dh)
                s = lax.dot_general(qas[hh], kaug_ref[hh, 0:n, :], NT_DIMS,
                                    preferred_element_type=F32)
                s_own = jnp.where(rowi >= coli, s[:, c * bs:], -jnp.inf)
                m = jnp.max(s_own, axis=1, keepdims=True)
                if c > 0:
                    s_past = s[:, :c * bs]
                    m = jnp.maximum(m, jnp.max(s_past, axis=1, keepdims=True))
                p_own = jnp.exp2(s_own - m)
                l = jnp.sum(p_own, axis=1, keepdims=True)
                acc = jnp.dot(p_own.astype(BF16), v_ref[0, c * bs:n, cs],
                              preferred_element_type=F32)
                if c > 0:
                    p_past = jnp.exp2(s_past - m)
                    l = l + jnp.sum(p_past, axis=1, keepdims=True)
                    acc = acc + jnp.dot(p_past.astype(BF16), v_ref[0, 0:c * bs, cs],
                                        preferred_element_type=F32)
                o_ref[0, :, cs] = (acc / l).astype(o_ref.dtype)


def _moba(qkv, slope_pieces, n_heads):
    b, t, _ = qkv.shape
    bs = MOBA_BLOCK
    nb = t // bs
    nbp = -(-nb // 8) * 8
    hps = MOBA_HEADS_PER_STEP
    assert n_heads % hps == 0
    ng = n_heads // hps
    w = hps * HEAD_DIM
    kern = functools.partial(_moba_kernel, nb=nb)
    return pl.pallas_call(
        kern,
        grid=(b, ng, nb),
        in_specs=[
            pl.BlockSpec(memory_space=pltpu.SMEM),
            pl.BlockSpec((1, bs, w), lambda bi, hi, qi: (bi, qi, hi)),
            pl.BlockSpec((1, t, w), lambda bi, hi, qi: (bi, 0, ng + hi)),
            pl.BlockSpec((1, t, w), lambda bi, hi, qi: (bi, 0, 2 * ng + hi)),
            pl.BlockSpec((t, LANES), lambda bi, hi, qi: (0, 0)),
        ],
        out_specs=pl.BlockSpec((1, bs, w), lambda bi, hi, qi: (bi, qi, hi)),
        out_shape=jax.ShapeDtypeStruct((b, t, n_heads * HEAD_DIM), BF16),
        scratch_shapes=[
            pltpu.VMEM((hps, t, HEAD_DIM + LANES), BF16),
            pltpu.VMEM((hps, 4 * nbp, HEAD_DIM), BF16),
        ],
        compiler_params=_cparams("parallel", "parallel", "arbitrary"),
        name="moba",
    )(slope_pieces, qkv, qkv, qkv, _moba_key_extras(t))


def _sigmoid(x):
    return 1.0 / (1.0 + jnp.exp(-x))


def _hgrn_chunk(q, f, iv, g, gn, st, bounded_decay):
    c = HGRN_CHUNK
    sub = HGRN_SUB
    half = sub // 2
    nsub = c // sub
    qf = q * _sigmoid(q)
    kf = jnp.maximum(1.0 - f, 0.0)
    r = lax.broadcasted_iota(jnp.int32, (c, c), 0)
    cc = lax.broadcasted_iota(jnp.int32, (c, c), 1)
    tril = jnp.where(r >= cc, 1.0, 0.0).astype(BF16)
    lf = jnp.log2(f)
    dk = lf.shape[1]
    lf_hi = lf.astype(BF16)
    lf_mid = (lf - lf_hi.astype(F32)).astype(BF16)
    lf_lo = (lf - lf_hi.astype(F32) - lf_mid.astype(F32)).astype(BF16)
    b3 = jnp.dot(tril, jnp.concatenate([lf_hi, lf_mid, lf_lo], axis=1),
                 preferred_element_type=F32)
    b2 = b3[:, :dk] + b3[:, dk:2 * dk] + b3[:, 2 * dk:]
    c2 = b2 - jnp.log2(kf)

    inter = lax.dot_general((qf * jnp.exp2(b2)).astype(BF16), st.astype(BF16), NT_DIMS,
                            preferred_element_type=F32)

    lane = lax.broadcasted_iota(jnp.int32, (sub, LANES), 1)
    tsub = lax.broadcasted_iota(jnp.int32, (sub, LANES), 0)
    colid = lax.broadcasted_iota(jnp.int32, (sub, c), 1)
    ones = jnp.ones((LANES, LANES), BF16)
    a_rows = []
    for bi in range(nsub):
        lo = bi * sub
        b_i = b2[lo:lo + sub]
        c_i = c2[lo:lo + sub]
        q_i = qf[lo:lo + sub]
        if bounded_decay:
            qt = q_i * (jnp.exp2(b_i - b2[lo - 1:lo, :]) if bi > 0 else jnp.exp2(b_i))
            kt = jnp.exp2((b2[lo - 1:lo, :] - c2[:lo + sub]) if bi > 0 else -c2[:sub])
            if lo + sub < c:
                kt = jnp.concatenate([kt, jnp.zeros((c - lo - sub, kt.shape[1]), F32)], axis=0)
            cross = lax.dot_general(qt.astype(BF16), kt.astype(BF16), NT_DIMS,
                                    preferred_element_type=F32)
            a_rows.append(jnp.where(colid <= lo + tsub[:, :c], cross, 0.0))
            continue
        pieces = []
        for s in range(sub):
            if s < half:
                pieces.append(q_i * jnp.exp2(b_i - c_i[s:s + 1, :]))
            else:
                pieces.append(q_i[half:] * jnp.exp2(b_i[half:] - c_i[s:s + 1, :]))
        pm = jnp.concatenate(pieces, axis=0).astype(BF16)
        rs = jnp.dot(pm, ones, preferred_element_type=F32)
        rel = lane - lo
        key = jnp.where((rel >= 0) & (rel <= tsub), rel, -1)
        key_lo, key_hi = key[:half], key[half:]
        a_lo = jnp.zeros((half, LANES), F32)
        a_hi = jnp.zeros((half, LANES), F32)
        off = 0
        for s in range(sub):
            if s < half:
                a_lo = jnp.where(key_lo == s, rs[off:off + half], a_lo)
                a_hi = jnp.where(key_hi == s, rs[off + half:off + sub], a_hi)
                off += sub
            else:
                a_hi = jnp.where(key_hi == s, rs[off:off + half], a_hi)
                off += half
        a_blk = jnp.concatenate([a_lo, a_hi], axis=0)[:, :c]
        if bi > 0:
            b0 = b2[lo - 1:lo, :]
            qt = q_i * jnp.exp2(b_i - b0)
            kt = jnp.exp2(jnp.minimum(b0 - c2, 0.0))
            cross = lax.dot_general(qt.astype(BF16), kt.astype(BF16), NT_DIMS,
                                    preferred_element_type=F32)
            a_blk = a_blk + jnp.where(colid < lo, cross, 0.0)
        a_rows.append(a_blk)
    a = jnp.concatenate(a_rows, axis=0)
    intra = jnp.dot(a.astype(BF16), iv.astype(BF16), preferred_element_type=F32)

    b_last = b2[c - 1:c, :]
    khat = jnp.exp2(b_last - c2)
    st_new = st * jnp.exp2(b_last) + lax.dot_general(
        iv.astype(BF16), khat.astype(BF16), TN_DIMS, preferred_element_type=F32)

    o = inter + intra
    y = o * lax.rsqrt(jnp.mean(o * o, axis=-1, keepdims=True) + RMS_EPS) * gn
    return y * (g * _sigmoid(g)), st_new


def _hgrn_kernel(q_ref, f_ref, i_ref, g_ref, lb_ref, gn_ref, o_ref, st_ref):
    @pl.when(pl.program_id(2) == 0)
    def _():
        st_ref[...] = jnp.zeros_like(st_ref)

    gn = gn_ref[...]
    d = HEAD_DIM
    lb = lb_ref[...]
    f_all = lb + (1.0 - lb) * _sigmoid(f_ref[0])
    f_min = jnp.min(f_all)

    def run(bounded_decay):
        for hh in range(HGRN_HEADS_PER_STEP):
            cs = slice(hh * d, (hh + 1) * d)
            st = st_ref[hh]
            for ci in range(HGRN_STEP // HGRN_CHUNK):
                sl = slice(ci * HGRN_CHUNK, (ci + 1) * HGRN_CHUNK)
                out, st = _hgrn_chunk(q_ref[0, sl, cs], f_all[sl, cs], i_ref[0, sl, cs],
                                      g_ref[0, sl, cs], gn, st, bounded_decay)
                o_ref[0, sl, cs] = out.astype(o_ref.dtype)
            st_ref[hh] = st

    @pl.when(f_min >= HGRN_MIN_SAFE_F)
    def _():
        run(True)

    @pl.when(jnp.logical_not(f_min >= HGRN_MIN_SAFE_F))
    def _():
        run(False)


def _hgrn(hp, lb, gn, n_heads):
    b, t, _ = hp.shape
    d = HEAD_DIM
    ts = HGRN_STEP
    hps = HGRN_HEADS_PER_STEP
    assert n_heads % hps == 0
    ng = n_heads // hps

    def col(group):
        return pl.BlockSpec((1, ts, hps * d), lambda bi, hi, ti: (bi, ti, group * ng + hi))

    return pl.pallas_call(
        _hgrn_kernel,
        grid=(b, ng, t // ts),
        in_specs=[col(0), col(1), col(2), col(3),
                  pl.BlockSpec((1, hps * d), lambda bi, hi, ti: (0, hi)),
                  pl.BlockSpec((1, d), lambda bi, hi, ti: (0, 0))],
        out_specs=pl.BlockSpec((1, ts, hps * d), lambda bi, hi, ti: (bi, ti, hi)),
        out_shape=jax.ShapeDtypeStruct((b, t, n_heads * d), BF16),
        scratch_shapes=[pltpu.VMEM((hps, d, d), F32)],
        compiler_params=_cparams("parallel", "parallel", "arbitrary"),
        name="hgrn2",
    )(hp, hp, hp, hp, lb, gn)


def _outproj_kernel(oa_ref, or_ref, x_ref, w_ref, g_ref, wr_ref, br_ref,
                    x1_ref, h2_ref, rt_ref, cnt_out_ref, cnt_ref):
    wa = oa_ref.shape[1]
    x1 = (x_ref[...]
          + jnp.dot(oa_ref[...], w_ref[0:wa, :], preferred_element_type=F32)
          + jnp.dot(or_ref[...], w_ref[wa:, :], preferred_element_type=F32))
    x1_ref[...] = x1
    h2 = x1 * lax.rsqrt(jnp.mean(x1 * x1, axis=-1, keepdims=True) + RMS_EPS) * g_ref[...]
    h2_ref[...] = h2

    h_hi = h2.astype(BF16)
    h_mid = (h2 - h_hi.astype(F32)).astype(BF16)
    part = jnp.dot(h_hi, wr_ref[...], preferred_element_type=F32)
    logits = (part[:, :LANES] + part[:, LANES:] + br_ref[...]
              + jnp.dot(h_mid, wr_ref[:, :LANES], preferred_element_type=F32))
    lane = lax.broadcasted_iota(jnp.int32, logits.shape, 1)
    big = jnp.int32(4 * LANES)
    ninf = -jnp.inf

    lg = jnp.where(lane < N_GROUPS, logits, ninf)
    mg = jnp.max(lg, axis=1, keepdims=True)
    gidx = jnp.min(jnp.where(lg == mg, lane, big), axis=1, keepdims=True)
    grp_w = 1.0 / jnp.sum(jnp.exp(lg - mg), axis=1, keepdims=True)

    lo = N_GROUPS + EXPERTS_PER_GROUP * gidx
    le = jnp.where((lane >= lo) & (lane < lo + EXPERTS_PER_GROUP), logits, ninf)
    m1 = jnp.max(le, axis=1, keepdims=True)
    i1 = jnp.min(jnp.where(le == m1, lane, big), axis=1, keepdims=True)
    le2 = jnp.where(lane == i1, ninf, le)
    m2 = jnp.max(le2, axis=1, keepdims=True)
    i2 = jnp.min(jnp.where(le2 == m2, lane, big), axis=1, keepdims=True)
    r21 = jnp.exp(m2 - m1)
    w1 = grp_w / (1.0 + r21)
    w2 = grp_w * r21 / (1.0 + r21)
    e1 = i1 - N_GROUPS
    e2 = i2 - N_GROUPS

    @pl.when(pl.program_id(0) == 0)
    def _():
        cnt_ref[...] = jnp.zeros_like(cnt_ref)

    tm = logits.shape[0]
    onehot = jnp.where((lane == e1) | (lane == e2), 1.0, 0.0)
    rr = lax.broadcasted_iota(jnp.int32, (tm, tm), 0)
    rc = lax.broadcasted_iota(jnp.int32, (tm, tm), 1)
    before = jnp.where(rr > rc, 1.0, 0.0).astype(BF16)
    prefix = (jnp.dot(before, onehot.astype(BF16), preferred_element_type=F32) + cnt_ref[0:1, :])
    rank1 = jnp.sum(jnp.where(lane == e1, prefix, 0.0), axis=1, keepdims=True)
    rank2 = jnp.sum(jnp.where(lane == e2, prefix, 0.0), axis=1, keepdims=True)
    total = cnt_ref[0:1, :] + jnp.sum(onehot, axis=0, keepdims=True)
    cnt_ref[0:1, :] = total
    cnt_out_ref[...] = jnp.broadcast_to(total, cnt_out_ref.shape)

    cols = [e1.astype(F32), e2.astype(F32), w1, w2, rank1, rank2]
    rt = jnp.zeros(logits.shape, F32)
    for ci, cv in enumerate(cols):
        rt = jnp.where(lane == ci, cv, rt)
    rt_ref[...] = rt


def _outproj(oa, orec, x2, w_out, g, wr, br, tm=256):
    n, d = x2.shape
    wa = oa.shape[1]
    wrc = orec.shape[1]
    row = lambda i: (i, 0)
    const = lambda i: (0, 0)
    return pl.pallas_call(
        _outproj_kernel,
        grid=(n // tm,),
        in_specs=[
            pl.BlockSpec((tm, wa), row),
            pl.BlockSpec((tm, wrc), row),
            pl.BlockSpec((tm, d), row),
            pl.BlockSpec((wa + wrc, d), const),
            pl.BlockSpec((1, d), const),
            pl.BlockSpec((d, 2 * LANES), const),
            pl.BlockSpec((1, LANES), const),
        ],
        out_specs=[pl.BlockSpec((tm, d), row), pl.BlockSpec((tm, d), row),
                   pl.BlockSpec((tm, LANES), row), pl.BlockSpec((8, LANES), const)],
        out_shape=[jax.ShapeDtypeStruct((n, d), F32), jax.ShapeDtypeStruct((n, d), F32),
                   jax.ShapeDtypeStruct((n, LANES), F32), jax.ShapeDtypeStruct((8, LANES), F32)],
        scratch_shapes=[pltpu.VMEM((8, LANES), F32)],
        compiler_params=_cparams("arbitrary"),
        name="outproj_route",
    )(oa, orec, x2, w_out, g, wr, br)


WEIGHT_CAST_ROWS = 256


def _cast_rows(dst_ref, src_ref):
    rows = dst_ref.shape[0]
    step = min(rows, WEIGHT_CAST_ROWS)
    assert rows % step == 0

    def body(c, carry):
        sl = pl.ds(pl.multiple_of(c * step, step), step)
        dst_ref[sl, :] = src_ref[0, sl, :].astype(BF16)
        return carry

    lax.fori_loop(0, rows // step, body, 0)


def _moe_up_kernel(te_ref, nu_ref, first_ref, src_ref, h2_hbm, wg_ref, wu_ref, hid_ref,
                   xbuf, sems, wgb, wub):
    s = pl.program_id(0)
    nu = nu_ref[0]
    tm = MOE_TILE

    def gather_row(tile, slot, r):
        tok = src_ref[tile * tm + r]
        return pltpu.make_async_copy(h2_hbm.at[pl.ds(tok, 1), :], xbuf.at[slot, pl.ds(r, 1), :],
                                     sems.at[slot])

    def start_gather(tile):
        slot = tile % 2
        for r in range(tm):
            gather_row(tile, slot, r).start()

    def wait_gather(tile):
        slot = tile % 2
        pltpu.make_async_copy(h2_hbm.at[pl.ds(0, tm), :], xbuf.at[slot], sems.at[slot]).wait()

    def gate_up(tile):
        x = xbuf[tile % 2].astype(BF16)
        a = jnp.dot(x, wgb[...], preferred_element_type=F32)
        u = jnp.dot(x, wub[...], preferred_element_type=F32)
        hid_ref[...] = (a * _sigmoid(a) * u).astype(BF16)

    @pl.when(s == 0)
    def _():
        start_gather(s)

    @pl.when((s >= 1) & (s <= nu) & (first_ref[jnp.maximum(s - 1, 0)] == 1))
    def _():
        _cast_rows(wgb, wg_ref)
        _cast_rows(wub, wu_ref)

    @pl.when((s >= 1) & (s < nu))
    def _():
        start_gather(s)
        wait_gather(s - 1)
        gate_up(s - 1)

    @pl.when((s >= 1) & (s == nu))
    def _():
        wait_gather(s - 1)
        gate_up(s - 1)

    @pl.when(s > nu)
    def _():
        hid_ref[...] = jnp.zeros_like(hid_ref)


def _moe_down_kernel(te_ref, nu_ref, first_ref, hid_ref, wd_ref, y_ref, wdb):
    n = pl.program_id(0)

    @pl.when((n < nu_ref[0]) & (first_ref[n] == 1))
    def _():
        _cast_rows(wdb, wd_ref)

    @pl.when(n < nu_ref[0])
    def _():
        y_ref[...] = jnp.dot(hid_ref[...], wdb[...], preferred_element_type=F32)

    @pl.when(n >= nu_ref[0])
    def _():
        y_ref[...] = jnp.zeros_like(y_ref)


def _moe(tile_expert, n_used, first, src, h2, wg, wu, wd):
    p = src.shape[0]
    d, f = wg.shape[1:]
    tm = MOE_TILE
    n_tiles = p // tm
    prev = lambda s: jnp.maximum(s - 1, 0)
    hid = pl.pallas_call(
        _moe_up_kernel,
        grid_spec=pltpu.PrefetchScalarGridSpec(
            num_scalar_prefetch=4,
            grid=(n_tiles + 1,),
            in_specs=[
                pl.BlockSpec(memory_space=pl.ANY),
                pl.BlockSpec((1, d, f), lambda s, te, nu, fi, sr: (te[prev(s)], 0, 0)),
                pl.BlockSpec((1, d, f), lambda s, te, nu, fi, sr: (te[prev(s)], 0, 0)),
            ],
            out_specs=pl.BlockSpec((tm, f), lambda s, te, nu, fi, sr: (prev(s), 0)),
            scratch_shapes=[pltpu.VMEM((2, tm, d), F32), pltpu.SemaphoreType.DMA((2,)),
                            pltpu.VMEM((d, f), BF16), pltpu.VMEM((d, f), BF16)],
        ),
        out_shape=jax.ShapeDtypeStruct((p, f), BF16),
        compiler_params=_cparams("arbitrary"),
        name="moe_gate_up",
    )(tile_expert, n_used, first, src, h2, wg, wu)
    return pl.pallas_call(
        _moe_down_kernel,
        grid_spec=pltpu.PrefetchScalarGridSpec(
            num_scalar_prefetch=3,
            grid=(n_tiles,),
            in_specs=[
                pl.BlockSpec((tm, f), lambda n, te, nu, fi: (n, 0)),
                pl.BlockSpec((1, f, d), lambda n, te, nu, fi: (te[n], 0, 0)),
            ],
            out_specs=pl.BlockSpec((tm, d), lambda n, te, nu, fi: (n, 0)),
            scratch_shapes=[pltpu.VMEM((f, d), BF16)],
        ),
        out_shape=jax.ShapeDtypeStruct((p, d), F32),
        compiler_params=_cparams("arbitrary"),
        name="moe_down",
    )(tile_expert, n_used, first, hid, wd)


def _final_kernel(x1_ref, y1_ref, y2_ref, rt_ref, g_ref, o_ref):
    rt = rt_ref[...]
    x2 = x1_ref[...] + rt[:, 2:3] * y1_ref[...] + rt[:, 3:4] * y2_ref[...]
    o_ref[...] = x2 * lax.rsqrt(jnp.mean(x2 * x2, axis=-1, keepdims=True) + RMS_EPS) * g_ref[...]


def _final(x1, y1, y2, rt, g, tm=256):
    n, d = x1.shape
    row = lambda i: (i, 0)
    return pl.pallas_call(
        _final_kernel,
        grid=(n // tm,),
        in_specs=[pl.BlockSpec((tm, d), row), pl.BlockSpec((tm, d), row), pl.BlockSpec((tm, d), row),
                  pl.BlockSpec((tm, LANES), row), pl.BlockSpec((1, d), lambda i: (0, 0))],
        out_specs=pl.BlockSpec((tm, d), row),
        out_shape=jax.ShapeDtypeStruct((n, d), F32),
        compiler_params=_cparams("parallel"),
        name="combine_norm",
    )(x1, y1, y2, rt, g)


def _dispatch_plan(rt, counts, n_tokens):
    tm = MOE_TILE
    n_rows = 2 * n_tokens + N_EXPERTS * tm
    n_tiles = n_rows // tm
    ids = rt[:, 0:6].astype(jnp.int32)
    ef = jnp.concatenate([ids[:, 0], ids[:, 1]])
    rank = jnp.concatenate([ids[:, 4], ids[:, 5]])
    counts = counts[0, :N_EXPERTS].astype(jnp.int32)
    tiles_per = (counts + tm - 1) // tm
    tile_end = jnp.cumsum(tiles_per)
    row_start = (tile_end - tiles_per) * tm
    pos = row_start[ef] + rank
    tok = jnp.concatenate([jnp.arange(n_tokens, dtype=jnp.int32)] * 2)
    src = jnp.zeros((n_rows,), jnp.int32).at[pos].set(tok)
    n_used = tile_end[-1]
    tile_ids = jnp.arange(n_tiles, dtype=jnp.int32)
    tile_expert = jnp.sum((tile_ids[:, None] >= tile_end[None, :]).astype(jnp.int32), axis=1)
    last_expert = jnp.sum((n_used - 1 >= tile_end).astype(jnp.int32))
    tile_expert = jnp.where(tile_ids < n_used, tile_expert, last_expert).astype(jnp.int32)
    first = jnp.concatenate([jnp.ones((1,), jnp.int32),
                             (tile_expert[1:] != tile_expert[:-1]).astype(jnp.int32)])
    return (src, pos[:n_tokens], pos[n_tokens:], tile_expert, first,
            n_used.reshape(1).astype(jnp.int32))


def kernel(x, norm_mix_g, w_in, hgrn_lb_logits, hgrn_out_norm_g, w_out, norm_ffn_g, w_group_router,
           b_group_router, w_expert_router, b_expert_router, w_gate, w_up, w_down, final_norm_g):
    b, t, d = x.shape
    n = b * t
    depth = w_in.shape[0]
    assert depth == 1, "the final norm is fused into the combine step of the only layer"
    attn_w = d // 2
    n_heads = attn_w // HEAD_DIM
    x2 = x.reshape(n, d)
    lb_all = jnp.cumsum(jax.nn.softmax(hgrn_lb_logits.astype(F32), axis=0), axis=0)[:depth]
    slope2 = jnp.asarray(2.0 ** (-8.0 * np.arange(1, n_heads + 1) / n_heads), dtype=F32) * F32(LOG2E)
    s_hi = slope2.astype(BF16).astype(F32)
    s_mid = (slope2 - s_hi).astype(BF16).astype(F32)
    s_lo = (slope2 - s_hi - s_mid).astype(BF16).astype(F32)
    slope_pieces = jnp.stack([s_hi, s_mid, s_lo])
    qkv_scale = jnp.concatenate([jnp.full((attn_w,), HEAD_DIM ** -0.5 * LOG2E, F32),
                                 jnp.ones((2 * attn_w,), F32)]).reshape(1, -1)

    for l in range(depth):
        g_mix = norm_mix_g[l].reshape(1, d)
        x2, w_in_l, w_out_l = lax.optimization_barrier(
            (x2, w_in[l].astype(BF16), w_out[l].astype(BF16)))
        qkv = _inproj(x2, g_mix, w_in_l, 0, qkv_scale, BF16)
        hp = _inproj(x2, g_mix, w_in_l, 3 * attn_w, jnp.ones((1, 4 * (d - attn_w)), F32), F32)
        o_attn = _moba(qkv.reshape(b, t, 3 * attn_w), slope_pieces, n_heads)
        o_rec = _hgrn(hp.reshape(b, t, -1), lb_all[l].reshape(1, -1),
                      hgrn_out_norm_g[l].reshape(1, HEAD_DIM), n_heads)

        wr = jnp.concatenate(
            [w_group_router[l],
             jnp.transpose(w_expert_router[l], (1, 0, 2)).reshape(d, N_EXPERTS)], axis=1)
        wr = jnp.pad(wr, ((0, 0), (0, LANES - wr.shape[1])))
        wr_hi = wr.astype(BF16)
        wr = jnp.concatenate([wr_hi, (wr - wr_hi.astype(F32)).astype(BF16)], axis=1)
        br =jnp.concatenate([b_group_router[l], b_expert_router[l].reshape(-1)])
        br = jnp.pad(br, (0, LANES - br.shape[0])).reshape(1, LANES)
        x1, h2, rt, counts = _outproj(o_attn.reshape(n, attn_w), o_rec.reshape(n, -1), x2,
                                      w_out_l, norm_ffn_g[l].reshape(1, d), wr, br)
        src, pos1, pos2, tile_expert, first, n_used = _dispatch_plan(rt, counts, n)
        ys = _moe(tile_expert, n_used, first, src, h2, w_gate[l], w_up[l], w_down[l])
        y1 = ys.at[pos1].get(mode="promise_in_bounds")
        y2 = ys.at[pos2].get(mode="promise_in_bounds")
        x2 = _final(x1, y1, y2, rt, final_norm_g.reshape(1, d))
    return x2.reshape(b, t, d)
```

```python
import functools

import jax
import jax.numpy as jnp
import numpy as np
from jax import lax
from jax.experimental import pallas as pl
from jax.experimental.pallas import tpu as pltpu

F32 = jnp.float32
BF16 = jnp.bfloat16
HIGHEST = lax.Precision.HIGHEST

HEAD_DIM = 128
MOBA_BLOCK = 256
MOBA_TOPK = 3
N_GROUPS = 4
EXPERTS_PER_GROUP = 4
N_EXPERTS = N_GROUPS * EXPERTS_PER_GROUP
RMS_EPS = 1e-6

LANES = 128
VMEM_LIMIT_BYTES = 56 * 1024 * 1024

HGRN_CHUNK = 64
HGRN_SUB = 16
HGRN_STEP = 256
HGRN_HEADS_PER_STEP = 8
HGRN_MIN_SAFE_F = 2.0 ** -7
MOE_TILE = 256
NT_DIMS = (((1,), (1,)), ((), ()))
TN_DIMS = (((0,), (0,)), ((), ()))


def _cparams(*sem):
    return pltpu.CompilerParams(dimension_semantics=sem, vmem_limit_bytes=VMEM_LIMIT_BYTES)


def _inproj_kernel(x_ref, g_ref, w_ref, cs_ref, o_ref, hn_ref):
    @pl.when(pl.program_id(1) == 0)
    def _():
        x = x_ref[...]
        ms = jnp.mean(x * x, axis=-1, keepdims=True)
        hn_ref[...] = (x * lax.rsqrt(ms + RMS_EPS) * g_ref[...]).astype(BF16)

    acc = jnp.dot(hn_ref[...], w_ref[...], preferred_element_type=F32)
    o_ref[...] = (acc * cs_ref[...]).astype(o_ref.dtype)


def _inproj(x2, g, w, col0, colscale, out_dtype, tm=1024, tn=1024):
    n, d = x2.shape
    cols = colscale.shape[1]
    j0 = col0 // tn
    return pl.pallas_call(
        _inproj_kernel,
        grid=(n // tm, cols // tn),
        in_specs=[
            pl.BlockSpec((tm, d), lambda i, j: (i, 0)),
            pl.BlockSpec((1, d), lambda i, j: (0, 0)),
            pl.BlockSpec((d, tn), lambda i, j: (0, j0 + j)),
            pl.BlockSpec((1, tn), lambda i, j: (0, j)),
        ],
        out_specs=pl.BlockSpec((tm, tn), lambda i, j: (i, j)),
        out_shape=jax.ShapeDtypeStruct((n, cols), out_dtype),
        scratch_shapes=[pltpu.VMEM((tm, d), BF16)],
        compiler_params=_cparams("parallel", "arbitrary"),
        name="inproj",
    )(x2, g, w, colscale)


MOBA_HEADS_PER_STEP = 4
MOBA_EXTRA_POS = 0
MOBA_EXTRA_SEL = 8
MASK_BIG = 2.0 ** 60
LOG2E = 1.4426950408889634


def _moba_key_extras(t):
    nb = t // MOBA_BLOCK
    assert MOBA_EXTRA_SEL + nb <= LANES and MOBA_BLOCK <= 256
    pos = np.arange(t)
    kx = np.zeros((t, LANES), np.float32)
    kx[:, MOBA_EXTRA_POS:MOBA_EXTRA_POS + 3] = ((pos // MOBA_BLOCK) * MOBA_BLOCK)[:, None]
    kx[:, MOBA_EXTRA_POS + 3:MOBA_EXTRA_POS + 6] = (pos % MOBA_BLOCK)[:, None]
    kx[pos, MOBA_EXTRA_SEL + pos // MOBA_BLOCK] = -MASK_BIG
    return jnp.asarray(kx, dtype=BF16)


def _moba_kernel(sl_ref, q_ref, k_ref, v_ref, kx_ref, o_ref, kaug_ref, kmean_ref, *, nb):
    hg = pl.program_id(1)
    i = pl.program_id(2)
    bs = MOBA_BLOCK
    dh = HEAD_DIM
    nbp = kmean_ref.shape[1] // 4
    nx = 8 + nbp

    @pl.when(i == 0)
    def _():
        for hh in range(MOBA_HEADS_PER_STEP):
            kaug_ref[hh, :, 0:dh] = k_ref[0, :, hh * dh:(hh + 1) * dh]
            kaug_ref[hh, :, dh:] = kx_ref[...]
            rows = [jnp.mean(k_ref[0, j * bs:(j + 1) * bs, hh * dh:(hh + 1) * dh].astype(F32),
                             axis=0, keepdims=True) for j in range(nb)]
            if nbp > nb:
                rows.append(jnp.zeros((nbp - nb, dh), F32))
            km = jnp.concatenate(rows, axis=0)
            hi = km.astype(BF16)
            mid = (km - hi.astype(F32)).astype(BF16)
            lo = (km - hi.astype(F32) - mid.astype(F32)).astype(BF16)
            kmean_ref[hh] = jnp.concatenate([hi, mid, lo, jnp.zeros_like(hi)], axis=0)

    r8 = lax.broadcasted_iota(jnp.int32, (8, bs), 0)
    piece = r8 % 3
    er = lax.broadcasted_iota(jnp.int32, (nx, LANES), 0)
    ec = lax.broadcasted_iota(jnp.int32, (nx, LANES), 1)
    embed = jnp.where(er == ec, 1.0, 0.0).astype(BF16)

    qas = []
    for hh in range(MOBA_HEADS_PER_STEP):
        h = hg * MOBA_HEADS_PER_STEP + hh
        q = q_ref[0, :, hh * dh:(hh + 1) * dh]
        g4 = lax.dot_general(kmean_ref[hh], q, NT_DIMS, preferred_element_type=F32)
        gate_t = g4[0:nbp] + g4[nbp:2 * nbp] + g4[2 * nbp:3 * nbp]
        blk = lax.broadcasted_iota(jnp.int32, gate_t.shape, 0)
        rank = jnp.zeros(gate_t.shape, F32)
        for jp in range(nb - 1):
            other = gate_t[jp:jp + 1, :]
            beats = (other > gate_t) | ((other == gate_t) & (blk > jp))
            rank = rank + jnp.where(beats, (jp < i).astype(F32), 0.0)
        notsel_t = jnp.where((blk < i) & (rank >= MOBA_TOPK), 1.0, 0.0)
        slope_rows = jnp.where(r8 >= 6, 0.0,
                               jnp.where(piece == 0, sl_ref[0, h],
                                         jnp.where(piece == 1, sl_ref[1, h], sl_ref[2, h])))
        qx_t = jnp.concatenate([slope_rows, notsel_t], axis=0).astype(BF16)
        qx = lax.dot_general(qx_t, embed, TN_DIMS, preferred_element_type=F32)
        qas.append(jnp.concatenate([q, qx.astype(BF16)], axis=1))

    rowi = lax.broadcasted_iota(jnp.int32, (bs, bs), 0)
    coli = lax.broadcasted_iota(jnp.int32, (bs, bs), 1)

    for c in range(nb):
        @pl.when(i == c)
        def _(c=c):
            n = (c + 1) * bs
            for hh in range(MOBA_HEADS_PER_STEP):
                cs = slice(hh * dh, (hh + 1) * dh)
                s = lax.dot_general(qas[hh], kaug_ref[hh, 0:n, :], NT_DIMS,
                                    preferred_element_type=F32)
                s_own = jnp.where(rowi >= coli, s[:, c * bs:], -jnp.inf)
                m = jnp.max(s_own, axis=1, keepdims=True)
                if c > 0:
                    s_past = s[:, :c * bs]
                    m = jnp.maximum(m, jnp.max(s_past, axis=1, keepdims=True))
                p_own = jnp.exp2(s_own - m)
                l = jnp.sum(p_own, axis=1, keepdims=True)
                acc = jnp.dot(p_own.astype(BF16), v_ref[0, c * bs:n, cs],
                              preferred_element_type=F32)
                if c > 0:
                    p_past = jnp.exp2(s_past - m)
                    l = l + jnp.sum(p_past, axis=1, keepdims=True)
                    acc = acc + jnp.dot(p_past.astype(BF16), v_ref[0, 0:c * bs, cs],
                                        preferred_element_type=F32)
                o_ref[0, :, cs] = (acc / l).astype(o_ref.dtype)


def _moba(qkv, slope_pieces, n_heads):
    b, t, _ = qkv.shape
    bs = MOBA_BLOCK
    nb = t // bs
    nbp = -(-nb // 8) * 8
    hps = MOBA_HEADS_PER_STEP
    assert n_heads % hps == 0
    ng = n_heads // hps
    w = hps * HEAD_DIM
    kern = functools.partial(_moba_kernel, nb=nb)
    return pl.pallas_call(
        kern,
        grid=(b, ng, nb),
        in_specs=[
            pl.BlockSpec(memory_space=pltpu.SMEM),
            pl.BlockSpec((1, bs, w), lambda bi, hi, qi: (bi, qi, hi)),
            pl.BlockSpec((1, t, w), lambda bi, hi, qi: (bi, 0, ng + hi)),
            pl.BlockSpec((1, t, w), lambda bi, hi, qi: (bi, 0, 2 * ng + hi)),
            pl.BlockSpec((t, LANES), lambda bi, hi, qi: (0, 0)),
        ],
        out_specs=pl.BlockSpec((1, bs, w), lambda bi, hi, qi: (bi, qi, hi)),
        out_shape=jax.ShapeDtypeStruct((b, t, n_heads * HEAD_DIM), BF16),
        scratch_shapes=[
            pltpu.VMEM((hps, t, HEAD_DIM + LANES), BF16),
            pltpu.VMEM((hps, 4 * nbp, HEAD_DIM), BF16),
        ],
        compiler_params=_cparams("parallel", "parallel", "arbitrary"),
        name="moba",
    )(slope_pieces, qkv, qkv, qkv, _moba_key_extras(t))


def _sigmoid(x):
    return 1.0 / (1.0 + jnp.exp(-x))


def _hgrn_chunk(q, f, iv, g, gn, st, bounded_decay):
    c = HGRN_CHUNK
    sub = HGRN_SUB
    half = sub // 2
    nsub = c // sub
    qf = q * _sigmoid(q)
    kf = jnp.maximum(1.0 - f, 0.0)
    r = lax.broadcasted_iota(jnp.int32, (c, c), 0)
    cc = lax.broadcasted_iota(jnp.int32, (c, c), 1)
    tril = jnp.where(r >= cc, 1.0, 0.0).astype(BF16)
    lf = jnp.log2(f)
    dk = lf.shape[1]
    lf_hi = lf.astype(BF16)
    lf_mid = (lf - lf_hi.astype(F32)).astype(BF16)
    lf_lo = (lf - lf_hi.astype(F32) - lf_mid.astype(F32)).astype(BF16)
    b3 = jnp.dot(tril, jnp.concatenate([lf_hi, lf_mid, lf_lo], axis=1),
                 preferred_element_type=F32)
    b2 = b3[:, :dk] + b3[:, dk:2 * dk] + b3[:, 2 * dk:]
    c2 = b2 - jnp.log2(kf)

    inter = lax.dot_general((qf * jnp.exp2(b2)).astype(BF16), st.astype(BF16), NT_DIMS,
                            preferred_element_type=F32)

    lane = lax.broadcasted_iota(jnp.int32, (sub, LANES), 1)
    tsub = lax.broadcasted_iota(jnp.int32, (sub, LANES), 0)
    colid = lax.broadcasted_iota(jnp.int32, (sub, c), 1)
    ones = jnp.ones((LANES, LANES), BF16)
    a_rows = []
    for bi in range(nsub):
        lo = bi * sub
        b_i = b2[lo:lo + sub]
        c_i = c2[lo:lo + sub]
        q_i = qf[lo:lo + sub]
        if bounded_decay:
            qt = q_i * (jnp.exp2(b_i - b2[lo - 1:lo, :]) if bi > 0 else jnp.exp2(b_i))
            kt = jnp.exp2((b2[lo - 1:lo, :] - c2[:lo + sub]) if bi > 0 else -c2[:sub])
            if lo + sub < c:
                kt = jnp.concatenate([kt, jnp.zeros((c - lo - sub, kt.shape[1]), F32)], axis=0)
            cross = lax.dot_general(qt.astype(BF16), kt.astype(BF16), NT_DIMS,
                                    preferred_element_type=F32)
            a_rows.append(jnp.where(colid <= lo + tsub[:, :c], cross, 0.0))
            continue
        pieces = []
        for s in range(sub):
            if s < half:
                pieces.append(q_i * jnp.exp2(b_i - c_i[s:s + 1, :]))
            else:
                pieces.append(q_i[half:] * jnp.exp2(b_i[half:] - c_i[s:s + 1, :]))
        pm = jnp.concatenate(pieces, axis=0).astype(BF16)
        rs = jnp.dot(pm, ones, preferred_element_type=F32)
        rel = lane - lo
        key = jnp.where((rel >= 0) & (rel <= tsub), rel, -1)
        key_lo, key_hi = key[:half], key[half:]
        a_lo = jnp.zeros((half, LANES), F32)
        a_hi = jnp.zeros((half, LANES), F32)
        off = 0
        for s in range(sub):
            if s < half:
                a_lo = jnp.where(key_lo == s, rs[off:off + half], a_lo)
                a_hi = jnp.where(key_hi == s, rs[off + half:off + sub], a_hi)
                off += sub
            else:
                a_hi = jnp.where(key_hi == s, rs[off:off + half], a_hi)
                off += half
        a_blk = jnp.concatenate([a_lo, a_hi], axis=0)[:, :c]
        if bi > 0:
            b0 = b2[lo - 1:lo, :]
            qt = q_i * jnp.exp2(b_i - b0)
            kt = jnp.exp2(jnp.minimum(b0 - c2, 0.0))
            cross = lax.dot_general(qt.astype(BF16), kt.astype(BF16), NT_DIMS,
                                    preferred_element_type=F32)
            a_blk = a_blk + jnp.where(colid < lo, cross, 0.0)
        a_rows.append(a_blk)
    a = jnp.concatenate(a_rows, axis=0)
    intra = jnp.dot(a.astype(BF16), iv.astype(BF16), preferred_element_type=F32)

    b_last = b2[c - 1:c, :]
    khat = jnp.exp2(b_last - c2)
    st_new = st * jnp.exp2(b_last) + lax.dot_general(
        iv.astype(BF16), khat.astype(BF16), TN_DIMS, preferred_element_type=F32)

    o = inter + intra
    y = o * lax.rsqrt(jnp.mean(o * o, axis=-1, keepdims=True) + RMS_EPS) * gn
    return y * (g * _sigmoid(g)), st_new


def _hgrn_kernel(q_ref, f_ref, i_ref, g_ref, lb_ref, gn_ref, o_ref, st_ref):
    @pl.when(pl.program_id(2) == 0)
    def _():
        st_ref[...] = jnp.zeros_like(st_ref)

    gn = gn_ref[...]
    d = HEAD_DIM
    lb = lb_ref[...]
    f_all = lb + (1.0 - lb) * _sigmoid(f_ref[0])
    f_min = jnp.min(f_all)

    def run(bounded_decay):
        for hh in range(HGRN_HEADS_PER_STEP):
            cs = slice(hh * d, (hh + 1) * d)
            st = st_ref[hh]
            for ci in range(HGRN_STEP // HGRN_CHUNK):
                sl = slice(ci * HGRN_CHUNK, (ci + 1) * HGRN_CHUNK)
                out, st = _hgrn_chunk(q_ref[0, sl, cs], f_all[sl, cs], i_ref[0, sl, cs],
                                      g_ref[0, sl, cs], gn, st, bounded_decay)
                o_ref[0, sl, cs] = out.astype(o_ref.dtype)
            st_ref[hh] = st

    @pl.when(f_min >= HGRN_MIN_SAFE_F)
    def _():
        run(True)

    @pl.when(jnp.logical_not(f_min >= HGRN_MIN_SAFE_F))
    def _():
        run(False)


def _hgrn(hp, lb, gn, n_heads):
    b, t, _ = hp.shape
    d = HEAD_DIM
    ts = HGRN_STEP
    hps = HGRN_HEADS_PER_STEP
    assert n_heads % hps == 0
    ng = n_heads // hps

    def col(group):
        return pl.BlockSpec((1, ts, hps * d), lambda bi, hi, ti: (bi, ti, group * ng + hi))

    return pl.pallas_call(
        _hgrn_kernel,
        grid=(b, ng, t // ts),
        in_specs=[col(0), col(1), col(2), col(3),
                  pl.BlockSpec((1, hps * d), lambda bi, hi, ti: (0, hi)),
                  pl.BlockSpec((1, d), lambda bi, hi, ti: (0, 0))],
        out_specs=pl.BlockSpec((1, ts, hps * d), lambda bi, hi, ti: (bi, ti, hi)),
        out_shape=jax.ShapeDtypeStruct((b, t, n_heads * d), BF16),
        scratch_shapes=[pltpu.VMEM((hps, d, d), F32)],
        compiler_params=_cparams("parallel", "parallel", "arbitrary"),
        name="hgrn2",
    )(hp, hp, hp, hp, lb, gn)


def _outproj_kernel(oa_ref, or_ref, x_ref, w_ref, g_ref, wr_ref, br_ref,
                    x1_ref, h2_ref, rt_ref, cnt_out_ref, cnt_ref):
    wa = oa_ref.shape[1]
    x1 = (x_ref[...]
          + jnp.dot(oa_ref[...], w_ref[0:wa, :], preferred_element_type=F32)
          + jnp.dot(or_ref[...], w_ref[wa:, :], preferred_element_type=F32))
    x1_ref[...] = x1
    h2 = x1 * lax.rsqrt(jnp.mean(x1 * x1, axis=-1, keepdims=True) + RMS_EPS) * g_ref[...]
    h2_ref[...] = h2

    h_hi = h2.astype(BF16)
    h_mid = (h2 - h_hi.astype(F32)).astype(BF16)
    part = jnp.dot(h_hi, wr_ref[...], preferred_element_type=F32)
    logits = (part[:, :LANES] + part[:, LANES:] + br_ref[...]
              + jnp.dot(h_mid, wr_ref[:, :LANES], preferred_element_type=F32))
    lane = lax.broadcasted_iota(jnp.int32, logits.shape, 1)
    big = jnp.int32(4 * LANES)
    ninf = -jnp.inf

    lg = jnp.where(lane < N_GROUPS, logits, ninf)
    mg = jnp.max(lg, axis=1, keepdims=True)
    gidx = jnp.min(jnp.where(lg == mg, lane, big), axis=1, keepdims=True)
    grp_w = 1.0 / jnp.sum(jnp.exp(lg - mg), axis=1, keepdims=True)

    lo = N_GROUPS + EXPERTS_PER_GROUP * gidx
    le = jnp.where((lane >= lo) & (lane < lo + EXPERTS_PER_GROUP), logits, ninf)
    m1 = jnp.max(le, axis=1, keepdims=True)
    i1 = jnp.min(jnp.where(le == m1, lane, big), axis=1, keepdims=True)
    le2 = jnp.where(lane == i1, ninf, le)
    m2 = jnp.max(le2, axis=1, keepdims=True)
    i2 = jnp.min(jnp.where(le2 == m2, lane, big), axis=1, keepdims=True)
    r21 = jnp.exp(m2 - m1)
    w1 = grp_w / (1.0 + r21)
    w2 = grp_w * r21 / (1.0 + r21)
    e1 = i1 - N_GROUPS
    e2 = i2 - N_GROUPS

    @pl.when(pl.program_id(0) == 0)
    def _():
        cnt_ref[...] = jnp.zeros_like(cnt_ref)

    tm = logits.shape[0]
    onehot = jnp.where((lane == e1) | (lane == e2), 1.0, 0.0)
    rr = lax.broadcasted_iota(jnp.int32, (tm, tm), 0)
    rc = lax.broadcasted_iota(jnp.int32, (tm, tm), 1)
    before = jnp.where(rr > rc, 1.0, 0.0).astype(BF16)
    prefix = (jnp.dot(before, onehot.astype(BF16), preferred_element_type=F32) + cnt_ref[0:1, :])
    rank1 = jnp.sum(jnp.where(lane == e1, prefix, 0.0), axis=1, keepdims=True)
    rank2 = jnp.sum(jnp.where(lane == e2, prefix, 0.0), axis=1, keepdims=True)
    total = cnt_ref[0:1, :] + jnp.sum(onehot, axis=0, keepdims=True)
    cnt_ref[0:1, :] = total
    cnt_out_ref[...] = jnp.broadcast_to(total, cnt_out_ref.shape)

    cols = [e1.astype(F32), e2.astype(F32), w1, w2, rank1, rank2]
    rt = jnp.zeros(logits.shape, F32)
    for ci, cv in enumerate(cols):
        rt = jnp.where(lane == ci, cv, rt)
    rt_ref[...] = rt


def _outproj(oa, orec, x2, w_out, g, wr, br, tm=256):
    n, d = x2.shape
    wa = oa.shape[1]
    wrc = orec.shape[1]
    row = lambda i: (i, 0)
    const = lambda i: (0, 0)
    return pl.pallas_call(
        _outproj_kernel,
        grid=(n // tm,),
        in_specs=[
            pl.BlockSpec((tm, wa), row),
            pl.BlockSpec((tm, wrc), row),
            pl.BlockSpec((tm, d), row),
            pl.BlockSpec((wa + wrc, d), const),
            pl.BlockSpec((1, d), const),
            pl.BlockSpec((d, 2 * LANES), const),
            pl.BlockSpec((1, LANES), const),
        ],
        out_specs=[pl.BlockSpec((tm, d), row), pl.BlockSpec((tm, d), row),
                   pl.BlockSpec((tm, LANES), row), pl.BlockSpec((8, LANES), const)],
        out_shape=[jax.ShapeDtypeStruct((n, d), F32), jax.ShapeDtypeStruct((n, d), F32),
                   jax.ShapeDtypeStruct((n, LANES), F32), jax.ShapeDtypeStruct((8, LANES), F32)],
        scratch_shapes=[pltpu.VMEM((8, LANES), F32)],
        compiler_params=_cparams("arbitrary"),
        name="outproj_route",
    )(oa, orec, x2, w_out, g, wr, br)


WEIGHT_CAST_ROWS = 256


def _cast_rows(dst_ref, src_ref):
    rows = dst_ref.shape[0]
    step = min(rows, WEIGHT_CAST_ROWS)
    assert rows % step == 0

    def body(c, carry):
        sl = pl.ds(pl.multiple_of(c * step, step), step)
        dst_ref[sl, :] = src_ref[0, sl, :].astype(BF16)
        return carry

    lax.fori_loop(0, rows // step, body, 0)


def _moe_up_kernel(te_ref, nu_ref, first_ref, src_ref, h2_hbm, wg_ref, wu_ref, hid_ref,
                   xbuf, sems, wgb, wub):
    s = pl.program_id(0)
    nu = nu_ref[0]
    tm = MOE_TILE

    def gather_row(tile, slot, r):
        tok = src_ref[tile * tm + r]
        return pltpu.make_async_copy(h2_hbm.at[pl.ds(tok, 1), :], xbuf.at[slot, pl.ds(r, 1), :],
                                     sems.at[slot])

    def start_gather(tile):
        slot = tile % 2
        for r in range(tm):
            gather_row(tile, slot, r).start()

    def wait_gather(tile):
        slot = tile % 2
        pltpu.make_async_copy(h2_hbm.at[pl.ds(0, tm), :], xbuf.at[slot], sems.at[slot]).wait()

    def gate_up(tile):
        x = xbuf[tile % 2].astype(BF16)
        a = jnp.dot(x, wgb[...], preferred_element_type=F32)
        u = jnp.dot(x, wub[...], preferred_element_type=F32)
        hid_ref[...] = (a * _sigmoid(a) * u).astype(BF16)

    @pl.when(s == 0)
    def _():
        start_gather(s)

    @pl.when((s >= 1) & (s <= nu) & (first_ref[jnp.maximum(s - 1, 0)] == 1))
    def _():
        _cast_rows(wgb, wg_ref)
        _cast_rows(wub, wu_ref)

    @pl.when((s >= 1) & (s < nu))
    def _():
        start_gather(s)
        wait_gather(s - 1)
        gate_up(s - 1)

    @pl.when((s >= 1) & (s == nu))
    def _():
        wait_gather(s - 1)
        gate_up(s - 1)

    @pl.when(s > nu)
    def _():
        hid_ref[...] = jnp.zeros_like(hid_ref)


def _moe_down_kernel(te_ref, nu_ref, first_ref, hid_ref, wd_ref, y_ref, wdb):
    n = pl.program_id(0)

    @pl.when((n < nu_ref[0]) & (first_ref[n] == 1))
    def _():
        _cast_rows(wdb, wd_ref)

    @pl.when(n < nu_ref[0])
    def _():
        y_ref[...] = jnp.dot(hid_ref[...], wdb[...], preferred_element_type=F32)

    @pl.when(n >= nu_ref[0])
    def _():
        y_ref[...] = jnp.zeros_like(y_ref)


def _moe(tile_expert, n_used, first, src, h2, wg, wu, wd):
    p = src.shape[0]
    d, f = wg.shape[1:]
    tm = MOE_TILE
    n_tiles = p // tm
    prev = lambda s: jnp.maximum(s - 1, 0)
    hid = pl.pallas_call(
        _moe_up_kernel,
        grid_spec=pltpu.PrefetchScalarGridSpec(
            num_scalar_prefetch=4,
            grid=(n_tiles + 1,),
            in_specs=[
                pl.BlockSpec(memory_space=pl.ANY),
                pl.BlockSpec((1, d, f), lambda s, te, nu, fi, sr: (te[prev(s)], 0, 0)),
                pl.BlockSpec((1, d, f), lambda s, te, nu, fi, sr: (te[prev(s)], 0, 0)),
            ],
            out_specs=pl.BlockSpec((tm, f), lambda s, te, nu, fi, sr: (prev(s), 0)),
            scratch_shapes=[pltpu.VMEM((2, tm, d), F32), pltpu.SemaphoreType.DMA((2,)),
                            pltpu.VMEM((d, f), BF16), pltpu.VMEM((d, f), BF16)],
        ),
        out_shape=jax.ShapeDtypeStruct((p, f), BF16),
        compiler_params=_cparams("arbitrary"),
        name="moe_gate_up",
    )(tile_expert, n_used, first, src, h2, wg, wu)
    return pl.pallas_call(
        _moe_down_kernel,
        grid_spec=pltpu.PrefetchScalarGridSpec(
            num_scalar_prefetch=3,
            grid=(n_tiles,),
            in_specs=[
                pl.BlockSpec((tm, f), lambda n, te, nu, fi: (n, 0)),
                pl.BlockSpec((1, f, d), lambda n, te, nu, fi: (te[n], 0, 0)),
            ],
            out_specs=pl.BlockSpec((tm, d), lambda n, te, nu, fi: (n, 0)),
            scratch_shapes=[pltpu.VMEM((f, d), BF16)],
        ),
        out_shape=jax.ShapeDtypeStruct((p, d), F32),
        compiler_params=_cparams("arbitrary"),
        name="moe_down",
    )(tile_expert, n_used, first, hid, wd)


COMBINE_TILE = 256


def _final_kernel(pos_ref, x1_ref, ys_hbm, rt_ref, g_ref, o_ref, ybuf, sems):
    s = pl.program_id(0)
    n_tiles = pl.num_programs(0) - 1
    tm = COMBINE_TILE
    n_tok = pos_ref.shape[0] // 2

    def start_gather(tile):
        slot = tile % 2
        for k in range(2):
            for r in range(tm):
                row = pos_ref[k * n_tok + tile * tm + r]
                pltpu.make_async_copy(ys_hbm.at[pl.ds(row, 1), :],
                                      ybuf.at[slot, k, pl.ds(r, 1), :], sems.at[slot]).start()

    def wait_gather(tile):
        slot = tile % 2
        for k in range(2):
            pltpu.make_async_copy(ys_hbm.at[pl.ds(0, tm), :], ybuf.at[slot, k], sems.at[slot]).wait()

    def finish(tile):
        slot = tile % 2
        rt = rt_ref[...]
        x2 = x1_ref[...] + rt[:, 2:3] * ybuf[slot, 0] + rt[:, 3:4] * ybuf[slot, 1]
        o_ref[...] = (x2 * lax.rsqrt(jnp.mean(x2 * x2, axis=-1, keepdims=True) + RMS_EPS)
                      * g_ref[...])

    @pl.when(s == 0)
    def _():
        start_gather(s)

    @pl.when((s >= 1) & (s < n_tiles))
    def _():
        start_gather(s)
        wait_gather(s - 1)
        finish(s - 1)

    @pl.when(s == n_tiles)
    def _():
        wait_gather(s - 1)
        finish(s - 1)


def _final(x1, ys, pos, rt, g):
    n, d = x1.shape
    tm = COMBINE_TILE
    prev = lambda s: jnp.maximum(s - 1, 0)
    return pl.pallas_call(
        _final_kernel,
        grid_spec=pltpu.PrefetchScalarGridSpec(
            num_scalar_prefetch=1,
            grid=(n // tm + 1,),
            in_specs=[pl.BlockSpec((tm, d), lambda s, p: (prev(s), 0)),
                      pl.BlockSpec(memory_space=pl.ANY),
                      pl.BlockSpec((tm, LANES), lambda s, p: (prev(s), 0)),
                      pl.BlockSpec((1, d), lambda s, p: (0, 0))],
            out_specs=pl.BlockSpec((tm, d), lambda s, p: (prev(s), 0)),
            scratch_shapes=[pltpu.VMEM((2, 2, tm, d), F32), pltpu.SemaphoreType.DMA((2,))],
        ),
        out_shape=jax.ShapeDtypeStruct((n, d), F32),
        compiler_params=_cparams("arbitrary"),
        name="combine_norm",
    )(pos, x1, ys, rt, g)


def _dispatch_plan(rt, counts, n_tokens):
    tm = MOE_TILE
    n_rows = 2 * n_tokens + N_EXPERTS * tm
    n_tiles = n_rows // tm
    ids = rt[:, 0:6].astype(jnp.int32)
    ef = jnp.concatenate([ids[:, 0], ids[:, 1]])
    rank = jnp.concatenate([ids[:, 4], ids[:, 5]])
    counts = counts[0, :N_EXPERTS].astype(jnp.int32)
    tiles_per = (counts + tm - 1) // tm
    tile_end = jnp.cumsum(tiles_per)
    row_start = (tile_end - tiles_per) * tm
    pos = row_start[ef] + rank
    tok = jnp.concatenate([jnp.arange(n_tokens, dtype=jnp.int32)] * 2)
    src = jnp.zeros((n_rows,), jnp.int32).at[pos].set(tok)
    n_used = tile_end[-1]
    tile_ids = jnp.arange(n_tiles, dtype=jnp.int32)
    tile_expert = jnp.sum((tile_ids[:, None] >= tile_end[None, :]).astype(jnp.int32), axis=1)
    last_expert = jnp.sum((n_used - 1 >= tile_end).astype(jnp.int32))
    tile_expert = jnp.where(tile_ids < n_used, tile_expert, last_expert).astype(jnp.int32)
    first = jnp.concatenate([jnp.ones((1,), jnp.int32),
                             (tile_expert[1:] != tile_expert[:-1]).astype(jnp.int32)])
    return (src, pos[:n_tokens], pos[n_tokens:], tile_expert, first,
            n_used.reshape(1).astype(jnp.int32))


def kernel(x, norm_mix_g, w_in, hgrn_lb_logits, hgrn_out_norm_g, w_out, norm_ffn_g, w_group_router,
           b_group_router, w_expert_router, b_expert_router, w_gate, w_up, w_down, final_norm_g):
    b, t, d = x.shape
    n = b * t
    depth = w_in.shape[0]
    assert depth == 1, "the final norm is fused into the combine step of the only layer"
    attn_w = d // 2
    n_heads = attn_w // HEAD_DIM
    x2 = x.reshape(n, d)
    lb_all = jnp.cumsum(jax.nn.softmax(hgrn_lb_logits.astype(F32), axis=0), axis=0)[:depth]
    slope2 = jnp.asarray(2.0 ** (-8.0 * np.arange(1, n_heads + 1) / n_heads), dtype=F32) * F32(LOG2E)
    s_hi = slope2.astype(BF16).astype(F32)
    s_mid = (slope2 - s_hi).astype(BF16).astype(F32)
    s_lo = (slope2 - s_hi - s_mid).astype(BF16).astype(F32)
    slope_pieces = jnp.stack([s_hi, s_mid, s_lo])
    qkv_scale = jnp.concatenate([jnp.full((attn_w,), HEAD_DIM ** -0.5 * LOG2E, F32),
                                 jnp.ones((2 * attn_w,), F32)]).reshape(1, -1)

    for l in range(depth):
        g_mix = norm_mix_g[l].reshape(1, d)
        x2, w_in_l, w_out_l = lax.optimization_barrier(
            (x2, w_in[l].astype(BF16), w_out[l].astype(BF16)))
        qkv = _inproj(x2, g_mix, w_in_l, 0, qkv_scale, BF16)
        hp = _inproj(x2, g_mix, w_in_l, 3 * attn_w, jnp.ones((1, 4 * (d - attn_w)), F32), F32)
        o_attn = _moba(qkv.reshape(b, t, 3 * attn_w), slope_pieces, n_heads)
        o_rec = _hgrn(hp.reshape(b, t, -1), lb_all[l].reshape(1, -1),
                      hgrn_out_norm_g[l].reshape(1, HEAD_DIM), n_heads)

        wr = jnp.concatenate(
            [w_group_router[l],
             jnp.transpose(w_expert_router[l], (1, 0, 2)).reshape(d, N_EXPERTS)], axis=1)
        wr = jnp.pad(wr, ((0, 0), (0, LANES - wr.shape[1])))
        wr_hi = wr.astype(BF16)
        wr = jnp.concatenate([wr_hi, (wr - wr_hi.astype(F32)).astype(BF16)], axis=1)
        br =jnp.concatenate([b_group_router[l], b_expert_router[l].reshape(-1)])
        br = jnp.pad(br, (0, LANES - br.shape[0])).reshape(1, LANES)
        x1, h2, rt, counts = _outproj(o_attn.reshape(n, attn_w), o_rec.reshape(n, -1), x2,
                                      w_out_l, norm_ffn_g[l].reshape(1, d), wr, br)
        src, pos1, pos2, tile_expert, first, n_used = _dispatch_plan(rt, counts, n)
        ys = _moe(tile_expert, n_used, first, src, h2, w_gate[l], w_up[l], w_down[l])
        x2 = _final(x1, ys, jnp.concatenate([pos1, pos2]), rt, final_norm_g.reshape(1, d))
    return x2.reshape(b, t, d)
```

```python
import functools

import jax
import jax.numpy as jnp
import numpy as np
from jax import lax
from jax.experimental import pallas as pl
from jax.experimental.pallas import tpu as pltpu

F32 = jnp.float32
BF16 = jnp.bfloat16
HIGHEST = lax.Precision.HIGHEST

HEAD_DIM = 128
MOBA_BLOCK = 256
MOBA_TOPK = 3
N_GROUPS = 4
EXPERTS_PER_GROUP = 4
N_EXPERTS = N_GROUPS * EXPERTS_PER_GROUP
RMS_EPS = 1e-6

LANES = 128
VMEM_LIMIT_BYTES = 56 * 1024 * 1024

HGRN_CHUNK = 64
HGRN_SUB = 16
HGRN_STEP = 256
HGRN_HEADS_PER_STEP = 8
HGRN_MIN_SAFE_F = 2.0 ** -7
MOE_TILE = 256
NT_DIMS = (((1,), (1,)), ((), ()))
TN_DIMS = (((0,), (0,)), ((), ()))


def _cparams(*sem):
    return pltpu.CompilerParams(dimension_semantics=sem, vmem_limit_bytes=VMEM_LIMIT_BYTES)


def _inproj_kernel(x_ref, g_ref, w_ref, cs_ref, o_ref, hn_ref):
    @pl.when(pl.program_id(1) == 0)
    def _():
        x = x_ref[...]
        ms = jnp.mean(x * x, axis=-1, keepdims=True)
        hn_ref[...] = (x * lax.rsqrt(ms + RMS_EPS) * g_ref[...]).astype(BF16)

    acc = jnp.dot(hn_ref[...], w_ref[...], preferred_element_type=F32)
    o_ref[...] = (acc * cs_ref[...]).astype(o_ref.dtype)


def _inproj(x2, g, w, col0, colscale, out_dtype, tm=1024, tn=1024):
    n, d = x2.shape
    cols = colscale.shape[1]
    j0 = col0 // tn
    return pl.pallas_call(
        _inproj_kernel,
        grid=(n // tm, cols // tn),
        in_specs=[
            pl.BlockSpec((tm, d), lambda i, j: (i, 0)),
            pl.BlockSpec((1, d), lambda i, j: (0, 0)),
            pl.BlockSpec((d, tn), lambda i, j: (0, j0 + j)),
            pl.BlockSpec((1, tn), lambda i, j: (0, j)),
        ],
        out_specs=pl.BlockSpec((tm, tn), lambda i, j: (i, j)),
        out_shape=jax.ShapeDtypeStruct((n, cols), out_dtype),
        scratch_shapes=[pltpu.VMEM((tm, d), BF16)],
        compiler_params=_cparams("parallel", "arbitrary"),
        name="inproj",
    )(x2, g, w, colscale)


MOBA_HEADS_PER_STEP = 4
MOBA_EXTRA_POS = 0
MOBA_EXTRA_SEL = 8
MASK_BIG = 2.0 ** 60
LOG2E = 1.4426950408889634


def _moba_key_extras(t):
    nb = t // MOBA_BLOCK
    assert MOBA_EXTRA_SEL + nb <= LANES and MOBA_BLOCK <= 256
    pos = np.arange(t)
    kx = np.zeros((t, LANES), np.float32)
    kx[:, MOBA_EXTRA_POS:MOBA_EXTRA_POS + 3] = ((pos // MOBA_BLOCK) * MOBA_BLOCK)[:, None]
    kx[:, MOBA_EXTRA_POS + 3:MOBA_EXTRA_POS + 6] = (pos % MOBA_BLOCK)[:, None]
    kx[pos, MOBA_EXTRA_SEL + pos // MOBA_BLOCK] = -MASK_BIG
    return jnp.asarray(kx, dtype=BF16)


def _moba_kernel(sl_ref, q_ref, k_ref, v_ref, kx_ref, o_ref, kaug_ref, kmean_ref, *, nb):
    hg = pl.program_id(1)
    i = pl.program_id(2)
    bs = MOBA_BLOCK
    dh = HEAD_DIM
    nbp = kmean_ref.shape[1] // 4
    nx = 8 + nbp

    @pl.when(i == 0)
    def _():
        for hh in range(MOBA_HEADS_PER_STEP):
            kaug_ref[hh, :, 0:dh] = k_ref[0, :, hh * dh:(hh + 1) * dh]
            kaug_ref[hh, :, dh:] = kx_ref[...]
            rows = [jnp.mean(k_ref[0, j * bs:(j + 1) * bs, hh * dh:(hh + 1) * dh].astype(F32),
                             axis=0, keepdims=True) for j in range(nb)]
            if nbp > nb:
                rows.append(jnp.zeros((nbp - nb, dh), F32))
            km = jnp.concatenate(rows, axis=0)
            hi = km.astype(BF16)
            mid = (km - hi.astype(F32)).astype(BF16)
            lo = (km - hi.astype(F32) - mid.astype(F32)).astype(BF16)
            kmean_ref[hh] = jnp.concatenate([hi, mid, lo, jnp.zeros_like(hi)], axis=0)

    r8 = lax.broadcasted_iota(jnp.int32, (8, bs), 0)
    piece = r8 % 3
    er = lax.broadcasted_iota(jnp.int32, (nx, LANES), 0)
    ec = lax.broadcasted_iota(jnp.int32, (nx, LANES), 1)
    embed = jnp.where(er == ec, 1.0, 0.0).astype(BF16)

    qas = []
    for hh in range(MOBA_HEADS_PER_STEP):
        h = hg * MOBA_HEADS_PER_STEP + hh
        q = q_ref[0, :, hh * dh:(hh + 1) * dh]
        g4 = lax.dot_general(kmean_ref[hh], q, NT_DIMS, preferred_element_type=F32)
        gate_t = g4[0:nbp] + g4[nbp:2 * nbp] + g4[2 * nbp:3 * nbp]
        blk = lax.broadcasted_iota(jnp.int32, gate_t.shape, 0)
        rank = jnp.zeros(gate_t.shape, F32)
        for jp in range(nb - 1):
            other = gate_t[jp:jp + 1, :]
            beats = (other > gate_t) | ((other == gate_t) & (blk > jp))
            rank = rank + jnp.where(beats, (jp < i).astype(F32), 0.0)
        notsel_t = jnp.where((blk < i) & (rank >= MOBA_TOPK), 1.0, 0.0)
        slope_rows = jnp.where(r8 >= 6, 0.0,
                               jnp.where(piece == 0, sl_ref[0, h],
                                         jnp.where(piece == 1, sl_ref[1, h], sl_ref[2, h])))
        qx_t = jnp.concatenate([slope_rows, notsel_t], axis=0).astype(BF16)
        qx = lax.dot_general(qx_t, embed, TN_DIMS, preferred_element_type=F32)
        qas.append(jnp.concatenate([q, qx.astype(BF16)], axis=1))

    rowi = lax.broadcasted_iota(jnp.int32, (bs, bs), 0)
    coli = lax.broadcasted_iota(jnp.int32, (bs, bs), 1)

    for c in range(nb):
        @pl.when(i == c)
        def _(c=c):
            n = (c + 1) * bs
            for hh in range(MOBA_HEADS_PER_STEP):
                cs = slice(hh * dh, (hh + 1) * dh)
                s = lax.dot_general(qas[hh], kaug_ref[hh, 0:n, :], NT_DIMS,
                                    preferred_element_type=F32)
                s_own = jnp.where(rowi >= coli, s[:, c * bs:], -jnp.inf)
                m = jnp.max(s_own, axis=1, keepdims=True)
                if c > 0:
                    s_past = s[:, :c * bs]
                    m = jnp.maximum(m, jnp.max(s_past, axis=1, keepdims=True))
                p_own = jnp.exp2(s_own - m)
                l = jnp.sum(p_own, axis=1, keepdims=True)
                acc = jnp.dot(p_own.astype(BF16), v_ref[0, c * bs:n, cs],
                              preferred_element_type=F32)
                if c > 0:
                    p_past = jnp.exp2(s_past - m)
                    l = l + jnp.sum(p_past, axis=1, keepdims=True)
                    acc = acc + jnp.dot(p_past.astype(BF16), v_ref[0, 0:c * bs, cs],
                                        preferred_element_type=F32)
                o_ref[0, :, cs] = (acc / l).astype(o_ref.dtype)


def _moba(qkv, slope_pieces, n_heads):
    b, t, _ = qkv.shape
    bs = MOBA_BLOCK
    nb = t // bs
    nbp = -(-nb // 8) * 8
    hps = MOBA_HEADS_PER_STEP
    assert n_heads % hps == 0
    ng = n_heads // hps
    w = hps * HEAD_DIM
    kern = functools.partial(_moba_kernel, nb=nb)
    return pl.pallas_call(
        kern,
        grid=(b, ng, nb),
        in_specs=[
            pl.BlockSpec(memory_space=pltpu.SMEM),
            pl.BlockSpec((1, bs, w), lambda bi, hi, qi: (bi, qi, hi)),
            pl.BlockSpec((1, t, w), lambda bi, hi, qi: (bi, 0, ng + hi)),
            pl.BlockSpec((1, t, w), lambda bi, hi, qi: (bi, 0, 2 * ng + hi)),
            pl.BlockSpec((t, LANES), lambda bi, hi, qi: (0, 0)),
        ],
        out_specs=pl.BlockSpec((1, bs, w), lambda bi, hi, qi: (bi, qi, hi)),
        out_shape=jax.ShapeDtypeStruct((b, t, n_heads * HEAD_DIM), BF16),
        scratch_shapes=[
            pltpu.VMEM((hps, t, HEAD_DIM + LANES), BF16),
            pltpu.VMEM((hps, 4 * nbp, HEAD_DIM), BF16),
        ],
        compiler_params=_cparams("parallel", "parallel", "arbitrary"),
        name="moba",
    )(slope_pieces, qkv, qkv, qkv, _moba_key_extras(t))


def _sigmoid(x):
    return 1.0 / (1.0 + jnp.exp(-x))


def _hgrn_chunk(q, f, iv, g, gn, st, bounded_decay):
    c = HGRN_CHUNK
    sub = HGRN_SUB
    half = sub // 2
    nsub = c // sub
    qf = q * _sigmoid(q)
    kf = jnp.maximum(1.0 - f, 0.0)
    r = lax.broadcasted_iota(jnp.int32, (c, c), 0)
    cc = lax.broadcasted_iota(jnp.int32, (c, c), 1)
    tril = jnp.where(r >= cc, 1.0, 0.0).astype(BF16)
    lf = jnp.log2(f)
    dk = lf.shape[1]
    lf_hi = lf.astype(BF16)
    lf_mid = (lf - lf_hi.astype(F32)).astype(BF16)
    lf_lo = (lf - lf_hi.astype(F32) - lf_mid.astype(F32)).astype(BF16)
    b3 = jnp.dot(tril, jnp.concatenate([lf_hi, lf_mid, lf_lo], axis=1),
                 preferred_element_type=F32)
    b2 = b3[:, :dk] + b3[:, dk:2 * dk] + b3[:, 2 * dk:]
    c2 = b2 - jnp.log2(kf)

    inter = lax.dot_general((qf * jnp.exp2(b2)).astype(BF16), st.astype(BF16), NT_DIMS,
                            preferred_element_type=F32)

    lane = lax.broadcasted_iota(jnp.int32, (sub, LANES), 1)
    tsub = lax.broadcasted_iota(jnp.int32, (sub, LANES), 0)
    colid = lax.broadcasted_iota(jnp.int32, (sub, c), 1)
    ones = jnp.ones((LANES, LANES), BF16)
    a_rows = []
    for bi in range(nsub):
        lo = bi * sub
        b_i = b2[lo:lo + sub]
        c_i = c2[lo:lo + sub]
        q_i = qf[lo:lo + sub]
        if bounded_decay:
            qt = q_i * (jnp.exp2(b_i - b2[lo - 1:lo, :]) if bi > 0 else jnp.exp2(b_i))
            kt = jnp.exp2((b2[lo - 1:lo, :] - c2[:lo + sub]) if bi > 0 else -c2[:sub])
            if lo + sub < c:
                kt = jnp.concatenate([kt, jnp.zeros((c - lo - sub, kt.shape[1]), F32)], axis=0)
            cross = lax.dot_general(qt.astype(BF16), kt.astype(BF16), NT_DIMS,
                                    preferred_element_type=F32)
            a_rows.append(jnp.where(colid <= lo + tsub[:, :c], cross, 0.0))
            continue
        pieces = []
        for s in range(sub):
            if s < half:
                pieces.append(q_i * jnp.exp2(b_i - c_i[s:s + 1, :]))
            else:
                pieces.append(q_i[half:] * jnp.exp2(b_i[half:] - c_i[s:s + 1, :]))
        pm = jnp.concatenate(pieces, axis=0).astype(BF16)
        rs = jnp.dot(pm, ones, preferred_element_type=F32)
        rel = lane - lo
        key = jnp.where((rel >= 0) & (rel <= tsub), rel, -1)
        key_lo, key_hi = key[:half], key[half:]
        a_lo = jnp.zeros((half, LANES), F32)
        a_hi = jnp.zeros((half, LANES), F32)
        off = 0
        for s in range(sub):
            if s < half:
                a_lo = jnp.where(key_lo == s, rs[off:off + half], a_lo)
                a_hi = jnp.where(key_hi == s, rs[off + half:off + sub], a_hi)
                off += sub
            else:
                a_hi = jnp.where(key_hi == s, rs[off:off + half], a_hi)
                off += half
        a_blk = jnp.concatenate([a_lo, a_hi], axis=0)[:, :c]
        if bi > 0:
            b0 = b2[lo - 1:lo, :]
            qt = q_i * jnp.exp2(b_i - b0)
            kt = jnp.exp2(jnp.minimum(b0 - c2, 0.0))
            cross = lax.dot_general(qt.astype(BF16), kt.astype(BF16), NT_DIMS,
                                    preferred_element_type=F32)
            a_blk = a_blk + jnp.where(colid < lo, cross, 0.0)
        a_rows.append(a_blk)
    a = jnp.concatenate(a_rows, axis=0)
    intra = jnp.dot(a.astype(BF16), iv.astype(BF16), preferred_element_type=F32)

    b_last = b2[c - 1:c, :]
    khat = jnp.exp2(b_last - c2)
    st_new = st * jnp.exp2(b_last) + lax.dot_general(
        iv.astype(BF16), khat.astype(BF16), TN_DIMS, preferred_element_type=F32)

    o = inter + intra
    y = o * lax.rsqrt(jnp.mean(o * o, axis=-1, keepdims=True) + RMS_EPS) * gn
    return y * (g * _sigmoid(g)), st_new


def _hgrn_kernel(q_ref, f_ref, i_ref, g_ref, lb_ref, gn_ref, o_ref, st_ref):
    @pl.when(pl.program_id(2) == 0)
    def _():
        st_ref[...] = jnp.zeros_like(st_ref)

    gn = gn_ref[...]
    d = HEAD_DIM
    lb = lb_ref[...]
    f_all = lb + (1.0 - lb) * _sigmoid(f_ref[0])
    f_min = jnp.min(f_all)

    def run(bounded_decay):
        for hh in range(HGRN_HEADS_PER_STEP):
            cs = slice(hh * d, (hh + 1) * d)
            st = st_ref[hh]
            for ci in range(HGRN_STEP // HGRN_CHUNK):
                sl = slice(ci * HGRN_CHUNK, (ci + 1) * HGRN_CHUNK)
                out, st = _hgrn_chunk(q_ref[0, sl, cs], f_all[sl, cs], i_ref[0, sl, cs],
                                      g_ref[0, sl, cs], gn, st, bounded_decay)
                o_ref[0, sl, cs] = out.astype(o_ref.dtype)
            st_ref[hh] = st

    @pl.when(f_min >= HGRN_MIN_SAFE_F)
    def _():
        run(True)

    @pl.when(jnp.logical_not(f_min >= HGRN_MIN_SAFE_F))
    def _():
        run(False)


def _hgrn(hp, lb, gn, n_heads):
    b, t, _ = hp.shape
    d = HEAD_DIM
    ts = HGRN_STEP
    hps = HGRN_HEADS_PER_STEP
    assert n_heads % hps == 0
    ng = n_heads // hps

    def col(group):
        return pl.BlockSpec((1, ts, hps * d), lambda bi, hi, ti: (bi, ti, group * ng + hi))

    return pl.pallas_call(
        _hgrn_kernel,
        grid=(b, ng, t // ts),
        in_specs=[col(0), col(1), col(2), col(3),
                  pl.BlockSpec((1, hps * d), lambda bi, hi, ti: (0, hi)),
                  pl.BlockSpec((1, d), lambda bi, hi, ti: (0, 0))],
        out_specs=pl.BlockSpec((1, ts, hps * d), lambda bi, hi, ti: (bi, ti, hi)),
        out_shape=jax.ShapeDtypeStruct((b, t, n_heads * d), BF16),
        scratch_shapes=[pltpu.VMEM((hps, d, d), F32)],
        compiler_params=_cparams("parallel", "parallel", "arbitrary"),
        name="hgrn2",
    )(hp, hp, hp, hp, lb, gn)


def _outproj_kernel(oa_ref, or_ref, x_ref, w_ref, g_ref, wr_ref, br_ref,
                    x1_ref, h2_ref, rt_ref, cnt_out_ref, cnt_ref):
    wa = oa_ref.shape[1]
    x1 = (x_ref[...]
          + jnp.dot(oa_ref[...], w_ref[0:wa, :], preferred_element_type=F32)
          + jnp.dot(or_ref[...], w_ref[wa:, :], preferred_element_type=F32))
    x1_ref[...] = x1
    h2 = x1 * lax.rsqrt(jnp.mean(x1 * x1, axis=-1, keepdims=True) + RMS_EPS) * g_ref[...]
    h2_ref[...] = h2

    h_hi = h2.astype(BF16)
    h_mid = (h2 - h_hi.astype(F32)).astype(BF16)
    part = jnp.dot(h_hi, wr_ref[...], preferred_element_type=F32)
    logits = (part[:, :LANES] + part[:, LANES:] + br_ref[...]
              + jnp.dot(h_mid, wr_ref[:, :LANES], preferred_element_type=F32))
    lane = lax.broadcasted_iota(jnp.int32, logits.shape, 1)
    big = jnp.int32(4 * LANES)
    ninf = -jnp.inf

    lg = jnp.where(lane < N_GROUPS, logits, ninf)
    mg = jnp.max(lg, axis=1, keepdims=True)
    gidx = jnp.min(jnp.where(lg == mg, lane, big), axis=1, keepdims=True)
    grp_w = 1.0 / jnp.sum(jnp.exp(lg - mg), axis=1, keepdims=True)

    lo = N_GROUPS + EXPERTS_PER_GROUP * gidx
    le = jnp.where((lane >= lo) & (lane < lo + EXPERTS_PER_GROUP), logits, ninf)
    m1 = jnp.max(le, axis=1, keepdims=True)
    i1 = jnp.min(jnp.where(le == m1, lane, big), axis=1, keepdims=True)
    le2 = jnp.where(lane == i1, ninf, le)
    m2 = jnp.max(le2, axis=1, keepdims=True)
    i2 = jnp.min(jnp.where(le2 == m2, lane, big), axis=1, keepdims=True)
    r21 = jnp.exp(m2 - m1)
    w1 = grp_w / (1.0 + r21)
    w2 = grp_w * r21 / (1.0 + r21)
    e1 = i1 - N_GROUPS
    e2 = i2 - N_GROUPS

    @pl.when(pl.program_id(0) == 0)
    def _():
        cnt_ref[...] = jnp.zeros_like(cnt_ref)

    tm = logits.shape[0]
    onehot = jnp.where((lane == e1) | (lane == e2), 1.0, 0.0)
    rr = lax.broadcasted_iota(jnp.int32, (tm, tm), 0)
    rc = lax.broadcasted_iota(jnp.int32, (tm, tm), 1)
    before = jnp.where(rr > rc, 1.0, 0.0).astype(BF16)
    prefix = (jnp.dot(before, onehot.astype(BF16), preferred_element_type=F32) + cnt_ref[0:1, :])
    rank1 = jnp.sum(jnp.where(lane == e1, prefix, 0.0), axis=1, keepdims=True)
    rank2 = jnp.sum(jnp.where(lane == e2, prefix, 0.0), axis=1, keepdims=True)
    total = cnt_ref[0:1, :] + jnp.sum(onehot, axis=0, keepdims=True)
    cnt_ref[0:1, :] = total
    cnt_out_ref[...] = jnp.broadcast_to(total, cnt_out_ref.shape)

    cols = [e1.astype(F32), e2.astype(F32), w1, w2, rank1, rank2]
    rt = jnp.zeros(logits.shape, F32)
    for ci, cv in enumerate(cols):
        rt = jnp.where(lane == ci, cv, rt)
    rt_ref[...] = rt


def _outproj(oa, orec, x2, w_out, g, wr, br, tm=256):
    n, d = x2.shape
    wa = oa.shape[1]
    wrc = orec.shape[1]
    row = lambda i: (i, 0)
    const = lambda i: (0, 0)
    return pl.pallas_call(
        _outproj_kernel,
        grid=(n // tm,),
        in_specs=[
            pl.BlockSpec((tm, wa), row),
            pl.BlockSpec((tm, wrc), row),
            pl.BlockSpec((tm, d), row),
            pl.BlockSpec((wa + wrc, d), const),
            pl.BlockSpec((1, d), const),
            pl.BlockSpec((d, 2 * LANES), const),
            pl.BlockSpec((1, LANES), const),
        ],
        out_specs=[pl.BlockSpec((tm, d), row), pl.BlockSpec((tm, d), row),
                   pl.BlockSpec((tm, LANES), row), pl.BlockSpec((8, LANES), const)],
        out_shape=[jax.ShapeDtypeStruct((n, d), F32), jax.ShapeDtypeStruct((n, d), F32),
                   jax.ShapeDtypeStruct((n, LANES), F32), jax.ShapeDtypeStruct((8, LANES), F32)],
        scratch_shapes=[pltpu.VMEM((8, LANES), F32)],
        compiler_params=_cparams("arbitrary"),
        name="outproj_route",
    )(oa, orec, x2, w_out, g, wr, br)


WEIGHT_CAST_ROWS = 256


def _cast_rows(dst_ref, src_ref):
    rows = dst_ref.shape[0]
    step = min(rows, WEIGHT_CAST_ROWS)
    assert rows % step == 0

    def body(c, carry):
        sl = pl.ds(pl.multiple_of(c * step, step), step)
        dst_ref[sl, :] = src_ref[0, sl, :].astype(BF16)
        return carry

    lax.fori_loop(0, rows // step, body, 0)


def _moe_up_kernel(te_ref, nu_ref, first_ref, src_ref, h2_hbm, wg_ref, wu_ref, hid_ref,
                   xbuf0, xbuf1, sems, wgb, wub):
    s = pl.program_id(0)
    nu = nu_ref[0]
    tm = MOE_TILE
    xbufs = (xbuf0, xbuf1)

    def start_gather(tile, slot):
        for r in range(tm):
            tok = src_ref[tile * tm + r]
            pltpu.make_async_copy(h2_hbm.at[pl.ds(tok, 1), :], xbufs[slot].at[pl.ds(r, 1), :],
                                  sems.at[slot]).start()

    def wait_gather(slot):
        pltpu.make_async_copy(h2_hbm.at[pl.ds(0, tm), :], xbufs[slot], sems.at[slot]).wait()

    def gate_up(slot):
        x = xbufs[slot][...].astype(BF16)
        a = jnp.dot(x, wgb[...], preferred_element_type=F32)
        u = jnp.dot(x, wub[...], preferred_element_type=F32)
        hid_ref[...] = (a * _sigmoid(a) * u).astype(BF16)

    @pl.when(s == 0)
    def _():
        start_gather(s, 0)

    @pl.when((s >= 1) & (s <= nu) & (first_ref[jnp.maximum(s - 1, 0)] == 1))
    def _():
        _cast_rows(wgb, wg_ref)
        _cast_rows(wub, wu_ref)

    for parity in range(2):
        @pl.when((s >= 1) & (s < nu) & (s % 2 == parity))
        def _(parity=parity):
            wait_gather(1 - parity)
            start_gather(s, parity)
            gate_up(1 - parity)

        @pl.when((s >= 1) & (s == nu) & (s % 2 == parity))
        def _(parity=parity):
            wait_gather(1 - parity)
            gate_up(1 - parity)

    @pl.when(s > nu)
    def _():
        hid_ref[...] = jnp.zeros_like(hid_ref)


def _moe_down_kernel(te_ref, nu_ref, first_ref, hid_ref, wd_ref, y_ref, wdb):
    n = pl.program_id(0)

    @pl.when((n < nu_ref[0]) & (first_ref[n] == 1))
    def _():
        _cast_rows(wdb, wd_ref)

    @pl.when(n < nu_ref[0])
    def _():
        y_ref[...] = jnp.dot(hid_ref[...], wdb[...], preferred_element_type=F32)

    @pl.when(n >= nu_ref[0])
    def _():
        y_ref[...] = jnp.zeros_like(y_ref)


def _moe(tile_expert, n_used, first, src, h2, wg, wu, wd):
    p = src.shape[0]
    d, f = wg.shape[1:]
    tm = MOE_TILE
    n_tiles = p // tm
    prev = lambda s: jnp.maximum(s - 1, 0)
    hid = pl.pallas_call(
        _moe_up_kernel,
        grid_spec=pltpu.PrefetchScalarGridSpec(
            num_scalar_prefetch=4,
            grid=(n_tiles + 1,),
            in_specs=[
                pl.BlockSpec(memory_space=pl.ANY),
                pl.BlockSpec((1, d, f), lambda s, te, nu, fi, sr: (te[prev(s)], 0, 0)),
                pl.BlockSpec((1, d, f), lambda s, te, nu, fi, sr: (te[prev(s)], 0, 0)),
            ],
            out_specs=pl.BlockSpec((tm, f), lambda s, te, nu, fi, sr: (prev(s), 0)),
            scratch_shapes=[pltpu.VMEM((tm, d), F32), pltpu.VMEM((tm, d), F32),
                            pltpu.SemaphoreType.DMA((2,)),
                            pltpu.VMEM((d, f), BF16), pltpu.VMEM((d, f), BF16)],
        ),
        out_shape=jax.ShapeDtypeStruct((p, f), BF16),
        compiler_params=_cparams("arbitrary"),
        name="moe_gate_up",
    )(tile_expert, n_used, first, src, h2, wg, wu)
    return pl.pallas_call(
        _moe_down_kernel,
        grid_spec=pltpu.PrefetchScalarGridSpec(
            num_scalar_prefetch=3,
            grid=(n_tiles,),
            in_specs=[
                pl.BlockSpec((tm, f), lambda n, te, nu, fi: (n, 0)),
                pl.BlockSpec((1, f, d), lambda n, te, nu, fi: (te[n], 0, 0)),
            ],
            out_specs=pl.BlockSpec((tm, d), lambda n, te, nu, fi: (n, 0)),
            scratch_shapes=[pltpu.VMEM((f, d), BF16)],
        ),
        out_shape=jax.ShapeDtypeStruct((p, d), F32),
        compiler_params=_cparams("arbitrary"),
        name="moe_down",
    )(tile_expert, n_used, first, hid, wd)


COMBINE_TILE = 256


def _final_kernel(pos_ref, x1_ref, ys_hbm, rt_ref, g_ref, o_ref, ya0, yb0, ya1, yb1, sems):
    s = pl.program_id(0)
    n_tiles = pl.num_programs(0) - 1
    tm = COMBINE_TILE
    n_tok = pos_ref.shape[0] // 2
    ybufs = ((ya0, yb0), (ya1, yb1))

    def start_gather(tile, slot):
        for k in range(2):
            for r in range(tm):
                row = pos_ref[k * n_tok + tile * tm + r]
                pltpu.make_async_copy(ys_hbm.at[pl.ds(row, 1), :],
                                      ybufs[slot][k].at[pl.ds(r, 1), :], sems.at[slot]).start()

    def wait_gather(slot):
        for k in range(2):
            pltpu.make_async_copy(ys_hbm.at[pl.ds(0, tm), :], ybufs[slot][k], sems.at[slot]).wait()

    def finish(slot):
        rt = rt_ref[...]
        x2 = (x1_ref[...] + rt[:, 2:3] * ybufs[slot][0][...] + rt[:, 3:4] * ybufs[slot][1][...])
        o_ref[...] = (x2 * lax.rsqrt(jnp.mean(x2 * x2, axis=-1, keepdims=True) + RMS_EPS)
                      * g_ref[...])

    @pl.when(s == 0)
    def _():
        start_gather(s, 0)

    for parity in range(2):
        @pl.when((s >= 1) & (s < n_tiles) & (s % 2 == parity))
        def _(parity=parity):
            wait_gather(1 - parity)
            start_gather(s, parity)
            finish(1 - parity)

        @pl.when((s == n_tiles) & (s % 2 == parity))
        def _(parity=parity):
            wait_gather(1 - parity)
            finish(1 - parity)


def _final(x1, ys, pos, rt, g):
    n, d = x1.shape
    tm = COMBINE_TILE
    prev = lambda s: jnp.maximum(s - 1, 0)
    return pl.pallas_call(
        _final_kernel,
        grid_spec=pltpu.PrefetchScalarGridSpec(
            num_scalar_prefetch=1,
            grid=(n // tm + 1,),
            in_specs=[pl.BlockSpec((tm, d), lambda s, p: (prev(s), 0)),
                      pl.BlockSpec(memory_space=pl.ANY),
                      pl.BlockSpec((tm, LANES), lambda s, p: (prev(s), 0)),
                      pl.BlockSpec((1, d), lambda s, p: (0, 0))],
            out_specs=pl.BlockSpec((tm, d), lambda s, p: (prev(s), 0)),
            scratch_shapes=[pltpu.VMEM((tm, d), F32)] * 4 + [pltpu.SemaphoreType.DMA((2,))],
        ),
        out_shape=jax.ShapeDtypeStruct((n, d), F32),
        compiler_params=_cparams("arbitrary"),
        name="combine_norm",
    )(pos, x1, ys, rt, g)


def _dispatch_plan(rt, counts, n_tokens):
    tm = MOE_TILE
    n_rows = 2 * n_tokens + N_EXPERTS * tm
    n_tiles = n_rows // tm
    ids = rt[:, 0:6].astype(jnp.int32)
    ef = jnp.concatenate([ids[:, 0], ids[:, 1]])
    rank = jnp.concatenate([ids[:, 4], ids[:, 5]])
    counts = counts[0, :N_EXPERTS].astype(jnp.int32)
    tiles_per = (counts + tm - 1) // tm
    tile_end = jnp.cumsum(tiles_per)
    row_start = (tile_end - tiles_per) * tm
    pos = row_start[ef] + rank
    tok = jnp.concatenate([jnp.arange(n_tokens, dtype=jnp.int32)] * 2)
    src = jnp.zeros((n_rows,), jnp.int32).at[pos].set(tok)
    n_used = tile_end[-1]
    tile_ids = jnp.arange(n_tiles, dtype=jnp.int32)
    tile_expert = jnp.sum((tile_ids[:, None] >= tile_end[None, :]).astype(jnp.int32), axis=1)
    last_expert = jnp.sum((n_used - 1 >= tile_end).astype(jnp.int32))
    tile_expert = jnp.where(tile_ids < n_used, tile_expert, last_expert).astype(jnp.int32)
    first = jnp.concatenate([jnp.ones((1,), jnp.int32),
                             (tile_expert[1:] != tile_expert[:-1]).astype(jnp.int32)])
    return (src, pos[:n_tokens], pos[n_tokens:], tile_expert, first,
            n_used.reshape(1).astype(jnp.int32))


def kernel(x, norm_mix_g, w_in, hgrn_lb_logits, hgrn_out_norm_g, w_out, norm_ffn_g, w_group_router,
           b_group_router, w_expert_router, b_expert_router, w_gate, w_up, w_down, final_norm_g):
    b, t, d = x.shape
    n = b * t
    depth = w_in.shape[0]
    assert depth == 1, "the final norm is fused into the combine step of the only layer"
    attn_w = d // 2
    n_heads = attn_w // HEAD_DIM
    x2 = x.reshape(n, d)
    lb_all = jnp.cumsum(jax.nn.softmax(hgrn_lb_logits.astype(F32), axis=0), axis=0)[:depth]
    slope2 = jnp.asarray(2.0 ** (-8.0 * np.arange(1, n_heads + 1) / n_heads), dtype=F32) * F32(LOG2E)
    s_hi = slope2.astype(BF16).astype(F32)
    s_mid = (slope2 - s_hi).astype(BF16).astype(F32)
    s_lo = (slope2 - s_hi - s_mid).astype(BF16).astype(F32)
    slope_pieces = jnp.stack([s_hi, s_mid, s_lo])
    qkv_scale = jnp.concatenate([jnp.full((attn_w,), HEAD_DIM ** -0.5 * LOG2E, F32),
                                 jnp.ones((2 * attn_w,), F32)]).reshape(1, -1)

    for l in range(depth):
        g_mix = norm_mix_g[l].reshape(1, d)
        x2, w_in_l, w_out_l = lax.optimization_barrier(
            (x2, w_in[l].astype(BF16), w_out[l].astype(BF16)))
        qkv = _inproj(x2, g_mix, w_in_l, 0, qkv_scale, BF16)
        hp = _inproj(x2, g_mix, w_in_l, 3 * attn_w, jnp.ones((1, 4 * (d - attn_w)), F32), F32)
        o_attn = _moba(qkv.reshape(b, t, 3 * attn_w), slope_pieces, n_heads)
        o_rec = _hgrn(hp.reshape(b, t, -1), lb_all[l].reshape(1, -1),
                      hgrn_out_norm_g[l].reshape(1, HEAD_DIM), n_heads)

        wr = jnp.concatenate(
            [w_group_router[l],
             jnp.transpose(w_expert_router[l], (1, 0, 2)).reshape(d, N_EXPERTS)], axis=1)
        wr = jnp.pad(wr, ((0, 0), (0, LANES - wr.shape[1])))
        wr_hi = wr.astype(BF16)
        wr = jnp.concatenate([wr_hi, (wr - wr_hi.astype(F32)).astype(BF16)], axis=1)
        br =jnp.concatenate([b_group_router[l], b_expert_router[l].reshape(-1)])
        br = jnp.pad(br, (0, LANES - br.shape[0])).reshape(1, LANES)
        x1, h2, rt, counts = _outproj(o_attn.reshape(n, attn_w), o_rec.reshape(n, -1), x2,
                                      w_out_l, norm_ffn_g[l].reshape(1, d), wr, br)
        src, pos1, pos2, tile_expert, first, n_used = _dispatch_plan(rt, counts, n)
        ys = _moe(tile_expert, n_used, first, src, h2, w_gate[l], w_up[l], w_down[l])
        x2 = _final(x1, ys, jnp.concatenate([pos1, pos2]), rt, final_norm_g.reshape(1, d))
    return x2.reshape(b, t, d)
```

```python
import functools

import jax
import jax.numpy as jnp
import numpy as np
from jax import lax
from jax.experimental import pallas as pl
from jax.experimental.pallas import tpu as pltpu

F32 = jnp.float32
BF16 = jnp.bfloat16
HIGHEST = lax.Precision.HIGHEST

HEAD_DIM = 128
MOBA_BLOCK = 256
MOBA_TOPK = 3
N_GROUPS = 4
EXPERTS_PER_GROUP = 4
N_EXPERTS = N_GROUPS * EXPERTS_PER_GROUP
RMS_EPS = 1e-6

LANES = 128
VMEM_LIMIT_BYTES = 56 * 1024 * 1024

HGRN_CHUNK = 64
HGRN_SUB = 16
HGRN_STEP = 256
HGRN_HEADS_PER_STEP = 8
HGRN_MIN_SAFE_F = 2.0 ** -7
MOE_TILE = 256
NT_DIMS = (((1,), (1,)), ((), ()))
TN_DIMS = (((0,), (0,)), ((), ()))


def _cparams(*sem):
    return pltpu.CompilerParams(dimension_semantics=sem, vmem_limit_bytes=VMEM_LIMIT_BYTES)


def _inproj_kernel(x_ref, g_ref, w_ref, cs_ref, o_ref, hn_ref):
    @pl.when(pl.program_id(1) == 0)
    def _():
        x = x_ref[...]
        ms = jnp.mean(x * x, axis=-1, keepdims=True)
        hn_ref[...] = (x * lax.rsqrt(ms + RMS_EPS) * g_ref[...]).astype(BF16)

    acc = jnp.dot(hn_ref[...], w_ref[...], preferred_element_type=F32)
    o_ref[...] = (acc * cs_ref[...]).astype(o_ref.dtype)


def _inproj(x2, g, w, col0, colscale, out_dtype, tm=1024, tn=1024):
    n, d = x2.shape
    cols = colscale.shape[1]
    j0 = col0 // tn
    return pl.pallas_call(
        _inproj_kernel,
        grid=(n // tm, cols // tn),
        in_specs=[
            pl.BlockSpec((tm, d), lambda i, j: (i, 0)),
            pl.BlockSpec((1, d), lambda i, j: (0, 0)),
            pl.BlockSpec((d, tn), lambda i, j: (0, j0 + j)),
            pl.BlockSpec((1, tn), lambda i, j: (0, j)),
        ],
        out_specs=pl.BlockSpec((tm, tn), lambda i, j: (i, j)),
        out_shape=jax.ShapeDtypeStruct((n, cols), out_dtype),
        scratch_shapes=[pltpu.VMEM((tm, d), BF16)],
        compiler_params=_cparams("parallel", "arbitrary"),
        name="inproj",
    )(x2, g, w, colscale)


MOBA_HEADS_PER_STEP = 4
MOBA_EXTRA_POS = 0
MOBA_EXTRA_SEL = 8
MASK_BIG = 2.0 ** 60
LOG2E = 1.4426950408889634


def _moba_key_extras(t):
    nb = t // MOBA_BLOCK
    assert MOBA_EXTRA_SEL + nb <= LANES and MOBA_BLOCK <= 256
    pos = np.arange(t)
    kx = np.zeros((t, LANES), np.float32)
    kx[:, MOBA_EXTRA_POS:MOBA_EXTRA_POS + 3] = ((pos // MOBA_BLOCK) * MOBA_BLOCK)[:, None]
    kx[:, MOBA_EXTRA_POS + 3:MOBA_EXTRA_POS + 6] = (pos % MOBA_BLOCK)[:, None]
    kx[pos, MOBA_EXTRA_SEL + pos // MOBA_BLOCK] = -MASK_BIG
    return jnp.asarray(kx, dtype=BF16)


def _moba_kernel(sl_ref, q_ref, k_ref, v_ref, kx_ref, o_ref, kaug_ref, kmean_ref, *, nb):
    hg = pl.program_id(1)
    i = pl.program_id(2)
    bs = MOBA_BLOCK
    dh = HEAD_DIM
    nbp = kmean_ref.shape[1] // 4
    nx = 8 + nbp

    @pl.when(i == 0)
    def _():
        for hh in range(MOBA_HEADS_PER_STEP):
            kaug_ref[hh, :, 0:dh] = k_ref[0, :, hh * dh:(hh + 1) * dh]
            kaug_ref[hh, :, dh:] = kx_ref[...]
            rows = [jnp.mean(k_ref[0, j * bs:(j + 1) * bs, hh * dh:(hh + 1) * dh].astype(F32),
                             axis=0, keepdims=True) for j in range(nb)]
            if nbp > nb:
                rows.append(jnp.zeros((nbp - nb, dh), F32))
            km = jnp.concatenate(rows, axis=0)
            hi = km.astype(BF16)
            mid = (km - hi.astype(F32)).astype(BF16)
            lo = (km - hi.astype(F32) - mid.astype(F32)).astype(BF16)
            kmean_ref[hh] = jnp.concatenate([hi, mid, lo, jnp.zeros_like(hi)], axis=0)

    r8 = lax.broadcasted_iota(jnp.int32, (8, bs), 0)
    piece = r8 % 3
    er = lax.broadcasted_iota(jnp.int32, (nx, LANES), 0)
    ec = lax.broadcasted_iota(jnp.int32, (nx, LANES), 1)
    embed = jnp.where(er == ec, 1.0, 0.0).astype(BF16)

    qas = []
    for hh in range(MOBA_HEADS_PER_STEP):
        h = hg * MOBA_HEADS_PER_STEP + hh
        q = q_ref[0, :, hh * dh:(hh + 1) * dh]
        g4 = lax.dot_general(kmean_ref[hh], q, NT_DIMS, preferred_element_type=F32)
        gate_t = g4[0:nbp] + g4[nbp:2 * nbp] + g4[2 * nbp:3 * nbp]
        blk = lax.broadcasted_iota(jnp.int32, gate_t.shape, 0)
        rank = jnp.zeros(gate_t.shape, F32)
        for jp in range(nb - 1):
            other = gate_t[jp:jp + 1, :]
            beats = (other > gate_t) | ((other == gate_t) & (blk > jp))
            rank = rank + jnp.where(beats, (jp < i).astype(F32), 0.0)
        notsel_t = jnp.where((blk < i) & (rank >= MOBA_TOPK), 1.0, 0.0)
        slope_rows = jnp.where(r8 >= 6, 0.0,
                               jnp.where(piece == 0, sl_ref[0, h],
                                         jnp.where(piece == 1, sl_ref[1, h], sl_ref[2, h])))
        qx_t = jnp.concatenate([slope_rows, notsel_t], axis=0).astype(BF16)
        qx = lax.dot_general(qx_t, embed, TN_DIMS, preferred_element_type=F32)
        qas.append(jnp.concatenate([q, qx.astype(BF16)], axis=1))

    rowi = lax.broadcasted_iota(jnp.int32, (bs, bs), 0)
    coli = lax.broadcasted_iota(jnp.int32, (bs, bs), 1)

    for c in range(nb):
        @pl.when(i == c)
        def _(c=c):
            n = (c + 1) * bs
            for hh in range(MOBA_HEADS_PER_STEP):
                cs = slice(hh * dh, (hh + 1) * dh)
                s = lax.dot_general(qas[hh], kaug_ref[hh, 0:n, :], NT_DIMS,
                                    preferred_element_type=F32)
                s_own = jnp.where(rowi >= coli, s[:, c * bs:], -jnp.inf)
                m = jnp.max(s_own, axis=1, keepdims=True)
                if c > 0:
                    s_past = s[:, :c * bs]
                    m = jnp.maximum(m, jnp.max(s_past, axis=1, keepdims=True))
                p_own = jnp.exp2(s_own - m)
                l = jnp.sum(p_own, axis=1, keepdims=True)
                acc = jnp.dot(p_own.astype(BF16), v_ref[0, c * bs:n, cs],
                              preferred_element_type=F32)
                if c > 0:
                    p_past = jnp.exp2(s_past - m)
                    l = l + jnp.sum(p_past, axis=1, keepdims=True)
                    acc = acc + jnp.dot(p_past.astype(BF16), v_ref[0, 0:c * bs, cs],
                                        preferred_element_type=F32)
                o_ref[0, :, cs] = (acc / l).astype(o_ref.dtype)


def _moba(qkv, slope_pieces, n_heads):
    b, t, _ = qkv.shape
    bs = MOBA_BLOCK
    nb = t // bs
    nbp = -(-nb // 8) * 8
    hps = MOBA_HEADS_PER_STEP
    assert n_heads % hps == 0
    ng = n_heads // hps
    w = hps * HEAD_DIM
    kern = functools.partial(_moba_kernel, nb=nb)
    return pl.pallas_call(
        kern,
        grid=(b, ng, nb),
        in_specs=[
            pl.BlockSpec(memory_space=pltpu.SMEM),
            pl.BlockSpec((1, bs, w), lambda bi, hi, qi: (bi, qi, hi)),
            pl.BlockSpec((1, t, w), lambda bi, hi, qi: (bi, 0, ng + hi)),
            pl.BlockSpec((1, t, w), lambda bi, hi, qi: (bi, 0, 2 * ng + hi)),
            pl.BlockSpec((t, LANES), lambda bi, hi, qi: (0, 0)),
        ],
        out_specs=pl.BlockSpec((1, bs, w), lambda bi, hi, qi: (bi, qi, hi)),
        out_shape=jax.ShapeDtypeStruct((b, t, n_heads * HEAD_DIM), BF16),
        scratch_shapes=[
            pltpu.VMEM((hps, t, HEAD_DIM + LANES), BF16),
            pltpu.VMEM((hps, 4 * nbp, HEAD_DIM), BF16),
        ],
        compiler_params=_cparams("parallel", "parallel", "arbitrary"),
        name="moba",
    )(slope_pieces, qkv, qkv, qkv, _moba_key_extras(t))


def _sigmoid(x):
    return 1.0 / (1.0 + jnp.exp(-x))


def _hgrn_chunk(q, f, iv, g, gn, st, bounded_decay):
    c = HGRN_CHUNK
    sub = HGRN_SUB
    half = sub // 2
    nsub = c // sub
    qf = q * _sigmoid(q)
    kf = jnp.maximum(1.0 - f, 0.0)
    r = lax.broadcasted_iota(jnp.int32, (c, c), 0)
    cc = lax.broadcasted_iota(jnp.int32, (c, c), 1)
    tril = jnp.where(r >= cc, 1.0, 0.0).astype(BF16)
    lf = jnp.log2(f)
    dk = lf.shape[1]
    lf_hi = lf.astype(BF16)
    lf_mid = (lf - lf_hi.astype(F32)).astype(BF16)
    lf_lo = (lf - lf_hi.astype(F32) - lf_mid.astype(F32)).astype(BF16)
    b3 = jnp.dot(tril, jnp.concatenate([lf_hi, lf_mid, lf_lo], axis=1),
                 preferred_element_type=F32)
    b2 = b3[:, :dk] + b3[:, dk:2 * dk] + b3[:, 2 * dk:]
    c2 = b2 - jnp.log2(kf)

    inter = lax.dot_general((qf * jnp.exp2(b2)).astype(BF16), st.astype(BF16), NT_DIMS,
                            preferred_element_type=F32)

    lane = lax.broadcasted_iota(jnp.int32, (sub, LANES), 1)
    tsub = lax.broadcasted_iota(jnp.int32, (sub, LANES), 0)
    colid = lax.broadcasted_iota(jnp.int32, (sub, c), 1)
    ones = jnp.ones((LANES, LANES), BF16)
    a_rows = []
    for bi in range(nsub):
        lo = bi * sub
        b_i = b2[lo:lo + sub]
        c_i = c2[lo:lo + sub]
        q_i = qf[lo:lo + sub]
        if bounded_decay:
            qt = q_i * (jnp.exp2(b_i - b2[lo - 1:lo, :]) if bi > 0 else jnp.exp2(b_i))
            kt = jnp.exp2((b2[lo - 1:lo, :] - c2[:lo + sub]) if bi > 0 else -c2[:sub])
            if lo + sub < c:
                kt = jnp.concatenate([kt, jnp.zeros((c - lo - sub, kt.shape[1]), F32)], axis=0)
            cross = lax.dot_general(qt.astype(BF16), kt.astype(BF16), NT_DIMS,
                                    preferred_element_type=F32)
            a_rows.append(jnp.where(colid <= lo + tsub[:, :c], cross, 0.0))
            continue
        pieces = []
        for s in range(sub):
            if s < half:
                pieces.append(q_i * jnp.exp2(b_i - c_i[s:s + 1, :]))
            else:
                pieces.append(q_i[half:] * jnp.exp2(b_i[half:] - c_i[s:s + 1, :]))
        pm = jnp.concatenate(pieces, axis=0).astype(BF16)
        rs = jnp.dot(pm, ones, preferred_element_type=F32)
        rel = lane - lo
        key = jnp.where((rel >= 0) & (rel <= tsub), rel, -1)
        key_lo, key_hi = key[:half], key[half:]
        a_lo = jnp.zeros((half, LANES), F32)
        a_hi = jnp.zeros((half, LANES), F32)
        off = 0
        for s in range(sub):
            if s < half:
                a_lo = jnp.where(key_lo == s, rs[off:off + half], a_lo)
                a_hi = jnp.where(key_hi == s, rs[off + half:off + sub], a_hi)
                off += sub
            else:
                a_hi = jnp.where(key_hi == s, rs[off:off + half], a_hi)
                off += half
        a_blk = jnp.concatenate([a_lo, a_hi], axis=0)[:, :c]
        if bi > 0:
            b0 = b2[lo - 1:lo, :]
            qt = q_i * jnp.exp2(b_i - b0)
            kt = jnp.exp2(jnp.minimum(b0 - c2, 0.0))
            cross = lax.dot_general(qt.astype(BF16), kt.astype(BF16), NT_DIMS,
                                    preferred_element_type=F32)
            a_blk = a_blk + jnp.where(colid < lo, cross, 0.0)
        a_rows.append(a_blk)
    a = jnp.concatenate(a_rows, axis=0)
    intra = jnp.dot(a.astype(BF16), iv.astype(BF16), preferred_element_type=F32)

    b_last = b2[c - 1:c, :]
    khat = jnp.exp2(b_last - c2)
    st_new = st * jnp.exp2(b_last) + lax.dot_general(
        iv.astype(BF16), khat.astype(BF16), TN_DIMS, preferred_element_type=F32)

    o = inter + intra
    y = o * lax.rsqrt(jnp.mean(o * o, axis=-1, keepdims=True) + RMS_EPS) * gn
    return y * (g * _sigmoid(g)), st_new


def _hgrn_kernel(q_ref, f_ref, i_ref, g_ref, lb_ref, gn_ref, o_ref, st_ref):
    @pl.when(pl.program_id(2) == 0)
    def _():
        st_ref[...] = jnp.zeros_like(st_ref)

    gn = gn_ref[...]
    d = HEAD_DIM
    lb = lb_ref[...]
    f_all = lb + (1.0 - lb) * _sigmoid(f_ref[0])
    f_min = jnp.min(f_all)

    def run(bounded_decay):
        for hh in range(HGRN_HEADS_PER_STEP):
            cs = slice(hh * d, (hh + 1) * d)
            st = st_ref[hh]
            for ci in range(HGRN_STEP // HGRN_CHUNK):
                sl = slice(ci * HGRN_CHUNK, (ci + 1) * HGRN_CHUNK)
                out, st = _hgrn_chunk(q_ref[0, sl, cs], f_all[sl, cs], i_ref[0, sl, cs],
                                      g_ref[0, sl, cs], gn, st, bounded_decay)
                o_ref[0, sl, cs] = out.astype(o_ref.dtype)
            st_ref[hh] = st

    @pl.when(f_min >= HGRN_MIN_SAFE_F)
    def _():
        run(True)

    @pl.when(jnp.logical_not(f_min >= HGRN_MIN_SAFE_F))
    def _():
        run(False)


def _hgrn(hp, lb, gn, n_heads):
    b, t, _ = hp.shape
    d = HEAD_DIM
    ts = HGRN_STEP
    hps = HGRN_HEADS_PER_STEP
    assert n_heads % hps == 0
    ng = n_heads // hps

    def col(group):
        return pl.BlockSpec((1, ts, hps * d), lambda bi, hi, ti: (bi, ti, group * ng + hi))

    return pl.pallas_call(
        _hgrn_kernel,
        grid=(b, ng, t // ts),
        in_specs=[col(0), col(1), col(2), col(3),
                  pl.BlockSpec((1, hps * d), lambda bi, hi, ti: (0, hi)),
                  pl.BlockSpec((1, d), lambda bi, hi, ti: (0, 0))],
        out_specs=pl.BlockSpec((1, ts, hps * d), lambda bi, hi, ti: (bi, ti, hi)),
        out_shape=jax.ShapeDtypeStruct((b, t, n_heads * d), BF16),
        scratch_shapes=[pltpu.VMEM((hps, d, d), F32)],
        compiler_params=_cparams("parallel", "parallel", "arbitrary"),
        name="hgrn2",
    )(hp, hp, hp, hp, lb, gn)


def _outproj_kernel(oa_ref, or_ref, x_ref, w_ref, g_ref, wr_ref, br_ref,
                    x1_ref, h2_ref, rt_ref, cnt_out_ref, cnt_ref):
    wa = oa_ref.shape[1]
    x1 = (x_ref[...]
          + jnp.dot(oa_ref[...], w_ref[0:wa, :], preferred_element_type=F32)
          + jnp.dot(or_ref[...], w_ref[wa:, :], preferred_element_type=F32))
    x1_ref[...] = x1
    h2 = x1 * lax.rsqrt(jnp.mean(x1 * x1, axis=-1, keepdims=True) + RMS_EPS) * g_ref[...]
    h2_ref[...] = h2

    h_hi = h2.astype(BF16)
    h_mid = (h2 - h_hi.astype(F32)).astype(BF16)
    part = jnp.dot(h_hi, wr_ref[...], preferred_element_type=F32)
    logits = (part[:, :LANES] + part[:, LANES:] + br_ref[...]
              + jnp.dot(h_mid, wr_ref[:, :LANES], preferred_element_type=F32))
    lane = lax.broadcasted_iota(jnp.int32, logits.shape, 1)
    big = jnp.int32(4 * LANES)
    ninf = -jnp.inf

    lg = jnp.where(lane < N_GROUPS, logits, ninf)
    mg = jnp.max(lg, axis=1, keepdims=True)
    gidx = jnp.min(jnp.where(lg == mg, lane, big), axis=1, keepdims=True)
    grp_w = 1.0 / jnp.sum(jnp.exp(lg - mg), axis=1, keepdims=True)

    lo = N_GROUPS + EXPERTS_PER_GROUP * gidx
    le = jnp.where((lane >= lo) & (lane < lo + EXPERTS_PER_GROUP), logits, ninf)
    m1 = jnp.max(le, axis=1, keepdims=True)
    i1 = jnp.min(jnp.where(le == m1, lane, big), axis=1, keepdims=True)
    le2 = jnp.where(lane == i1, ninf, le)
    m2 = jnp.max(le2, axis=1, keepdims=True)
    i2 = jnp.min(jnp.where(le2 == m2, lane, big), axis=1, keepdims=True)
    r21 = jnp.exp(m2 - m1)
    w1 = grp_w / (1.0 + r21)
    w2 = grp_w * r21 / (1.0 + r21)
    e1 = i1 - N_GROUPS
    e2 = i2 - N_GROUPS

    @pl.when(pl.program_id(0) == 0)
    def _():
        cnt_ref[...] = jnp.zeros_like(cnt_ref)

    tm = logits.shape[0]
    onehot = jnp.where((lane == e1) | (lane == e2), 1.0, 0.0)
    rr = lax.broadcasted_iota(jnp.int32, (tm, tm), 0)
    rc = lax.broadcasted_iota(jnp.int32, (tm, tm), 1)
    before = jnp.where(rr > rc, 1.0, 0.0).astype(BF16)
    prefix = (jnp.dot(before, onehot.astype(BF16), preferred_element_type=F32) + cnt_ref[0:1, :])
    rank1 = jnp.sum(jnp.where(lane == e1, prefix, 0.0), axis=1, keepdims=True)
    rank2 = jnp.sum(jnp.where(lane == e2, prefix, 0.0), axis=1, keepdims=True)
    total = cnt_ref[0:1, :] + jnp.sum(onehot, axis=0, keepdims=True)
    cnt_ref[0:1, :] = total
    cnt_out_ref[...] = jnp.broadcast_to(total, cnt_out_ref.shape)

    cols = [e1.astype(F32), e2.astype(F32), w1, w2, rank1, rank2]
    rt = jnp.zeros(logits.shape, F32)
    for ci, cv in enumerate(cols):
        rt = jnp.where(lane == ci, cv, rt)
    rt_ref[...] = rt


def _outproj(oa, orec, x2, w_out, g, wr, br, tm=256):
    n, d = x2.shape
    wa = oa.shape[1]
    wrc = orec.shape[1]
    row = lambda i: (i, 0)
    const = lambda i: (0, 0)
    return pl.pallas_call(
        _outproj_kernel,
        grid=(n // tm,),
        in_specs=[
            pl.BlockSpec((tm, wa), row),
            pl.BlockSpec((tm, wrc), row),
            pl.BlockSpec((tm, d), row),
            pl.BlockSpec((wa + wrc, d), const),
            pl.BlockSpec((1, d), const),
            pl.BlockSpec((d, 2 * LANES), const),
            pl.BlockSpec((1, LANES), const),
        ],
        out_specs=[pl.BlockSpec((tm, d), row), pl.BlockSpec((tm, d), row),
                   pl.BlockSpec((tm, LANES), row), pl.BlockSpec((8, LANES), const)],
        out_shape=[jax.ShapeDtypeStruct((n, d), F32), jax.ShapeDtypeStruct((n, d), F32),
                   jax.ShapeDtypeStruct((n, LANES), F32), jax.ShapeDtypeStruct((8, LANES), F32)],
        scratch_shapes=[pltpu.VMEM((8, LANES), F32)],
        compiler_params=_cparams("arbitrary"),
        name="outproj_route",
    )(oa, orec, x2, w_out, g, wr, br)


WEIGHT_CAST_ROWS = 256


def _cast_rows(dst_ref, src_ref):
    rows = dst_ref.shape[0]
    step = min(rows, WEIGHT_CAST_ROWS)
    assert rows % step == 0

    def body(c, carry):
        sl = pl.ds(pl.multiple_of(c * step, step), step)
        dst_ref[sl, :] = src_ref[0, sl, :].astype(BF16)
        return carry

    lax.fori_loop(0, rows // step, body, 0)


def _moe_up_kernel(te_ref, nu_ref, first_ref, src_ref, h2_hbm, wg_ref, wu_ref, hid_ref,
                   xbuf0, xbuf1, sems, wgb, wub):
    s = pl.program_id(0)
    nu = nu_ref[0]
    tm = MOE_TILE
    xbufs = (xbuf0, xbuf1)

    def start_rows(tile, slot, r0, count, zero=0):
        for r in range(r0, r0 + count):
            tok = src_ref[tile * tm + r] + zero
            pltpu.make_async_copy(h2_hbm.at[pl.ds(tok, 1), :], xbufs[slot].at[pl.ds(r, 1), :],
                                  sems.at[slot]).start()

    def wait_gather(slot):
        pltpu.make_async_copy(h2_hbm.at[pl.ds(0, tm), :], xbufs[slot], sems.at[slot]).wait()

    def gate_up(slot, next_tile=None, next_slot=None):
        third = tm // 3

        def zero_after(v):
            return lax.bitcast_convert_type(jnp.abs(v), jnp.int32) >> 31

        if next_tile is not None:
            start_rows(next_tile, next_slot, 0, third)
        x = xbufs[slot][...].astype(BF16)
        a = jnp.dot(x, wgb[...], preferred_element_type=F32)
        if next_tile is not None:
            start_rows(next_tile, next_slot, third, third, zero=zero_after(a[0, 0]))
            start_rows(next_tile, next_slot, 2 * third, tm - 2 * third,
                       zero=zero_after(a[tm - 1, a.shape[1] - 1]))
        u = jnp.dot(x, wub[...], preferred_element_type=F32)
        hid_ref[...] = (a * _sigmoid(a) * u).astype(BF16)

    @pl.when(s == 0)
    def _():
        start_rows(s, 0, 0, tm)

    @pl.when((s >= 1) & (s <= nu) & (first_ref[jnp.maximum(s - 1, 0)] == 1))
    def _():
        _cast_rows(wgb, wg_ref)
        _cast_rows(wub, wu_ref)

    for parity in range(2):
        @pl.when((s >= 1) & (s < nu) & (s % 2 == parity))
        def _(parity=parity):
            wait_gather(1 - parity)
            gate_up(1 - parity, s, parity)

        @pl.when((s >= 1) & (s == nu) & (s % 2 == parity))
        def _(parity=parity):
            wait_gather(1 - parity)
            gate_up(1 - parity)

    @pl.when(s > nu)
    def _():
        hid_ref[...] = jnp.zeros_like(hid_ref)


def _moe_down_kernel(te_ref, nu_ref, first_ref, hid_ref, wd_ref, y_ref, wdb):
    n = pl.program_id(0)

    @pl.when((n < nu_ref[0]) & (first_ref[n] == 1))
    def _():
        _cast_rows(wdb, wd_ref)

    @pl.when(n < nu_ref[0])
    def _():
        y_ref[...] = jnp.dot(hid_ref[...], wdb[...], preferred_element_type=F32)

    @pl.when(n >= nu_ref[0])
    def _():
        y_ref[...] = jnp.zeros_like(y_ref)


def _moe(tile_expert, n_used, first, src, h2, wg, wu, wd):
    p = src.shape[0]
    d, f = wg.shape[1:]
    tm = MOE_TILE
    n_tiles = p // tm
    prev = lambda s: jnp.maximum(s - 1, 0)
    hid = pl.pallas_call(
        _moe_up_kernel,
        grid_spec=pltpu.PrefetchScalarGridSpec(
            num_scalar_prefetch=4,
            grid=(n_tiles + 1,),
            in_specs=[
                pl.BlockSpec(memory_space=pl.ANY),
                pl.BlockSpec((1, d, f), lambda s, te, nu, fi, sr: (te[prev(s)], 0, 0)),
                pl.BlockSpec((1, d, f), lambda s, te, nu, fi, sr: (te[prev(s)], 0, 0)),
            ],
            out_specs=pl.BlockSpec((tm, f), lambda s, te, nu, fi, sr: (prev(s), 0)),
            scratch_shapes=[pltpu.VMEM((tm, d), F32), pltpu.VMEM((tm, d), F32),
                            pltpu.SemaphoreType.DMA((2,)),
                            pltpu.VMEM((d, f), BF16), pltpu.VMEM((d, f), BF16)],
        ),
        out_shape=jax.ShapeDtypeStruct((p, f), BF16),
        compiler_params=_cparams("arbitrary"),
        name="moe_gate_up",
    )(tile_expert, n_used, first, src, h2, wg, wu)
    return pl.pallas_call(
        _moe_down_kernel,
        grid_spec=pltpu.PrefetchScalarGridSpec(
            num_scalar_prefetch=3,
            grid=(n_tiles,),
            in_specs=[
                pl.BlockSpec((tm, f), lambda n, te, nu, fi: (n, 0)),
                pl.BlockSpec((1, f, d), lambda n, te, nu, fi: (te[n], 0, 0)),
            ],
            out_specs=pl.BlockSpec((tm, d), lambda n, te, nu, fi: (n, 0)),
            scratch_shapes=[pltpu.VMEM((f, d), BF16)],
        ),
        out_shape=jax.ShapeDtypeStruct((p, d), F32),
        compiler_params=_cparams("arbitrary"),
        name="moe_down",
    )(tile_expert, n_used, first, hid, wd)


COMBINE_TILE = 256


def _final_kernel(pos_ref, x1_ref, ys_hbm, rt_ref, g_ref, o_ref, ya0, yb0, ya1, yb1, sems):
    s = pl.program_id(0)
    n_tiles = pl.num_programs(0) - 1
    tm = COMBINE_TILE
    n_tok = pos_ref.shape[0] // 2
    ybufs = ((ya0, yb0), (ya1, yb1))

    def start_gather(tile, slot):
        for k in range(2):
            for r in range(tm):
                row = pos_ref[k * n_tok + tile * tm + r]
                pltpu.make_async_copy(ys_hbm.at[pl.ds(row, 1), :],
                                      ybufs[slot][k].at[pl.ds(r, 1), :], sems.at[slot]).start()

    def wait_gather(slot):
        for k in range(2):
            pltpu.make_async_copy(ys_hbm.at[pl.ds(0, tm), :], ybufs[slot][k], sems.at[slot]).wait()

    def finish(slot):
        rt = rt_ref[...]
        x2 = (x1_ref[...] + rt[:, 2:3] * ybufs[slot][0][...] + rt[:, 3:4] * ybufs[slot][1][...])
        o_ref[...] = (x2 * lax.rsqrt(jnp.mean(x2 * x2, axis=-1, keepdims=True) + RMS_EPS)
                      * g_ref[...])

    @pl.when(s == 0)
    def _():
        start_gather(s, 0)

    for parity in range(2):
        @pl.when((s >= 1) & (s < n_tiles) & (s % 2 == parity))
        def _(parity=parity):
            wait_gather(1 - parity)
            start_gather(s, parity)
            finish(1 - parity)

        @pl.when((s == n_tiles) & (s % 2 == parity))
        def _(parity=parity):
            wait_gather(1 - parity)
            finish(1 - parity)


def _final(x1, ys, pos, rt, g):
    n, d = x1.shape
    tm = COMBINE_TILE
    prev = lambda s: jnp.maximum(s - 1, 0)
    return pl.pallas_call(
        _final_kernel,
        grid_spec=pltpu.PrefetchScalarGridSpec(
            num_scalar_prefetch=1,
            grid=(n // tm + 1,),
            in_specs=[pl.BlockSpec((tm, d), lambda s, p: (prev(s), 0)),
                      pl.BlockSpec(memory_space=pl.ANY),
                      pl.BlockSpec((tm, LANES), lambda s, p: (prev(s), 0)),
                      pl.BlockSpec((1, d), lambda s, p: (0, 0))],
            out_specs=pl.BlockSpec((tm, d), lambda s, p: (prev(s), 0)),
            scratch_shapes=[pltpu.VMEM((tm, d), F32)] * 4 + [pltpu.SemaphoreType.DMA((2,))],
        ),
        out_shape=jax.ShapeDtypeStruct((n, d), F32),
        compiler_params=_cparams("arbitrary"),
        name="combine_norm",
    )(pos, x1, ys, rt, g)


def _dispatch_plan(rt, counts, n_tokens):
    tm = MOE_TILE
    n_rows = 2 * n_tokens + N_EXPERTS * tm
    n_tiles = n_rows // tm
    ids = rt[:, 0:6].astype(jnp.int32)
    ef = jnp.concatenate([ids[:, 0], ids[:, 1]])
    rank = jnp.concatenate([ids[:, 4], ids[:, 5]])
    counts = counts[0, :N_EXPERTS].astype(jnp.int32)
    tiles_per = (counts + tm - 1) // tm
    tile_end = jnp.cumsum(tiles_per)
    row_start = (tile_end - tiles_per) * tm
    pos = row_start[ef] + rank
    tok = jnp.concatenate([jnp.arange(n_tokens, dtype=jnp.int32)] * 2)
    src = jnp.zeros((n_rows,), jnp.int32).at[pos].set(tok)
    n_used = tile_end[-1]
    tile_ids = jnp.arange(n_tiles, dtype=jnp.int32)
    tile_expert = jnp.sum((tile_ids[:, None] >= tile_end[None, :]).astype(jnp.int32), axis=1)
    last_expert = jnp.sum((n_used - 1 >= tile_end).astype(jnp.int32))
    tile_expert = jnp.where(tile_ids < n_used, tile_expert, last_expert).astype(jnp.int32)
    first = jnp.concatenate([jnp.ones((1,), jnp.int32),
                             (tile_expert[1:] != tile_expert[:-1]).astype(jnp.int32)])
    return (src, pos[:n_tokens], pos[n_tokens:], tile_expert, first,
            n_used.reshape(1).astype(jnp.int32))


def kernel(x, norm_mix_g, w_in, hgrn_lb_logits, hgrn_out_norm_g, w_out, norm_ffn_g, w_group_router,
           b_group_router, w_expert_router, b_expert_router, w_gate, w_up, w_down, final_norm_g):
    b, t, d = x.shape
    n = b * t
    depth = w_in.shape[0]
    assert depth == 1, "the final norm is fused into the combine step of the only layer"
    attn_w = d // 2
    n_heads = attn_w // HEAD_DIM
    x2 = x.reshape(n, d)
    lb_all = jnp.cumsum(jax.nn.softmax(hgrn_lb_logits.astype(F32), axis=0), axis=0)[:depth]
    slope2 = jnp.asarray(2.0 ** (-8.0 * np.arange(1, n_heads + 1) / n_heads), dtype=F32) * F32(LOG2E)
    s_hi = slope2.astype(BF16).astype(F32)
    s_mid = (slope2 - s_hi).astype(BF16).astype(F32)
    s_lo = (slope2 - s_hi - s_mid).astype(BF16).astype(F32)
    slope_pieces = jnp.stack([s_hi, s_mid, s_lo])
    qkv_scale = jnp.concatenate([jnp.full((attn_w,), HEAD_DIM ** -0.5 * LOG2E, F32),
                                 jnp.ones((2 * attn_w,), F32)]).reshape(1, -1)

    for l in range(depth):
        g_mix = norm_mix_g[l].reshape(1, d)
        x2, w_in_l, w_out_l = lax.optimization_barrier(
            (x2, w_in[l].astype(BF16), w_out[l].astype(BF16)))
        qkv = _inproj(x2, g_mix, w_in_l, 0, qkv_scale, BF16)
        hp = _inproj(x2, g_mix, w_in_l, 3 * attn_w, jnp.ones((1, 4 * (d - attn_w)), F32), F32)
        o_attn = _moba(qkv.reshape(b, t, 3 * attn_w), slope_pieces, n_heads)
        o_rec = _hgrn(hp.reshape(b, t, -1), lb_all[l].reshape(1, -1),
                      hgrn_out_norm_g[l].reshape(1, HEAD_DIM), n_heads)

        wr = jnp.concatenate(
            [w_group_router[l],
             jnp.transpose(w_expert_router[l], (1, 0, 2)).reshape(d, N_EXPERTS)], axis=1)
        wr = jnp.pad(wr, ((0, 0), (0, LANES - wr.shape[1])))
        wr_hi = wr.astype(BF16)
        wr = jnp.concatenate([wr_hi, (wr - wr_hi.astype(F32)).astype(BF16)], axis=1)
        br =jnp.concatenate([b_group_router[l], b_expert_router[l].reshape(-1)])
        br = jnp.pad(br, (0, LANES - br.shape[0])).reshape(1, LANES)
        x1, h2, rt, counts = _outproj(o_attn.reshape(n, attn_w), o_rec.reshape(n, -1), x2,
                                      w_out_l, norm_ffn_g[l].reshape(1, d), wr, br)
        src, pos1, pos2, tile_expert, first, n_used = _dispatch_plan(rt, counts, n)
        ys = _moe(tile_expert, n_used, first, src, h2, w_gate[l], w_up[l], w_down[l])
        x2 = _final(x1, ys, jnp.concatenate([pos1, pos2]), rt, final_norm_g.reshape(1, d))
    return x2.reshape(b, t, d)
```

```python
import functools

import jax
import jax.numpy as jnp
import numpy as np
from jax import lax
from jax.experimental import pallas as pl
from jax.experimental.pallas import tpu as pltpu

F32 = jnp.float32
BF16 = jnp.bfloat16
HIGHEST = lax.Precision.HIGHEST

HEAD_DIM = 128
MOBA_BLOCK = 256
MOBA_TOPK = 3
N_GROUPS = 4
EXPERTS_PER_GROUP = 4
N_EXPERTS = N_GROUPS * EXPERTS_PER_GROUP
RMS_EPS = 1e-6

LANES = 128
VMEM_LIMIT_BYTES = 56 * 1024 * 1024

HGRN_CHUNK = 64
HGRN_SUB = 16
HGRN_STEP = 256
HGRN_HEADS_PER_STEP = 8
HGRN_MIN_SAFE_F = 2.0 ** -7
MOE_TILE = 256
NT_DIMS = (((1,), (1,)), ((), ()))
TN_DIMS = (((0,), (0,)), ((), ()))


def _cparams(*sem):
    return pltpu.CompilerParams(dimension_semantics=sem, vmem_limit_bytes=VMEM_LIMIT_BYTES)


def _inproj_kernel(x_ref, g_ref, w_ref, cs_ref, o_ref, hn_ref):
    @pl.when(pl.program_id(1) == 0)
    def _():
        x = x_ref[...]
        ms = jnp.mean(x * x, axis=-1, keepdims=True)
        hn_ref[...] = (x * lax.rsqrt(ms + RMS_EPS) * g_ref[...]).astype(BF16)

    acc = jnp.dot(hn_ref[...], w_ref[...], preferred_element_type=F32)
    o_ref[...] = (acc * cs_ref[...]).astype(o_ref.dtype)


def _inproj(x2, g, w, col0, colscale, out_dtype, tm=1024, tn=1024):
    n, d = x2.shape
    cols = colscale.shape[1]
    j0 = col0 // tn
    return pl.pallas_call(
        _inproj_kernel,
        grid=(n // tm, cols // tn),
        in_specs=[
            pl.BlockSpec((tm, d), lambda i, j: (i, 0)),
            pl.BlockSpec((1, d), lambda i, j: (0, 0)),
            pl.BlockSpec((d, tn), lambda i, j: (0, j0 + j)),
            pl.BlockSpec((1, tn), lambda i, j: (0, j)),
        ],
        out_specs=pl.BlockSpec((tm, tn), lambda i, j: (i, j)),
        out_shape=jax.ShapeDtypeStruct((n, cols), out_dtype),
        scratch_shapes=[pltpu.VMEM((tm, d), BF16)],
        compiler_params=_cparams("parallel", "arbitrary"),
        name="inproj",
    )(x2, g, w, colscale)


MOBA_HEADS_PER_STEP = 4
MOBA_EXTRA_POS = 0
MOBA_EXTRA_SEL = 8
MASK_BIG = 2.0 ** 60
LOG2E = 1.4426950408889634


def _moba_key_extras(t):
    nb = t // MOBA_BLOCK
    assert MOBA_EXTRA_SEL + nb <= LANES and MOBA_BLOCK <= 256
    pos = np.arange(t)
    kx = np.zeros((t, LANES), np.float32)
    kx[:, MOBA_EXTRA_POS:MOBA_EXTRA_POS + 3] = ((pos // MOBA_BLOCK) * MOBA_BLOCK)[:, None]
    kx[:, MOBA_EXTRA_POS + 3:MOBA_EXTRA_POS + 6] = (pos % MOBA_BLOCK)[:, None]
    kx[pos, MOBA_EXTRA_SEL + pos // MOBA_BLOCK] = -MASK_BIG
    return jnp.asarray(kx, dtype=BF16)


def _moba_kernel(sl_ref, q_ref, k_ref, v_ref, kx_ref, o_ref, kaug_ref, kmean_ref, *, nb):
    hg = pl.program_id(1)
    i = pl.program_id(2)
    bs = MOBA_BLOCK
    dh = HEAD_DIM
    nbp = kmean_ref.shape[1] // 4
    nx = 8 + nbp

    @pl.when(i == 0)
    def _():
        for hh in range(MOBA_HEADS_PER_STEP):
            kaug_ref[hh, :, 0:dh] = k_ref[0, :, hh * dh:(hh + 1) * dh]
            kaug_ref[hh, :, dh:] = kx_ref[...]
            rows = [jnp.mean(k_ref[0, j * bs:(j + 1) * bs, hh * dh:(hh + 1) * dh].astype(F32),
                             axis=0, keepdims=True) for j in range(nb)]
            if nbp > nb:
                rows.append(jnp.zeros((nbp - nb, dh), F32))
            km = jnp.concatenate(rows, axis=0)
            hi = km.astype(BF16)
            mid = (km - hi.astype(F32)).astype(BF16)
            lo = (km - hi.astype(F32) - mid.astype(F32)).astype(BF16)
            kmean_ref[hh] = jnp.concatenate([hi, mid, lo, jnp.zeros_like(hi)], axis=0)

    r8 = lax.broadcasted_iota(jnp.int32, (8, bs), 0)
    piece = r8 % 3
    er = lax.broadcasted_iota(jnp.int32, (nx, LANES), 0)
    ec = lax.broadcasted_iota(jnp.int32, (nx, LANES), 1)
    embed = jnp.where(er == ec, 1.0, 0.0).astype(BF16)

    qas = []
    for hh in range(MOBA_HEADS_PER_STEP):
        h = hg * MOBA_HEADS_PER_STEP + hh
        q = q_ref[0, :, hh * dh:(hh + 1) * dh]
        g4 = lax.dot_general(kmean_ref[hh], q, NT_DIMS, preferred_element_type=F32)
        gate_t = g4[0:nbp] + g4[nbp:2 * nbp] + g4[2 * nbp:3 * nbp]
        blk = lax.broadcasted_iota(jnp.int32, gate_t.shape, 0)
        rank = jnp.zeros(gate_t.shape, F32)
        for jp in range(nb - 1):
            other = gate_t[jp:jp + 1, :]
            beats = (other > gate_t) | ((other == gate_t) & (blk > jp))
            rank = rank + jnp.where(beats, (jp < i).astype(F32), 0.0)
        notsel_t = jnp.where((blk < i) & (rank >= MOBA_TOPK), 1.0, 0.0)
        slope_rows = jnp.where(r8 >= 6, 0.0,
                               jnp.where(piece == 0, sl_ref[0, h],
                                         jnp.where(piece == 1, sl_ref[1, h], sl_ref[2, h])))
        qx_t = jnp.concatenate([slope_rows, notsel_t], axis=0).astype(BF16)
        qx = lax.dot_general(qx_t, embed, TN_DIMS, preferred_element_type=F32)
        qas.append(jnp.concatenate([q, qx.astype(BF16)], axis=1))

    rowi = lax.broadcasted_iota(jnp.int32, (bs, bs), 0)
    coli = lax.broadcasted_iota(jnp.int32, (bs, bs), 1)

    for c in range(nb):
        @pl.when(i == c)
        def _(c=c):
            n = (c + 1) * bs
            for hh in range(MOBA_HEADS_PER_STEP):
                cs = slice(hh * dh, (hh + 1) * dh)
                s = lax.dot_general(qas[hh], kaug_ref[hh, 0:n, :], NT_DIMS,
                                    preferred_element_type=F32)
                s_own = jnp.where(rowi >= coli, s[:, c * bs:], -jnp.inf)
                m = jnp.max(s_own, axis=1, keepdims=True)
                if c > 0:
                    s_past = s[:, :c * bs]
                    m = jnp.maximum(m, jnp.max(s_past, axis=1, keepdims=True))
                p_own = jnp.exp2(s_own - m)
                l = jnp.sum(p_own, axis=1, keepdims=True)
                acc = jnp.dot(p_own.astype(BF16), v_ref[0, c * bs:n, cs],
                              preferred_element_type=F32)
                if c > 0:
                    p_past = jnp.exp2(s_past - m)
                    l = l + jnp.sum(p_past, axis=1, keepdims=True)
                    acc = acc + jnp.dot(p_past.astype(BF16), v_ref[0, 0:c * bs, cs],
                                        preferred_element_type=F32)
                o_ref[0, :, cs] = (acc / l).astype(o_ref.dtype)


def _moba(qkv, slope_pieces, n_heads):
    b, t, _ = qkv.shape
    bs = MOBA_BLOCK
    nb = t // bs
    nbp = -(-nb // 8) * 8
    hps = MOBA_HEADS_PER_STEP
    assert n_heads % hps == 0
    ng = n_heads // hps
    w = hps * HEAD_DIM
    kern = functools.partial(_moba_kernel, nb=nb)
    return pl.pallas_call(
        kern,
        grid=(b, ng, nb),
        in_specs=[
            pl.BlockSpec(memory_space=pltpu.SMEM),
            pl.BlockSpec((1, bs, w), lambda bi, hi, qi: (bi, qi, hi)),
            pl.BlockSpec((1, t, w), lambda bi, hi, qi: (bi, 0, ng + hi)),
            pl.BlockSpec((1, t, w), lambda bi, hi, qi: (bi, 0, 2 * ng + hi)),
            pl.BlockSpec((t, LANES), lambda bi, hi, qi: (0, 0)),
        ],
        out_specs=pl.BlockSpec((1, bs, w), lambda bi, hi, qi: (bi, qi, hi)),
        out_shape=jax.ShapeDtypeStruct((b, t, n_heads * HEAD_DIM), BF16),
        scratch_shapes=[
            pltpu.VMEM((hps, t, HEAD_DIM + LANES), BF16),
            pltpu.VMEM((hps, 4 * nbp, HEAD_DIM), BF16),
        ],
        compiler_params=_cparams("parallel", "parallel", "arbitrary"),
        name="moba",
    )(slope_pieces, qkv, qkv, qkv, _moba_key_extras(t))


def _sigmoid(x):
    return 1.0 / (1.0 + jnp.exp(-x))


def _hgrn_chunk(q, f, iv, g, gn, st, bounded_decay):
    c = HGRN_CHUNK
    sub = HGRN_SUB
    half = sub // 2
    nsub = c // sub
    qf = q * _sigmoid(q)
    kf = jnp.maximum(1.0 - f, 0.0)
    r = lax.broadcasted_iota(jnp.int32, (c, c), 0)
    cc = lax.broadcasted_iota(jnp.int32, (c, c), 1)
    tril = jnp.where(r >= cc, 1.0, 0.0).astype(BF16)
    lf = jnp.log2(f)
    dk = lf.shape[1]
    lf_hi = lf.astype(BF16)
    lf_mid = (lf - lf_hi.astype(F32)).astype(BF16)
    lf_lo = (lf - lf_hi.astype(F32) - lf_mid.astype(F32)).astype(BF16)
    b3 = jnp.dot(tril, jnp.concatenate([lf_hi, lf_mid, lf_lo], axis=1),
                 preferred_element_type=F32)
    b2 = b3[:, :dk] + b3[:, dk:2 * dk] + b3[:, 2 * dk:]
    c2 = b2 - jnp.log2(kf)

    inter = lax.dot_general((qf * jnp.exp2(b2)).astype(BF16), st.astype(BF16), NT_DIMS,
                            preferred_element_type=F32)

    lane = lax.broadcasted_iota(jnp.int32, (sub, LANES), 1)
    tsub = lax.broadcasted_iota(jnp.int32, (sub, LANES), 0)
    colid = lax.broadcasted_iota(jnp.int32, (sub, c), 1)
    ones = jnp.ones((LANES, LANES), BF16)
    a_rows = []
    for bi in range(nsub):
        lo = bi * sub
        b_i = b2[lo:lo + sub]
        c_i = c2[lo:lo + sub]
        q_i = qf[lo:lo + sub]
        if bounded_decay:
            qt = q_i * (jnp.exp2(b_i - b2[lo - 1:lo, :]) if bi > 0 else jnp.exp2(b_i))
            kt = jnp.exp2((b2[lo - 1:lo, :] - c2[:lo + sub]) if bi > 0 else -c2[:sub])
            if lo + sub < c:
                kt = jnp.concatenate([kt, jnp.zeros((c - lo - sub, kt.shape[1]), F32)], axis=0)
            cross = lax.dot_general(qt.astype(BF16), kt.astype(BF16), NT_DIMS,
                                    preferred_element_type=F32)
            a_rows.append(jnp.where(colid <= lo + tsub[:, :c], cross, 0.0))
            continue
        pieces = []
        for s in range(sub):
            if s < half:
                pieces.append(q_i * jnp.exp2(b_i - c_i[s:s + 1, :]))
            else:
                pieces.append(q_i[half:] * jnp.exp2(b_i[half:] - c_i[s:s + 1, :]))
        pm = jnp.concatenate(pieces, axis=0).astype(BF16)
        rs = jnp.dot(pm, ones, preferred_element_type=F32)
        rel = lane - lo
        key = jnp.where((rel >= 0) & (rel <= tsub), rel, -1)
        key_lo, key_hi = key[:half], key[half:]
        a_lo = jnp.zeros((half, LANES), F32)
        a_hi = jnp.zeros((half, LANES), F32)
        off = 0
        for s in range(sub):
            if s < half:
                a_lo = jnp.where(key_lo == s, rs[off:off + half], a_lo)
                a_hi = jnp.where(key_hi == s, rs[off + half:off + sub], a_hi)
                off += sub
            else:
                a_hi = jnp.where(key_hi == s, rs[off:off + half], a_hi)
                off += half
        a_blk = jnp.concatenate([a_lo, a_hi], axis=0)[:, :c]
        if bi > 0:
            b0 = b2[lo - 1:lo, :]
            qt = q_i * jnp.exp2(b_i - b0)
            kt = jnp.exp2(jnp.minimum(b0 - c2, 0.0))
            cross = lax.dot_general(qt.astype(BF16), kt.astype(BF16), NT_DIMS,
                                    preferred_element_type=F32)
            a_blk = a_blk + jnp.where(colid < lo, cross, 0.0)
        a_rows.append(a_blk)
    a = jnp.concatenate(a_rows, axis=0)
    intra = jnp.dot(a.astype(BF16), iv.astype(BF16), preferred_element_type=F32)

    b_last = b2[c - 1:c, :]
    khat = jnp.exp2(b_last - c2)
    st_new = st * jnp.exp2(b_last) + lax.dot_general(
        iv.astype(BF16), khat.astype(BF16), TN_DIMS, preferred_element_type=F32)

    o = inter + intra
    y = o * lax.rsqrt(jnp.mean(o * o, axis=-1, keepdims=True) + RMS_EPS) * gn
    return y * (g * _sigmoid(g)), st_new


def _hgrn_kernel(q_ref, f_ref, i_ref, g_ref, lb_ref, gn_ref, o_ref, st_ref):
    @pl.when(pl.program_id(2) == 0)
    def _():
        st_ref[...] = jnp.zeros_like(st_ref)

    gn = gn_ref[...]
    d = HEAD_DIM
    lb = lb_ref[...]
    f_all = lb + (1.0 - lb) * _sigmoid(f_ref[0])
    f_min = jnp.min(f_all)

    def run(bounded_decay):
        for hh in range(HGRN_HEADS_PER_STEP):
            cs = slice(hh * d, (hh + 1) * d)
            st = st_ref[hh]
            for ci in range(HGRN_STEP // HGRN_CHUNK):
                sl = slice(ci * HGRN_CHUNK, (ci + 1) * HGRN_CHUNK)
                out, st = _hgrn_chunk(q_ref[0, sl, cs], f_all[sl, cs], i_ref[0, sl, cs],
                                      g_ref[0, sl, cs], gn, st, bounded_decay)
                o_ref[0, sl, cs] = out.astype(o_ref.dtype)
            st_ref[hh] = st

    @pl.when(f_min >= HGRN_MIN_SAFE_F)
    def _():
        run(True)

    @pl.when(jnp.logical_not(f_min >= HGRN_MIN_SAFE_F))
    def _():
        run(False)


def _hgrn(hp, lb, gn, n_heads):
    b, t, _ = hp.shape
    d = HEAD_DIM
    ts = HGRN_STEP
    hps = HGRN_HEADS_PER_STEP
    assert n_heads % hps == 0
    ng = n_heads // hps

    def col(group):
        return pl.BlockSpec((1, ts, hps * d), lambda bi, hi, ti: (bi, ti, group * ng + hi))

    return pl.pallas_call(
        _hgrn_kernel,
        grid=(b, ng, t // ts),
        in_specs=[col(0), col(1), col(2), col(3),
                  pl.BlockSpec((1, hps * d), lambda bi, hi, ti: (0, hi)),
                  pl.BlockSpec((1, d), lambda bi, hi, ti: (0, 0))],
        out_specs=pl.BlockSpec((1, ts, hps * d), lambda bi, hi, ti: (bi, ti, hi)),
        out_shape=jax.ShapeDtypeStruct((b, t, n_heads * d), BF16),
        scratch_shapes=[pltpu.VMEM((hps, d, d), F32)],
        compiler_params=_cparams("parallel", "parallel", "arbitrary"),
        name="hgrn2",
    )(hp, hp, hp, hp, lb, gn)


def _outproj_kernel(oa_ref, or_ref, x_ref, w_ref, g_ref, wr_ref, br_ref,
                    x1_ref, h2_ref, rt_ref, cnt_out_ref, cnt_ref):
    wa = oa_ref.shape[1]
    x1 = (x_ref[...]
          + jnp.dot(oa_ref[...], w_ref[0:wa, :], preferred_element_type=F32)
          + jnp.dot(or_ref[...], w_ref[wa:, :], preferred_element_type=F32))
    x1_ref[...] = x1
    h2 = x1 * lax.rsqrt(jnp.mean(x1 * x1, axis=-1, keepdims=True) + RMS_EPS) * g_ref[...]
    h2_ref[...] = h2

    h_hi = h2.astype(BF16)
    h_mid = (h2 - h_hi.astype(F32)).astype(BF16)
    part = jnp.dot(h_hi, wr_ref[...], preferred_element_type=F32)
    logits = (part[:, :LANES] + part[:, LANES:] + br_ref[...]
              + jnp.dot(h_mid, wr_ref[:, :LANES], preferred_element_type=F32))
    lane = lax.broadcasted_iota(jnp.int32, logits.shape, 1)
    big = jnp.int32(4 * LANES)
    ninf = -jnp.inf

    lg = jnp.where(lane < N_GROUPS, logits, ninf)
    mg = jnp.max(lg, axis=1, keepdims=True)
    gidx = jnp.min(jnp.where(lg == mg, lane, big), axis=1, keepdims=True)
    grp_w = 1.0 / jnp.sum(jnp.exp(lg - mg), axis=1, keepdims=True)

    lo = N_GROUPS + EXPERTS_PER_GROUP * gidx
    le = jnp.where((lane >= lo) & (lane < lo + EXPERTS_PER_GROUP), logits, ninf)
    m1 = jnp.max(le, axis=1, keepdims=True)
    i1 = jnp.min(jnp.where(le == m1, lane, big), axis=1, keepdims=True)
    le2 = jnp.where(lane == i1, ninf, le)
    m2 = jnp.max(le2, axis=1, keepdims=True)
    i2 = jnp.min(jnp.where(le2 == m2, lane, big), axis=1, keepdims=True)
    r21 = jnp.exp(m2 - m1)
    w1 = grp_w / (1.0 + r21)
    w2 = grp_w * r21 / (1.0 + r21)
    e1 = i1 - N_GROUPS
    e2 = i2 - N_GROUPS

    @pl.when(pl.program_id(0) == 0)
    def _():
        cnt_ref[...] = jnp.zeros_like(cnt_ref)

    tm = logits.shape[0]
    onehot = jnp.where((lane == e1) | (lane == e2), 1.0, 0.0)
    rr = lax.broadcasted_iota(jnp.int32, (tm, tm), 0)
    rc = lax.broadcasted_iota(jnp.int32, (tm, tm), 1)
    before = jnp.where(rr > rc, 1.0, 0.0).astype(BF16)
    prefix = (jnp.dot(before, onehot.astype(BF16), preferred_element_type=F32) + cnt_ref[0:1, :])
    rank1 = jnp.sum(jnp.where(lane == e1, prefix, 0.0), axis=1, keepdims=True)
    rank2 = jnp.sum(jnp.where(lane == e2, prefix, 0.0), axis=1, keepdims=True)
    total = cnt_ref[0:1, :] + jnp.sum(onehot, axis=0, keepdims=True)
    cnt_ref[0:1, :] = total
    cnt_out_ref[...] = jnp.broadcast_to(total, cnt_out_ref.shape)

    cols = [e1.astype(F32), e2.astype(F32), w1, w2, rank1, rank2]
    rt = jnp.zeros(logits.shape, F32)
    for ci, cv in enumerate(cols):
        rt = jnp.where(lane == ci, cv, rt)
    rt_ref[...] = rt


def _outproj(oa, orec, x2, w_out, g, wr, br, tm=256):
    n, d = x2.shape
    wa = oa.shape[1]
    wrc = orec.shape[1]
    row = lambda i: (i, 0)
    const = lambda i: (0, 0)
    return pl.pallas_call(
        _outproj_kernel,
        grid=(n // tm,),
        in_specs=[
            pl.BlockSpec((tm, wa), row),
            pl.BlockSpec((tm, wrc), row),
            pl.BlockSpec((tm, d), row),
            pl.BlockSpec((wa + wrc, d), const),
            pl.BlockSpec((1, d), const),
            pl.BlockSpec((d, 2 * LANES), const),
            pl.BlockSpec((1, LANES), const),
        ],
        out_specs=[pl.BlockSpec((tm, d), row), pl.BlockSpec((tm, d), row),
                   pl.BlockSpec((tm, LANES), row), pl.BlockSpec((8, LANES), const)],
        out_shape=[jax.ShapeDtypeStruct((n, d), F32), jax.ShapeDtypeStruct((n, d), F32),
                   jax.ShapeDtypeStruct((n, LANES), F32), jax.ShapeDtypeStruct((8, LANES), F32)],
        scratch_shapes=[pltpu.VMEM((8, LANES), F32)],
        compiler_params=_cparams("arbitrary"),
        name="outproj_route",
    )(oa, orec, x2, w_out, g, wr, br)


WEIGHT_CAST_ROWS = 256


def _cast_rows(dst_ref, src_ref):
    rows = dst_ref.shape[0]
    step = min(rows, WEIGHT_CAST_ROWS)
    assert rows % step == 0

    def body(c, carry):
        sl = pl.ds(pl.multiple_of(c * step, step), step)
        dst_ref[sl, :] = src_ref[sl, :].astype(BF16)
        return carry

    lax.fori_loop(0, rows // step, body, 0)


def _moe_kernel(te_ref, nu_ref, first_ref, nxt_ref, src_ref, h2_hbm, wg_hbm, wu_hbm, wd_hbm, y_ref,
                xbuf0, xbuf1, sems, wsems, wg32, wu32, wd32, wgb, wub, wdb):
    s = pl.program_id(0)
    nu = nu_ref[0]
    tm = MOE_TILE
    xbufs = (xbuf0, xbuf1)
    prev_tile = jnp.maximum(s - 1, 0)

    def weight_copies(e):
        return (pltpu.make_async_copy(wg_hbm.at[e], wg32, wsems.at[0]),
                pltpu.make_async_copy(wu_hbm.at[e], wu32, wsems.at[1]),
                pltpu.make_async_copy(wd_hbm.at[e], wd32, wsems.at[2]))

    def start_gather(tile, slot):
        for r in range(tm):
            tok = src_ref[tile * tm + r]
            pltpu.make_async_copy(h2_hbm.at[pl.ds(tok, 1), :], xbufs[slot].at[pl.ds(r, 1), :],
                                  sems.at[slot]).start()

    def wait_gather(slot):
        pltpu.make_async_copy(h2_hbm.at[pl.ds(0, tm), :], xbufs[slot], sems.at[slot]).wait()

    def ffn(slot):
        x = xbufs[slot][...].astype(BF16)
        a = jnp.dot(x, wgb[...], preferred_element_type=F32)
        u = jnp.dot(x, wub[...], preferred_element_type=F32)
        hid = (a * _sigmoid(a) * u).astype(BF16)
        y_ref[...] = jnp.dot(hid, wdb[...], preferred_element_type=F32)

    @pl.when(s == 0)
    def _():
        for cp in weight_copies(te_ref[0]):
            cp.start()
        start_gather(s, 0)

    @pl.when((s >= 1) & (s <= nu) & (first_ref[prev_tile] == 1))
    def _():
        for cp in weight_copies(te_ref[prev_tile]):
            cp.wait()
        _cast_rows(wgb, wg32)
        _cast_rows(wub, wu32)
        _cast_rows(wdb, wd32)

        @pl.when(nxt_ref[prev_tile] >= 0)
        def _():
            for cp in weight_copies(nxt_ref[prev_tile]):
                cp.start()

    for parity in range(2):
        @pl.when((s >= 1) & (s < nu) & (s % 2 == parity))
        def _(parity=parity):
            wait_gather(1 - parity)
            start_gather(s, parity)
            ffn(1 - parity)

        @pl.when((s >= 1) & (s == nu) & (s % 2 == parity))
        def _(parity=parity):
            wait_gather(1 - parity)
            ffn(1 - parity)

    @pl.when(s > nu)
    def _():
        y_ref[...] = jnp.zeros_like(y_ref)


def _moe(tile_expert, n_used, first, nxt, src, h2, wg, wu, wd):
    p = src.shape[0]
    d, f = wg.shape[1:]
    tm = MOE_TILE
    n_tiles = p // tm
    any_spec = pl.BlockSpec(memory_space=pl.ANY)
    return pl.pallas_call(
        _moe_kernel,
        grid_spec=pltpu.PrefetchScalarGridSpec(
            num_scalar_prefetch=5,
            grid=(n_tiles + 1,),
            in_specs=[any_spec, any_spec, any_spec, any_spec],
            out_specs=pl.BlockSpec((tm, d), lambda s, *_: (jnp.maximum(s - 1, 0), 0)),
            scratch_shapes=[pltpu.VMEM((tm, d), F32), pltpu.VMEM((tm, d), F32),
                            pltpu.SemaphoreType.DMA((2,)), pltpu.SemaphoreType.DMA((3,)),
                            pltpu.VMEM((d, f), F32), pltpu.VMEM((d, f), F32), pltpu.VMEM((f, d), F32),
                            pltpu.VMEM((d, f), BF16), pltpu.VMEM((d, f), BF16),
                            pltpu.VMEM((f, d), BF16)],
        ),
        out_shape=jax.ShapeDtypeStruct((p, d), F32),
        compiler_params=_cparams("arbitrary"),
        name="moe_ffn",
    )(tile_expert, n_used, first, nxt, src, h2, wg, wu, wd)


COMBINE_TILE = 256


def _final_kernel(pos_ref, x1_ref, ys_hbm, rt_ref, g_ref, o_ref, ya0, yb0, ya1, yb1, sems):
    s = pl.program_id(0)
    n_tiles = pl.num_programs(0) - 1
    tm = COMBINE_TILE
    n_tok = pos_ref.shape[0] // 2
    ybufs = ((ya0, yb0), (ya1, yb1))

    def start_gather(tile, slot):
        for k in range(2):
            for r in range(tm):
                row = pos_ref[k * n_tok + tile * tm + r]
                pltpu.make_async_copy(ys_hbm.at[pl.ds(row, 1), :],
                                      ybufs[slot][k].at[pl.ds(r, 1), :], sems.at[slot]).start()

    def wait_gather(slot):
        for k in range(2):
            pltpu.make_async_copy(ys_hbm.at[pl.ds(0, tm), :], ybufs[slot][k], sems.at[slot]).wait()

    def finish(slot):
        rt = rt_ref[...]
        x2 = (x1_ref[...] + rt[:, 2:3] * ybufs[slot][0][...] + rt[:, 3:4] * ybufs[slot][1][...])
        o_ref[...] = (x2 * lax.rsqrt(jnp.mean(x2 * x2, axis=-1, keepdims=True) + RMS_EPS)
                      * g_ref[...])

    @pl.when(s == 0)
    def _():
        start_gather(s, 0)

    for parity in range(2):
        @pl.when((s >= 1) & (s < n_tiles) & (s % 2 == parity))
        def _(parity=parity):
            wait_gather(1 - parity)
            start_gather(s, parity)
            finish(1 - parity)

        @pl.when((s == n_tiles) & (s % 2 == parity))
        def _(parity=parity):
            wait_gather(1 - parity)
            finish(1 - parity)


def _final(x1, ys, pos, rt, g):
    n, d = x1.shape
    tm = COMBINE_TILE
    prev = lambda s: jnp.maximum(s - 1, 0)
    return pl.pallas_call(
        _final_kernel,
        grid_spec=pltpu.PrefetchScalarGridSpec(
            num_scalar_prefetch=1,
            grid=(n // tm + 1,),
            in_specs=[pl.BlockSpec((tm, d), lambda s, p: (prev(s), 0)),
                      pl.BlockSpec(memory_space=pl.ANY),
                      pl.BlockSpec((tm, LANES), lambda s, p: (prev(s), 0)),
                      pl.BlockSpec((1, d), lambda s, p: (0, 0))],
            out_specs=pl.BlockSpec((tm, d), lambda s, p: (prev(s), 0)),
            scratch_shapes=[pltpu.VMEM((tm, d), F32)] * 4 + [pltpu.SemaphoreType.DMA((2,))],
        ),
        out_shape=jax.ShapeDtypeStruct((n, d), F32),
        compiler_params=_cparams("arbitrary"),
        name="combine_norm",
    )(pos, x1, ys, rt, g)


def _dispatch_plan(rt, counts, n_tokens):
    tm = MOE_TILE
    n_rows = 2 * n_tokens + N_EXPERTS * tm
    n_tiles = n_rows // tm
    ids = rt[:, 0:6].astype(jnp.int32)
    ef = jnp.concatenate([ids[:, 0], ids[:, 1]])
    rank = jnp.concatenate([ids[:, 4], ids[:, 5]])
    counts = counts[0, :N_EXPERTS].astype(jnp.int32)
    tiles_per = (counts + tm - 1) // tm
    tile_end = jnp.cumsum(tiles_per)
    row_start = (tile_end - tiles_per) * tm
    pos = row_start[ef] + rank
    tok = jnp.concatenate([jnp.arange(n_tokens, dtype=jnp.int32)] * 2)
    src = jnp.zeros((n_rows,), jnp.int32).at[pos].set(tok)
    n_used = tile_end[-1]
    tile_ids = jnp.arange(n_tiles, dtype=jnp.int32)
    tile_expert = jnp.sum((tile_ids[:, None] >= tile_end[None, :]).astype(jnp.int32), axis=1)
    last_expert = jnp.sum((n_used - 1 >= tile_end).astype(jnp.int32))
    tile_expert = jnp.where(tile_ids < n_used, tile_expert, last_expert).astype(jnp.int32)
    first = jnp.concatenate([jnp.ones((1,), jnp.int32),
                             (tile_expert[1:] != tile_expert[:-1]).astype(jnp.int32)])
    eid = jnp.arange(N_EXPERTS, dtype=jnp.int32)
    later = (tiles_per[None, :] > 0) & (eid[None, :] > eid[:, None])
    next_expert = jnp.min(jnp.where(later, eid[None, :], N_EXPERTS), axis=1)
    next_expert = jnp.where(next_expert == N_EXPERTS, -1, next_expert).astype(jnp.int32)
    nxt = next_expert[tile_expert]
    return (src, pos[:n_tokens], pos[n_tokens:], tile_expert, first, nxt,
            n_used.reshape(1).astype(jnp.int32))


def kernel(x, norm_mix_g, w_in, hgrn_lb_logits, hgrn_out_norm_g, w_out, norm_ffn_g, w_group_router,
           b_group_router, w_expert_router, b_expert_router, w_gate, w_up, w_down, final_norm_g):
    b, t, d = x.shape
    n = b * t
    depth = w_in.shape[0]
    assert depth == 1, "the final norm is fused into the combine step of the only layer"
    attn_w = d // 2
    n_heads = attn_w // HEAD_DIM
    x2 = x.reshape(n, d)
    lb_all = jnp.cumsum(jax.nn.softmax(hgrn_lb_logits.astype(F32), axis=0), axis=0)[:depth]
    slope2 = jnp.asarray(2.0 ** (-8.0 * np.arange(1, n_heads + 1) / n_heads), dtype=F32) * F32(LOG2E)
    s_hi = slope2.astype(BF16).astype(F32)
    s_mid = (slope2 - s_hi).astype(BF16).astype(F32)
    s_lo = (slope2 - s_hi - s_mid).astype(BF16).astype(F32)
    slope_pieces = jnp.stack([s_hi, s_mid, s_lo])
    qkv_scale = jnp.concatenate([jnp.full((attn_w,), HEAD_DIM ** -0.5 * LOG2E, F32),
                                 jnp.ones((2 * attn_w,), F32)]).reshape(1, -1)

    for l in range(depth):
        g_mix = norm_mix_g[l].reshape(1, d)
        x2, w_in_l, w_out_l = lax.optimization_barrier(
            (x2, w_in[l].astype(BF16), w_out[l].astype(BF16)))
        qkv = _inproj(x2, g_mix, w_in_l, 0, qkv_scale, BF16)
        hp = _inproj(x2, g_mix, w_in_l, 3 * attn_w, jnp.ones((1, 4 * (d - attn_w)), F32), F32)
        o_attn = _moba(qkv.reshape(b, t, 3 * attn_w), slope_pieces, n_heads)
        o_rec = _hgrn(hp.reshape(b, t, -1), lb_all[l].reshape(1, -1),
                      hgrn_out_norm_g[l].reshape(1, HEAD_DIM), n_heads)

        wr = jnp.concatenate(
            [w_group_router[l],
             jnp.transpose(w_expert_router[l], (1, 0, 2)).reshape(d, N_EXPERTS)], axis=1)
        wr = jnp.pad(wr, ((0, 0), (0, LANES - wr.shape[1])))
        wr_hi = wr.astype(BF16)
        wr = jnp.concatenate([wr_hi, (wr - wr_hi.astype(F32)).astype(BF16)], axis=1)
        br =jnp.concatenate([b_group_router[l], b_expert_router[l].reshape(-1)])
        br = jnp.pad(br, (0, LANES - br.shape[0])).reshape(1, LANES)
        x1, h2, rt, counts = _outproj(o_attn.reshape(n, attn_w), o_rec.reshape(n, -1), x2,
                                      w_out_l, norm_ffn_g[l].reshape(1, d), wr, br)
        src, pos1, pos2, tile_expert, first, nxt, n_used = _dispatch_plan(rt, counts, n)
        ys = _moe(tile_expert, n_used, first, nxt, src, h2, w_gate[l], w_up[l], w_down[l])
        x2 = _final(x1, ys, jnp.concatenate([pos1, pos2]), rt, final_norm_g.reshape(1, d))
    return x2.reshape(b, t, d)
```

```python
import functools

import jax
import jax.numpy as jnp
import numpy as np
from jax import lax
from jax.experimental import pallas as pl
from jax.experimental.pallas import tpu as pltpu

F32 = jnp.float32
BF16 = jnp.bfloat16
HIGHEST = lax.Precision.HIGHEST

HEAD_DIM = 128
MOBA_BLOCK = 256
MOBA_TOPK = 3
N_GROUPS = 4
EXPERTS_PER_GROUP = 4
N_EXPERTS = N_GROUPS * EXPERTS_PER_GROUP
RMS_EPS = 1e-6

LANES = 128
VMEM_LIMIT_BYTES = 56 * 1024 * 1024

HGRN_CHUNK = 64
HGRN_SUB = 16
HGRN_STEP = 256
HGRN_HEADS_PER_STEP = 8
HGRN_MIN_SAFE_F = 2.0 ** -7
MOE_TILE = 256
NT_DIMS = (((1,), (1,)), ((), ()))
TN_DIMS = (((0,), (0,)), ((), ()))


def _cparams(*sem):
    return pltpu.CompilerParams(dimension_semantics=sem, vmem_limit_bytes=VMEM_LIMIT_BYTES)


def _inproj_kernel(x_ref, g_ref, w_ref, cs_ref, o_ref, hn_ref):
    @pl.when(pl.program_id(1) == 0)
    def _():
        x = x_ref[...]
        ms = jnp.mean(x * x, axis=-1, keepdims=True)
        hn_ref[...] = (x * lax.rsqrt(ms + RMS_EPS) * g_ref[...]).astype(BF16)

    acc = jnp.dot(hn_ref[...], w_ref[...], preferred_element_type=F32)
    o_ref[...] = (acc * cs_ref[...]).astype(o_ref.dtype)


def _inproj(x2, g, w, col0, colscale, out_dtype, tm=1024, tn=1024):
    n, d = x2.shape
    cols = colscale.shape[1]
    j0 = col0 // tn
    return pl.pallas_call(
        _inproj_kernel,
        grid=(n // tm, cols // tn),
        in_specs=[
            pl.BlockSpec((tm, d), lambda i, j: (i, 0)),
            pl.BlockSpec((1, d), lambda i, j: (0, 0)),
            pl.BlockSpec((d, tn), lambda i, j: (0, j0 + j)),
            pl.BlockSpec((1, tn), lambda i, j: (0, j)),
        ],
        out_specs=pl.BlockSpec((tm, tn), lambda i, j: (i, j)),
        out_shape=jax.ShapeDtypeStruct((n, cols), out_dtype),
        scratch_shapes=[pltpu.VMEM((tm, d), BF16)],
        compiler_params=_cparams("parallel", "arbitrary"),
        name="inproj",
    )(x2, g, w, colscale)


MOBA_HEADS_PER_STEP = 4
MOBA_EXTRA_POS = 0
MOBA_EXTRA_SEL = 8
MASK_BIG = 2.0 ** 60
LOG2E = 1.4426950408889634


def _moba_key_extras(t):
    nb = t // MOBA_BLOCK
    assert MOBA_EXTRA_SEL + nb <= LANES and MOBA_BLOCK <= 256
    pos = np.arange(t)
    kx = np.zeros((t, LANES), np.float32)
    kx[:, MOBA_EXTRA_POS:MOBA_EXTRA_POS + 3] = ((pos // MOBA_BLOCK) * MOBA_BLOCK)[:, None]
    kx[:, MOBA_EXTRA_POS + 3:MOBA_EXTRA_POS + 6] = (pos % MOBA_BLOCK)[:, None]
    kx[pos, MOBA_EXTRA_SEL + pos // MOBA_BLOCK] = -MASK_BIG
    return jnp.asarray(kx, dtype=BF16)


def _moba_kernel(sl_ref, q_ref, k_ref, v_ref, kx_ref, o_ref, kaug_ref, kmean_ref, *, nb):
    hg = pl.program_id(1)
    i = pl.program_id(2)
    bs = MOBA_BLOCK
    dh = HEAD_DIM
    nbp = kmean_ref.shape[1] // 4
    nx = 8 + nbp

    @pl.when(i == 0)
    def _():
        for hh in range(MOBA_HEADS_PER_STEP):
            kaug_ref[hh, :, 0:dh] = k_ref[0, :, hh * dh:(hh + 1) * dh]
            kaug_ref[hh, :, dh:] = kx_ref[...]
            rows = [jnp.mean(k_ref[0, j * bs:(j + 1) * bs, hh * dh:(hh + 1) * dh].astype(F32),
                             axis=0, keepdims=True) for j in range(nb)]
            if nbp > nb:
                rows.append(jnp.zeros((nbp - nb, dh), F32))
            km = jnp.concatenate(rows, axis=0)
            hi = km.astype(BF16)
            mid = (km - hi.astype(F32)).astype(BF16)
            lo = (km - hi.astype(F32) - mid.astype(F32)).astype(BF16)
            kmean_ref[hh] = jnp.concatenate([hi, mid, lo, jnp.zeros_like(hi)], axis=0)

    r8 = lax.broadcasted_iota(jnp.int32, (8, bs), 0)
    piece = r8 % 3
    er = lax.broadcasted_iota(jnp.int32, (nx, LANES), 0)
    ec = lax.broadcasted_iota(jnp.int32, (nx, LANES), 1)
    embed = jnp.where(er == ec, 1.0, 0.0).astype(BF16)

    qas = []
    for hh in range(MOBA_HEADS_PER_STEP):
        h = hg * MOBA_HEADS_PER_STEP + hh
        q = q_ref[0, :, hh * dh:(hh + 1) * dh]
        g4 = lax.dot_general(kmean_ref[hh], q, NT_DIMS, preferred_element_type=F32)
        gate_t = g4[0:nbp] + g4[nbp:2 * nbp] + g4[2 * nbp:3 * nbp]
        blk = lax.broadcasted_iota(jnp.int32, gate_t.shape, 0)
        rank = jnp.zeros(gate_t.shape, F32)
        for jp in range(nb - 1):
            other = gate_t[jp:jp + 1, :]
            beats = (other > gate_t) | ((other == gate_t) & (blk > jp))
            rank = rank + jnp.where(beats, (jp < i).astype(F32), 0.0)
        notsel_t = jnp.where((blk < i) & (rank >= MOBA_TOPK), 1.0, 0.0)
        slope_rows = jnp.where(r8 >= 6, 0.0,
                               jnp.where(piece == 0, sl_ref[0, h],
                                         jnp.where(piece == 1, sl_ref[1, h], sl_ref[2, h])))
        qx_t = jnp.concatenate([slope_rows, notsel_t], axis=0).astype(BF16)
        qx = lax.dot_general(qx_t, embed, TN_DIMS, preferred_element_type=F32)
        qas.append(jnp.concatenate([q, qx.astype(BF16)], axis=1))

    rowi = lax.broadcasted_iota(jnp.int32, (bs, bs), 0)
    coli = lax.broadcasted_iota(jnp.int32, (bs, bs), 1)

    for c in range(nb):
        @pl.when(i == c)
        def _(c=c):
            n = (c + 1) * bs
            for hh in range(MOBA_HEADS_PER_STEP):
                cs = slice(hh * dh, (hh + 1) * dh)
                s = lax.dot_general(qas[hh], kaug_ref[hh, 0:n, :], NT_DIMS,
                                    preferred_element_type=F32)
                s_own = jnp.where(rowi >= coli, s[:, c * bs:], -jnp.inf)
                m = jnp.max(s_own, axis=1, keepdims=True)
                if c > 0:
                    s_past = s[:, :c * bs]
                    m = jnp.maximum(m, jnp.max(s_past, axis=1, keepdims=True))
                p_own = jnp.exp2(s_own - m)
                l = jnp.sum(p_own, axis=1, keepdims=True)
                acc = jnp.dot(p_own.astype(BF16), v_ref[0, c * bs:n, cs],
                              preferred_element_type=F32)
                if c > 0:
                    p_past = jnp.exp2(s_past - m)
                    l = l + jnp.sum(p_past, axis=1, keepdims=True)
                    acc = acc + jnp.dot(p_past.astype(BF16), v_ref[0, 0:c * bs, cs],
                                        preferred_element_type=F32)
                o_ref[0, :, cs] = (acc / l).astype(o_ref.dtype)


def _moba(qkv, slope_pieces, n_heads):
    b, t, _ = qkv.shape
    bs = MOBA_BLOCK
    nb = t // bs
    nbp = -(-nb // 8) * 8
    hps = MOBA_HEADS_PER_STEP
    assert n_heads % hps == 0
    ng = n_heads // hps
    w = hps * HEAD_DIM
    kern = functools.partial(_moba_kernel, nb=nb)
    return pl.pallas_call(
        kern,
        grid=(b, ng, nb),
        in_specs=[
            pl.BlockSpec(memory_space=pltpu.SMEM),
            pl.BlockSpec((1, bs, w), lambda bi, hi, qi: (bi, qi, hi)),
            pl.BlockSpec((1, t, w), lambda bi, hi, qi: (bi, 0, ng + hi)),
            pl.BlockSpec((1, t, w), lambda bi, hi, qi: (bi, 0, 2 * ng + hi)),
            pl.BlockSpec((t, LANES), lambda bi, hi, qi: (0, 0)),
        ],
        out_specs=pl.BlockSpec((1, bs, w), lambda bi, hi, qi: (bi, qi, hi)),
        out_shape=jax.ShapeDtypeStruct((b, t, n_heads * HEAD_DIM), BF16),
        scratch_shapes=[
            pltpu.VMEM((hps, t, HEAD_DIM + LANES), BF16),
            pltpu.VMEM((hps, 4 * nbp, HEAD_DIM), BF16),
        ],
        compiler_params=_cparams("parallel", "parallel", "arbitrary"),
        name="moba",
    )(slope_pieces, qkv, qkv, qkv, _moba_key_extras(t))


def _sigmoid(x):
    return 1.0 / (1.0 + jnp.exp(-x))


def _hgrn_chunk(q, f, iv, g, gn, st, bounded_decay):
    c = HGRN_CHUNK
    sub = HGRN_SUB
    half = sub // 2
    nsub = c // sub
    qf = q * _sigmoid(q)
    kf = jnp.maximum(1.0 - f, 0.0)
    r = lax.broadcasted_iota(jnp.int32, (c, c), 0)
    cc = lax.broadcasted_iota(jnp.int32, (c, c), 1)
    tril = jnp.where(r >= cc, 1.0, 0.0).astype(BF16)
    lf = jnp.log2(f)
    dk = lf.shape[1]
    lf_hi = lf.astype(BF16)
    lf_mid = (lf - lf_hi.astype(F32)).astype(BF16)
    lf_lo = (lf - lf_hi.astype(F32) - lf_mid.astype(F32)).astype(BF16)
    b3 = jnp.dot(tril, jnp.concatenate([lf_hi, lf_mid, lf_lo], axis=1),
                 preferred_element_type=F32)
    b2 = b3[:, :dk] + b3[:, dk:2 * dk] + b3[:, 2 * dk:]
    c2 = b2 - jnp.log2(kf)

    inter = lax.dot_general((qf * jnp.exp2(b2)).astype(BF16), st.astype(BF16), NT_DIMS,
                            preferred_element_type=F32)

    lane = lax.broadcasted_iota(jnp.int32, (sub, LANES), 1)
    tsub = lax.broadcasted_iota(jnp.int32, (sub, LANES), 0)
    colid = lax.broadcasted_iota(jnp.int32, (sub, c), 1)
    ones = jnp.ones((LANES, LANES), BF16)
    a_rows = []
    for bi in range(nsub):
        lo = bi * sub
        b_i = b2[lo:lo + sub]
        c_i = c2[lo:lo + sub]
        q_i = qf[lo:lo + sub]
        if bounded_decay:
            qt = q_i * (jnp.exp2(b_i - b2[lo - 1:lo, :]) if bi > 0 else jnp.exp2(b_i))
            kt = jnp.exp2((b2[lo - 1:lo, :] - c2[:lo + sub]) if bi > 0 else -c2[:sub])
            if lo + sub < c:
                kt = jnp.concatenate([kt, jnp.zeros((c - lo - sub, kt.shape[1]), F32)], axis=0)
            cross = lax.dot_general(qt.astype(BF16), kt.astype(BF16), NT_DIMS,
                                    preferred_element_type=F32)
            a_rows.append(jnp.where(colid <= lo + tsub[:, :c], cross, 0.0))
            continue
        pieces = []
        for s in range(sub):
            if s < half:
                pieces.append(q_i * jnp.exp2(b_i - c_i[s:s + 1, :]))
            else:
                pieces.append(q_i[half:] * jnp.exp2(b_i[half:] - c_i[s:s + 1, :]))
        pm = jnp.concatenate(pieces, axis=0).astype(BF16)
        rs = jnp.dot(pm, ones, preferred_element_type=F32)
        rel = lane - lo
        key = jnp.where((rel >= 0) & (rel <= tsub), rel, -1)
        key_lo, key_hi = key[:half], key[half:]
        a_lo = jnp.zeros((half, LANES), F32)
        a_hi = jnp.zeros((half, LANES), F32)
        off = 0
        for s in range(sub):
            if s < half:
                a_lo = jnp.where(key_lo == s, rs[off:off + half], a_lo)
                a_hi = jnp.where(key_hi == s, rs[off + half:off + sub], a_hi)
                off += sub
            else:
                a_hi = jnp.where(key_hi == s, rs[off:off + half], a_hi)
                off += half
        a_blk = jnp.concatenate([a_lo, a_hi], axis=0)[:, :c]
        if bi > 0:
            b0 = b2[lo - 1:lo, :]
            qt = q_i * jnp.exp2(b_i - b0)
            kt = jnp.exp2(jnp.minimum(b0 - c2, 0.0))
            cross = lax.dot_general(qt.astype(BF16), kt.astype(BF16), NT_DIMS,
                                    preferred_element_type=F32)
            a_blk = a_blk + jnp.where(colid < lo, cross, 0.0)
        a_rows.append(a_blk)
    a = jnp.concatenate(a_rows, axis=0)
    intra = jnp.dot(a.astype(BF16), iv.astype(BF16), preferred_element_type=F32)

    b_last = b2[c - 1:c, :]
    khat = jnp.exp2(b_last - c2)
    st_new = st * jnp.exp2(b_last) + lax.dot_general(
        iv.astype(BF16), khat.astype(BF16), TN_DIMS, preferred_element_type=F32)

    o = inter + intra
    y = o * lax.rsqrt(jnp.mean(o * o, axis=-1, keepdims=True) + RMS_EPS) * gn
    return y * (g * _sigmoid(g)), st_new


def _hgrn_kernel(q_ref, f_ref, i_ref, g_ref, lb_ref, gn_ref, o_ref, st_ref):
    @pl.when(pl.program_id(2) == 0)
    def _():
        st_ref[...] = jnp.zeros_like(st_ref)

    gn = gn_ref[...]
    d = HEAD_DIM
    lb = lb_ref[...]
    f_all = lb + (1.0 - lb) * _sigmoid(f_ref[0])
    f_min = jnp.min(f_all)

    def run(bounded_decay):
        for hh in range(HGRN_HEADS_PER_STEP):
            cs = slice(hh * d, (hh + 1) * d)
            st = st_ref[hh]
            for ci in range(HGRN_STEP // HGRN_CHUNK):
                sl = slice(ci * HGRN_CHUNK, (ci + 1) * HGRN_CHUNK)
                out, st = _hgrn_chunk(q_ref[0, sl, cs], f_all[sl, cs], i_ref[0, sl, cs],
                                      g_ref[0, sl, cs], gn, st, bounded_decay)
                o_ref[0, sl, cs] = out.astype(o_ref.dtype)
            st_ref[hh] = st

    @pl.when(f_min >= HGRN_MIN_SAFE_F)
    def _():
        run(True)

    @pl.when(jnp.logical_not(f_min >= HGRN_MIN_SAFE_F))
    def _():
        run(False)


def _hgrn(hp, lb, gn, n_heads):
    b, t, _ = hp.shape
    d = HEAD_DIM
    ts = HGRN_STEP
    hps = HGRN_HEADS_PER_STEP
    assert n_heads % hps == 0
    ng = n_heads // hps

    def col(group):
        return pl.BlockSpec((1, ts, hps * d), lambda bi, hi, ti: (bi, ti, group * ng + hi))

    return pl.pallas_call(
        _hgrn_kernel,
        grid=(b, ng, t // ts),
        in_specs=[col(0), col(1), col(2), col(3),
                  pl.BlockSpec((1, hps * d), lambda bi, hi, ti: (0, hi)),
                  pl.BlockSpec((1, d), lambda bi, hi, ti: (0, 0))],
        out_specs=pl.BlockSpec((1, ts, hps * d), lambda bi, hi, ti: (bi, ti, hi)),
        out_shape=jax.ShapeDtypeStruct((b, t, n_heads * d), BF16),
        scratch_shapes=[pltpu.VMEM((hps, d, d), F32)],
        compiler_params=_cparams("parallel", "parallel", "arbitrary"),
        name="hgrn2",
    )(hp, hp, hp, hp, lb, gn)


def _outproj_kernel(oa_ref, or_ref, x_ref, w_ref, g_ref, wr_ref, br_ref,
                    x1_ref, h2_ref, rt_ref, cnt_out_ref, cnt_ref):
    wa = oa_ref.shape[1]
    x1 = (x_ref[...]
          + jnp.dot(oa_ref[...], w_ref[0:wa, :], preferred_element_type=F32)
          + jnp.dot(or_ref[...], w_ref[wa:, :], preferred_element_type=F32))
    x1_ref[...] = x1
    h2 = x1 * lax.rsqrt(jnp.mean(x1 * x1, axis=-1, keepdims=True) + RMS_EPS) * g_ref[...]
    h2_ref[...] = h2

    h_hi = h2.astype(BF16)
    h_mid = (h2 - h_hi.astype(F32)).astype(BF16)
    part = jnp.dot(h_hi, wr_ref[...], preferred_element_type=F32)
    logits = (part[:, :LANES] + part[:, LANES:] + br_ref[...]
              + jnp.dot(h_mid, wr_ref[:, :LANES], preferred_element_type=F32))
    lane = lax.broadcasted_iota(jnp.int32, logits.shape, 1)
    big = jnp.int32(4 * LANES)
    ninf = -jnp.inf

    lg = jnp.where(lane < N_GROUPS, logits, ninf)
    mg = jnp.max(lg, axis=1, keepdims=True)
    gidx = jnp.min(jnp.where(lg == mg, lane, big), axis=1, keepdims=True)
    grp_w = 1.0 / jnp.sum(jnp.exp(lg - mg), axis=1, keepdims=True)

    lo = N_GROUPS + EXPERTS_PER_GROUP * gidx
    le = jnp.where((lane >= lo) & (lane < lo + EXPERTS_PER_GROUP), logits, ninf)
    m1 = jnp.max(le, axis=1, keepdims=True)
    i1 = jnp.min(jnp.where(le == m1, lane, big), axis=1, keepdims=True)
    le2 = jnp.where(lane == i1, ninf, le)
    m2 = jnp.max(le2, axis=1, keepdims=True)
    i2 = jnp.min(jnp.where(le2 == m2, lane, big), axis=1, keepdims=True)
    r21 = jnp.exp(m2 - m1)
    w1 = grp_w / (1.0 + r21)
    w2 = grp_w * r21 / (1.0 + r21)
    e1 = i1 - N_GROUPS
    e2 = i2 - N_GROUPS

    @pl.when(pl.program_id(0) == 0)
    def _():
        cnt_ref[...] = jnp.zeros_like(cnt_ref)

    tm = logits.shape[0]
    onehot = jnp.where((lane == e1) | (lane == e2), 1.0, 0.0)
    rr = lax.broadcasted_iota(jnp.int32, (tm, tm), 0)
    rc = lax.broadcasted_iota(jnp.int32, (tm, tm), 1)
    before = jnp.where(rr > rc, 1.0, 0.0).astype(BF16)
    prefix = (jnp.dot(before, onehot.astype(BF16), preferred_element_type=F32) + cnt_ref[0:1, :])
    rank1 = jnp.sum(jnp.where(lane == e1, prefix, 0.0), axis=1, keepdims=True)
    rank2 = jnp.sum(jnp.where(lane == e2, prefix, 0.0), axis=1, keepdims=True)
    total = cnt_ref[0:1, :] + jnp.sum(onehot, axis=0, keepdims=True)
    cnt_ref[0:1, :] = total
    cnt_out_ref[...] = jnp.broadcast_to(total, cnt_out_ref.shape)

    cols = [e1.astype(F32), e2.astype(F32), w1, w2, rank1, rank2]
    rt = jnp.zeros(logits.shape, F32)
    for ci, cv in enumerate(cols):
        rt = jnp.where(lane == ci, cv, rt)
    rt_ref[...] = rt


def _outproj(oa, orec, x2, w_out, g, wr, br, tm=256):
    n, d = x2.shape
    wa = oa.shape[1]
    wrc = orec.shape[1]
    row = lambda i: (i, 0)
    const = lambda i: (0, 0)
    return pl.pallas_call(
        _outproj_kernel,
        grid=(n // tm,),
        in_specs=[
            pl.BlockSpec((tm, wa), row),
            pl.BlockSpec((tm, wrc), row),
            pl.BlockSpec((tm, d), row),
            pl.BlockSpec((wa + wrc, d), const),
            pl.BlockSpec((1, d), const),
            pl.BlockSpec((d, 2 * LANES), const),
            pl.BlockSpec((1, LANES), const),
        ],
        out_specs=[pl.BlockSpec((tm, d), row), pl.BlockSpec((tm, d), row),
                   pl.BlockSpec((tm, LANES), row), pl.BlockSpec((8, LANES), const)],
        out_shape=[jax.ShapeDtypeStruct((n, d), F32), jax.ShapeDtypeStruct((n, d), F32),
                   jax.ShapeDtypeStruct((n, LANES), F32), jax.ShapeDtypeStruct((8, LANES), F32)],
        scratch_shapes=[pltpu.VMEM((8, LANES), F32)],
        compiler_params=_cparams("arbitrary"),
        name="outproj_route",
    )(oa, orec, x2, w_out, g, wr, br)


WEIGHT_CAST_ROWS = 256


def _cast_rows(dst_ref, src_ref):
    rows = dst_ref.shape[0]
    step = min(rows, WEIGHT_CAST_ROWS)
    assert rows % step == 0

    def body(c, carry):
        sl = pl.ds(pl.multiple_of(c * step, step), step)
        dst_ref[sl, :] = src_ref[sl, :].astype(BF16)
        return carry

    lax.fori_loop(0, rows // step, body, 0)


def _moe_kernel(te_ref, nu_ref, first_ref, nxt_ref, src_ref, h2_hbm, wg_hbm, wu_hbm, wd_hbm, y_ref,
                xbuf0, xbuf1, sems, wsems, wg32, wu32, wd32, wgb, wub, wdb):
    s = pl.program_id(0)
    nu = nu_ref[0]
    tm = MOE_TILE
    xbufs = (xbuf0, xbuf1)
    prev_tile = jnp.maximum(s - 1, 0)

    def weight_copies(e):
        return (pltpu.make_async_copy(wg_hbm.at[e], wg32, wsems.at[0]),
                pltpu.make_async_copy(wu_hbm.at[e], wu32, wsems.at[1]),
                pltpu.make_async_copy(wd_hbm.at[e], wd32, wsems.at[2]))

    def start_gather(tile, slot):
        for r in range(tm):
            tok = src_ref[tile * tm + r]
            pltpu.make_async_copy(h2_hbm.at[pl.ds(tok, 1), :], xbufs[slot].at[pl.ds(r, 1), :],
                                  sems.at[slot]).start()

    def wait_gather(slot):
        pltpu.make_async_copy(h2_hbm.at[pl.ds(0, tm), :], xbufs[slot], sems.at[slot]).wait()

    def ffn(slot):
        x = xbufs[slot][...].astype(BF16)
        a = jnp.dot(x, wgb[...], preferred_element_type=F32)
        u = jnp.dot(x, wub[...], preferred_element_type=F32)
        hid = (a * _sigmoid(a) * u).astype(BF16)
        y_ref[...] = jnp.dot(hid, wdb[...], preferred_element_type=F32)

    @pl.when(s == 0)
    def _():
        for cp in weight_copies(te_ref[0]):
            cp.start()
        start_gather(s, 0)

    @pl.when((s >= 1) & (s <= nu) & (first_ref[prev_tile] == 1))
    def _():
        for cp in weight_copies(te_ref[prev_tile]):
            cp.wait()
        _cast_rows(wgb, wg32)
        _cast_rows(wub, wu32)
        _cast_rows(wdb, wd32)

        @pl.when(nxt_ref[prev_tile] >= 0)
        def _():
            for cp in weight_copies(nxt_ref[prev_tile]):
                cp.start(priority=1)

    for parity in range(2):
        @pl.when((s >= 1) & (s < nu) & (s % 2 == parity))
        def _(parity=parity):
            wait_gather(1 - parity)
            start_gather(s, parity)
            ffn(1 - parity)

        @pl.when((s >= 1) & (s == nu) & (s % 2 == parity))
        def _(parity=parity):
            wait_gather(1 - parity)
            ffn(1 - parity)

    @pl.when(s > nu)
    def _():
        y_ref[...] = jnp.zeros_like(y_ref)


def _moe(tile_expert, n_used, first, nxt, src, h2, wg, wu, wd):
    p = src.shape[0]
    d, f = wg.shape[1:]
    tm = MOE_TILE
    n_tiles = p // tm
    any_spec = pl.BlockSpec(memory_space=pl.ANY)
    return pl.pallas_call(
        _moe_kernel,
        grid_spec=pltpu.PrefetchScalarGridSpec(
            num_scalar_prefetch=5,
            grid=(n_tiles + 1,),
            in_specs=[any_spec, any_spec, any_spec, any_spec],
            out_specs=pl.BlockSpec((tm, d), lambda s, *_: (jnp.maximum(s - 1, 0), 0)),
            scratch_shapes=[pltpu.VMEM((tm, d), F32), pltpu.VMEM((tm, d), F32),
                            pltpu.SemaphoreType.DMA((2,)), pltpu.SemaphoreType.DMA((3,)),
                            pltpu.VMEM((d, f), F32), pltpu.VMEM((d, f), F32), pltpu.VMEM((f, d), F32),
                            pltpu.VMEM((d, f), BF16), pltpu.VMEM((d, f), BF16),
                            pltpu.VMEM((f, d), BF16)],
        ),
        out_shape=jax.ShapeDtypeStruct((p, d), F32),
        compiler_params=_cparams("arbitrary"),
        name="moe_ffn",
    )(tile_expert, n_used, first, nxt, src, h2, wg, wu, wd)


COMBINE_TILE = 256


def _final_kernel(pos_ref, x1_ref, ys_hbm, rt_ref, g_ref, o_ref, ya0, yb0, ya1, yb1, sems):
    s = pl.program_id(0)
    n_tiles = pl.num_programs(0) - 1
    tm = COMBINE_TILE
    n_tok = pos_ref.shape[0] // 2
    ybufs = ((ya0, yb0), (ya1, yb1))

    def start_gather(tile, slot):
        for k in range(2):
            for r in range(tm):
                row = pos_ref[k * n_tok + tile * tm + r]
                pltpu.make_async_copy(ys_hbm.at[pl.ds(row, 1), :],
                                      ybufs[slot][k].at[pl.ds(r, 1), :], sems.at[slot]).start()

    def wait_gather(slot):
        for k in range(2):
            pltpu.make_async_copy(ys_hbm.at[pl.ds(0, tm), :], ybufs[slot][k], sems.at[slot]).wait()

    def finish(slot):
        rt = rt_ref[...]
        x2 = (x1_ref[...] + rt[:, 2:3] * ybufs[slot][0][...] + rt[:, 3:4] * ybufs[slot][1][...])
        o_ref[...] = (x2 * lax.rsqrt(jnp.mean(x2 * x2, axis=-1, keepdims=True) + RMS_EPS)
                      * g_ref[...])

    @pl.when(s == 0)
    def _():
        start_gather(s, 0)

    for parity in range(2):
        @pl.when((s >= 1) & (s < n_tiles) & (s % 2 == parity))
        def _(parity=parity):
            wait_gather(1 - parity)
            start_gather(s, parity)
            finish(1 - parity)

        @pl.when((s == n_tiles) & (s % 2 == parity))
        def _(parity=parity):
            wait_gather(1 - parity)
            finish(1 - parity)


def _final(x1, ys, pos, rt, g):
    n, d = x1.shape
    tm = COMBINE_TILE
    prev = lambda s: jnp.maximum(s - 1, 0)
    return pl.pallas_call(
        _final_kernel,
        grid_spec=pltpu.PrefetchScalarGridSpec(
            num_scalar_prefetch=1,
            grid=(n // tm + 1,),
            in_specs=[pl.BlockSpec((tm, d), lambda s, p: (prev(s), 0)),
                      pl.BlockSpec(memory_space=pl.ANY),
                      pl.BlockSpec((tm, LANES), lambda s, p: (prev(s), 0)),
                      pl.BlockSpec((1, d), lambda s, p: (0, 0))],
            out_specs=pl.BlockSpec((tm, d), lambda s, p: (prev(s), 0)),
            scratch_shapes=[pltpu.VMEM((tm, d), F32)] * 4 + [pltpu.SemaphoreType.DMA((2,))],
        ),
        out_shape=jax.ShapeDtypeStruct((n, d), F32),
        compiler_params=_cparams("arbitrary"),
        name="combine_norm",
    )(pos, x1, ys, rt, g)


def _dispatch_plan(rt, counts, n_tokens):
    tm = MOE_TILE
    n_rows = 2 * n_tokens + N_EXPERTS * tm
    n_tiles = n_rows // tm
    ids = rt[:, 0:6].astype(jnp.int32)
    ef = jnp.concatenate([ids[:, 0], ids[:, 1]])
    rank = jnp.concatenate([ids[:, 4], ids[:, 5]])
    counts = counts[0, :N_EXPERTS].astype(jnp.int32)
    tiles_per = (counts + tm - 1) // tm
    tile_end = jnp.cumsum(tiles_per)
    row_start = (tile_end - tiles_per) * tm
    pos = row_start[ef] + rank
    tok = jnp.concatenate([jnp.arange(n_tokens, dtype=jnp.int32)] * 2)
    src = jnp.zeros((n_rows,), jnp.int32).at[pos].set(tok)
    n_used = tile_end[-1]
    tile_ids = jnp.arange(n_tiles, dtype=jnp.int32)
    tile_expert = jnp.sum((tile_ids[:, None] >= tile_end[None, :]).astype(jnp.int32), axis=1)
    last_expert = jnp.sum((n_used - 1 >= tile_end).astype(jnp.int32))
    tile_expert = jnp.where(tile_ids < n_used, tile_expert, last_expert).astype(jnp.int32)
    first = jnp.concatenate([jnp.ones((1,), jnp.int32),
                             (tile_expert[1:] != tile_expert[:-1]).astype(jnp.int32)])
    eid = jnp.arange(N_EXPERTS, dtype=jnp.int32)
    later = (tiles_per[None, :] > 0) & (eid[None, :] > eid[:, None])
    next_expert = jnp.min(jnp.where(later, eid[None, :], N_EXPERTS), axis=1)
    next_expert = jnp.where(next_expert == N_EXPERTS, -1, next_expert).astype(jnp.int32)
    nxt = next_expert[tile_expert]
    return (src, pos[:n_tokens], pos[n_tokens:], tile_expert, first, nxt,
            n_used.reshape(1).astype(jnp.int32))


def kernel(x, norm_mix_g, w_in, hgrn_lb_logits, hgrn_out_norm_g, w_out, norm_ffn_g, w_group_router,
           b_group_router, w_expert_router, b_expert_router, w_gate, w_up, w_down, final_norm_g):
    b, t, d = x.shape
    n = b * t
    depth = w_in.shape[0]
    assert depth == 1, "the final norm is fused into the combine step of the only layer"
    attn_w = d // 2
    n_heads = attn_w // HEAD_DIM
    x2 = x.reshape(n, d)
    lb_all = jnp.cumsum(jax.nn.softmax(hgrn_lb_logits.astype(F32), axis=0), axis=0)[:depth]
    slope2 = jnp.asarray(2.0 ** (-8.0 * np.arange(1, n_heads + 1) / n_heads), dtype=F32) * F32(LOG2E)
    s_hi = slope2.astype(BF16).astype(F32)
    s_mid = (slope2 - s_hi).astype(BF16).astype(F32)
    s_lo = (slope2 - s_hi - s_mid).astype(BF16).astype(F32)
    slope_pieces = jnp.stack([s_hi, s_mid, s_lo])
    qkv_scale = jnp.concatenate([jnp.full((attn_w,), HEAD_DIM ** -0.5 * LOG2E, F32),
                                 jnp.ones((2 * attn_w,), F32)]).reshape(1, -1)

    for l in range(depth):
        g_mix = norm_mix_g[l].reshape(1, d)
        x2, w_in_l, w_out_l = lax.optimization_barrier(
            (x2, w_in[l].astype(BF16), w_out[l].astype(BF16)))
        qkv = _inproj(x2, g_mix, w_in_l, 0, qkv_scale, BF16)
        hp = _inproj(x2, g_mix, w_in_l, 3 * attn_w, jnp.ones((1, 4 * (d - attn_w)), F32), F32)
        o_attn = _moba(qkv.reshape(b, t, 3 * attn_w), slope_pieces, n_heads)
        o_rec = _hgrn(hp.reshape(b, t, -1), lb_all[l].reshape(1, -1),
                      hgrn_out_norm_g[l].reshape(1, HEAD_DIM), n_heads)

        wr = jnp.concatenate(
            [w_group_router[l],
             jnp.transpose(w_expert_router[l], (1, 0, 2)).reshape(d, N_EXPERTS)], axis=1)
        wr = jnp.pad(wr, ((0, 0), (0, LANES - wr.shape[1])))
        wr_hi = wr.astype(BF16)
        wr = jnp.concatenate([wr_hi, (wr - wr_hi.astype(F32)).astype(BF16)], axis=1)
        br =jnp.concatenate([b_group_router[l], b_expert_router[l].reshape(-1)])
        br = jnp.pad(br, (0, LANES - br.shape[0])).reshape(1, LANES)
        x1, h2, rt, counts = _outproj(o_attn.reshape(n, attn_w), o_rec.reshape(n, -1), x2,
                                      w_out_l, norm_ffn_g[l].reshape(1, d), wr, br)
        src, pos1, pos2, tile_expert, first, nxt, n_used = _dispatch_plan(rt, counts, n)
        ys = _moe(tile_expert, n_used, first, nxt, src, h2, w_gate[l], w_up[l], w_down[l])
        x2 = _final(x1, ys, jnp.concatenate([pos1, pos2]), rt, final_norm_g.reshape(1, d))
    return x2.reshape(b, t, d)
```

```python
import functools

import jax
import jax.numpy as jnp
import numpy as np
from jax import lax
from jax.experimental import pallas as pl
from jax.experimental.pallas import tpu as pltpu

F32 = jnp.float32
BF16 = jnp.bfloat16
HIGHEST = lax.Precision.HIGHEST

HEAD_DIM = 128
MOBA_BLOCK = 256
MOBA_TOPK = 3
N_GROUPS = 4
EXPERTS_PER_GROUP = 4
N_EXPERTS = N_GROUPS * EXPERTS_PER_GROUP
RMS_EPS = 1e-6

LANES = 128
VMEM_LIMIT_BYTES = 56 * 1024 * 1024

HGRN_CHUNK = 64
HGRN_SUB = 16
HGRN_STEP = 256
HGRN_HEADS_PER_STEP = 8
HGRN_MIN_SAFE_F = 2.0 ** -7
MOE_TILE = 256
NT_DIMS = (((1,), (1,)), ((), ()))
TN_DIMS = (((0,), (0,)), ((), ()))


def _cparams(*sem):
    return pltpu.CompilerParams(dimension_semantics=sem, vmem_limit_bytes=VMEM_LIMIT_BYTES)


def _inproj_kernel(x_ref, g_ref, w_ref, cs_ref, o_ref, hn_ref):
    @pl.when(pl.program_id(1) == 0)
    def _():
        x = x_ref[...]
        ms = jnp.mean(x * x, axis=-1, keepdims=True)
        hn_ref[...] = (x * lax.rsqrt(ms + RMS_EPS) * g_ref[...]).astype(BF16)

    acc = jnp.dot(hn_ref[...], w_ref[...], preferred_element_type=F32)
    o_ref[...] = (acc * cs_ref[...]).astype(o_ref.dtype)


def _inproj(x2, g, w, col0, colscale, out_dtype, tm=1024, tn=1024):
    n, d = x2.shape
    cols = colscale.shape[1]
    j0 = col0 // tn
    return pl.pallas_call(
        _inproj_kernel,
        grid=(n // tm, cols // tn),
        in_specs=[
            pl.BlockSpec((tm, d), lambda i, j: (i, 0)),
            pl.BlockSpec((1, d), lambda i, j: (0, 0)),
            pl.BlockSpec((d, tn), lambda i, j: (0, j0 + j)),
            pl.BlockSpec((1, tn), lambda i, j: (0, j)),
        ],
        out_specs=pl.BlockSpec((tm, tn), lambda i, j: (i, j)),
        out_shape=jax.ShapeDtypeStruct((n, cols), out_dtype),
        scratch_shapes=[pltpu.VMEM((tm, d), BF16)],
        compiler_params=_cparams("parallel", "arbitrary"),
        name="inproj",
    )(x2, g, w, colscale)


MOBA_HEADS_PER_STEP = 4
MOBA_EXTRA_POS = 0
MOBA_EXTRA_SEL = 8
MASK_BIG = 2.0 ** 60
LOG2E = 1.4426950408889634


def _moba_key_extras(t):
    nb = t // MOBA_BLOCK
    assert MOBA_EXTRA_SEL + nb <= LANES and MOBA_BLOCK <= 256
    pos = np.arange(t)
    kx = np.zeros((t, LANES), np.float32)
    kx[:, MOBA_EXTRA_POS:MOBA_EXTRA_POS + 3] = ((pos // MOBA_BLOCK) * MOBA_BLOCK)[:, None]
    kx[:, MOBA_EXTRA_POS + 3:MOBA_EXTRA_POS + 6] = (pos % MOBA_BLOCK)[:, None]
    kx[pos, MOBA_EXTRA_SEL + pos // MOBA_BLOCK] = -MASK_BIG
    return jnp.asarray(kx, dtype=BF16)


def _moba_kernel(sl_ref, q_ref, k_ref, v_ref, kx_ref, o_ref, kaug_ref, kmean_ref, *, nb):
    hg = pl.program_id(1)
    i = pl.program_id(2)
    bs = MOBA_BLOCK
    dh = HEAD_DIM
    nbp = kmean_ref.shape[1] // 4
    nx = 8 + nbp

    @pl.when(i == 0)
    def _():
        for hh in range(MOBA_HEADS_PER_STEP):
            kaug_ref[hh, :, 0:dh] = k_ref[0, :, hh * dh:(hh + 1) * dh]
            kaug_ref[hh, :, dh:] = kx_ref[...]
            rows = [jnp.mean(k_ref[0, j * bs:(j + 1) * bs, hh * dh:(hh + 1) * dh].astype(F32),
                             axis=0, keepdims=True) for j in range(nb)]
            if nbp > nb:
                rows.append(jnp.zeros((nbp - nb, dh), F32))
            km = jnp.concatenate(rows, axis=0)
            hi = km.astype(BF16)
            mid = (km - hi.astype(F32)).astype(BF16)
            lo = (km - hi.astype(F32) - mid.astype(F32)).astype(BF16)
            kmean_ref[hh] = jnp.concatenate([hi, mid, lo, jnp.zeros_like(hi)], axis=0)

    r8 = lax.broadcasted_iota(jnp.int32, (8, bs), 0)
    piece = r8 % 3
    er = lax.broadcasted_iota(jnp.int32, (nx, LANES), 0)
    ec = lax.broadcasted_iota(jnp.int32, (nx, LANES), 1)
    embed = jnp.where(er == ec, 1.0, 0.0).astype(BF16)

    qas = []
    for hh in range(MOBA_HEADS_PER_STEP):
        h = hg * MOBA_HEADS_PER_STEP + hh
        q = q_ref[0, :, hh * dh:(hh + 1) * dh]
        g4 = lax.dot_general(kmean_ref[hh], q, NT_DIMS, preferred_element_type=F32)
        gate_t = g4[0:nbp] + g4[nbp:2 * nbp] + g4[2 * nbp:3 * nbp]
        blk = lax.broadcasted_iota(jnp.int32, gate_t.shape, 0)
        rank = jnp.zeros(gate_t.shape, F32)
        for jp in range(nb - 1):
            other = gate_t[jp:jp + 1, :]
            beats = (other > gate_t) | ((other == gate_t) & (blk > jp))
            rank = rank + jnp.where(beats, (jp < i).astype(F32), 0.0)
        notsel_t = jnp.where((blk < i) & (rank >= MOBA_TOPK), 1.0, 0.0)
        slope_rows = jnp.where(r8 >= 6, 0.0,
                               jnp.where(piece == 0, sl_ref[0, h],
                                         jnp.where(piece == 1, sl_ref[1, h], sl_ref[2, h])))
        qx_t = jnp.concatenate([slope_rows, notsel_t], axis=0).astype(BF16)
        qx = lax.dot_general(qx_t, embed, TN_DIMS, preferred_element_type=F32)
        qas.append(jnp.concatenate([q, qx.astype(BF16)], axis=1))

    rowi = lax.broadcasted_iota(jnp.int32, (bs, bs), 0)
    coli = lax.broadcasted_iota(jnp.int32, (bs, bs), 1)

    for c in range(nb):
        @pl.when(i == c)
        def _(c=c):
            n = (c + 1) * bs
            for hh in range(MOBA_HEADS_PER_STEP):
                cs = slice(hh * dh, (hh + 1) * dh)
                s = lax.dot_general(qas[hh], kaug_ref[hh, 0:n, :], NT_DIMS,
                                    preferred_element_type=F32)
                s_own = jnp.where(rowi >= coli, s[:, c * bs:], -jnp.inf)
                m = jnp.max(s_own, axis=1, keepdims=True)
                if c > 0:
                    s_past = s[:, :c * bs]
                    m = jnp.maximum(m, jnp.max(s_past, axis=1, keepdims=True))
                p_own = jnp.exp2(s_own - m)
                l = jnp.sum(p_own, axis=1, keepdims=True)
                acc = jnp.dot(p_own.astype(BF16), v_ref[0, c * bs:n, cs],
                              preferred_element_type=F32)
                if c > 0:
                    p_past = jnp.exp2(s_past - m)
                    l = l + jnp.sum(p_past, axis=1, keepdims=True)
                    acc = acc + jnp.dot(p_past.astype(BF16), v_ref[0, 0:c * bs, cs],
                                        preferred_element_type=F32)
                o_ref[0, :, cs] = (acc / l).astype(o_ref.dtype)


def _moba(qkv, slope_pieces, n_heads):
    b, t, _ = qkv.shape
    bs = MOBA_BLOCK
    nb = t // bs
    nbp = -(-nb // 8) * 8
    hps = MOBA_HEADS_PER_STEP
    assert n_heads % hps == 0
    ng = n_heads // hps
    w = hps * HEAD_DIM
    kern = functools.partial(_moba_kernel, nb=nb)
    return pl.pallas_call(
        kern,
        grid=(b, ng, nb),
        in_specs=[
            pl.BlockSpec(memory_space=pltpu.SMEM),
            pl.BlockSpec((1, bs, w), lambda bi, hi, qi: (bi, qi, hi)),
            pl.BlockSpec((1, t, w), lambda bi, hi, qi: (bi, 0, ng + hi)),
            pl.BlockSpec((1, t, w), lambda bi, hi, qi: (bi, 0, 2 * ng + hi)),
            pl.BlockSpec((t, LANES), lambda bi, hi, qi: (0, 0)),
        ],
        out_specs=pl.BlockSpec((1, bs, w), lambda bi, hi, qi: (bi, qi, hi)),
        out_shape=jax.ShapeDtypeStruct((b, t, n_heads * HEAD_DIM), BF16),
        scratch_shapes=[
            pltpu.VMEM((hps, t, HEAD_DIM + LANES), BF16),
            pltpu.VMEM((hps, 4 * nbp, HEAD_DIM), BF16),
        ],
        compiler_params=_cparams("parallel", "parallel", "arbitrary"),
        name="moba",
    )(slope_pieces, qkv, qkv, qkv, _moba_key_extras(t))


def _sigmoid(x):
    return 1.0 / (1.0 + jnp.exp(-x))


def _hgrn_chunk(q, f, iv, g, gn, st, bounded_decay):
    c = HGRN_CHUNK
    sub = HGRN_SUB
    half = sub // 2
    nsub = c // sub
    qf = q * _sigmoid(q)
    kf = jnp.maximum(1.0 - f, 0.0)
    r = lax.broadcasted_iota(jnp.int32, (c, c), 0)
    cc = lax.broadcasted_iota(jnp.int32, (c, c), 1)
    tril = jnp.where(r >= cc, 1.0, 0.0).astype(BF16)
    lf = jnp.log2(f)
    dk = lf.shape[1]
    lf_hi = lf.astype(BF16)
    lf_mid = (lf - lf_hi.astype(F32)).astype(BF16)
    lf_lo = (lf - lf_hi.astype(F32) - lf_mid.astype(F32)).astype(BF16)
    b3 = jnp.dot(tril, jnp.concatenate([lf_hi, lf_mid, lf_lo], axis=1),
                 preferred_element_type=F32)
    b2 = b3[:, :dk] + b3[:, dk:2 * dk] + b3[:, 2 * dk:]
    c2 = b2 - jnp.log2(kf)

    inter = lax.dot_general((qf * jnp.exp2(b2)).astype(BF16), st.astype(BF16), NT_DIMS,
                            preferred_element_type=F32)

    lane = lax.broadcasted_iota(jnp.int32, (sub, LANES), 1)
    tsub = lax.broadcasted_iota(jnp.int32, (sub, LANES), 0)
    colid = lax.broadcasted_iota(jnp.int32, (sub, c), 1)
    ones = jnp.ones((LANES, LANES), BF16)
    a_rows = []
    for bi in range(nsub):
        lo = bi * sub
        b_i = b2[lo:lo + sub]
        c_i = c2[lo:lo + sub]
        q_i = qf[lo:lo + sub]
        if bounded_decay:
            qt = q_i * (jnp.exp2(b_i - b2[lo - 1:lo, :]) if bi > 0 else jnp.exp2(b_i))
            kt = jnp.exp2((b2[lo - 1:lo, :] - c2[:lo + sub]) if bi > 0 else -c2[:sub])
            if lo + sub < c:
                kt = jnp.concatenate([kt, jnp.zeros((c - lo - sub, kt.shape[1]), F32)], axis=0)
            cross = lax.dot_general(qt.astype(BF16), kt.astype(BF16), NT_DIMS,
                                    preferred_element_type=F32)
            a_rows.append(jnp.where(colid <= lo + tsub[:, :c], cross, 0.0))
            continue
        pieces = []
        for s in range(sub):
            if s < half:
                pieces.append(q_i * jnp.exp2(b_i - c_i[s:s + 1, :]))
            else:
                pieces.append(q_i[half:] * jnp.exp2(b_i[half:] - c_i[s:s + 1, :]))
        pm = jnp.concatenate(pieces, axis=0).astype(BF16)
        rs = jnp.dot(pm, ones, preferred_element_type=F32)
        rel = lane - lo
        key = jnp.where((rel >= 0) & (rel <= tsub), rel, -1)
        key_lo, key_hi = key[:half], key[half:]
        a_lo = jnp.zeros((half, LANES), F32)
        a_hi = jnp.zeros((half, LANES), F32)
        off = 0
        for s in range(sub):
            if s < half:
                a_lo = jnp.where(key_lo == s, rs[off:off + half], a_lo)
                a_hi = jnp.where(key_hi == s, rs[off + half:off + sub], a_hi)
                off += sub
            else:
                a_hi = jnp.where(key_hi == s, rs[off:off + half], a_hi)
                off += half
        a_blk = jnp.concatenate([a_lo, a_hi], axis=0)[:, :c]
        if bi > 0:
            b0 = b2[lo - 1:lo, :]
            qt = q_i * jnp.exp2(b_i - b0)
            kt = jnp.exp2(jnp.minimum(b0 - c2, 0.0))
            cross = lax.dot_general(qt.astype(BF16), kt.astype(BF16), NT_DIMS,
                                    preferred_element_type=F32)
            a_blk = a_blk + jnp.where(colid < lo, cross, 0.0)
        a_rows.append(a_blk)
    a = jnp.concatenate(a_rows, axis=0)
    intra = jnp.dot(a.astype(BF16), iv.astype(BF16), preferred_element_type=F32)

    b_last = b2[c - 1:c, :]
    khat = jnp.exp2(b_last - c2)
    st_new = st * jnp.exp2(b_last) + lax.dot_general(
        iv.astype(BF16), khat.astype(BF16), TN_DIMS, preferred_element_type=F32)

    o = inter + intra
    y = o * lax.rsqrt(jnp.mean(o * o, axis=-1, keepdims=True) + RMS_EPS) * gn
    return y * (g * _sigmoid(g)), st_new


def _hgrn_kernel(q_ref, f_ref, i_ref, g_ref, lb_ref, gn_ref, o_ref, st_ref):
    @pl.when(pl.program_id(2) == 0)
    def _():
        st_ref[...] = jnp.zeros_like(st_ref)

    gn = gn_ref[...]
    d = HEAD_DIM
    lb = lb_ref[...]
    f_all = lb + (1.0 - lb) * _sigmoid(f_ref[0].astype(F32))
    f_min = jnp.min(f_all)

    def run(bounded_decay):
        for hh in range(HGRN_HEADS_PER_STEP):
            cs = slice(hh * d, (hh + 1) * d)
            st = st_ref[hh]
            for ci in range(HGRN_STEP // HGRN_CHUNK):
                sl = slice(ci * HGRN_CHUNK, (ci + 1) * HGRN_CHUNK)
                out, st = _hgrn_chunk(q_ref[0, sl, cs].astype(F32), f_all[sl, cs],
                                      i_ref[0, sl, cs], g_ref[0, sl, cs].astype(F32), gn, st,
                                      bounded_decay)
                o_ref[0, sl, cs] = out.astype(o_ref.dtype)
            st_ref[hh] = st

    @pl.when(f_min >= HGRN_MIN_SAFE_F)
    def _():
        run(True)

    @pl.when(jnp.logical_not(f_min >= HGRN_MIN_SAFE_F))
    def _():
        run(False)


def _hgrn(hp, col0, lb, gn, n_heads):
    b, t, _ = hp.shape
    d = HEAD_DIM
    ts = HGRN_STEP
    hps = HGRN_HEADS_PER_STEP
    assert n_heads % hps == 0 and col0 % (hps * d) == 0
    ng = n_heads // hps
    blk0 = col0 // (hps * d)

    def col(group):
        return pl.BlockSpec((1, ts, hps * d), lambda bi, hi, ti: (bi, ti, blk0 + group * ng + hi))

    return pl.pallas_call(
        _hgrn_kernel,
        grid=(b, ng, t // ts),
        in_specs=[col(0), col(1), col(2), col(3),
                  pl.BlockSpec((1, hps * d), lambda bi, hi, ti: (0, hi)),
                  pl.BlockSpec((1, d), lambda bi, hi, ti: (0, 0))],
        out_specs=pl.BlockSpec((1, ts, hps * d), lambda bi, hi, ti: (bi, ti, hi)),
        out_shape=jax.ShapeDtypeStruct((b, t, n_heads * d), BF16),
        scratch_shapes=[pltpu.VMEM((hps, d, d), F32)],
        compiler_params=_cparams("parallel", "parallel", "arbitrary"),
        name="hgrn2",
    )(hp, hp, hp, hp, lb, gn)


def _outproj_kernel(oa_ref, or_ref, x_ref, w_ref, g_ref, wr_ref, br_ref,
                    x1_ref, h2_ref, rt_ref, cnt_out_ref, cnt_ref):
    wa = oa_ref.shape[1]
    x1 = (x_ref[...]
          + jnp.dot(oa_ref[...], w_ref[0:wa, :], preferred_element_type=F32)
          + jnp.dot(or_ref[...], w_ref[wa:, :], preferred_element_type=F32))
    x1_ref[...] = x1
    h2 = x1 * lax.rsqrt(jnp.mean(x1 * x1, axis=-1, keepdims=True) + RMS_EPS) * g_ref[...]
    h2_ref[...] = h2

    h_hi = h2.astype(BF16)
    h_mid = (h2 - h_hi.astype(F32)).astype(BF16)
    part = jnp.dot(h_hi, wr_ref[...], preferred_element_type=F32)
    logits = (part[:, :LANES] + part[:, LANES:] + br_ref[...]
              + jnp.dot(h_mid, wr_ref[:, :LANES], preferred_element_type=F32))
    lane = lax.broadcasted_iota(jnp.int32, logits.shape, 1)
    big = jnp.int32(4 * LANES)
    ninf = -jnp.inf

    lg = jnp.where(lane < N_GROUPS, logits, ninf)
    mg = jnp.max(lg, axis=1, keepdims=True)
    gidx = jnp.min(jnp.where(lg == mg, lane, big), axis=1, keepdims=True)
    grp_w = 1.0 / jnp.sum(jnp.exp(lg - mg), axis=1, keepdims=True)

    lo = N_GROUPS + EXPERTS_PER_GROUP * gidx
    le = jnp.where((lane >= lo) & (lane < lo + EXPERTS_PER_GROUP), logits, ninf)
    m1 = jnp.max(le, axis=1, keepdims=True)
    i1 = jnp.min(jnp.where(le == m1, lane, big), axis=1, keepdims=True)
    le2 = jnp.where(lane == i1, ninf, le)
    m2 = jnp.max(le2, axis=1, keepdims=True)
    i2 = jnp.min(jnp.where(le2 == m2, lane, big), axis=1, keepdims=True)
    r21 = jnp.exp(m2 - m1)
    w1 = grp_w / (1.0 + r21)
    w2 = grp_w * r21 / (1.0 + r21)
    e1 = i1 - N_GROUPS
    e2 = i2 - N_GROUPS

    @pl.when(pl.program_id(0) == 0)
    def _():
        cnt_ref[...] = jnp.zeros_like(cnt_ref)

    tm = logits.shape[0]
    onehot = jnp.where((lane == e1) | (lane == e2), 1.0, 0.0)
    rr = lax.broadcasted_iota(jnp.int32, (tm, tm), 0)
    rc = lax.broadcasted_iota(jnp.int32, (tm, tm), 1)
    before = jnp.where(rr > rc, 1.0, 0.0).astype(BF16)
    prefix = (jnp.dot(before, onehot.astype(BF16), preferred_element_type=F32) + cnt_ref[0:1, :])
    rank1 = jnp.sum(jnp.where(lane == e1, prefix, 0.0), axis=1, keepdims=True)
    rank2 = jnp.sum(jnp.where(lane == e2, prefix, 0.0), axis=1, keepdims=True)
    total = cnt_ref[0:1, :] + jnp.sum(onehot, axis=0, keepdims=True)
    cnt_ref[0:1, :] = total
    cnt_out_ref[...] = jnp.broadcast_to(total, cnt_out_ref.shape)

    cols = [e1.astype(F32), e2.astype(F32), w1, w2, rank1, rank2]
    rt = jnp.zeros(logits.shape, F32)
    for ci, cv in enumerate(cols):
        rt = jnp.where(lane == ci, cv, rt)
    rt_ref[...] = rt


def _outproj(oa, orec, x2, w_out, g, wr, br, tm=256):
    n, d = x2.shape
    wa = oa.shape[1]
    wrc = orec.shape[1]
    row = lambda i: (i, 0)
    const = lambda i: (0, 0)
    return pl.pallas_call(
        _outproj_kernel,
        grid=(n // tm,),
        in_specs=[
            pl.BlockSpec((tm, wa), row),
            pl.BlockSpec((tm, wrc), row),
            pl.BlockSpec((tm, d), row),
            pl.BlockSpec((wa + wrc, d), const),
            pl.BlockSpec((1, d), const),
            pl.BlockSpec((d, 2 * LANES), const),
            pl.BlockSpec((1, LANES), const),
        ],
        out_specs=[pl.BlockSpec((tm, d), row), pl.BlockSpec((tm, d), row),
                   pl.BlockSpec((tm, LANES), row), pl.BlockSpec((8, LANES), const)],
        out_shape=[jax.ShapeDtypeStruct((n, d), F32), jax.ShapeDtypeStruct((n, d), F32),
                   jax.ShapeDtypeStruct((n, LANES), F32), jax.ShapeDtypeStruct((8, LANES), F32)],
        scratch_shapes=[pltpu.VMEM((8, LANES), F32)],
        compiler_params=_cparams("arbitrary"),
        name="outproj_route",
    )(oa, orec, x2, w_out, g, wr, br)


WEIGHT_CAST_ROWS = 256


def _cast_rows(dst_ref, src_ref):
    rows = dst_ref.shape[0]
    step = min(rows, WEIGHT_CAST_ROWS)
    assert rows % step == 0

    def body(c, carry):
        sl = pl.ds(pl.multiple_of(c * step, step), step)
        dst_ref[sl, :] = src_ref[sl, :].astype(BF16)
        return carry

    lax.fori_loop(0, rows // step, body, 0)


def _moe_kernel(te_ref, nu_ref, first_ref, nxt_ref, src_ref, h2_hbm, wg_hbm, wu_hbm, wd_hbm, y_ref,
                xbuf0, xbuf1, sems, wsems, wg32, wu32, wd32, wgb, wub, wdb):
    s = pl.program_id(0)
    nu = nu_ref[0]
    tm = MOE_TILE
    xbufs = (xbuf0, xbuf1)
    prev_tile = jnp.maximum(s - 1, 0)

    def weight_copies(e):
        return (pltpu.make_async_copy(wg_hbm.at[e], wg32, wsems.at[0]),
                pltpu.make_async_copy(wu_hbm.at[e], wu32, wsems.at[1]),
                pltpu.make_async_copy(wd_hbm.at[e], wd32, wsems.at[2]))

    def start_gather(tile, slot):
        for r in range(tm):
            tok = src_ref[tile * tm + r]
            pltpu.make_async_copy(h2_hbm.at[pl.ds(tok, 1), :], xbufs[slot].at[pl.ds(r, 1), :],
                                  sems.at[slot]).start()

    def wait_gather(slot):
        pltpu.make_async_copy(h2_hbm.at[pl.ds(0, tm), :], xbufs[slot], sems.at[slot]).wait()

    def ffn(slot):
        x = xbufs[slot][...].astype(BF16)
        a = jnp.dot(x, wgb[...], preferred_element_type=F32)
        u = jnp.dot(x, wub[...], preferred_element_type=F32)
        hid = (a * _sigmoid(a) * u).astype(BF16)
        y_ref[...] = jnp.dot(hid, wdb[...], preferred_element_type=F32)

    @pl.when(s == 0)
    def _():
        for cp in weight_copies(te_ref[0]):
            cp.start()
        start_gather(s, 0)

    @pl.when((s >= 1) & (s <= nu) & (first_ref[prev_tile] == 1))
    def _():
        for cp in weight_copies(te_ref[prev_tile]):
            cp.wait()
        _cast_rows(wgb, wg32)
        _cast_rows(wub, wu32)
        _cast_rows(wdb, wd32)

        @pl.when(nxt_ref[prev_tile] >= 0)
        def _():
            for cp in weight_copies(nxt_ref[prev_tile]):
                cp.start(priority=1)

    for parity in range(2):
        @pl.when((s >= 1) & (s < nu) & (s % 2 == parity))
        def _(parity=parity):
            wait_gather(1 - parity)
            start_gather(s, parity)
            ffn(1 - parity)

        @pl.when((s >= 1) & (s == nu) & (s % 2 == parity))
        def _(parity=parity):
            wait_gather(1 - parity)
            ffn(1 - parity)

    @pl.when(s > nu)
    def _():
        y_ref[...] = jnp.zeros_like(y_ref)


def _moe(tile_expert, n_used, first, nxt, src, h2, wg, wu, wd):
    p = src.shape[0]
    d, f = wg.shape[1:]
    tm = MOE_TILE
    n_tiles = p // tm
    any_spec = pl.BlockSpec(memory_space=pl.ANY)
    return pl.pallas_call(
        _moe_kernel,
        grid_spec=pltpu.PrefetchScalarGridSpec(
            num_scalar_prefetch=5,
            grid=(n_tiles + 1,),
            in_specs=[any_spec, any_spec, any_spec, any_spec],
            out_specs=pl.BlockSpec((tm, d), lambda s, *_: (jnp.maximum(s - 1, 0), 0)),
            scratch_shapes=[pltpu.VMEM((tm, d), F32), pltpu.VMEM((tm, d), F32),
                            pltpu.SemaphoreType.DMA((2,)), pltpu.SemaphoreType.DMA((3,)),
                            pltpu.VMEM((d, f), F32), pltpu.VMEM((d, f), F32), pltpu.VMEM((f, d), F32),
                            pltpu.VMEM((d, f), BF16), pltpu.VMEM((d, f), BF16),
                            pltpu.VMEM((f, d), BF16)],
        ),
        out_shape=jax.ShapeDtypeStruct((p, d), F32),
        compiler_params=_cparams("arbitrary"),
        name="moe_ffn",
    )(tile_expert, n_used, first, nxt, src, h2, wg, wu, wd)


COMBINE_TILE = 256


def _final_kernel(pos_ref, x1_ref, ys_hbm, rt_ref, g_ref, o_ref, ya0, yb0, ya1, yb1, sems):
    s = pl.program_id(0)
    n_tiles = pl.num_programs(0) - 1
    tm = COMBINE_TILE
    n_tok = pos_ref.shape[0] // 2
    ybufs = ((ya0, yb0), (ya1, yb1))

    def start_gather(tile, slot):
        for k in range(2):
            for r in range(tm):
                row = pos_ref[k * n_tok + tile * tm + r]
                pltpu.make_async_copy(ys_hbm.at[pl.ds(row, 1), :],
                                      ybufs[slot][k].at[pl.ds(r, 1), :], sems.at[slot]).start()

    def wait_gather(slot):
        for k in range(2):
            pltpu.make_async_copy(ys_hbm.at[pl.ds(0, tm), :], ybufs[slot][k], sems.at[slot]).wait()

    def finish(slot):
        rt = rt_ref[...]
        x2 = (x1_ref[...] + rt[:, 2:3] * ybufs[slot][0][...] + rt[:, 3:4] * ybufs[slot][1][...])
        o_ref[...] = (x2 * lax.rsqrt(jnp.mean(x2 * x2, axis=-1, keepdims=True) + RMS_EPS)
                      * g_ref[...])

    @pl.when(s == 0)
    def _():
        start_gather(s, 0)

    for parity in range(2):
        @pl.when((s >= 1) & (s < n_tiles) & (s % 2 == parity))
        def _(parity=parity):
            wait_gather(1 - parity)
            start_gather(s, parity)
            finish(1 - parity)

        @pl.when((s == n_tiles) & (s % 2 == parity))
        def _(parity=parity):
            wait_gather(1 - parity)
            finish(1 - parity)


def _final(x1, ys, pos, rt, g):
    n, d = x1.shape
    tm = COMBINE_TILE
    prev = lambda s: jnp.maximum(s - 1, 0)
    return pl.pallas_call(
        _final_kernel,
        grid_spec=pltpu.PrefetchScalarGridSpec(
            num_scalar_prefetch=1,
            grid=(n // tm + 1,),
            in_specs=[pl.BlockSpec((tm, d), lambda s, p: (prev(s), 0)),
                      pl.BlockSpec(memory_space=pl.ANY),
                      pl.BlockSpec((tm, LANES), lambda s, p: (prev(s), 0)),
                      pl.BlockSpec((1, d), lambda s, p: (0, 0))],
            out_specs=pl.BlockSpec((tm, d), lambda s, p: (prev(s), 0)),
            scratch_shapes=[pltpu.VMEM((tm, d), F32)] * 4 + [pltpu.SemaphoreType.DMA((2,))],
        ),
        out_shape=jax.ShapeDtypeStruct((n, d), F32),
        compiler_params=_cparams("arbitrary"),
        name="combine_norm",
    )(pos, x1, ys, rt, g)


def _dispatch_plan(rt, counts, n_tokens):
    tm = MOE_TILE
    n_rows = 2 * n_tokens + N_EXPERTS * tm
    n_tiles = n_rows // tm
    ids = rt[:, 0:6].astype(jnp.int32)
    ef = jnp.concatenate([ids[:, 0], ids[:, 1]])
    rank = jnp.concatenate([ids[:, 4], ids[:, 5]])
    counts = counts[0, :N_EXPERTS].astype(jnp.int32)
    tiles_per = (counts + tm - 1) // tm
    tile_end = jnp.cumsum(tiles_per)
    row_start = (tile_end - tiles_per) * tm
    pos = row_start[ef] + rank
    tok = jnp.concatenate([jnp.arange(n_tokens, dtype=jnp.int32)] * 2)
    src = jnp.zeros((n_rows,), jnp.int32).at[pos].set(tok)
    n_used = tile_end[-1]
    tile_ids = jnp.arange(n_tiles, dtype=jnp.int32)
    tile_expert = jnp.sum((tile_ids[:, None] >= tile_end[None, :]).astype(jnp.int32), axis=1)
    last_expert = jnp.sum((n_used - 1 >= tile_end).astype(jnp.int32))
    tile_expert = jnp.where(tile_ids < n_used, tile_expert, last_expert).astype(jnp.int32)
    first = jnp.concatenate([jnp.ones((1,), jnp.int32),
                             (tile_expert[1:] != tile_expert[:-1]).astype(jnp.int32)])
    eid = jnp.arange(N_EXPERTS, dtype=jnp.int32)
    later = (tiles_per[None, :] > 0) & (eid[None, :] > eid[:, None])
    next_expert = jnp.min(jnp.where(later, eid[None, :], N_EXPERTS), axis=1)
    next_expert = jnp.where(next_expert == N_EXPERTS, -1, next_expert).astype(jnp.int32)
    nxt = next_expert[tile_expert]
    return (src, pos[:n_tokens], pos[n_tokens:], tile_expert, first, nxt,
            n_used.reshape(1).astype(jnp.int32))


def kernel(x, norm_mix_g, w_in, hgrn_lb_logits, hgrn_out_norm_g, w_out, norm_ffn_g, w_group_router,
           b_group_router, w_expert_router, b_expert_router, w_gate, w_up, w_down, final_norm_g):
    b, t, d = x.shape
    n = b * t
    depth = w_in.shape[0]
    assert depth == 1, "the final norm is fused into the combine step of the only layer"
    attn_w = d // 2
    n_heads = attn_w // HEAD_DIM
    x2 = x.reshape(n, d)
    lb_all = jnp.cumsum(jax.nn.softmax(hgrn_lb_logits.astype(F32), axis=0), axis=0)[:depth]
    slope2 = jnp.asarray(2.0 ** (-8.0 * np.arange(1, n_heads + 1) / n_heads), dtype=F32) * F32(LOG2E)
    s_hi = slope2.astype(BF16).astype(F32)
    s_mid = (slope2 - s_hi).astype(BF16).astype(F32)
    s_lo = (slope2 - s_hi - s_mid).astype(BF16).astype(F32)
    slope_pieces = jnp.stack([s_hi, s_mid, s_lo])
    proj_scale = jnp.concatenate([jnp.full((attn_w,), HEAD_DIM ** -0.5 * LOG2E, F32),
                                  jnp.ones((w_in.shape[2] - attn_w,), F32)]).reshape(1, -1)

    for l in range(depth):
        g_mix = norm_mix_g[l].reshape(1, d)
        x2, w_in_l, w_out_l = lax.optimization_barrier(
            (x2, w_in[l].astype(BF16), w_out[l].astype(BF16)))
        proj = _inproj(x2, g_mix, w_in_l, 0, proj_scale, BF16).reshape(b, t, -1)
        o_attn = _moba(proj, slope_pieces, n_heads)
        o_rec = _hgrn(proj, 3 * attn_w, lb_all[l].reshape(1, -1),
                      hgrn_out_norm_g[l].reshape(1, HEAD_DIM), n_heads)

        wr = jnp.concatenate(
            [w_group_router[l],
             jnp.transpose(w_expert_router[l], (1, 0, 2)).reshape(d, N_EXPERTS)], axis=1)
        wr = jnp.pad(wr, ((0, 0), (0, LANES - wr.shape[1])))
        wr_hi = wr.astype(BF16)
        wr = jnp.concatenate([wr_hi, (wr - wr_hi.astype(F32)).astype(BF16)], axis=1)
        br =jnp.concatenate([b_group_router[l], b_expert_router[l].reshape(-1)])
        br = jnp.pad(br, (0, LANES - br.shape[0])).reshape(1, LANES)
        x1, h2, rt, counts = _outproj(o_attn.reshape(n, attn_w), o_rec.reshape(n, -1), x2,
                                      w_out_l, norm_ffn_g[l].reshape(1, d), wr, br)
        src, pos1, pos2, tile_expert, first, nxt, n_used = _dispatch_plan(rt, counts, n)
        ys = _moe(tile_expert, n_used, first, nxt, src, h2, w_gate[l], w_up[l], w_down[l])
        x2 = _final(x1, ys, jnp.concatenate([pos1, pos2]), rt, final_norm_g.reshape(1, d))
    return x2.reshape(b, t, d)
```

```python
import functools

import jax
import jax.numpy as jnp
import numpy as np
from jax import lax
from jax.experimental import pallas as pl
from jax.experimental.pallas import tpu as pltpu

F32 = jnp.float32
BF16 = jnp.bfloat16
HIGHEST = lax.Precision.HIGHEST

HEAD_DIM = 128
MOBA_BLOCK = 256
MOBA_TOPK = 3
N_GROUPS = 4
EXPERTS_PER_GROUP = 4
N_EXPERTS = N_GROUPS * EXPERTS_PER_GROUP
RMS_EPS = 1e-6

LANES = 128
VMEM_LIMIT_BYTES = 56 * 1024 * 1024

HGRN_CHUNK = 64
HGRN_SUB = 16
HGRN_STEP = 256
HGRN_HEADS_PER_STEP = 8
HGRN_MIN_SAFE_F = 2.0 ** -7
MOE_TILE = 256
NT_DIMS = (((1,), (1,)), ((), ()))
TN_DIMS = (((0,), (0,)), ((), ()))


def _cparams(*sem):
    return pltpu.CompilerParams(dimension_semantics=sem, vmem_limit_bytes=VMEM_LIMIT_BYTES)


def _inproj_kernel(x_ref, g_ref, w_ref, cs_ref, o_ref, hn_ref):
    @pl.when(pl.program_id(1) == 0)
    def _():
        x = x_ref[...]
        ms = jnp.mean(x * x, axis=-1, keepdims=True)
        hn_ref[...] = (x * lax.rsqrt(ms + RMS_EPS) * g_ref[...]).astype(BF16)

    acc = jnp.dot(hn_ref[...], w_ref[...], preferred_element_type=F32)
    o_ref[...] = (acc * cs_ref[...]).astype(o_ref.dtype)


def _inproj(x2, g, w, col0, colscale, out_dtype, tm=1024, tn=1024):
    n, d = x2.shape
    cols = colscale.shape[1]
    j0 = col0 // tn
    return pl.pallas_call(
        _inproj_kernel,
        grid=(n // tm, cols // tn),
        in_specs=[
            pl.BlockSpec((tm, d), lambda i, j: (i, 0)),
            pl.BlockSpec((1, d), lambda i, j: (0, 0)),
            pl.BlockSpec((d, tn), lambda i, j: (0, j0 + j)),
            pl.BlockSpec((1, tn), lambda i, j: (0, j)),
        ],
        out_specs=pl.BlockSpec((tm, tn), lambda i, j: (i, j)),
        out_shape=jax.ShapeDtypeStruct((n, cols), out_dtype),
        scratch_shapes=[pltpu.VMEM((tm, d), BF16)],
        compiler_params=_cparams("parallel", "arbitrary"),
        name="inproj",
    )(x2, g, w, colscale)


MOBA_HEADS_PER_STEP = 4
MOBA_EXTRA_POS = 0
MOBA_EXTRA_SEL = 8
MASK_BIG = 2.0 ** 60
LOG2E = 1.4426950408889634


def _moba_key_extras(t):
    nb = t // MOBA_BLOCK
    assert MOBA_EXTRA_SEL + nb <= LANES and MOBA_BLOCK <= 256
    pos = np.arange(t)
    kx = np.zeros((t, LANES), np.float32)
    kx[:, MOBA_EXTRA_POS:MOBA_EXTRA_POS + 3] = ((pos // MOBA_BLOCK) * MOBA_BLOCK)[:, None]
    kx[:, MOBA_EXTRA_POS + 3:MOBA_EXTRA_POS + 6] = (pos % MOBA_BLOCK)[:, None]
    kx[pos, MOBA_EXTRA_SEL + pos // MOBA_BLOCK] = -MASK_BIG
    return jnp.asarray(kx, dtype=BF16)


def _moba_kernel(sl_ref, q_ref, k_ref, v_ref, kx_ref, o_ref, kaug_ref, qaug_ref, *, nb):
    hg = pl.program_id(1)
    i = pl.program_id(2)
    bs = MOBA_BLOCK
    dh = HEAD_DIM
    t = q_ref.shape[1]
    nbp = -(-nb // 8) * 8
    nx = 8 + nbp

    @pl.when(i == 0)
    def _():
        r8 = lax.broadcasted_iota(jnp.int32, (8, t), 0)
        piece = r8 % 3
        er = lax.broadcasted_iota(jnp.int32, (nx, LANES), 0)
        ec = lax.broadcasted_iota(jnp.int32, (nx, LANES), 1)
        embed = jnp.where(er == ec, 1.0, 0.0).astype(BF16)
        blk = lax.broadcasted_iota(jnp.int32, (nbp, t), 0)
        qblk = lax.broadcasted_iota(jnp.int32, (nbp, t), 1) // bs
        for hh in range(MOBA_HEADS_PER_STEP):
            h = hg * MOBA_HEADS_PER_STEP + hh
            cs = slice(hh * dh, (hh + 1) * dh)
            kaug_ref[hh, :, 0:dh] = k_ref[0, :, cs]
            kaug_ref[hh, :, dh:] = kx_ref[...]
            rows = [jnp.mean(k_ref[0, j * bs:(j + 1) * bs, cs].astype(F32), axis=0, keepdims=True)
                    for j in range(nb)]
            if nbp > nb:
                rows.append(jnp.zeros((nbp - nb, dh), F32))
            km = jnp.concatenate(rows, axis=0)
            hi = km.astype(BF16)
            mid = (km - hi.astype(F32)).astype(BF16)
            lo = (km - hi.astype(F32) - mid.astype(F32)).astype(BF16)
            km4 = jnp.concatenate([hi, mid, lo, jnp.zeros_like(hi)], axis=0)
            q = q_ref[0, :, cs]
            g4 = lax.dot_general(km4, q, NT_DIMS, preferred_element_type=F32)
            gate_t = g4[0:nbp] + g4[nbp:2 * nbp] + g4[2 * nbp:3 * nbp]
            rank = jnp.zeros(gate_t.shape, F32)
            for jp in range(nb - 1):
                other = gate_t[jp:jp + 1, :]
                beats = (other > gate_t) | ((other == gate_t) & (blk > jp))
                rank = rank + jnp.where(beats & (qblk > jp), 1.0, 0.0)
            notsel_t = jnp.where((blk < qblk) & (rank >= MOBA_TOPK), 1.0, 0.0)
            slope_rows = jnp.where(r8 >= 6, 0.0,
                                   jnp.where(piece == 0, sl_ref[0, h],
                                             jnp.where(piece == 1, sl_ref[1, h], sl_ref[2, h])))
            qx_t = jnp.concatenate([slope_rows, notsel_t], axis=0).astype(BF16)
            qx = lax.dot_general(qx_t, embed, TN_DIMS, preferred_element_type=F32)
            qaug_ref[hh, :, 0:dh] = q
            qaug_ref[hh, :, dh:] = qx.astype(BF16)

    rowi = lax.broadcasted_iota(jnp.int32, (bs, bs), 0)
    coli = lax.broadcasted_iota(jnp.int32, (bs, bs), 1)

    for c in range(nb):
        @pl.when(i == c)
        def _(c=c):
            n = (c + 1) * bs
            for hh in range(MOBA_HEADS_PER_STEP):
                cs = slice(hh * dh, (hh + 1) * dh)
                s = lax.dot_general(qaug_ref[hh, c * bs:n, :], kaug_ref[hh, 0:n, :], NT_DIMS,
                                    preferred_element_type=F32)
                s_own = jnp.where(rowi >= coli, s[:, c * bs:], -jnp.inf)
                m = jnp.max(s_own, axis=1, keepdims=True)
                if c > 0:
                    s_past = s[:, :c * bs]
                    m = jnp.maximum(m, jnp.max(s_past, axis=1, keepdims=True))
                p_own = jnp.exp2(s_own - m)
                l = jnp.sum(p_own, axis=1, keepdims=True)
                acc = jnp.dot(p_own.astype(BF16), v_ref[0, c * bs:n, cs],
                              preferred_element_type=F32)
                if c > 0:
                    p_past = jnp.exp2(s_past - m)
                    l = l + jnp.sum(p_past, axis=1, keepdims=True)
                    acc = acc + jnp.dot(p_past.astype(BF16), v_ref[0, 0:c * bs, cs],
                                        preferred_element_type=F32)
                o_ref[0, :, cs] = (acc / l).astype(o_ref.dtype)


def _moba(qkv, slope_pieces, n_heads):
    b, t, _ = qkv.shape
    bs = MOBA_BLOCK
    nb = t // bs
    hps = MOBA_HEADS_PER_STEP
    assert n_heads % hps == 0
    ng = n_heads // hps
    w = hps * HEAD_DIM
    kern = functools.partial(_moba_kernel, nb=nb)
    return pl.pallas_call(
        kern,
        grid=(b, ng, nb),
        in_specs=[
            pl.BlockSpec(memory_space=pltpu.SMEM),
            pl.BlockSpec((1, t, w), lambda bi, hi, qi: (bi, 0, hi)),
            pl.BlockSpec((1, t, w), lambda bi, hi, qi: (bi, 0, ng + hi)),
            pl.BlockSpec((1, t, w), lambda bi, hi, qi: (bi, 0, 2 * ng + hi)),
            pl.BlockSpec((t, LANES), lambda bi, hi, qi: (0, 0)),
        ],
        out_specs=pl.BlockSpec((1, bs, w), lambda bi, hi, qi: (bi, qi, hi)),
        out_shape=jax.ShapeDtypeStruct((b, t, n_heads * HEAD_DIM), BF16),
        scratch_shapes=[
            pltpu.VMEM((hps, t, HEAD_DIM + LANES), BF16),
            pltpu.VMEM((hps, t, HEAD_DIM + LANES), BF16),
        ],
        compiler_params=_cparams("parallel", "parallel", "arbitrary"),
        name="moba",
    )(slope_pieces, qkv, qkv, qkv, _moba_key_extras(t))


def _sigmoid(x):
    return 1.0 / (1.0 + jnp.exp(-x))


def _hgrn_chunk(q, f, iv, g, gn, st, bounded_decay):
    c = HGRN_CHUNK
    sub = HGRN_SUB
    half = sub // 2
    nsub = c // sub
    qf = q * _sigmoid(q)
    kf = jnp.maximum(1.0 - f, 0.0)
    r = lax.broadcasted_iota(jnp.int32, (c, c), 0)
    cc = lax.broadcasted_iota(jnp.int32, (c, c), 1)
    tril = jnp.where(r >= cc, 1.0, 0.0).astype(BF16)
    lf = jnp.log2(f)
    dk = lf.shape[1]
    lf_hi = lf.astype(BF16)
    lf_mid = (lf - lf_hi.astype(F32)).astype(BF16)
    lf_lo = (lf - lf_hi.astype(F32) - lf_mid.astype(F32)).astype(BF16)
    b3 = jnp.dot(tril, jnp.concatenate([lf_hi, lf_mid, lf_lo], axis=1),
                 preferred_element_type=F32)
    b2 = b3[:, :dk] + b3[:, dk:2 * dk] + b3[:, 2 * dk:]
    c2 = b2 - jnp.log2(kf)

    inter = lax.dot_general((qf * jnp.exp2(b2)).astype(BF16), st.astype(BF16), NT_DIMS,
                            preferred_element_type=F32)

    lane = lax.broadcasted_iota(jnp.int32, (sub, LANES), 1)
    tsub = lax.broadcasted_iota(jnp.int32, (sub, LANES), 0)
    colid = lax.broadcasted_iota(jnp.int32, (sub, c), 1)
    ones = jnp.ones((LANES, LANES), BF16)
    a_rows = []
    for bi in range(nsub):
        lo = bi * sub
        b_i = b2[lo:lo + sub]
        c_i = c2[lo:lo + sub]
        q_i = qf[lo:lo + sub]
        if bounded_decay:
            qt = q_i * (jnp.exp2(b_i - b2[lo - 1:lo, :]) if bi > 0 else jnp.exp2(b_i))
            kt = jnp.exp2((b2[lo - 1:lo, :] - c2[:lo + sub]) if bi > 0 else -c2[:sub])
            if lo + sub < c:
                kt = jnp.concatenate([kt, jnp.zeros((c - lo - sub, kt.shape[1]), F32)], axis=0)
            cross = lax.dot_general(qt.astype(BF16), kt.astype(BF16), NT_DIMS,
                                    preferred_element_type=F32)
            a_rows.append(jnp.where(colid <= lo + tsub[:, :c], cross, 0.0))
            continue
        pieces = []
        for s in range(sub):
            if s < half:
                pieces.append(q_i * jnp.exp2(b_i - c_i[s:s + 1, :]))
            else:
                pieces.append(q_i[half:] * jnp.exp2(b_i[half:] - c_i[s:s + 1, :]))
        pm = jnp.concatenate(pieces, axis=0).astype(BF16)
        rs = jnp.dot(pm, ones, preferred_element_type=F32)
        rel = lane - lo
        key = jnp.where((rel >= 0) & (rel <= tsub), rel, -1)
        key_lo, key_hi = key[:half], key[half:]
        a_lo = jnp.zeros((half, LANES), F32)
        a_hi = jnp.zeros((half, LANES), F32)
        off = 0
        for s in range(sub):
            if s < half:
                a_lo = jnp.where(key_lo == s, rs[off:off + half], a_lo)
                a_hi = jnp.where(key_hi == s, rs[off + half:off + sub], a_hi)
                off += sub
            else:
                a_hi = jnp.where(key_hi == s, rs[off:off + half], a_hi)
                off += half
        a_blk = jnp.concatenate([a_lo, a_hi], axis=0)[:, :c]
        if bi > 0:
            b0 = b2[lo - 1:lo, :]
            qt = q_i * jnp.exp2(b_i - b0)
            kt = jnp.exp2(jnp.minimum(b0 - c2, 0.0))
            cross = lax.dot_general(qt.astype(BF16), kt.astype(BF16), NT_DIMS,
                                    preferred_element_type=F32)
            a_blk = a_blk + jnp.where(colid < lo, cross, 0.0)
        a_rows.append(a_blk)
    a = jnp.concatenate(a_rows, axis=0)
    intra = jnp.dot(a.astype(BF16), iv.astype(BF16), preferred_element_type=F32)

    b_last = b2[c - 1:c, :]
    khat = jnp.exp2(b_last - c2)
    st_new = st * jnp.exp2(b_last) + lax.dot_general(
        iv.astype(BF16), khat.astype(BF16), TN_DIMS, preferred_element_type=F32)

    o = inter + intra
    y = o * lax.rsqrt(jnp.mean(o * o, axis=-1, keepdims=True) + RMS_EPS) * gn
    return y * (g * _sigmoid(g)), st_new


def _hgrn_kernel(q_ref, f_ref, i_ref, g_ref, lb_ref, gn_ref, o_ref, st_ref):
    @pl.when(pl.program_id(2) == 0)
    def _():
        st_ref[...] = jnp.zeros_like(st_ref)

    gn = gn_ref[...]
    d = HEAD_DIM
    lb = lb_ref[...]
    f_all = lb + (1.0 - lb) * _sigmoid(f_ref[0].astype(F32))
    f_min = jnp.min(f_all)

    def run(bounded_decay):
        for hh in range(HGRN_HEADS_PER_STEP):
            cs = slice(hh * d, (hh + 1) * d)
            st = st_ref[hh]
            for ci in range(HGRN_STEP // HGRN_CHUNK):
                sl = slice(ci * HGRN_CHUNK, (ci + 1) * HGRN_CHUNK)
                out, st = _hgrn_chunk(q_ref[0, sl, cs].astype(F32), f_all[sl, cs],
                                      i_ref[0, sl, cs], g_ref[0, sl, cs].astype(F32), gn, st,
                                      bounded_decay)
                o_ref[0, sl, cs] = out.astype(o_ref.dtype)
            st_ref[hh] = st

    @pl.when(f_min >= HGRN_MIN_SAFE_F)
    def _():
        run(True)

    @pl.when(jnp.logical_not(f_min >= HGRN_MIN_SAFE_F))
    def _():
        run(False)


def _hgrn(hp, col0, lb, gn, n_heads):
    b, t, _ = hp.shape
    d = HEAD_DIM
    ts = HGRN_STEP
    hps = HGRN_HEADS_PER_STEP
    assert n_heads % hps == 0 and col0 % (hps * d) == 0
    ng = n_heads // hps
    blk0 = col0 // (hps * d)

    def col(group):
        return pl.BlockSpec((1, ts, hps * d), lambda bi, hi, ti: (bi, ti, blk0 + group * ng + hi))

    return pl.pallas_call(
        _hgrn_kernel,
        grid=(b, ng, t // ts),
        in_specs=[col(0), col(1), col(2), col(3),
                  pl.BlockSpec((1, hps * d), lambda bi, hi, ti: (0, hi)),
                  pl.BlockSpec((1, d), lambda bi, hi, ti: (0, 0))],
        out_specs=pl.BlockSpec((1, ts, hps * d), lambda bi, hi, ti: (bi, ti, hi)),
        out_shape=jax.ShapeDtypeStruct((b, t, n_heads * d), BF16),
        scratch_shapes=[pltpu.VMEM((hps, d, d), F32)],
        compiler_params=_cparams("parallel", "parallel", "arbitrary"),
        name="hgrn2",
    )(hp, hp, hp, hp, lb, gn)


def _outproj_kernel(oa_ref, or_ref, x_ref, w_ref, g_ref, wr_ref, br_ref,
                    x1_ref, h2_ref, rt_ref, cnt_out_ref, cnt_ref):
    wa = oa_ref.shape[1]
    x1 = (x_ref[...]
          + jnp.dot(oa_ref[...], w_ref[0:wa, :], preferred_element_type=F32)
          + jnp.dot(or_ref[...], w_ref[wa:, :], preferred_element_type=F32))
    x1_ref[...] = x1
    h2 = x1 * lax.rsqrt(jnp.mean(x1 * x1, axis=-1, keepdims=True) + RMS_EPS) * g_ref[...]
    h2_ref[...] = h2

    h_hi = h2.astype(BF16)
    h_mid = (h2 - h_hi.astype(F32)).astype(BF16)
    part = jnp.dot(h_hi, wr_ref[...], preferred_element_type=F32)
    logits = (part[:, :LANES] + part[:, LANES:] + br_ref[...]
              + jnp.dot(h_mid, wr_ref[:, :LANES], preferred_element_type=F32))
    lane = lax.broadcasted_iota(jnp.int32, logits.shape, 1)
    big = jnp.int32(4 * LANES)
    ninf = -jnp.inf

    lg = jnp.where(lane < N_GROUPS, logits, ninf)
    mg = jnp.max(lg, axis=1, keepdims=True)
    gidx = jnp.min(jnp.where(lg == mg, lane, big), axis=1, keepdims=True)
    grp_w = 1.0 / jnp.sum(jnp.exp(lg - mg), axis=1, keepdims=True)

    lo = N_GROUPS + EXPERTS_PER_GROUP * gidx
    le = jnp.where((lane >= lo) & (lane < lo + EXPERTS_PER_GROUP), logits, ninf)
    m1 = jnp.max(le, axis=1, keepdims=True)
    i1 = jnp.min(jnp.where(le == m1, lane, big), axis=1, keepdims=True)
    le2 = jnp.where(lane == i1, ninf, le)
    m2 = jnp.max(le2, axis=1, keepdims=True)
    i2 = jnp.min(jnp.where(le2 == m2, lane, big), axis=1, keepdims=True)
    r21 = jnp.exp(m2 - m1)
    w1 = grp_w / (1.0 + r21)
    w2 = grp_w * r21 / (1.0 + r21)
    e1 = i1 - N_GROUPS
    e2 = i2 - N_GROUPS

    @pl.when(pl.program_id(0) == 0)
    def _():
        cnt_ref[...] = jnp.zeros_like(cnt_ref)

    tm = logits.shape[0]
    onehot = jnp.where((lane == e1) | (lane == e2), 1.0, 0.0)
    rr = lax.broadcasted_iota(jnp.int32, (tm, tm), 0)
    rc = lax.broadcasted_iota(jnp.int32, (tm, tm), 1)
    before = jnp.where(rr > rc, 1.0, 0.0).astype(BF16)
    prefix = (jnp.dot(before, onehot.astype(BF16), preferred_element_type=F32) + cnt_ref[0:1, :])
    rank1 = jnp.sum(jnp.where(lane == e1, prefix, 0.0), axis=1, keepdims=True)
    rank2 = jnp.sum(jnp.where(lane == e2, prefix, 0.0), axis=1, keepdims=True)
    total = cnt_ref[0:1, :] + jnp.sum(onehot, axis=0, keepdims=True)
    cnt_ref[0:1, :] = total
    cnt_out_ref[...] = jnp.broadcast_to(total, cnt_out_ref.shape)

    cols = [e1.astype(F32), e2.astype(F32), w1, w2, rank1, rank2]
    rt = jnp.zeros(logits.shape, F32)
    for ci, cv in enumerate(cols):
        rt = jnp.where(lane == ci, cv, rt)
    rt_ref[...] = rt


def _outproj(oa, orec, x2, w_out, g, wr, br, tm=256):
    n, d = x2.shape
    wa = oa.shape[1]
    wrc = orec.shape[1]
    row = lambda i: (i, 0)
    const = lambda i: (0, 0)
    return pl.pallas_call(
        _outproj_kernel,
        grid=(n // tm,),
        in_specs=[
            pl.BlockSpec((tm, wa), row),
            pl.BlockSpec((tm, wrc), row),
            pl.BlockSpec((tm, d), row),
            pl.BlockSpec((wa + wrc, d), const),
            pl.BlockSpec((1, d), const),
            pl.BlockSpec((d, 2 * LANES), const),
            pl.BlockSpec((1, LANES), const),
        ],
        out_specs=[pl.BlockSpec((tm, d), row), pl.BlockSpec((tm, d), row),
                   pl.BlockSpec((tm, LANES), row), pl.BlockSpec((8, LANES), const)],
        out_shape=[jax.ShapeDtypeStruct((n, d), F32), jax.ShapeDtypeStruct((n, d), F32),
                   jax.ShapeDtypeStruct((n, LANES), F32), jax.ShapeDtypeStruct((8, LANES), F32)],
        scratch_shapes=[pltpu.VMEM((8, LANES), F32)],
        compiler_params=_cparams("arbitrary"),
        name="outproj_route",
    )(oa, orec, x2, w_out, g, wr, br)


WEIGHT_CAST_ROWS = 256


def _cast_rows(dst_ref, src_ref):
    rows = dst_ref.shape[0]
    step = min(rows, WEIGHT_CAST_ROWS)
    assert rows % step == 0

    def body(c, carry):
        sl = pl.ds(pl.multiple_of(c * step, step), step)
        dst_ref[sl, :] = src_ref[sl, :].astype(BF16)
        return carry

    lax.fori_loop(0, rows // step, body, 0)


def _moe_kernel(te_ref, nu_ref, first_ref, nxt_ref, src_ref, h2_hbm, wg_hbm, wu_hbm, wd_hbm, y_ref,
                xbuf0, xbuf1, sems, wsems, wg32, wu32, wd32, wgb, wub, wdb):
    s = pl.program_id(0)
    nu = nu_ref[0]
    tm = MOE_TILE
    xbufs = (xbuf0, xbuf1)
    prev_tile = jnp.maximum(s - 1, 0)

    def weight_copies(e):
        return (pltpu.make_async_copy(wg_hbm.at[e], wg32, wsems.at[0]),
                pltpu.make_async_copy(wu_hbm.at[e], wu32, wsems.at[1]),
                pltpu.make_async_copy(wd_hbm.at[e], wd32, wsems.at[2]))

    def start_gather(tile, slot):
        for r in range(tm):
            tok = src_ref[tile * tm + r]
            pltpu.make_async_copy(h2_hbm.at[pl.ds(tok, 1), :], xbufs[slot].at[pl.ds(r, 1), :],
                                  sems.at[slot]).start()

    def wait_gather(slot):
        pltpu.make_async_copy(h2_hbm.at[pl.ds(0, tm), :], xbufs[slot], sems.at[slot]).wait()

    def ffn(slot):
        x = xbufs[slot][...].astype(BF16)
        a = jnp.dot(x, wgb[...], preferred_element_type=F32)
        u = jnp.dot(x, wub[...], preferred_element_type=F32)
        hid = (a * _sigmoid(a) * u).astype(BF16)
        y_ref[...] = jnp.dot(hid, wdb[...], preferred_element_type=F32)

    @pl.when(s == 0)
    def _():
        for cp in weight_copies(te_ref[0]):
            cp.start()
        start_gather(s, 0)

    @pl.when((s >= 1) & (s <= nu) & (first_ref[prev_tile] == 1))
    def _():
        for cp in weight_copies(te_ref[prev_tile]):
            cp.wait()
        _cast_rows(wgb, wg32)
        _cast_rows(wub, wu32)
        _cast_rows(wdb, wd32)

        @pl.when(nxt_ref[prev_tile] >= 0)
        def _():
            for cp in weight_copies(nxt_ref[prev_tile]):
                cp.start(priority=1)

    for parity in range(2):
        @pl.when((s >= 1) & (s < nu) & (s % 2 == parity))
        def _(parity=parity):
            wait_gather(1 - parity)
            start_gather(s, parity)
            ffn(1 - parity)

        @pl.when((s >= 1) & (s == nu) & (s % 2 == parity))
        def _(parity=parity):
            wait_gather(1 - parity)
            ffn(1 - parity)

    @pl.when(s > nu)
    def _():
        y_ref[...] = jnp.zeros_like(y_ref)


def _moe(tile_expert, n_used, first, nxt, src, h2, wg, wu, wd):
    p = src.shape[0]
    d, f = wg.shape[1:]
    tm = MOE_TILE
    n_tiles = p // tm
    any_spec = pl.BlockSpec(memory_space=pl.ANY)
    return pl.pallas_call(
        _moe_kernel,
        grid_spec=pltpu.PrefetchScalarGridSpec(
            num_scalar_prefetch=5,
            grid=(n_tiles + 1,),
            in_specs=[any_spec, any_spec, any_spec, any_spec],
            out_specs=pl.BlockSpec((tm, d), lambda s, *_: (jnp.maximum(s - 1, 0), 0)),
            scratch_shapes=[pltpu.VMEM((tm, d), F32), pltpu.VMEM((tm, d), F32),
                            pltpu.SemaphoreType.DMA((2,)), pltpu.SemaphoreType.DMA((3,)),
                            pltpu.VMEM((d, f), F32), pltpu.VMEM((d, f), F32), pltpu.VMEM((f, d), F32),
                            pltpu.VMEM((d, f), BF16), pltpu.VMEM((d, f), BF16),
                            pltpu.VMEM((f, d), BF16)],
        ),
        out_shape=jax.ShapeDtypeStruct((p, d), F32),
        compiler_params=_cparams("arbitrary"),
        name="moe_ffn",
    )(tile_expert, n_used, first, nxt, src, h2, wg, wu, wd)


COMBINE_TILE = 256


def _final_kernel(pos_ref, x1_ref, ys_hbm, rt_ref, g_ref, o_ref, ya0, yb0, ya1, yb1, sems):
    s = pl.program_id(0)
    n_tiles = pl.num_programs(0) - 1
    tm = COMBINE_TILE
    n_tok = pos_ref.shape[0] // 2
    ybufs = ((ya0, yb0), (ya1, yb1))

    def start_gather(tile, slot):
        for k in range(2):
            for r in range(tm):
                row = pos_ref[k * n_tok + tile * tm + r]
                pltpu.make_async_copy(ys_hbm.at[pl.ds(row, 1), :],
                                      ybufs[slot][k].at[pl.ds(r, 1), :], sems.at[slot]).start()

    def wait_gather(slot):
        for k in range(2):
            pltpu.make_async_copy(ys_hbm.at[pl.ds(0, tm), :], ybufs[slot][k], sems.at[slot]).wait()

    def finish(slot):
        rt = rt_ref[...]
        x2 = (x1_ref[...] + rt[:, 2:3] * ybufs[slot][0][...] + rt[:, 3:4] * ybufs[slot][1][...])
        o_ref[...] = (x2 * lax.rsqrt(jnp.mean(x2 * x2, axis=-1, keepdims=True) + RMS_EPS)
                      * g_ref[...])

    @pl.when(s == 0)
    def _():
        start_gather(s, 0)

    for parity in range(2):
        @pl.when((s >= 1) & (s < n_tiles) & (s % 2 == parity))
        def _(parity=parity):
            wait_gather(1 - parity)
            start_gather(s, parity)
            finish(1 - parity)

        @pl.when((s == n_tiles) & (s % 2 == parity))
        def _(parity=parity):
            wait_gather(1 - parity)
            finish(1 - parity)


def _final(x1, ys, pos, rt, g):
    n, d = x1.shape
    tm = COMBINE_TILE
    prev = lambda s: jnp.maximum(s - 1, 0)
    return pl.pallas_call(
        _final_kernel,
        grid_spec=pltpu.PrefetchScalarGridSpec(
            num_scalar_prefetch=1,
            grid=(n // tm + 1,),
            in_specs=[pl.BlockSpec((tm, d), lambda s, p: (prev(s), 0)),
                      pl.BlockSpec(memory_space=pl.ANY),
                      pl.BlockSpec((tm, LANES), lambda s, p: (prev(s), 0)),
                      pl.BlockSpec((1, d), lambda s, p: (0, 0))],
            out_specs=pl.BlockSpec((tm, d), lambda s, p: (prev(s), 0)),
            scratch_shapes=[pltpu.VMEM((tm, d), F32)] * 4 + [pltpu.SemaphoreType.DMA((2,))],
        ),
        out_shape=jax.ShapeDtypeStruct((n, d), F32),
        compiler_params=_cparams("arbitrary"),
        name="combine_norm",
    )(pos, x1, ys, rt, g)


def _dispatch_plan(rt, counts, n_tokens):
    tm = MOE_TILE
    n_rows = 2 * n_tokens + N_EXPERTS * tm
    n_tiles = n_rows // tm
    ids = rt[:, 0:6].astype(jnp.int32)
    ef = jnp.concatenate([ids[:, 0], ids[:, 1]])
    rank = jnp.concatenate([ids[:, 4], ids[:, 5]])
    counts = counts[0, :N_EXPERTS].astype(jnp.int32)
    tiles_per = (counts + tm - 1) // tm
    tile_end = jnp.cumsum(tiles_per)
    row_start = (tile_end - tiles_per) * tm
    pos = row_start[ef] + rank
    tok = jnp.concatenate([jnp.arange(n_tokens, dtype=jnp.int32)] * 2)
    src = jnp.zeros((n_rows,), jnp.int32).at[pos].set(tok)
    n_used = tile_end[-1]
    tile_ids = jnp.arange(n_tiles, dtype=jnp.int32)
    tile_expert = jnp.sum((tile_ids[:, None] >= tile_end[None, :]).astype(jnp.int32), axis=1)
    last_expert = jnp.sum((n_used - 1 >= tile_end).astype(jnp.int32))
    tile_expert = jnp.where(tile_ids < n_used, tile_expert, last_expert).astype(jnp.int32)
    first = jnp.concatenate([jnp.ones((1,), jnp.int32),
                             (tile_expert[1:] != tile_expert[:-1]).astype(jnp.int32)])
    eid = jnp.arange(N_EXPERTS, dtype=jnp.int32)
    later = (tiles_per[None, :] > 0) & (eid[None, :] > eid[:, None])
    next_expert = jnp.min(jnp.where(later, eid[None, :], N_EXPERTS), axis=1)
    next_expert = jnp.where(next_expert == N_EXPERTS, -1, next_expert).astype(jnp.int32)
    nxt = next_expert[tile_expert]
    return (src, pos[:n_tokens], pos[n_tokens:], tile_expert, first, nxt,
            n_used.reshape(1).astype(jnp.int32))


def kernel(x, norm_mix_g, w_in, hgrn_lb_logits, hgrn_out_norm_g, w_out, norm_ffn_g, w_group_router,
           b_group_router, w_expert_router, b_expert_router, w_gate, w_up, w_down, final_norm_g):
    b, t, d = x.shape
    n = b * t
    depth = w_in.shape[0]
    assert depth == 1, "the final norm is fused into the combine step of the only layer"
    attn_w = d // 2
    n_heads = attn_w // HEAD_DIM
    x2 = x.reshape(n, d)
    lb_all = jnp.cumsum(jax.nn.softmax(hgrn_lb_logits.astype(F32), axis=0), axis=0)[:depth]
    slope2 = jnp.asarray(2.0 ** (-8.0 * np.arange(1, n_heads + 1) / n_heads), dtype=F32) * F32(LOG2E)
    s_hi = slope2.astype(BF16).astype(F32)
    s_mid = (slope2 - s_hi).astype(BF16).astype(F32)
    s_lo = (slope2 - s_hi - s_mid).astype(BF16).astype(F32)
    slope_pieces = jnp.stack([s_hi, s_mid, s_lo])
    proj_scale = jnp.concatenate([jnp.full((attn_w,), HEAD_DIM ** -0.5 * LOG2E, F32),
                                  jnp.ones((w_in.shape[2] - attn_w,), F32)]).reshape(1, -1)

    for l in range(depth):
        g_mix = norm_mix_g[l].reshape(1, d)
        x2, w_in_l, w_out_l = lax.optimization_barrier(
            (x2, w_in[l].astype(BF16), w_out[l].astype(BF16)))
        proj = _inproj(x2, g_mix, w_in_l, 0, proj_scale, BF16).reshape(b, t, -1)
        o_attn = _moba(proj, slope_pieces, n_heads)
        o_rec = _hgrn(proj, 3 * attn_w, lb_all[l].reshape(1, -1),
                      hgrn_out_norm_g[l].reshape(1, HEAD_DIM), n_heads)

        wr = jnp.concatenate(
            [w_group_router[l],
             jnp.transpose(w_expert_router[l], (1, 0, 2)).reshape(d, N_EXPERTS)], axis=1)
        wr = jnp.pad(wr, ((0, 0), (0, LANES - wr.shape[1])))
        wr_hi = wr.astype(BF16)
        wr = jnp.concatenate([wr_hi, (wr - wr_hi.astype(F32)).astype(BF16)], axis=1)
        br =jnp.concatenate([b_group_router[l], b_expert_router[l].reshape(-1)])
        br = jnp.pad(br, (0, LANES - br.shape[0])).reshape(1, LANES)
        x1, h2, rt, counts = _outproj(o_attn.reshape(n, attn_w), o_rec.reshape(n, -1), x2,
                                      w_out_l, norm_ffn_g[l].reshape(1, d), wr, br)
        src, pos1, pos2, tile_expert, first, nxt, n_used = _dispatch_plan(rt, counts, n)
        ys = _moe(tile_expert, n_used, first, nxt, src, h2, w_gate[l], w_up[l], w_down[l])
        x2 = _final(x1, ys, jnp.concatenate([pos1, pos2]), rt, final_norm_g.reshape(1, d))
    return x2.reshape(b, t, d)
```

```python
import functools

import jax
import jax.numpy as jnp
import numpy as np
from jax import lax
from jax.experimental import pallas as pl
from jax.experimental.pallas import tpu as pltpu

F32 = jnp.float32
BF16 = jnp.bfloat16
HIGHEST = lax.Precision.HIGHEST

HEAD_DIM = 128
MOBA_BLOCK = 256
MOBA_TOPK = 3
N_GROUPS = 4
EXPERTS_PER_GROUP = 4
N_EXPERTS = N_GROUPS * EXPERTS_PER_GROUP
RMS_EPS = 1e-6

LANES = 128
VMEM_LIMIT_BYTES = 56 * 1024 * 1024

HGRN_CHUNK = 64
HGRN_SUB = 16
HGRN_STEP = 256
HGRN_HEADS_PER_STEP = 8
HGRN_MIN_SAFE_F = 2.0 ** -3.5
MOE_TILE = 256
NT_DIMS = (((1,), (1,)), ((), ()))
TN_DIMS = (((0,), (0,)), ((), ()))


def _cparams(*sem):
    return pltpu.CompilerParams(dimension_semantics=sem, vmem_limit_bytes=VMEM_LIMIT_BYTES)


def _inproj_kernel(x_ref, g_ref, w_ref, cs_ref, o_ref, hn_ref):
    @pl.when(pl.program_id(1) == 0)
    def _():
        x = x_ref[...]
        ms = jnp.mean(x * x, axis=-1, keepdims=True)
        hn_ref[...] = (x * lax.rsqrt(ms + RMS_EPS) * g_ref[...]).astype(BF16)

    acc = jnp.dot(hn_ref[...], w_ref[...], preferred_element_type=F32)
    o_ref[...] = (acc * cs_ref[...]).astype(o_ref.dtype)


def _inproj(x2, g, w, col0, colscale, out_dtype, tm=1024, tn=1024):
    n, d = x2.shape
    cols = colscale.shape[1]
    j0 = col0 // tn
    return pl.pallas_call(
        _inproj_kernel,
        grid=(n // tm, cols // tn),
        in_specs=[
            pl.BlockSpec((tm, d), lambda i, j: (i, 0)),
            pl.BlockSpec((1, d), lambda i, j: (0, 0)),
            pl.BlockSpec((d, tn), lambda i, j: (0, j0 + j)),
            pl.BlockSpec((1, tn), lambda i, j: (0, j)),
        ],
        out_specs=pl.BlockSpec((tm, tn), lambda i, j: (i, j)),
        out_shape=jax.ShapeDtypeStruct((n, cols), out_dtype),
        scratch_shapes=[pltpu.VMEM((tm, d), BF16)],
        compiler_params=_cparams("parallel", "arbitrary"),
        name="inproj",
    )(x2, g, w, colscale)


MOBA_HEADS_PER_STEP = 4
MOBA_EXTRA_POS = 0
MOBA_EXTRA_SEL = 8
MASK_BIG = 2.0 ** 60
LOG2E = 1.4426950408889634


def _moba_key_extras(t):
    nb = t // MOBA_BLOCK
    assert MOBA_EXTRA_SEL + nb <= LANES and MOBA_BLOCK <= 256
    pos = np.arange(t)
    kx = np.zeros((t, LANES), np.float32)
    kx[:, MOBA_EXTRA_POS:MOBA_EXTRA_POS + 3] = ((pos // MOBA_BLOCK) * MOBA_BLOCK)[:, None]
    kx[:, MOBA_EXTRA_POS + 3:MOBA_EXTRA_POS + 6] = (pos % MOBA_BLOCK)[:, None]
    kx[pos, MOBA_EXTRA_SEL + pos // MOBA_BLOCK] = -MASK_BIG
    return jnp.asarray(kx, dtype=BF16)


def _moba_kernel(sl_ref, q_ref, k_ref, v_ref, kx_ref, o_ref, kaug_ref, qaug_ref, *, nb):
    hg = pl.program_id(1)
    i = pl.program_id(2)
    bs = MOBA_BLOCK
    dh = HEAD_DIM
    t = q_ref.shape[1]
    nbp = -(-nb // 8) * 8
    nx = 8 + nbp

    @pl.when(i == 0)
    def _():
        r8 = lax.broadcasted_iota(jnp.int32, (8, t), 0)
        piece = r8 % 3
        er = lax.broadcasted_iota(jnp.int32, (nx, LANES), 0)
        ec = lax.broadcasted_iota(jnp.int32, (nx, LANES), 1)
        embed = jnp.where(er == ec, 1.0, 0.0).astype(BF16)
        blk = lax.broadcasted_iota(jnp.int32, (nbp, t), 0)
        qblk = lax.broadcasted_iota(jnp.int32, (nbp, t), 1) // bs
        for hh in range(MOBA_HEADS_PER_STEP):
            h = hg * MOBA_HEADS_PER_STEP + hh
            cs = slice(hh * dh, (hh + 1) * dh)
            kaug_ref[hh, :, 0:dh] = k_ref[0, :, cs]
            kaug_ref[hh, :, dh:] = kx_ref[...]
            rows = [jnp.mean(k_ref[0, j * bs:(j + 1) * bs, cs].astype(F32), axis=0, keepdims=True)
                    for j in range(nb)]
            if nbp > nb:
                rows.append(jnp.zeros((nbp - nb, dh), F32))
            km = jnp.concatenate(rows, axis=0)
            hi = km.astype(BF16)
            mid = (km - hi.astype(F32)).astype(BF16)
            lo = (km - hi.astype(F32) - mid.astype(F32)).astype(BF16)
            km4 = jnp.concatenate([hi, mid, lo, jnp.zeros_like(hi)], axis=0)
            q = q_ref[0, :, cs]
            g4 = lax.dot_general(km4, q, NT_DIMS, preferred_element_type=F32)
            gate_t = g4[0:nbp] + g4[nbp:2 * nbp] + g4[2 * nbp:3 * nbp]
            rank = jnp.zeros(gate_t.shape, F32)
            for jp in range(nb - 1):
                other = gate_t[jp:jp + 1, :]
                beats = (other > gate_t) | ((other == gate_t) & (blk > jp))
                rank = rank + jnp.where(beats & (qblk > jp), 1.0, 0.0)
            notsel_t = jnp.where((blk < qblk) & (rank >= MOBA_TOPK), 1.0, 0.0)
            slope_rows = jnp.where(r8 >= 6, 0.0,
                                   jnp.where(piece == 0, sl_ref[0, h],
                                             jnp.where(piece == 1, sl_ref[1, h], sl_ref[2, h])))
            qx_t = jnp.concatenate([slope_rows, notsel_t], axis=0).astype(BF16)
            qx = lax.dot_general(qx_t, embed, TN_DIMS, preferred_element_type=F32)
            qaug_ref[hh, :, 0:dh] = q
            qaug_ref[hh, :, dh:] = qx.astype(BF16)

    rowi = lax.broadcasted_iota(jnp.int32, (bs, bs), 0)
    coli = lax.broadcasted_iota(jnp.int32, (bs, bs), 1)

    for c in range(nb):
        @pl.when(i == c)
        def _(c=c):
            n = (c + 1) * bs
            for hh in range(MOBA_HEADS_PER_STEP):
                cs = slice(hh * dh, (hh + 1) * dh)
                s = lax.dot_general(qaug_ref[hh, c * bs:n, :], kaug_ref[hh, 0:n, :], NT_DIMS,
                                    preferred_element_type=F32)
                s_own = jnp.where(rowi >= coli, s[:, c * bs:], -jnp.inf)
                m = jnp.max(s_own, axis=1, keepdims=True)
                if c > 0:
                    s_past = s[:, :c * bs]
                    m = jnp.maximum(m, jnp.max(s_past, axis=1, keepdims=True))
                p_own = jnp.exp2(s_own - m)
                l = jnp.sum(p_own, axis=1, keepdims=True)
                acc = jnp.dot(p_own.astype(BF16), v_ref[0, c * bs:n, cs],
                              preferred_element_type=F32)
                if c > 0:
                    p_past = jnp.exp2(s_past - m)
                    l = l + jnp.sum(p_past, axis=1, keepdims=True)
                    acc = acc + jnp.dot(p_past.astype(BF16), v_ref[0, 0:c * bs, cs],
                                        preferred_element_type=F32)
                o_ref[0, :, cs] = (acc / l).astype(o_ref.dtype)


def _moba(qkv, slope_pieces, n_heads):
    b, t, _ = qkv.shape
    bs = MOBA_BLOCK
    nb = t // bs
    hps = MOBA_HEADS_PER_STEP
    assert n_heads % hps == 0
    ng = n_heads // hps
    w = hps * HEAD_DIM
    kern = functools.partial(_moba_kernel, nb=nb)
    return pl.pallas_call(
        kern,
        grid=(b, ng, nb),
        in_specs=[
            pl.BlockSpec(memory_space=pltpu.SMEM),
            pl.BlockSpec((1, t, w), lambda bi, hi, qi: (bi, 0, hi)),
            pl.BlockSpec((1, t, w), lambda bi, hi, qi: (bi, 0, ng + hi)),
            pl.BlockSpec((1, t, w), lambda bi, hi, qi: (bi, 0, 2 * ng + hi)),
            pl.BlockSpec((t, LANES), lambda bi, hi, qi: (0, 0)),
        ],
        out_specs=pl.BlockSpec((1, bs, w), lambda bi, hi, qi: (bi, qi, hi)),
        out_shape=jax.ShapeDtypeStruct((b, t, n_heads * HEAD_DIM), BF16),
        scratch_shapes=[
            pltpu.VMEM((hps, t, HEAD_DIM + LANES), BF16),
            pltpu.VMEM((hps, t, HEAD_DIM + LANES), BF16),
        ],
        compiler_params=_cparams("parallel", "parallel", "arbitrary"),
        name="moba",
    )(slope_pieces, qkv, qkv, qkv, _moba_key_extras(t))


def _sigmoid(x):
    return 1.0 / (1.0 + jnp.exp(-x))


def _hgrn_chunk(q, f, iv, g, gn, st, bounded_decay):
    c = HGRN_CHUNK
    sub = HGRN_SUB
    half = sub // 2
    nsub = c // sub
    qf = q * _sigmoid(q)
    kf = jnp.maximum(1.0 - f, 0.0)
    b2 = jnp.log2(f)
    rowc = lax.broadcasted_iota(jnp.int32, b2.shape, 0)
    shift = 1
    while shift < c:
        b2 = b2 + jnp.where(rowc >= shift, pltpu.roll(b2, shift, 0), 0.0)
        shift *= 2
    c2 = b2 - jnp.log2(kf)

    inter = lax.dot_general((qf * jnp.exp2(b2)).astype(BF16), st.astype(BF16), NT_DIMS,
                            preferred_element_type=F32)

    lane = lax.broadcasted_iota(jnp.int32, (sub, LANES), 1)
    tsub = lax.broadcasted_iota(jnp.int32, (sub, LANES), 0)
    colid = lax.broadcasted_iota(jnp.int32, (sub, c), 1)
    ones = jnp.ones((LANES, LANES), BF16)
    a_rows = []
    if bounded_decay:
        b_mid = b2[c // 2 - 1:c // 2, :]
        qt = qf * jnp.exp2(b2 - b_mid)
        kt = jnp.exp2(b_mid - c2)
        pair = lax.dot_general(qt.astype(BF16), kt.astype(BF16), NT_DIMS,
                               preferred_element_type=F32)
        rr = lax.broadcasted_iota(jnp.int32, (c, c), 0)
        rc = lax.broadcasted_iota(jnp.int32, (c, c), 1)
        a_rows.append(jnp.where(rr >= rc, pair, 0.0))
    for bi in range(0 if bounded_decay else nsub):
        lo = bi * sub
        b_i = b2[lo:lo + sub]
        c_i = c2[lo:lo + sub]
        q_i = qf[lo:lo + sub]
        pieces = []
        for s in range(sub):
            if s < half:
                pieces.append(q_i * jnp.exp2(b_i - c_i[s:s + 1, :]))
            else:
                pieces.append(q_i[half:] * jnp.exp2(b_i[half:] - c_i[s:s + 1, :]))
        pm = jnp.concatenate(pieces, axis=0).astype(BF16)
        rs = jnp.dot(pm, ones, preferred_element_type=F32)
        rel = lane - lo
        key = jnp.where((rel >= 0) & (rel <= tsub), rel, -1)
        key_lo, key_hi = key[:half], key[half:]
        a_lo = jnp.zeros((half, LANES), F32)
        a_hi = jnp.zeros((half, LANES), F32)
        off = 0
        for s in range(sub):
            if s < half:
                a_lo = jnp.where(key_lo == s, rs[off:off + half], a_lo)
                a_hi = jnp.where(key_hi == s, rs[off + half:off + sub], a_hi)
                off += sub
            else:
                a_hi = jnp.where(key_hi == s, rs[off:off + half], a_hi)
                off += half
        a_blk = jnp.concatenate([a_lo, a_hi], axis=0)[:, :c]
        if bi > 0:
            b0 = b2[lo - 1:lo, :]
            qt = q_i * jnp.exp2(b_i - b0)
            kt = jnp.exp2(jnp.minimum(b0 - c2, 0.0))
            cross = lax.dot_general(qt.astype(BF16), kt.astype(BF16), NT_DIMS,
                                    preferred_element_type=F32)
            a_blk = a_blk + jnp.where(colid < lo, cross, 0.0)
        a_rows.append(a_blk)
    a = jnp.concatenate(a_rows, axis=0)
    intra = jnp.dot(a.astype(BF16), iv.astype(BF16), preferred_element_type=F32)

    b_last = b2[c - 1:c, :]
    khat = jnp.exp2(b_last - c2)
    st_new = st * jnp.exp2(b_last) + lax.dot_general(
        iv.astype(BF16), khat.astype(BF16), TN_DIMS, preferred_element_type=F32)

    o = inter + intra
    y = o * lax.rsqrt(jnp.mean(o * o, axis=-1, keepdims=True) + RMS_EPS) * gn
    return y * (g * _sigmoid(g)), st_new


def _hgrn_kernel(q_ref, f_ref, i_ref, g_ref, lb_ref, gn_ref, o_ref, st_ref):
    @pl.when(pl.program_id(2) == 0)
    def _():
        st_ref[...] = jnp.zeros_like(st_ref)

    gn = gn_ref[...]
    d = HEAD_DIM
    lb = lb_ref[...]
    f_all = lb + (1.0 - lb) * _sigmoid(f_ref[0].astype(F32))
    f_min = jnp.min(f_all)

    def run(bounded_decay):
        sts = [st_ref[hh] for hh in range(HGRN_HEADS_PER_STEP)]
        for ci in range(HGRN_STEP // HGRN_CHUNK):
            sl = slice(ci * HGRN_CHUNK, (ci + 1) * HGRN_CHUNK)
            for hh in range(HGRN_HEADS_PER_STEP):
                cs = slice(hh * d, (hh + 1) * d)
                out, sts[hh] = _hgrn_chunk(q_ref[0, sl, cs].astype(F32), f_all[sl, cs],
                                           i_ref[0, sl, cs], g_ref[0, sl, cs].astype(F32), gn,
                                           sts[hh], bounded_decay)
                o_ref[0, sl, cs] = out.astype(o_ref.dtype)
        for hh in range(HGRN_HEADS_PER_STEP):
            st_ref[hh] = sts[hh]

    @pl.when(f_min >= HGRN_MIN_SAFE_F)
    def _():
        run(True)

    @pl.when(jnp.logical_not(f_min >= HGRN_MIN_SAFE_F))
    def _():
        run(False)


def _hgrn(hp, col0, lb, gn, n_heads):
    b, t, _ = hp.shape
    d = HEAD_DIM
    ts = HGRN_STEP
    hps = HGRN_HEADS_PER_STEP
    assert n_heads % hps == 0 and col0 % (hps * d) == 0
    ng = n_heads // hps
    blk0 = col0 // (hps * d)

    def col(group):
        return pl.BlockSpec((1, ts, hps * d), lambda bi, hi, ti: (bi, ti, blk0 + group * ng + hi))

    return pl.pallas_call(
        _hgrn_kernel,
        grid=(b, ng, t // ts),
        in_specs=[col(0), col(1), col(2), col(3),
                  pl.BlockSpec((1, hps * d), lambda bi, hi, ti: (0, hi)),
                  pl.BlockSpec((1, d), lambda bi, hi, ti: (0, 0))],
        out_specs=pl.BlockSpec((1, ts, hps * d), lambda bi, hi, ti: (bi, ti, hi)),
        out_shape=jax.ShapeDtypeStruct((b, t, n_heads * d), BF16),
        scratch_shapes=[pltpu.VMEM((hps, d, d), F32)],
        compiler_params=_cparams("parallel", "parallel", "arbitrary"),
        name="hgrn2",
    )(hp, hp, hp, hp, lb, gn)


def _outproj_kernel(oa_ref, or_ref, x_ref, w_ref, g_ref, wr_ref, br_ref,
                    x1_ref, h2_ref, rt_ref, cnt_out_ref, cnt_ref):
    wa = oa_ref.shape[1]
    x1 = (x_ref[...]
          + jnp.dot(oa_ref[...], w_ref[0:wa, :], preferred_element_type=F32)
          + jnp.dot(or_ref[...], w_ref[wa:, :], preferred_element_type=F32))
    x1_ref[...] = x1
    h2 = x1 * lax.rsqrt(jnp.mean(x1 * x1, axis=-1, keepdims=True) + RMS_EPS) * g_ref[...]
    h2_ref[...] = h2

    h_hi = h2.astype(BF16)
    h_mid = (h2 - h_hi.astype(F32)).astype(BF16)
    part = jnp.dot(h_hi, wr_ref[...], preferred_element_type=F32)
    logits = (part[:, :LANES] + part[:, LANES:] + br_ref[...]
              + jnp.dot(h_mid, wr_ref[:, :LANES], preferred_element_type=F32))
    lane = lax.broadcasted_iota(jnp.int32, logits.shape, 1)
    big = jnp.int32(4 * LANES)
    ninf = -jnp.inf

    lg = jnp.where(lane < N_GROUPS, logits, ninf)
    mg = jnp.max(lg, axis=1, keepdims=True)
    gidx = jnp.min(jnp.where(lg == mg, lane, big), axis=1, keepdims=True)
    grp_w = 1.0 / jnp.sum(jnp.exp(lg - mg), axis=1, keepdims=True)

    lo = N_GROUPS + EXPERTS_PER_GROUP * gidx
    le = jnp.where((lane >= lo) & (lane < lo + EXPERTS_PER_GROUP), logits, ninf)
    m1 = jnp.max(le, axis=1, keepdims=True)
    i1 = jnp.min(jnp.where(le == m1, lane, big), axis=1, keepdims=True)
    le2 = jnp.where(lane == i1, ninf, le)
    m2 = jnp.max(le2, axis=1, keepdims=True)
    i2 = jnp.min(jnp.where(le2 == m2, lane, big), axis=1, keepdims=True)
    r21 = jnp.exp(m2 - m1)
    w1 = grp_w / (1.0 + r21)
    w2 = grp_w * r21 / (1.0 + r21)
    e1 = i1 - N_GROUPS
    e2 = i2 - N_GROUPS

    @pl.when(pl.program_id(0) == 0)
    def _():
        cnt_ref[...] = jnp.zeros_like(cnt_ref)

    tm = logits.shape[0]
    onehot = jnp.where((lane == e1) | (lane == e2), 1.0, 0.0)
    rr = lax.broadcasted_iota(jnp.int32, (tm, tm), 0)
    rc = lax.broadcasted_iota(jnp.int32, (tm, tm), 1)
    before = jnp.where(rr > rc, 1.0, 0.0).astype(BF16)
    prefix = (jnp.dot(before, onehot.astype(BF16), preferred_element_type=F32) + cnt_ref[0:1, :])
    rank1 = jnp.sum(jnp.where(lane == e1, prefix, 0.0), axis=1, keepdims=True)
    rank2 = jnp.sum(jnp.where(lane == e2, prefix, 0.0), axis=1, keepdims=True)
    total = cnt_ref[0:1, :] + jnp.sum(onehot, axis=0, keepdims=True)
    cnt_ref[0:1, :] = total
    cnt_out_ref[...] = jnp.broadcast_to(total, cnt_out_ref.shape)

    cols = [e1.astype(F32), e2.astype(F32), w1, w2, rank1, rank2]
    rt = jnp.zeros(logits.shape, F32)
    for ci, cv in enumerate(cols):
        rt = jnp.where(lane == ci, cv, rt)
    rt_ref[...] = rt


def _outproj(oa, orec, x2, w_out, g, wr, br, tm=256):
    n, d = x2.shape
    wa = oa.shape[1]
    wrc = orec.shape[1]
    row = lambda i: (i, 0)
    const = lambda i: (0, 0)
    return pl.pallas_call(
        _outproj_kernel,
        grid=(n // tm,),
        in_specs=[
            pl.BlockSpec((tm, wa), row),
            pl.BlockSpec((tm, wrc), row),
            pl.BlockSpec((tm, d), row),
            pl.BlockSpec((wa + wrc, d), const),
            pl.BlockSpec((1, d), const),
            pl.BlockSpec((d, 2 * LANES), const),
            pl.BlockSpec((1, LANES), const),
        ],
        out_specs=[pl.BlockSpec((tm, d), row), pl.BlockSpec((tm, d), row),
                   pl.BlockSpec((tm, LANES), row), pl.BlockSpec((8, LANES), const)],
        out_shape=[jax.ShapeDtypeStruct((n, d), F32), jax.ShapeDtypeStruct((n, d), F32),
                   jax.ShapeDtypeStruct((n, LANES), F32), jax.ShapeDtypeStruct((8, LANES), F32)],
        scratch_shapes=[pltpu.VMEM((8, LANES), F32)],
        compiler_params=_cparams("arbitrary"),
        name="outproj_route",
    )(oa, orec, x2, w_out, g, wr, br)


WEIGHT_CAST_ROWS = 256


def _cast_rows(dst_ref, src_ref):
    rows = dst_ref.shape[0]
    step = min(rows, WEIGHT_CAST_ROWS)
    assert rows % step == 0

    def body(c, carry):
        sl = pl.ds(pl.multiple_of(c * step, step), step)
        dst_ref[sl, :] = src_ref[sl, :].astype(BF16)
        return carry

    lax.fori_loop(0, rows // step, body, 0)


def _moe_kernel(te_ref, nu_ref, first_ref, nxt_ref, src_ref, h2_hbm, wg_hbm, wu_hbm, wd_hbm, y_ref,
                xbuf0, xbuf1, sems, wsems, wg32, wu32, wd32, wgb, wub, wdb):
    s = pl.program_id(0)
    nu = nu_ref[0]
    tm = MOE_TILE
    xbufs = (xbuf0, xbuf1)
    prev_tile = jnp.maximum(s - 1, 0)

    def weight_copies(e):
        return (pltpu.make_async_copy(wg_hbm.at[e], wg32, wsems.at[0]),
                pltpu.make_async_copy(wu_hbm.at[e], wu32, wsems.at[1]),
                pltpu.make_async_copy(wd_hbm.at[e], wd32, wsems.at[2]))

    def start_gather(tile, slot):
        for r in range(tm):
            tok = src_ref[tile * tm + r]
            pltpu.make_async_copy(h2_hbm.at[pl.ds(tok, 1), :], xbufs[slot].at[pl.ds(r, 1), :],
                                  sems.at[slot]).start()

    def wait_gather(slot):
        pltpu.make_async_copy(h2_hbm.at[pl.ds(0, tm), :], xbufs[slot], sems.at[slot]).wait()

    def ffn(slot):
        x = xbufs[slot][...].astype(BF16)
        a = jnp.dot(x, wgb[...], preferred_element_type=F32)
        u = jnp.dot(x, wub[...], preferred_element_type=F32)
        hid = (a * _sigmoid(a) * u).astype(BF16)
        y_ref[...] = jnp.dot(hid, wdb[...], preferred_element_type=F32)

    @pl.when(s == 0)
    def _():
        for cp in weight_copies(te_ref[0]):
            cp.start()
        start_gather(s, 0)

    @pl.when((s >= 1) & (s <= nu) & (first_ref[prev_tile] == 1))
    def _():
        for cp in weight_copies(te_ref[prev_tile]):
            cp.wait()
        _cast_rows(wgb, wg32)
        _cast_rows(wub, wu32)
        _cast_rows(wdb, wd32)

        @pl.when(nxt_ref[prev_tile] >= 0)
        def _():
            for cp in weight_copies(nxt_ref[prev_tile]):
                cp.start(priority=1)

    for parity in range(2):
        @pl.when((s >= 1) & (s < nu) & (s % 2 == parity))
        def _(parity=parity):
            wait_gather(1 - parity)
            start_gather(s, parity)
            ffn(1 - parity)

        @pl.when((s >= 1) & (s == nu) & (s % 2 == parity))
        def _(parity=parity):
            wait_gather(1 - parity)
            ffn(1 - parity)

    @pl.when(s > nu)
    def _():
        y_ref[...] = jnp.zeros_like(y_ref)


def _moe(tile_expert, n_used, first, nxt, src, h2, wg, wu, wd):
    p = src.shape[0]
    d, f = wg.shape[1:]
    tm = MOE_TILE
    n_tiles = p // tm
    any_spec = pl.BlockSpec(memory_space=pl.ANY)
    return pl.pallas_call(
        _moe_kernel,
        grid_spec=pltpu.PrefetchScalarGridSpec(
            num_scalar_prefetch=5,
            grid=(n_tiles + 1,),
            in_specs=[any_spec, any_spec, any_spec, any_spec],
            out_specs=pl.BlockSpec((tm, d), lambda s, *_: (jnp.maximum(s - 1, 0), 0)),
            scratch_shapes=[pltpu.VMEM((tm, d), F32), pltpu.VMEM((tm, d), F32),
                            pltpu.SemaphoreType.DMA((2,)), pltpu.SemaphoreType.DMA((3,)),
                            pltpu.VMEM((d, f), F32), pltpu.VMEM((d, f), F32), pltpu.VMEM((f, d), F32),
                            pltpu.VMEM((d, f), BF16), pltpu.VMEM((d, f), BF16),
                            pltpu.VMEM((f, d), BF16)],
        ),
        out_shape=jax.ShapeDtypeStruct((p, d), F32),
        compiler_params=_cparams("arbitrary"),
        name="moe_ffn",
    )(tile_expert, n_used, first, nxt, src, h2, wg, wu, wd)


COMBINE_TILE = 256


def _final_kernel(pos_ref, x1_ref, ys_hbm, rt_ref, g_ref, o_ref, ya0, yb0, ya1, yb1, sems):
    s = pl.program_id(0)
    n_tiles = pl.num_programs(0) - 1
    tm = COMBINE_TILE
    n_tok = pos_ref.shape[0] // 2
    ybufs = ((ya0, yb0), (ya1, yb1))

    def start_gather(tile, slot):
        for k in range(2):
            for r in range(tm):
                row = pos_ref[k * n_tok + tile * tm + r]
                pltpu.make_async_copy(ys_hbm.at[pl.ds(row, 1), :],
                                      ybufs[slot][k].at[pl.ds(r, 1), :], sems.at[slot]).start()

    def wait_gather(slot):
        for k in range(2):
            pltpu.make_async_copy(ys_hbm.at[pl.ds(0, tm), :], ybufs[slot][k], sems.at[slot]).wait()

    def finish(slot):
        rt = rt_ref[...]
        x2 = (x1_ref[...] + rt[:, 2:3] * ybufs[slot][0][...] + rt[:, 3:4] * ybufs[slot][1][...])
        o_ref[...] = (x2 * lax.rsqrt(jnp.mean(x2 * x2, axis=-1, keepdims=True) + RMS_EPS)
                      * g_ref[...])

    @pl.when(s == 0)
    def _():
        start_gather(s, 0)

    for parity in range(2):
        @pl.when((s >= 1) & (s < n_tiles) & (s % 2 == parity))
        def _(parity=parity):
            wait_gather(1 - parity)
            start_gather(s, parity)
            finish(1 - parity)

        @pl.when((s == n_tiles) & (s % 2 == parity))
        def _(parity=parity):
            wait_gather(1 - parity)
            finish(1 - parity)


def _final(x1, ys, pos, rt, g):
    n, d = x1.shape
    tm = COMBINE_TILE
    prev = lambda s: jnp.maximum(s - 1, 0)
    return pl.pallas_call(
        _final_kernel,
        grid_spec=pltpu.PrefetchScalarGridSpec(
            num_scalar_prefetch=1,
            grid=(n // tm + 1,),
            in_specs=[pl.BlockSpec((tm, d), lambda s, p: (prev(s), 0)),
                      pl.BlockSpec(memory_space=pl.ANY),
                      pl.BlockSpec((tm, LANES), lambda s, p: (prev(s), 0)),
                      pl.BlockSpec((1, d), lambda s, p: (0, 0))],
            out_specs=pl.BlockSpec((tm, d), lambda s, p: (prev(s), 0)),
            scratch_shapes=[pltpu.VMEM((tm, d), F32)] * 4 + [pltpu.SemaphoreType.DMA((2,))],
        ),
        out_shape=jax.ShapeDtypeStruct((n, d), F32),
        compiler_params=_cparams("arbitrary"),
        name="combine_norm",
    )(pos, x1, ys, rt, g)


def _dispatch_plan(rt, counts, n_tokens):
    tm = MOE_TILE
    n_rows = 2 * n_tokens + N_EXPERTS * tm
    n_tiles = n_rows // tm
    ids = rt[:, 0:6].astype(jnp.int32)
    ef = jnp.concatenate([ids[:, 0], ids[:, 1]])
    rank = jnp.concatenate([ids[:, 4], ids[:, 5]])
    counts = counts[0, :N_EXPERTS].astype(jnp.int32)
    tiles_per = (counts + tm - 1) // tm
    tile_end = jnp.cumsum(tiles_per)
    row_start = (tile_end - tiles_per) * tm
    pos = row_start[ef] + rank
    tok = jnp.concatenate([jnp.arange(n_tokens, dtype=jnp.int32)] * 2)
    src = jnp.zeros((n_rows,), jnp.int32).at[pos].set(tok)
    n_used = tile_end[-1]
    tile_ids = jnp.arange(n_tiles, dtype=jnp.int32)
    tile_expert = jnp.sum((tile_ids[:, None] >= tile_end[None, :]).astype(jnp.int32), axis=1)
    last_expert = jnp.sum((n_used - 1 >= tile_end).astype(jnp.int32))
    tile_expert = jnp.where(tile_ids < n_used, tile_expert, last_expert).astype(jnp.int32)
    first = jnp.concatenate([jnp.ones((1,), jnp.int32),
                             (tile_expert[1:] != tile_expert[:-1]).astype(jnp.int32)])
    eid = jnp.arange(N_EXPERTS, dtype=jnp.int32)
    later = (tiles_per[None, :] > 0) & (eid[None, :] > eid[:, None])
    next_expert = jnp.min(jnp.where(later, eid[None, :], N_EXPERTS), axis=1)
    next_expert = jnp.where(next_expert == N_EXPERTS, -1, next_expert).astype(jnp.int32)
    nxt = next_expert[tile_expert]
    return (src, pos[:n_tokens], pos[n_tokens:], tile_expert, first, nxt,
            n_used.reshape(1).astype(jnp.int32))


def kernel(x, norm_mix_g, w_in, hgrn_lb_logits, hgrn_out_norm_g, w_out, norm_ffn_g, w_group_router,
           b_group_router, w_expert_router, b_expert_router, w_gate, w_up, w_down, final_norm_g):
    b, t, d = x.shape
    n = b * t
    depth = w_in.shape[0]
    assert depth == 1, "the final norm is fused into the combine step of the only layer"
    attn_w = d // 2
    n_heads = attn_w // HEAD_DIM
    x2 = x.reshape(n, d)
    lb_all = jnp.cumsum(jax.nn.softmax(hgrn_lb_logits.astype(F32), axis=0), axis=0)[:depth]
    slope2 = jnp.asarray(2.0 ** (-8.0 * np.arange(1, n_heads + 1) / n_heads), dtype=F32) * F32(LOG2E)
    s_hi = slope2.astype(BF16).astype(F32)
    s_mid = (slope2 - s_hi).astype(BF16).astype(F32)
    s_lo = (slope2 - s_hi - s_mid).astype(BF16).astype(F32)
    slope_pieces = jnp.stack([s_hi, s_mid, s_lo])
    proj_scale = jnp.concatenate([jnp.full((attn_w,), HEAD_DIM ** -0.5 * LOG2E, F32),
                                  jnp.ones((w_in.shape[2] - attn_w,), F32)]).reshape(1, -1)

    for l in range(depth):
        g_mix = norm_mix_g[l].reshape(1, d)
        x2, w_in_l, w_out_l = lax.optimization_barrier(
            (x2, w_in[l].astype(BF16), w_out[l].astype(BF16)))
        proj = _inproj(x2, g_mix, w_in_l, 0, proj_scale, BF16).reshape(b, t, -1)
        o_attn = _moba(proj, slope_pieces, n_heads)
        o_rec = _hgrn(proj, 3 * attn_w, lb_all[l].reshape(1, -1),
                      hgrn_out_norm_g[l].reshape(1, HEAD_DIM), n_heads)

        wr = jnp.concatenate(
            [w_group_router[l],
             jnp.transpose(w_expert_router[l], (1, 0, 2)).reshape(d, N_EXPERTS)], axis=1)
        wr = jnp.pad(wr, ((0, 0), (0, LANES - wr.shape[1])))
        wr_hi = wr.astype(BF16)
        wr = jnp.concatenate([wr_hi, (wr - wr_hi.astype(F32)).astype(BF16)], axis=1)
        br =jnp.concatenate([b_group_router[l], b_expert_router[l].reshape(-1)])
        br = jnp.pad(br, (0, LANES - br.shape[0])).reshape(1, LANES)
        x1, h2, rt, counts = _outproj(o_attn.reshape(n, attn_w), o_rec.reshape(n, -1), x2,
                                      w_out_l, norm_ffn_g[l].reshape(1, d), wr, br)
        src, pos1, pos2, tile_expert, first, nxt, n_used = _dispatch_plan(rt, counts, n)
        ys = _moe(tile_expert, n_used, first, nxt, src, h2, w_gate[l], w_up[l], w_down[l])
        x2 = _final(x1, ys, jnp.concatenate([pos1, pos2]), rt, final_norm_g.reshape(1, d))
    return x2.reshape(b, t, d)
```

```python
import functools

import jax
import jax.numpy as jnp
import numpy as np
from jax import lax
from jax.experimental import pallas as pl
from jax.experimental.pallas import tpu as pltpu

F32 = jnp.float32
BF16 = jnp.bfloat16
HIGHEST = lax.Precision.HIGHEST

HEAD_DIM = 128
MOBA_BLOCK = 256
MOBA_TOPK = 3
N_GROUPS = 4
EXPERTS_PER_GROUP = 4
N_EXPERTS = N_GROUPS * EXPERTS_PER_GROUP
RMS_EPS = 1e-6

LANES = 128
VMEM_LIMIT_BYTES = 56 * 1024 * 1024

HGRN_CHUNK = 64
HGRN_SUB = 16
HGRN_STEP = 256
HGRN_HEADS_PER_STEP = 8
HGRN_MIN_SAFE_F = 2.0 ** -3.5
MOE_TILE = 256
NT_DIMS = (((1,), (1,)), ((), ()))
TN_DIMS = (((0,), (0,)), ((), ()))


def _cparams(*sem):
    return pltpu.CompilerParams(dimension_semantics=sem, vmem_limit_bytes=VMEM_LIMIT_BYTES)


def _inproj_kernel(x_ref, g_ref, w_ref, cs_ref, o_ref, hn_ref):
    @pl.when(pl.program_id(1) == 0)
    def _():
        x = x_ref[...]
        ms = jnp.mean(x * x, axis=-1, keepdims=True)
        hn_ref[...] = (x * lax.rsqrt(ms + RMS_EPS) * g_ref[...]).astype(BF16)

    acc = jnp.dot(hn_ref[...], w_ref[...], preferred_element_type=F32)
    o_ref[...] = (acc * cs_ref[...]).astype(o_ref.dtype)


def _inproj(x2, g, w, col0, colscale, out_dtype, tm=1024, tn=1792):
    n, d = x2.shape
    cols = colscale.shape[1]
    j0 = col0 // tn
    return pl.pallas_call(
        _inproj_kernel,
        grid=(n // tm, cols // tn),
        in_specs=[
            pl.BlockSpec((tm, d), lambda i, j: (i, 0)),
            pl.BlockSpec((1, d), lambda i, j: (0, 0)),
            pl.BlockSpec((d, tn), lambda i, j: (0, j0 + j)),
            pl.BlockSpec((1, tn), lambda i, j: (0, j)),
        ],
        out_specs=pl.BlockSpec((tm, tn), lambda i, j: (i, j)),
        out_shape=jax.ShapeDtypeStruct((n, cols), out_dtype),
        scratch_shapes=[pltpu.VMEM((tm, d), BF16)],
        compiler_params=_cparams("parallel", "arbitrary"),
        name="inproj",
    )(x2, g, w, colscale)


MOBA_HEADS_PER_STEP = 4
MOBA_EXTRA_POS = 0
MOBA_EXTRA_SEL = 8
MASK_BIG = 2.0 ** 60
LOG2E = 1.4426950408889634


def _moba_key_extras(t):
    nb = t // MOBA_BLOCK
    assert MOBA_EXTRA_SEL + nb <= LANES and MOBA_BLOCK <= 256
    pos = np.arange(t)
    kx = np.zeros((t, LANES), np.float32)
    kx[:, MOBA_EXTRA_POS:MOBA_EXTRA_POS + 3] = ((pos // MOBA_BLOCK) * MOBA_BLOCK)[:, None]
    kx[:, MOBA_EXTRA_POS + 3:MOBA_EXTRA_POS + 6] = (pos % MOBA_BLOCK)[:, None]
    kx[pos, MOBA_EXTRA_SEL + pos // MOBA_BLOCK] = -MASK_BIG
    return jnp.asarray(kx, dtype=BF16)


def _moba_kernel(sl_ref, q_ref, k_ref, v_ref, kx_ref, o_ref, kaug_ref, qaug_ref, *, nb):
    hg = pl.program_id(1)
    i = pl.program_id(2)
    bs = MOBA_BLOCK
    dh = HEAD_DIM
    t = q_ref.shape[1]
    nbp = -(-nb // 8) * 8
    nx = 8 + nbp

    @pl.when(i == 0)
    def _():
        r8 = lax.broadcasted_iota(jnp.int32, (8, t), 0)
        piece = r8 % 3
        er = lax.broadcasted_iota(jnp.int32, (nx, LANES), 0)
        ec = lax.broadcasted_iota(jnp.int32, (nx, LANES), 1)
        embed = jnp.where(er == ec, 1.0, 0.0).astype(BF16)
        blk = lax.broadcasted_iota(jnp.int32, (nbp, t), 0)
        qblk = lax.broadcasted_iota(jnp.int32, (nbp, t), 1) // bs
        for hh in range(MOBA_HEADS_PER_STEP):
            h = hg * MOBA_HEADS_PER_STEP + hh
            cs = slice(hh * dh, (hh + 1) * dh)
            kaug_ref[hh, :, 0:dh] = k_ref[0, :, cs]
            kaug_ref[hh, :, dh:] = kx_ref[...]
            rows = [jnp.mean(k_ref[0, j * bs:(j + 1) * bs, cs].astype(F32), axis=0, keepdims=True)
                    for j in range(nb)]
            if nbp > nb:
                rows.append(jnp.zeros((nbp - nb, dh), F32))
            km = jnp.concatenate(rows, axis=0)
            hi = km.astype(BF16)
            mid = (km - hi.astype(F32)).astype(BF16)
            lo = (km - hi.astype(F32) - mid.astype(F32)).astype(BF16)
            km4 = jnp.concatenate([hi, mid, lo, jnp.zeros_like(hi)], axis=0)
            q = q_ref[0, :, cs]
            g4 = lax.dot_general(km4, q, NT_DIMS, preferred_element_type=F32)
            gate_t = g4[0:nbp] + g4[nbp:2 * nbp] + g4[2 * nbp:3 * nbp]
            rank = jnp.zeros(gate_t.shape, F32)
            for jp in range(nb - 1):
                other = gate_t[jp:jp + 1, :]
                beats = (other > gate_t) | ((other == gate_t) & (blk > jp))
                rank = rank + jnp.where(beats & (qblk > jp), 1.0, 0.0)
            notsel_t = jnp.where((blk < qblk) & (rank >= MOBA_TOPK), 1.0, 0.0)
            slope_rows = jnp.where(r8 >= 6, 0.0,
                                   jnp.where(piece == 0, sl_ref[0, h],
                                             jnp.where(piece == 1, sl_ref[1, h], sl_ref[2, h])))
            qx_t = jnp.concatenate([slope_rows, notsel_t], axis=0).astype(BF16)
            qx = lax.dot_general(qx_t, embed, TN_DIMS, preferred_element_type=F32)
            qaug_ref[hh, :, 0:dh] = q
            qaug_ref[hh, :, dh:] = qx.astype(BF16)

    rowi = lax.broadcasted_iota(jnp.int32, (bs, bs), 0)
    coli = lax.broadcasted_iota(jnp.int32, (bs, bs), 1)

    for c in range(nb):
        @pl.when(i == c)
        def _(c=c):
            n = (c + 1) * bs
            for hh in range(MOBA_HEADS_PER_STEP):
                cs = slice(hh * dh, (hh + 1) * dh)
                s = lax.dot_general(qaug_ref[hh, c * bs:n, :], kaug_ref[hh, 0:n, :], NT_DIMS,
                                    preferred_element_type=F32)
                s_own = jnp.where(rowi >= coli, s[:, c * bs:], -jnp.inf)
                m = jnp.max(s_own, axis=1, keepdims=True)
                if c > 0:
                    s_past = s[:, :c * bs]
                    m = jnp.maximum(m, jnp.max(s_past, axis=1, keepdims=True))
                p_own = jnp.exp2(s_own - m)
                l = jnp.sum(p_own, axis=1, keepdims=True)
                acc = jnp.dot(p_own.astype(BF16), v_ref[0, c * bs:n, cs],
                              preferred_element_type=F32)
                if c > 0:
                    p_past = jnp.exp2(s_past - m)
                    l = l + jnp.sum(p_past, axis=1, keepdims=True)
                    acc = acc + jnp.dot(p_past.astype(BF16), v_ref[0, 0:c * bs, cs],
                                        preferred_element_type=F32)
                o_ref[0, :, cs] = (acc / l).astype(o_ref.dtype)


def _moba(qkv, slope_pieces, n_heads):
    b, t, _ = qkv.shape
    bs = MOBA_BLOCK
    nb = t // bs
    hps = MOBA_HEADS_PER_STEP
    assert n_heads % hps == 0
    ng = n_heads // hps
    w = hps * HEAD_DIM
    kern = functools.partial(_moba_kernel, nb=nb)
    return pl.pallas_call(
        kern,
        grid=(b, ng, nb),
        in_specs=[
            pl.BlockSpec(memory_space=pltpu.SMEM),
            pl.BlockSpec((1, t, w), lambda bi, hi, qi: (bi, 0, hi)),
            pl.BlockSpec((1, t, w), lambda bi, hi, qi: (bi, 0, ng + hi)),
            pl.BlockSpec((1, t, w), lambda bi, hi, qi: (bi, 0, 2 * ng + hi)),
            pl.BlockSpec((t, LANES), lambda bi, hi, qi: (0, 0)),
        ],
        out_specs=pl.BlockSpec((1, bs, w), lambda bi, hi, qi: (bi, qi, hi)),
        out_shape=jax.ShapeDtypeStruct((b, t, n_heads * HEAD_DIM), BF16),
        scratch_shapes=[
            pltpu.VMEM((hps, t, HEAD_DIM + LANES), BF16),
            pltpu.VMEM((hps, t, HEAD_DIM + LANES), BF16),
        ],
        compiler_params=_cparams("parallel", "parallel", "arbitrary"),
        name="moba",
    )(slope_pieces, qkv, qkv, qkv, _moba_key_extras(t))


def _sigmoid(x):
    return 1.0 / (1.0 + jnp.exp(-x))


def _hgrn_chunk(q, f, iv, g, gn, st, bounded_decay):
    c = HGRN_CHUNK
    sub = HGRN_SUB
    half = sub // 2
    nsub = c // sub
    qf = q * _sigmoid(q)
    kf = jnp.maximum(1.0 - f, 0.0)
    b2 = jnp.log2(f)
    rowc = lax.broadcasted_iota(jnp.int32, b2.shape, 0)
    shift = 1
    while shift < c:
        b2 = b2 + jnp.where(rowc >= shift, pltpu.roll(b2, shift, 0), 0.0)
        shift *= 2
    c2 = b2 - jnp.log2(kf)

    inter = lax.dot_general((qf * jnp.exp2(b2)).astype(BF16), st.astype(BF16), NT_DIMS,
                            preferred_element_type=F32)

    lane = lax.broadcasted_iota(jnp.int32, (sub, LANES), 1)
    tsub = lax.broadcasted_iota(jnp.int32, (sub, LANES), 0)
    colid = lax.broadcasted_iota(jnp.int32, (sub, c), 1)
    ones = jnp.ones((LANES, LANES), BF16)
    a_rows = []
    if bounded_decay:
        b_mid = b2[c // 2 - 1:c // 2, :]
        qt = qf * jnp.exp2(b2 - b_mid)
        kt = jnp.exp2(b_mid - c2)
        pair = lax.dot_general(qt.astype(BF16), kt.astype(BF16), NT_DIMS,
                               preferred_element_type=F32)
        rr = lax.broadcasted_iota(jnp.int32, (c, c), 0)
        rc = lax.broadcasted_iota(jnp.int32, (c, c), 1)
        a_rows.append(jnp.where(rr >= rc, pair, 0.0))
    for bi in range(0 if bounded_decay else nsub):
        lo = bi * sub
        b_i = b2[lo:lo + sub]
        c_i = c2[lo:lo + sub]
        q_i = qf[lo:lo + sub]
        pieces = []
        for s in range(sub):
            if s < half:
                pieces.append(q_i * jnp.exp2(b_i - c_i[s:s + 1, :]))
            else:
                pieces.append(q_i[half:] * jnp.exp2(b_i[half:] - c_i[s:s + 1, :]))
        pm = jnp.concatenate(pieces, axis=0).astype(BF16)
        rs = jnp.dot(pm, ones, preferred_element_type=F32)
        rel = lane - lo
        key = jnp.where((rel >= 0) & (rel <= tsub), rel, -1)
        key_lo, key_hi = key[:half], key[half:]
        a_lo = jnp.zeros((half, LANES), F32)
        a_hi = jnp.zeros((half, LANES), F32)
        off = 0
        for s in range(sub):
            if s < half:
                a_lo = jnp.where(key_lo == s, rs[off:off + half], a_lo)
                a_hi = jnp.where(key_hi == s, rs[off + half:off + sub], a_hi)
                off += sub
            else:
                a_hi = jnp.where(key_hi == s, rs[off:off + half], a_hi)
                off += half
        a_blk = jnp.concatenate([a_lo, a_hi], axis=0)[:, :c]
        if bi > 0:
            b0 = b2[lo - 1:lo, :]
            qt = q_i * jnp.exp2(b_i - b0)
            kt = jnp.exp2(jnp.minimum(b0 - c2, 0.0))
            cross = lax.dot_general(qt.astype(BF16), kt.astype(BF16), NT_DIMS,
                                    preferred_element_type=F32)
            a_blk = a_blk + jnp.where(colid < lo, cross, 0.0)
        a_rows.append(a_blk)
    a = jnp.concatenate(a_rows, axis=0)
    intra = jnp.dot(a.astype(BF16), iv.astype(BF16), preferred_element_type=F32)

    b_last = b2[c - 1:c, :]
    khat = jnp.exp2(b_last - c2)
    st_new = st * jnp.exp2(b_last) + lax.dot_general(
        iv.astype(BF16), khat.astype(BF16), TN_DIMS, preferred_element_type=F32)

    o = inter + intra
    y = o * lax.rsqrt(jnp.mean(o * o, axis=-1, keepdims=True) + RMS_EPS) * gn
    return y * (g * _sigmoid(g)), st_new


def _hgrn_kernel(q_ref, f_ref, i_ref, g_ref, lb_ref, gn_ref, o_ref, st_ref):
    @pl.when(pl.program_id(2) == 0)
    def _():
        st_ref[...] = jnp.zeros_like(st_ref)

    gn = gn_ref[...]
    d = HEAD_DIM
    lb = lb_ref[...]
    f_all = lb + (1.0 - lb) * _sigmoid(f_ref[0].astype(F32))
    f_min = jnp.min(f_all)

    def run(bounded_decay):
        sts = [st_ref[hh] for hh in range(HGRN_HEADS_PER_STEP)]
        for ci in range(HGRN_STEP // HGRN_CHUNK):
            sl = slice(ci * HGRN_CHUNK, (ci + 1) * HGRN_CHUNK)
            for hh in range(HGRN_HEADS_PER_STEP):
                cs = slice(hh * d, (hh + 1) * d)
                out, sts[hh] = _hgrn_chunk(q_ref[0, sl, cs].astype(F32), f_all[sl, cs],
                                           i_ref[0, sl, cs], g_ref[0, sl, cs].astype(F32), gn,
                                           sts[hh], bounded_decay)
                o_ref[0, sl, cs] = out.astype(o_ref.dtype)
        for hh in range(HGRN_HEADS_PER_STEP):
            st_ref[hh] = sts[hh]

    @pl.when(f_min >= HGRN_MIN_SAFE_F)
    def _():
        run(True)

    @pl.when(jnp.logical_not(f_min >= HGRN_MIN_SAFE_F))
    def _():
        run(False)


def _hgrn(hp, col0, lb, gn, n_heads):
    b, t, _ = hp.shape
    d = HEAD_DIM
    ts = HGRN_STEP
    hps = HGRN_HEADS_PER_STEP
    assert n_heads % hps == 0 and col0 % (hps * d) == 0
    ng = n_heads // hps
    blk0 = col0 // (hps * d)

    def col(group):
        return pl.BlockSpec((1, ts, hps * d), lambda bi, hi, ti: (bi, ti, blk0 + group * ng + hi))

    return pl.pallas_call(
        _hgrn_kernel,
        grid=(b, ng, t // ts),
        in_specs=[col(0), col(1), col(2), col(3),
                  pl.BlockSpec((1, hps * d), lambda bi, hi, ti: (0, hi)),
                  pl.BlockSpec((1, d), lambda bi, hi, ti: (0, 0))],
        out_specs=pl.BlockSpec((1, ts, hps * d), lambda bi, hi, ti: (bi, ti, hi)),
        out_shape=jax.ShapeDtypeStruct((b, t, n_heads * d), BF16),
        scratch_shapes=[pltpu.VMEM((hps, d, d), F32)],
        compiler_params=_cparams("parallel", "parallel", "arbitrary"),
        name="hgrn2",
    )(hp, hp, hp, hp, lb, gn)


def _outproj_kernel(oa_ref, or_ref, x_ref, w_ref, g_ref, wr_ref, br_ref,
                    x1_ref, h2_ref, rt_ref, cnt_out_ref, cnt_ref):
    wa = oa_ref.shape[1]
    x1 = (x_ref[...]
          + jnp.dot(oa_ref[...], w_ref[0:wa, :], preferred_element_type=F32)
          + jnp.dot(or_ref[...], w_ref[wa:, :], preferred_element_type=F32))
    x1_ref[...] = x1
    h2 = x1 * lax.rsqrt(jnp.mean(x1 * x1, axis=-1, keepdims=True) + RMS_EPS) * g_ref[...]
    h2_ref[...] = h2

    h_hi = h2.astype(BF16)
    h_mid = (h2 - h_hi.astype(F32)).astype(BF16)
    part = jnp.dot(h_hi, wr_ref[...], preferred_element_type=F32)
    logits = (part[:, :LANES] + part[:, LANES:] + br_ref[...]
              + jnp.dot(h_mid, wr_ref[:, :LANES], preferred_element_type=F32))
    lane = lax.broadcasted_iota(jnp.int32, logits.shape, 1)
    big = jnp.int32(4 * LANES)
    ninf = -jnp.inf

    lg = jnp.where(lane < N_GROUPS, logits, ninf)
    mg = jnp.max(lg, axis=1, keepdims=True)
    gidx = jnp.min(jnp.where(lg == mg, lane, big), axis=1, keepdims=True)
    grp_w = 1.0 / jnp.sum(jnp.exp(lg - mg), axis=1, keepdims=True)

    lo = N_GROUPS + EXPERTS_PER_GROUP * gidx
    le = jnp.where((lane >= lo) & (lane < lo + EXPERTS_PER_GROUP), logits, ninf)
    m1 = jnp.max(le, axis=1, keepdims=True)
    i1 = jnp.min(jnp.where(le == m1, lane, big), axis=1, keepdims=True)
    le2 = jnp.where(lane == i1, ninf, le)
    m2 = jnp.max(le2, axis=1, keepdims=True)
    i2 = jnp.min(jnp.where(le2 == m2, lane, big), axis=1, keepdims=True)
    r21 = jnp.exp(m2 - m1)
    w1 = grp_w / (1.0 + r21)
    w2 = grp_w * r21 / (1.0 + r21)
    e1 = i1 - N_GROUPS
    e2 = i2 - N_GROUPS

    @pl.when(pl.program_id(0) == 0)
    def _():
        cnt_ref[...] = jnp.zeros_like(cnt_ref)

    tm = logits.shape[0]
    onehot = jnp.where((lane == e1) | (lane == e2), 1.0, 0.0)
    rr = lax.broadcasted_iota(jnp.int32, (tm, tm), 0)
    rc = lax.broadcasted_iota(jnp.int32, (tm, tm), 1)
    before = jnp.where(rr > rc, 1.0, 0.0).astype(BF16)
    prefix = (jnp.dot(before, onehot.astype(BF16), preferred_element_type=F32) + cnt_ref[0:1, :])
    rank1 = jnp.sum(jnp.where(lane == e1, prefix, 0.0), axis=1, keepdims=True)
    rank2 = jnp.sum(jnp.where(lane == e2, prefix, 0.0), axis=1, keepdims=True)
    total = cnt_ref[0:1, :] + jnp.sum(onehot, axis=0, keepdims=True)
    cnt_ref[0:1, :] = total
    cnt_out_ref[...] = jnp.broadcast_to(total, cnt_out_ref.shape)

    cols = [e1.astype(F32), e2.astype(F32), w1, w2, rank1, rank2]
    rt = jnp.zeros(logits.shape, F32)
    for ci, cv in enumerate(cols):
        rt = jnp.where(lane == ci, cv, rt)
    rt_ref[...] = rt


def _outproj(oa, orec, x2, w_out, g, wr, br, tm=512):
    n, d = x2.shape
    wa = oa.shape[1]
    wrc = orec.shape[1]
    row = lambda i: (i, 0)
    const = lambda i: (0, 0)
    return pl.pallas_call(
        _outproj_kernel,
        grid=(n // tm,),
        in_specs=[
            pl.BlockSpec((tm, wa), row),
            pl.BlockSpec((tm, wrc), row),
            pl.BlockSpec((tm, d), row),
            pl.BlockSpec((wa + wrc, d), const),
            pl.BlockSpec((1, d), const),
            pl.BlockSpec((d, 2 * LANES), const),
            pl.BlockSpec((1, LANES), const),
        ],
        out_specs=[pl.BlockSpec((tm, d), row), pl.BlockSpec((tm, d), row),
                   pl.BlockSpec((tm, LANES), row), pl.BlockSpec((8, LANES), const)],
        out_shape=[jax.ShapeDtypeStruct((n, d), F32), jax.ShapeDtypeStruct((n, d), F32),
                   jax.ShapeDtypeStruct((n, LANES), F32), jax.ShapeDtypeStruct((8, LANES), F32)],
        scratch_shapes=[pltpu.VMEM((8, LANES), F32)],
        compiler_params=_cparams("arbitrary"),
        name="outproj_route",
    )(oa, orec, x2, w_out, g, wr, br)


WEIGHT_CAST_ROWS = 256


def _cast_rows(dst_ref, src_ref):
    rows = dst_ref.shape[0]
    step = min(rows, WEIGHT_CAST_ROWS)
    assert rows % step == 0

    def body(c, carry):
        sl = pl.ds(pl.multiple_of(c * step, step), step)
        dst_ref[sl, :] = src_ref[sl, :].astype(BF16)
        return carry

    lax.fori_loop(0, rows // step, body, 0)


def _moe_kernel(te_ref, nu_ref, first_ref, nxt_ref, src_ref, h2_hbm, wg_hbm, wu_hbm, wd_hbm, y_ref,
                xbuf0, xbuf1, sems, wsems, wg32, wu32, wd32, wgb, wub, wdb):
    s = pl.program_id(0)
    nu = nu_ref[0]
    tm = MOE_TILE
    xbufs = (xbuf0, xbuf1)
    prev_tile = jnp.maximum(s - 1, 0)

    def weight_copies(e):
        return (pltpu.make_async_copy(wg_hbm.at[e], wg32, wsems.at[0]),
                pltpu.make_async_copy(wu_hbm.at[e], wu32, wsems.at[1]),
                pltpu.make_async_copy(wd_hbm.at[e], wd32, wsems.at[2]))

    def start_gather(tile, slot):
        for r in range(tm):
            tok = src_ref[tile * tm + r]
            pltpu.make_async_copy(h2_hbm.at[pl.ds(tok, 1), :], xbufs[slot].at[pl.ds(r, 1), :],
                                  sems.at[slot]).start()

    def wait_gather(slot):
        pltpu.make_async_copy(h2_hbm.at[pl.ds(0, tm), :], xbufs[slot], sems.at[slot]).wait()

    def ffn(slot):
        x = xbufs[slot][...].astype(BF16)
        a = jnp.dot(x, wgb[...], preferred_element_type=F32)
        u = jnp.dot(x, wub[...], preferred_element_type=F32)
        hid = (a * _sigmoid(a) * u).astype(BF16)
        y_ref[...] = jnp.dot(hid, wdb[...], preferred_element_type=F32)

    @pl.when(s == 0)
    def _():
        for cp in weight_copies(te_ref[0]):
            cp.start()
        start_gather(s, 0)

    @pl.when((s >= 1) & (s <= nu) & (first_ref[prev_tile] == 1))
    def _():
        for cp in weight_copies(te_ref[prev_tile]):
            cp.wait()
        _cast_rows(wgb, wg32)
        _cast_rows(wub, wu32)
        _cast_rows(wdb, wd32)

        @pl.when(nxt_ref[prev_tile] >= 0)
        def _():
            for cp in weight_copies(nxt_ref[prev_tile]):
                cp.start(priority=1)

    for parity in range(2):
        @pl.when((s >= 1) & (s < nu) & (s % 2 == parity))
        def _(parity=parity):
            wait_gather(1 - parity)
            start_gather(s, parity)
            ffn(1 - parity)

        @pl.when((s >= 1) & (s == nu) & (s % 2 == parity))
        def _(parity=parity):
            wait_gather(1 - parity)
            ffn(1 - parity)

    @pl.when(s > nu)
    def _():
        y_ref[...] = jnp.zeros_like(y_ref)


def _moe(tile_expert, n_used, first, nxt, src, h2, wg, wu, wd):
    p = src.shape[0]
    d, f = wg.shape[1:]
    tm = MOE_TILE
    n_tiles = p // tm
    any_spec = pl.BlockSpec(memory_space=pl.ANY)
    return pl.pallas_call(
        _moe_kernel,
        grid_spec=pltpu.PrefetchScalarGridSpec(
            num_scalar_prefetch=5,
            grid=(n_tiles + 1,),
            in_specs=[any_spec, any_spec, any_spec, any_spec],
            out_specs=pl.BlockSpec((tm, d), lambda s, *_: (jnp.maximum(s - 1, 0), 0)),
            scratch_shapes=[pltpu.VMEM((tm, d), F32), pltpu.VMEM((tm, d), F32),
                            pltpu.SemaphoreType.DMA((2,)), pltpu.SemaphoreType.DMA((3,)),
                            pltpu.VMEM((d, f), F32), pltpu.VMEM((d, f), F32), pltpu.VMEM((f, d), F32),
                            pltpu.VMEM((d, f), BF16), pltpu.VMEM((d, f), BF16),
                            pltpu.VMEM((f, d), BF16)],
        ),
        out_shape=jax.ShapeDtypeStruct((p, d), F32),
        compiler_params=_cparams("arbitrary"),
        name="moe_ffn",
    )(tile_expert, n_used, first, nxt, src, h2, wg, wu, wd)


COMBINE_TILE = 256


def _final_kernel(pos_ref, x1_ref, ys_hbm, rt_ref, g_ref, o_ref, ya0, yb0, ya1, yb1, sems):
    s = pl.program_id(0)
    n_tiles = pl.num_programs(0) - 1
    tm = COMBINE_TILE
    n_tok = pos_ref.shape[0] // 2
    ybufs = ((ya0, yb0), (ya1, yb1))

    def start_gather(tile, slot):
        for k in range(2):
            for r in range(tm):
                row = pos_ref[k * n_tok + tile * tm + r]
                pltpu.make_async_copy(ys_hbm.at[pl.ds(row, 1), :],
                                      ybufs[slot][k].at[pl.ds(r, 1), :], sems.at[slot]).start()

    def wait_gather(slot):
        for k in range(2):
            pltpu.make_async_copy(ys_hbm.at[pl.ds(0, tm), :], ybufs[slot][k], sems.at[slot]).wait()

    def finish(slot):
        rt = rt_ref[...]
        x2 = (x1_ref[...] + rt[:, 2:3] * ybufs[slot][0][...] + rt[:, 3:4] * ybufs[slot][1][...])
        o_ref[...] = (x2 * lax.rsqrt(jnp.mean(x2 * x2, axis=-1, keepdims=True) + RMS_EPS)
                      * g_ref[...])

    @pl.when(s == 0)
    def _():
        start_gather(s, 0)

    for parity in range(2):
        @pl.when((s >= 1) & (s < n_tiles) & (s % 2 == parity))
        def _(parity=parity):
            wait_gather(1 - parity)
            start_gather(s, parity)
            finish(1 - parity)

        @pl.when((s == n_tiles) & (s % 2 == parity))
        def _(parity=parity):
            wait_gather(1 - parity)
            finish(1 - parity)


def _final(x1, ys, pos, rt, g):
    n, d = x1.shape
    tm = COMBINE_TILE
    prev = lambda s: jnp.maximum(s - 1, 0)
    return pl.pallas_call(
        _final_kernel,
        grid_spec=pltpu.PrefetchScalarGridSpec(
            num_scalar_prefetch=1,
            grid=(n // tm + 1,),
            in_specs=[pl.BlockSpec((tm, d), lambda s, p: (prev(s), 0)),
                      pl.BlockSpec(memory_space=pl.ANY),
                      pl.BlockSpec((tm, LANES), lambda s, p: (prev(s), 0)),
                      pl.BlockSpec((1, d), lambda s, p: (0, 0))],
            out_specs=pl.BlockSpec((tm, d), lambda s, p: (prev(s), 0)),
            scratch_shapes=[pltpu.VMEM((tm, d), F32)] * 4 + [pltpu.SemaphoreType.DMA((2,))],
        ),
        out_shape=jax.ShapeDtypeStruct((n, d), F32),
        compiler_params=_cparams("arbitrary"),
        name="combine_norm",
    )(pos, x1, ys, rt, g)


def _dispatch_plan(rt, counts, n_tokens):
    tm = MOE_TILE
    n_rows = 2 * n_tokens + N_EXPERTS * tm
    n_tiles = n_rows // tm
    ids = rt[:, 0:6].astype(jnp.int32)
    ef = jnp.concatenate([ids[:, 0], ids[:, 1]])
    rank = jnp.concatenate([ids[:, 4], ids[:, 5]])
    counts = counts[0, :N_EXPERTS].astype(jnp.int32)
    tiles_per = (counts + tm - 1) // tm
    tile_end = jnp.cumsum(tiles_per)
    row_start = (tile_end - tiles_per) * tm
    pos = row_start[ef] + rank
    tok = jnp.concatenate([jnp.arange(n_tokens, dtype=jnp.int32)] * 2)
    src = jnp.zeros((n_rows,), jnp.int32).at[pos].set(tok)
    n_used = tile_end[-1]
    tile_ids = jnp.arange(n_tiles, dtype=jnp.int32)
    tile_expert = jnp.sum((tile_ids[:, None] >= tile_end[None, :]).astype(jnp.int32), axis=1)
    last_expert = jnp.sum((n_used - 1 >= tile_end).astype(jnp.int32))
    tile_expert = jnp.where(tile_ids < n_used, tile_expert, last_expert).astype(jnp.int32)
    first = jnp.concatenate([jnp.ones((1,), jnp.int32),
                             (tile_expert[1:] != tile_expert[:-1]).astype(jnp.int32)])
    eid = jnp.arange(N_EXPERTS, dtype=jnp.int32)
    later = (tiles_per[None, :] > 0) & (eid[None, :] > eid[:, None])
    next_expert = jnp.min(jnp.where(later, eid[None, :], N_EXPERTS), axis=1)
    next_expert = jnp.where(next_expert == N_EXPERTS, -1, next_expert).astype(jnp.int32)
    nxt = next_expert[tile_expert]
    return (src, pos[:n_tokens], pos[n_tokens:], tile_expert, first, nxt,
            n_used.reshape(1).astype(jnp.int32))


def kernel(x, norm_mix_g, w_in, hgrn_lb_logits, hgrn_out_norm_g, w_out, norm_ffn_g, w_group_router,
           b_group_router, w_expert_router, b_expert_router, w_gate, w_up, w_down, final_norm_g):
    b, t, d = x.shape
    n = b * t
    depth = w_in.shape[0]
    assert depth == 1, "the final norm is fused into the combine step of the only layer"
    attn_w = d // 2
    n_heads = attn_w // HEAD_DIM
    x2 = x.reshape(n, d)
    lb_all = jnp.cumsum(jax.nn.softmax(hgrn_lb_logits.astype(F32), axis=0), axis=0)[:depth]
    slope2 = jnp.asarray(2.0 ** (-8.0 * np.arange(1, n_heads + 1) / n_heads), dtype=F32) * F32(LOG2E)
    s_hi = slope2.astype(BF16).astype(F32)
    s_mid = (slope2 - s_hi).astype(BF16).astype(F32)
    s_lo = (slope2 - s_hi - s_mid).astype(BF16).astype(F32)
    slope_pieces = jnp.stack([s_hi, s_mid, s_lo])
    proj_scale = jnp.concatenate([jnp.full((attn_w,), HEAD_DIM ** -0.5 * LOG2E, F32),
                                  jnp.ones((w_in.shape[2] - attn_w,), F32)]).reshape(1, -1)

    for l in range(depth):
        g_mix = norm_mix_g[l].reshape(1, d)
        x2, w_in_l, w_out_l = lax.optimization_barrier(
            (x2, w_in[l].astype(BF16), w_out[l].astype(BF16)))
        proj = _inproj(x2, g_mix, w_in_l, 0, proj_scale, BF16).reshape(b, t, -1)
        o_attn = _moba(proj, slope_pieces, n_heads)
        o_rec = _hgrn(proj, 3 * attn_w, lb_all[l].reshape(1, -1),
                      hgrn_out_norm_g[l].reshape(1, HEAD_DIM), n_heads)

        wr = jnp.concatenate(
            [w_group_router[l],
             jnp.transpose(w_expert_router[l], (1, 0, 2)).reshape(d, N_EXPERTS)], axis=1)
        wr = jnp.pad(wr, ((0, 0), (0, LANES - wr.shape[1])))
        wr_hi = wr.astype(BF16)
        wr = jnp.concatenate([wr_hi, (wr - wr_hi.astype(F32)).astype(BF16)], axis=1)
        br =jnp.concatenate([b_group_router[l], b_expert_router[l].reshape(-1)])
        br = jnp.pad(br, (0, LANES - br.shape[0])).reshape(1, LANES)
        x1, h2, rt, counts = _outproj(o_attn.reshape(n, attn_w), o_rec.reshape(n, -1), x2,
                                      w_out_l, norm_ffn_g[l].reshape(1, d), wr, br)
        src, pos1, pos2, tile_expert, first, nxt, n_used = _dispatch_plan(rt, counts, n)
        ys = _moe(tile_expert, n_used, first, nxt, src, h2, w_gate[l], w_up[l], w_down[l])
        x2 = _final(x1, ys, jnp.concatenate([pos1, pos2]), rt, final_norm_g.reshape(1, d))
    return x2.reshape(b, t, d)
```

```python
import functools

import jax
import jax.numpy as jnp
import numpy as np
from jax import lax
from jax.experimental import pallas as pl
from jax.experimental.pallas import tpu as pltpu

F32 = jnp.float32
BF16 = jnp.bfloat16
HIGHEST = lax.Precision.HIGHEST

HEAD_DIM = 128
MOBA_BLOCK = 256
MOBA_TOPK = 3
N_GROUPS = 4
EXPERTS_PER_GROUP = 4
N_EXPERTS = N_GROUPS * EXPERTS_PER_GROUP
RMS_EPS = 1e-6

LANES = 128
VMEM_LIMIT_BYTES = 56 * 1024 * 1024

HGRN_CHUNK = 64
HGRN_SUB = 16
HGRN_STEP = 256
HGRN_HEADS_PER_STEP = 8
HGRN_MIN_SAFE_F = 2.0 ** -3.5
MOE_TILE = 256
NT_DIMS = (((1,), (1,)), ((), ()))
TN_DIMS = (((0,), (0,)), ((), ()))


def _cparams(*sem):
    return pltpu.CompilerParams(dimension_semantics=sem, vmem_limit_bytes=VMEM_LIMIT_BYTES)


def _inproj_kernel(x_ref, g_ref, w_ref, cs_ref, o_ref, hn_ref):
    @pl.when(pl.program_id(1) == 0)
    def _():
        x = x_ref[...]
        ms = jnp.mean(x * x, axis=-1, keepdims=True)
        hn_ref[...] = (x * lax.rsqrt(ms + RMS_EPS) * g_ref[...]).astype(BF16)

    acc = jnp.dot(hn_ref[...], w_ref[...], preferred_element_type=F32)
    o_ref[...] = (acc * cs_ref[...]).astype(o_ref.dtype)


def _inproj(x2, g, w, col0, colscale, out_dtype, tm=1024, tn=1792):
    n, d = x2.shape
    cols = colscale.shape[1]
    j0 = col0 // tn
    return pl.pallas_call(
        _inproj_kernel,
        grid=(n // tm, cols // tn),
        in_specs=[
            pl.BlockSpec((tm, d), lambda i, j: (i, 0)),
            pl.BlockSpec((1, d), lambda i, j: (0, 0)),
            pl.BlockSpec((d, tn), lambda i, j: (0, j0 + j)),
            pl.BlockSpec((1, tn), lambda i, j: (0, j)),
        ],
        out_specs=pl.BlockSpec((tm, tn), lambda i, j: (i, j)),
        out_shape=jax.ShapeDtypeStruct((n, cols), out_dtype),
        scratch_shapes=[pltpu.VMEM((tm, d), BF16)],
        compiler_params=_cparams("parallel", "arbitrary"),
        name="inproj",
    )(x2, g, w, colscale)


MOBA_HEADS_PER_STEP = 4
MOBA_EXTRA_POS = 0
MOBA_EXTRA_SEL = 8
MASK_BIG = 2.0 ** 60
LOG2E = 1.4426950408889634


def _moba_key_extras(t):
    nb = t // MOBA_BLOCK
    assert MOBA_EXTRA_SEL + nb <= LANES and MOBA_BLOCK <= 256
    pos = np.arange(t)
    kx = np.zeros((t, LANES), np.float32)
    kx[:, MOBA_EXTRA_POS:MOBA_EXTRA_POS + 3] = ((pos // MOBA_BLOCK) * MOBA_BLOCK)[:, None]
    kx[:, MOBA_EXTRA_POS + 3:MOBA_EXTRA_POS + 6] = (pos % MOBA_BLOCK)[:, None]
    kx[pos, MOBA_EXTRA_SEL + pos // MOBA_BLOCK] = -MASK_BIG
    return jnp.asarray(kx, dtype=BF16)


def _moba_kernel(sl_ref, q_ref, k_ref, v_ref, kx_ref, o_ref, kaug_ref, qaug_ref, *, nb):
    hg = pl.program_id(1)
    i = pl.program_id(2)
    bs = MOBA_BLOCK
    dh = HEAD_DIM
    t = q_ref.shape[1]
    nbp = -(-nb // 8) * 8
    nx = 8 + nbp

    @pl.when(i == 0)
    def _():
        r8 = lax.broadcasted_iota(jnp.int32, (8, t), 0)
        piece = r8 % 3
        er = lax.broadcasted_iota(jnp.int32, (nx, LANES), 0)
        ec = lax.broadcasted_iota(jnp.int32, (nx, LANES), 1)
        embed = jnp.where(er == ec, 1.0, 0.0).astype(BF16)
        blk = lax.broadcasted_iota(jnp.int32, (nbp, t), 0)
        qblk = lax.broadcasted_iota(jnp.int32, (nbp, t), 1) // bs
        for hh in range(MOBA_HEADS_PER_STEP):
            h = hg * MOBA_HEADS_PER_STEP + hh
            cs = slice(hh * dh, (hh + 1) * dh)
            kaug_ref[hh, :, 0:dh] = k_ref[0, :, cs]
            kaug_ref[hh, :, dh:] = kx_ref[...]
            rows = [jnp.mean(k_ref[0, j * bs:(j + 1) * bs, cs].astype(F32), axis=0, keepdims=True)
                    for j in range(nb)]
            if nbp > nb:
                rows.append(jnp.zeros((nbp - nb, dh), F32))
            km = jnp.concatenate(rows, axis=0)
            hi = km.astype(BF16)
            mid = (km - hi.astype(F32)).astype(BF16)
            lo = (km - hi.astype(F32) - mid.astype(F32)).astype(BF16)
            km4 = jnp.concatenate([hi, mid, lo, jnp.zeros_like(hi)], axis=0)
            q = q_ref[0, :, cs]
            g4 = lax.dot_general(km4, q, NT_DIMS, preferred_element_type=F32)
            gate_t = g4[0:nbp] + g4[nbp:2 * nbp] + g4[2 * nbp:3 * nbp]
            rank = jnp.zeros(gate_t.shape, F32)
            for jp in range(nb - 1):
                other = gate_t[jp:jp + 1, :]
                beats = (other > gate_t) | ((other == gate_t) & (blk > jp))
                rank = rank + jnp.where(beats & (qblk > jp), 1.0, 0.0)
            notsel_t = jnp.where((blk < qblk) & (rank >= MOBA_TOPK), 1.0, 0.0)
            slope_rows = jnp.where(r8 >= 6, 0.0,
                                   jnp.where(piece == 0, sl_ref[0, h],
                                             jnp.where(piece == 1, sl_ref[1, h], sl_ref[2, h])))
            qx_t = jnp.concatenate([slope_rows, notsel_t], axis=0).astype(BF16)
            qx = lax.dot_general(qx_t, embed, TN_DIMS, preferred_element_type=F32)
            qaug_ref[hh, :, 0:dh] = q
            qaug_ref[hh, :, dh:] = qx.astype(BF16)

    rowi = lax.broadcasted_iota(jnp.int32, (bs, bs), 0)
    coli = lax.broadcasted_iota(jnp.int32, (bs, bs), 1)

    for c in range(nb):
        @pl.when(i == c)
        def _(c=c):
            n = (c + 1) * bs
            for hh in range(MOBA_HEADS_PER_STEP):
                cs = slice(hh * dh, (hh + 1) * dh)
                s = lax.dot_general(qaug_ref[hh, c * bs:n, :], kaug_ref[hh, 0:n, :], NT_DIMS,
                                    preferred_element_type=F32)
                s_own = jnp.where(rowi >= coli, s[:, c * bs:], -jnp.inf)
                m = jnp.max(s_own, axis=1, keepdims=True)
                if c > 0:
                    s_past = s[:, :c * bs]
                    m = jnp.maximum(m, jnp.max(s_past, axis=1, keepdims=True))
                p_own = jnp.exp2(s_own - m)
                l = jnp.sum(p_own, axis=1, keepdims=True)
                acc = jnp.dot(p_own.astype(BF16), v_ref[0, c * bs:n, cs],
                              preferred_element_type=F32)
                if c > 0:
                    p_past = jnp.exp2(s_past - m)
                    l = l + jnp.sum(p_past, axis=1, keepdims=True)
                    acc = acc + jnp.dot(p_past.astype(BF16), v_ref[0, 0:c * bs, cs],
                                        preferred_element_type=F32)
                o_ref[0, :, cs] = (acc / l).astype(o_ref.dtype)


def _moba(qkv, slope_pieces, n_heads):
    b, t, _ = qkv.shape
    bs = MOBA_BLOCK
    nb = t // bs
    hps = MOBA_HEADS_PER_STEP
    assert n_heads % hps == 0
    ng = n_heads // hps
    w = hps * HEAD_DIM
    kern = functools.partial(_moba_kernel, nb=nb)
    return pl.pallas_call(
        kern,
        grid=(b, ng, nb),
        in_specs=[
            pl.BlockSpec(memory_space=pltpu.SMEM),
            pl.BlockSpec((1, t, w), lambda bi, hi, qi: (bi, 0, hi)),
            pl.BlockSpec((1, t, w), lambda bi, hi, qi: (bi, 0, ng + hi)),
            pl.BlockSpec((1, t, w), lambda bi, hi, qi: (bi, 0, 2 * ng + hi)),
            pl.BlockSpec((t, LANES), lambda bi, hi, qi: (0, 0)),
        ],
        out_specs=pl.BlockSpec((1, bs, w), lambda bi, hi, qi: (bi, qi, hi)),
        out_shape=jax.ShapeDtypeStruct((b, t, n_heads * HEAD_DIM), BF16),
        scratch_shapes=[
            pltpu.VMEM((hps, t, HEAD_DIM + LANES), BF16),
            pltpu.VMEM((hps, t, HEAD_DIM + LANES), BF16),
        ],
        compiler_params=_cparams("parallel", "parallel", "arbitrary"),
        name="moba",
    )(slope_pieces, qkv, qkv, qkv, _moba_key_extras(t))


def _sigmoid(x):
    return 1.0 / (1.0 + jnp.exp(-x))


def _hgrn_chunk(q, f, iv, g, gn, st, bounded_decay):
    c = HGRN_CHUNK
    sub = HGRN_SUB
    half = sub // 2
    nsub = c // sub
    qf = q * _sigmoid(q)
    kf = jnp.maximum(1.0 - f, 0.0)
    b2 = jnp.log2(f)
    rowc = lax.broadcasted_iota(jnp.int32, b2.shape, 0)
    shift = 1
    while shift < c:
        b2 = b2 + jnp.where(rowc >= shift, pltpu.roll(b2, shift, 0), 0.0)
        shift *= 2
    c2 = b2 - jnp.log2(kf)

    inter = lax.dot_general((qf * jnp.exp2(b2)).astype(BF16), st.astype(BF16), NT_DIMS,
                            preferred_element_type=F32)

    lane = lax.broadcasted_iota(jnp.int32, (sub, LANES), 1)
    tsub = lax.broadcasted_iota(jnp.int32, (sub, LANES), 0)
    colid = lax.broadcasted_iota(jnp.int32, (sub, c), 1)
    ones = jnp.ones((LANES, LANES), BF16)
    a_rows = []
    if bounded_decay:
        b_mid = b2[c // 2 - 1:c // 2, :]
        qt = qf * jnp.exp2(b2 - b_mid)
        kt = jnp.exp2(b_mid - c2)
        pair = lax.dot_general(qt.astype(BF16), kt.astype(BF16), NT_DIMS,
                               preferred_element_type=F32)
        rr = lax.broadcasted_iota(jnp.int32, (c, c), 0)
        rc = lax.broadcasted_iota(jnp.int32, (c, c), 1)
        a_rows.append(jnp.where(rr >= rc, pair, 0.0))
    for bi in range(0 if bounded_decay else nsub):
        lo = bi * sub
        b_i = b2[lo:lo + sub]
        c_i = c2[lo:lo + sub]
        q_i = qf[lo:lo + sub]
        pieces = []
        for s in range(sub):
            if s < half:
                pieces.append(q_i * jnp.exp2(b_i - c_i[s:s + 1, :]))
            else:
                pieces.append(q_i[half:] * jnp.exp2(b_i[half:] - c_i[s:s + 1, :]))
        pm = jnp.concatenate(pieces, axis=0).astype(BF16)
        rs = jnp.dot(pm, ones, preferred_element_type=F32)
        rel = lane - lo
        key = jnp.where((rel >= 0) & (rel <= tsub), rel, -1)
        key_lo, key_hi = key[:half], key[half:]
        a_lo = jnp.zeros((half, LANES), F32)
        a_hi = jnp.zeros((half, LANES), F32)
        off = 0
        for s in range(sub):
            if s < half:
                a_lo = jnp.where(key_lo == s, rs[off:off + half], a_lo)
                a_hi = jnp.where(key_hi == s, rs[off + half:off + sub], a_hi)
                off += sub
            else:
                a_hi = jnp.where(key_hi == s, rs[off:off + half], a_hi)
                off += half
        a_blk = jnp.concatenate([a_lo, a_hi], axis=0)[:, :c]
        if bi > 0:
            b0 = b2[lo - 1:lo, :]
            qt = q_i * jnp.exp2(b_i - b0)
            kt = jnp.exp2(jnp.minimum(b0 - c2, 0.0))
            cross = lax.dot_general(qt.astype(BF16), kt.astype(BF16), NT_DIMS,
                                    preferred_element_type=F32)
            a_blk = a_blk + jnp.where(colid < lo, cross, 0.0)
        a_rows.append(a_blk)
    a = jnp.concatenate(a_rows, axis=0)
    intra = jnp.dot(a.astype(BF16), iv.astype(BF16), preferred_element_type=F32)

    b_last = b2[c - 1:c, :]
    khat = jnp.exp2(b_last - c2)
    st_new = st * jnp.exp2(b_last) + lax.dot_general(
        iv.astype(BF16), khat.astype(BF16), TN_DIMS, preferred_element_type=F32)

    o = inter + intra
    y = o * lax.rsqrt(jnp.mean(o * o, axis=-1, keepdims=True) + RMS_EPS) * gn
    return y * (g * _sigmoid(g)), st_new


def _hgrn_kernel(q_ref, f_ref, i_ref, g_ref, lb_ref, gn_ref, o_ref, st_ref):
    @pl.when(pl.program_id(2) == 0)
    def _():
        st_ref[...] = jnp.zeros_like(st_ref)

    gn = gn_ref[...]
    d = HEAD_DIM
    lb = lb_ref[...]
    f_all = lb + (1.0 - lb) * _sigmoid(f_ref[0].astype(F32))
    f_min = jnp.min(f_all)

    def run(bounded_decay):
        sts = [st_ref[hh] for hh in range(HGRN_HEADS_PER_STEP)]
        for ci in range(HGRN_STEP // HGRN_CHUNK):
            sl = slice(ci * HGRN_CHUNK, (ci + 1) * HGRN_CHUNK)
            for hh in range(HGRN_HEADS_PER_STEP):
                cs = slice(hh * d, (hh + 1) * d)
                out, sts[hh] = _hgrn_chunk(q_ref[0, sl, cs].astype(F32), f_all[sl, cs],
                                           i_ref[0, sl, cs], g_ref[0, sl, cs].astype(F32), gn,
                                           sts[hh], bounded_decay)
                o_ref[0, sl, cs] = out.astype(o_ref.dtype)
        for hh in range(HGRN_HEADS_PER_STEP):
            st_ref[hh] = sts[hh]

    @pl.when(f_min >= HGRN_MIN_SAFE_F)
    def _():
        run(True)

    @pl.when(jnp.logical_not(f_min >= HGRN_MIN_SAFE_F))
    def _():
        run(False)


def _hgrn(hp, col0, lb, gn, n_heads):
    b, t, _ = hp.shape
    d = HEAD_DIM
    ts = HGRN_STEP
    hps = HGRN_HEADS_PER_STEP
    assert n_heads % hps == 0 and col0 % (hps * d) == 0
    ng = n_heads // hps
    blk0 = col0 // (hps * d)

    def col(group):
        return pl.BlockSpec((1, ts, hps * d), lambda bi, hi, ti: (bi, ti, blk0 + group * ng + hi))

    return pl.pallas_call(
        _hgrn_kernel,
        grid=(b, ng, t // ts),
        in_specs=[col(0), col(1), col(2), col(3),
                  pl.BlockSpec((1, hps * d), lambda bi, hi, ti: (0, hi)),
                  pl.BlockSpec((1, d), lambda bi, hi, ti: (0, 0))],
        out_specs=pl.BlockSpec((1, ts, hps * d), lambda bi, hi, ti: (bi, ti, hi)),
        out_shape=jax.ShapeDtypeStruct((b, t, n_heads * d), BF16),
        scratch_shapes=[pltpu.VMEM((hps, d, d), F32)],
        compiler_params=_cparams("parallel", "parallel", "arbitrary"),
        name="hgrn2",
    )(hp, hp, hp, hp, lb, gn)


def _outproj_kernel(oa_ref, or_ref, x_ref, w_ref, g_ref, wr_ref, br_ref,
                    x1_ref, h2_ref, rt_ref, cnt_out_ref, cnt_ref):
    wa = oa_ref.shape[1]
    x1 = (x_ref[...]
          + jnp.dot(oa_ref[...], w_ref[0:wa, :], preferred_element_type=F32)
          + jnp.dot(or_ref[...], w_ref[wa:, :], preferred_element_type=F32))
    x1_ref[...] = x1
    h2 = x1 * lax.rsqrt(jnp.mean(x1 * x1, axis=-1, keepdims=True) + RMS_EPS) * g_ref[...]
    h2_ref[...] = h2

    h_hi = h2.astype(BF16)
    h_mid = (h2 - h_hi.astype(F32)).astype(BF16)
    part = jnp.dot(h_hi, wr_ref[...], preferred_element_type=F32)
    logits = (part[:, :LANES] + part[:, LANES:] + br_ref[...]
              + jnp.dot(h_mid, wr_ref[:, :LANES], preferred_element_type=F32))
    lane = lax.broadcasted_iota(jnp.int32, logits.shape, 1)
    big = jnp.int32(4 * LANES)
    ninf = -jnp.inf

    lg = jnp.where(lane < N_GROUPS, logits, ninf)
    mg = jnp.max(lg, axis=1, keepdims=True)
    gidx = jnp.min(jnp.where(lg == mg, lane, big), axis=1, keepdims=True)
    grp_w = 1.0 / jnp.sum(jnp.exp(lg - mg), axis=1, keepdims=True)

    lo = N_GROUPS + EXPERTS_PER_GROUP * gidx
    le = jnp.where((lane >= lo) & (lane < lo + EXPERTS_PER_GROUP), logits, ninf)
    m1 = jnp.max(le, axis=1, keepdims=True)
    i1 = jnp.min(jnp.where(le == m1, lane, big), axis=1, keepdims=True)
    le2 = jnp.where(lane == i1, ninf, le)
    m2 = jnp.max(le2, axis=1, keepdims=True)
    i2 = jnp.min(jnp.where(le2 == m2, lane, big), axis=1, keepdims=True)
    r21 = jnp.exp(m2 - m1)
    w1 = grp_w / (1.0 + r21)
    w2 = grp_w * r21 / (1.0 + r21)
    e1 = i1 - N_GROUPS
    e2 = i2 - N_GROUPS

    @pl.when(pl.program_id(0) == 0)
    def _():
        cnt_ref[...] = jnp.zeros_like(cnt_ref)

    tm = logits.shape[0]
    onehot = jnp.where((lane == e1) | (lane == e2), 1.0, 0.0)
    rr = lax.broadcasted_iota(jnp.int32, (tm, tm), 0)
    rc = lax.broadcasted_iota(jnp.int32, (tm, tm), 1)
    before = jnp.where(rr > rc, 1.0, 0.0).astype(BF16)
    prefix = (jnp.dot(before, onehot.astype(BF16), preferred_element_type=F32) + cnt_ref[0:1, :])
    rank1 = jnp.sum(jnp.where(lane == e1, prefix, 0.0), axis=1, keepdims=True)
    rank2 = jnp.sum(jnp.where(lane == e2, prefix, 0.0), axis=1, keepdims=True)
    total = cnt_ref[0:1, :] + jnp.sum(onehot, axis=0, keepdims=True)
    cnt_ref[0:1, :] = total
    cnt_out_ref[...] = jnp.broadcast_to(total, cnt_out_ref.shape)

    cols = [e1.astype(F32), e2.astype(F32), w1, w2, rank1, rank2]
    rt = jnp.zeros(logits.shape, F32)
    for ci, cv in enumerate(cols):
        rt = jnp.where(lane == ci, cv, rt)
    rt_ref[...] = rt


def _outproj(oa, orec, x2, w_out, g, wr, br, tm=512):
    n, d = x2.shape
    wa = oa.shape[1]
    wrc = orec.shape[1]
    row = lambda i: (i, 0)
    const = lambda i: (0, 0)
    return pl.pallas_call(
        _outproj_kernel,
        grid=(n // tm,),
        in_specs=[
            pl.BlockSpec((tm, wa), row),
            pl.BlockSpec((tm, wrc), row),
            pl.BlockSpec((tm, d), row),
            pl.BlockSpec((wa + wrc, d), const),
            pl.BlockSpec((1, d), const),
            pl.BlockSpec((d, 2 * LANES), const),
            pl.BlockSpec((1, LANES), const),
        ],
        out_specs=[pl.BlockSpec((tm, d), row), pl.BlockSpec((tm, d), row),
                   pl.BlockSpec((tm, LANES), row), pl.BlockSpec((8, LANES), const)],
        out_shape=[jax.ShapeDtypeStruct((n, d), F32), jax.ShapeDtypeStruct((n, d), F32),
                   jax.ShapeDtypeStruct((n, LANES), F32), jax.ShapeDtypeStruct((8, LANES), F32)],
        scratch_shapes=[pltpu.VMEM((8, LANES), F32)],
        compiler_params=_cparams("arbitrary"),
        name="outproj_route",
    )(oa, orec, x2, w_out, g, wr, br)


WEIGHT_CAST_ROWS = 256


def _cast_rows(dst_ref, src_ref):
    rows = dst_ref.shape[0]
    step = min(rows, WEIGHT_CAST_ROWS)
    assert rows % step == 0

    def body(c, carry):
        sl = pl.ds(pl.multiple_of(c * step, step), step)
        dst_ref[sl, :] = src_ref[sl, :].astype(BF16)
        return carry

    lax.fori_loop(0, rows // step, body, 0)


def _moe_kernel(te_ref, nu_ref, first_ref, nxt_ref, src_ref, h2_hbm, wg_hbm, wu_hbm, wd_hbm, y_ref,
                xbuf0, xbuf1, sems, wsems, wg32, wu32, wd32, wgb, wub, wdb):
    s = pl.program_id(0)
    nu = nu_ref[0]
    tm = MOE_TILE
    xbufs = (xbuf0, xbuf1)
    prev_tile = jnp.maximum(s - 1, 0)

    def weight_copies(e):
        return (pltpu.make_async_copy(wg_hbm.at[e], wg32, wsems.at[0]),
                pltpu.make_async_copy(wu_hbm.at[e], wu32, wsems.at[1]),
                pltpu.make_async_copy(wd_hbm.at[e], wd32, wsems.at[2]))

    def start_gather(tile, slot):
        for r in range(tm):
            tok = src_ref[tile * tm + r]
            pltpu.make_async_copy(h2_hbm.at[pl.ds(tok, 1), :], xbufs[slot].at[pl.ds(r, 1), :],
                                  sems.at[slot]).start()

    def wait_gather(slot):
        pltpu.make_async_copy(h2_hbm.at[pl.ds(0, tm), :], xbufs[slot], sems.at[slot]).wait()

    def ffn(slot):
        x = xbufs[slot][...].astype(BF16)
        a = jnp.dot(x, wgb[...], preferred_element_type=F32)
        u = jnp.dot(x, wub[...], preferred_element_type=F32)
        hid = (a * _sigmoid(a) * u).astype(BF16)
        y_ref[...] = jnp.dot(hid, wdb[...], preferred_element_type=F32)

    @pl.when(s == 0)
    def _():
        for cp in weight_copies(te_ref[0]):
            cp.start()
        start_gather(s, 0)

    @pl.when((s >= 1) & (s <= nu) & (first_ref[prev_tile] == 1))
    def _():
        for cp in weight_copies(te_ref[prev_tile]):
            cp.wait()
        _cast_rows(wgb, wg32)
        _cast_rows(wub, wu32)
        _cast_rows(wdb, wd32)

        @pl.when(nxt_ref[prev_tile] >= 0)
        def _():
            for cp in weight_copies(nxt_ref[prev_tile]):
                cp.start(priority=1)

    for parity in range(2):
        @pl.when((s >= 1) & (s < nu) & (s % 2 == parity))
        def _(parity=parity):
            wait_gather(1 - parity)
            start_gather(s, parity)
            ffn(1 - parity)

        @pl.when((s >= 1) & (s == nu) & (s % 2 == parity))
        def _(parity=parity):
            wait_gather(1 - parity)
            ffn(1 - parity)

    @pl.when(s > nu)
    def _():
        y_ref[...] = jnp.zeros_like(y_ref)


def _moe(tile_expert, n_used, first, nxt, src, h2, wg, wu, wd):
    p = src.shape[0]
    d, f = wg.shape[1:]
    tm = MOE_TILE
    n_tiles = p // tm
    any_spec = pl.BlockSpec(memory_space=pl.ANY)
    return pl.pallas_call(
        _moe_kernel,
        grid_spec=pltpu.PrefetchScalarGridSpec(
            num_scalar_prefetch=5,
            grid=(n_tiles + 1,),
            in_specs=[any_spec, any_spec, any_spec, any_spec],
            out_specs=pl.BlockSpec((tm, d), lambda s, *_: (jnp.maximum(s - 1, 0), 0)),
            scratch_shapes=[pltpu.VMEM((tm, d), F32), pltpu.VMEM((tm, d), F32),
                            pltpu.SemaphoreType.DMA((2,)), pltpu.SemaphoreType.DMA((3,)),
                            pltpu.VMEM((d, f), F32), pltpu.VMEM((d, f), F32), pltpu.VMEM((f, d), F32),
                            pltpu.VMEM((d, f), BF16), pltpu.VMEM((d, f), BF16),
                            pltpu.VMEM((f, d), BF16)],
        ),
        out_shape=jax.ShapeDtypeStruct((p, d), F32),
        compiler_params=_cparams("arbitrary"),
        name="moe_ffn",
    )(tile_expert, n_used, first, nxt, src, h2, wg, wu, wd)


COMBINE_TILE = 256


def _final_kernel(pos_ref, x1_ref, ys_hbm, rt_ref, g_ref, o_ref, ya0, yb0, ya1, yb1, sems):
    s = pl.program_id(0)
    n_tiles = pl.num_programs(0) - 1
    tm = COMBINE_TILE
    n_tok = pos_ref.shape[0] // 2
    ybufs = ((ya0, yb0), (ya1, yb1))

    def start_gather(tile, slot):
        for k in range(2):
            for r in range(tm):
                row = pos_ref[k * n_tok + tile * tm + r]
                pltpu.make_async_copy(ys_hbm.at[pl.ds(row, 1), :],
                                      ybufs[slot][k].at[pl.ds(r, 1), :], sems.at[slot]).start()

    def wait_gather(slot):
        for k in range(2):
            pltpu.make_async_copy(ys_hbm.at[pl.ds(0, tm), :], ybufs[slot][k], sems.at[slot]).wait()

    def finish(slot):
        rt = rt_ref[...]
        x2 = (x1_ref[...] + rt[:, 2:3] * ybufs[slot][0][...] + rt[:, 3:4] * ybufs[slot][1][...])
        o_ref[...] = (x2 * lax.rsqrt(jnp.mean(x2 * x2, axis=-1, keepdims=True) + RMS_EPS)
                      * g_ref[...])

    @pl.when(s == 0)
    def _():
        start_gather(s, 0)

    for parity in range(2):
        @pl.when((s >= 1) & (s < n_tiles) & (s % 2 == parity))
        def _(parity=parity):
            wait_gather(1 - parity)
            start_gather(s, parity)
            finish(1 - parity)

        @pl.when((s == n_tiles) & (s % 2 == parity))
        def _(parity=parity):
            wait_gather(1 - parity)
            finish(1 - parity)


def _final(x1, ys, pos, rt, g):
    n, d = x1.shape
    tm = COMBINE_TILE
    prev = lambda s: jnp.maximum(s - 1, 0)
    return pl.pallas_call(
        _final_kernel,
        grid_spec=pltpu.PrefetchScalarGridSpec(
            num_scalar_prefetch=1,
            grid=(n // tm + 1,),
            in_specs=[pl.BlockSpec((tm, d), lambda s, p: (prev(s), 0)),
                      pl.BlockSpec(memory_space=pl.ANY),
                      pl.BlockSpec((tm, LANES), lambda s, p: (prev(s), 0)),
                      pl.BlockSpec((1, d), lambda s, p: (0, 0))],
            out_specs=pl.BlockSpec((tm, d), lambda s, p: (prev(s), 0)),
            scratch_shapes=[pltpu.VMEM((tm, d), F32)] * 4 + [pltpu.SemaphoreType.DMA((2,))],
        ),
        out_shape=jax.ShapeDtypeStruct((n, d), F32),
        compiler_params=_cparams("arbitrary"),
        name="combine_norm",
    )(pos, x1, ys, rt, g)


def _dispatch_plan(rt, counts, n_tokens):
    tm = MOE_TILE
    n_rows = 2 * n_tokens + N_EXPERTS * tm
    n_tiles = n_rows // tm
    ids = rt[:, 0:6].astype(jnp.int32)
    ef = jnp.concatenate([ids[:, 0], ids[:, 1]])
    rank = jnp.concatenate([ids[:, 4], ids[:, 5]])
    counts = counts[0, :N_EXPERTS].astype(jnp.int32)
    tiles_per = (counts + tm - 1) // tm
    tile_end = jnp.cumsum(tiles_per)
    row_start = (tile_end - tiles_per) * tm
    pos = row_start[ef] + rank
    tok = jnp.concatenate([jnp.arange(n_tokens, dtype=jnp.int32)] * 2)
    src = jnp.zeros((n_rows,), jnp.int32).at[pos].set(tok, unique_indices=True,
                                                      mode="promise_in_bounds")
    n_used = tile_end[-1]
    tile_ids = jnp.arange(n_tiles, dtype=jnp.int32)
    tile_expert = jnp.sum((tile_ids[:, None] >= tile_end[None, :]).astype(jnp.int32), axis=1)
    last_expert = jnp.sum((n_used - 1 >= tile_end).astype(jnp.int32))
    tile_expert = jnp.where(tile_ids < n_used, tile_expert, last_expert).astype(jnp.int32)
    first = jnp.concatenate([jnp.ones((1,), jnp.int32),
                             (tile_expert[1:] != tile_expert[:-1]).astype(jnp.int32)])
    eid = jnp.arange(N_EXPERTS, dtype=jnp.int32)
    later = (tiles_per[None, :] > 0) & (eid[None, :] > eid[:, None])
    next_expert = jnp.min(jnp.where(later, eid[None, :], N_EXPERTS), axis=1)
    next_expert = jnp.where(next_expert == N_EXPERTS, -1, next_expert).astype(jnp.int32)
    nxt = next_expert[tile_expert]
    return (src, pos[:n_tokens], pos[n_tokens:], tile_expert, first, nxt,
            n_used.reshape(1).astype(jnp.int32))


def kernel(x, norm_mix_g, w_in, hgrn_lb_logits, hgrn_out_norm_g, w_out, norm_ffn_g, w_group_router,
           b_group_router, w_expert_router, b_expert_router, w_gate, w_up, w_down, final_norm_g):
    b, t, d = x.shape
    n = b * t
    depth = w_in.shape[0]
    assert depth == 1, "the final norm is fused into the combine step of the only layer"
    attn_w = d // 2
    n_heads = attn_w // HEAD_DIM
    x2 = x.reshape(n, d)
    lb_all = jnp.cumsum(jax.nn.softmax(hgrn_lb_logits.astype(F32), axis=0), axis=0)[:depth]
    slope2 = jnp.asarray(2.0 ** (-8.0 * np.arange(1, n_heads + 1) / n_heads), dtype=F32) * F32(LOG2E)
    s_hi = slope2.astype(BF16).astype(F32)
    s_mid = (slope2 - s_hi).astype(BF16).astype(F32)
    s_lo = (slope2 - s_hi - s_mid).astype(BF16).astype(F32)
    slope_pieces = jnp.stack([s_hi, s_mid, s_lo])
    proj_scale = jnp.concatenate([jnp.full((attn_w,), HEAD_DIM ** -0.5 * LOG2E, F32),
                                  jnp.ones((w_in.shape[2] - attn_w,), F32)]).reshape(1, -1)

    for l in range(depth):
        g_mix = norm_mix_g[l].reshape(1, d)
        x2, w_in_l, w_out_l = lax.optimization_barrier(
            (x2, w_in[l].astype(BF16), w_out[l].astype(BF16)))
        proj = _inproj(x2, g_mix, w_in_l, 0, proj_scale, BF16).reshape(b, t, -1)
        o_attn = _moba(proj, slope_pieces, n_heads)
        o_rec = _hgrn(proj, 3 * attn_w, lb_all[l].reshape(1, -1),
                      hgrn_out_norm_g[l].reshape(1, HEAD_DIM), n_heads)

        wr = jnp.concatenate(
            [w_group_router[l],
             jnp.transpose(w_expert_router[l], (1, 0, 2)).reshape(d, N_EXPERTS)], axis=1)
        wr = jnp.pad(wr, ((0, 0), (0, LANES - wr.shape[1])))
        wr_hi = wr.astype(BF16)
        wr = jnp.concatenate([wr_hi, (wr - wr_hi.astype(F32)).astype(BF16)], axis=1)
        br =jnp.concatenate([b_group_router[l], b_expert_router[l].reshape(-1)])
        br = jnp.pad(br, (0, LANES - br.shape[0])).reshape(1, LANES)
        x1, h2, rt, counts = _outproj(o_attn.reshape(n, attn_w), o_rec.reshape(n, -1), x2,
                                      w_out_l, norm_ffn_g[l].reshape(1, d), wr, br)
        src, pos1, pos2, tile_expert, first, nxt, n_used = _dispatch_plan(rt, counts, n)
        ys = _moe(tile_expert, n_used, first, nxt, src, h2, w_gate[l], w_up[l], w_down[l])
        x2 = _final(x1, ys, jnp.concatenate([pos1, pos2]), rt, final_norm_g.reshape(1, d))
    return x2.reshape(b, t, d)
```

```python
import functools

import jax
import jax.numpy as jnp
import numpy as np
from jax import lax
from jax.experimental import pallas as pl
from jax.experimental.pallas import tpu as pltpu

F32 = jnp.float32
BF16 = jnp.bfloat16
HIGHEST = lax.Precision.HIGHEST

HEAD_DIM = 128
MOBA_BLOCK = 256
MOBA_TOPK = 3
N_GROUPS = 4
EXPERTS_PER_GROUP = 4
N_EXPERTS = N_GROUPS * EXPERTS_PER_GROUP
RMS_EPS = 1e-6

LANES = 128
VMEM_LIMIT_BYTES = 56 * 1024 * 1024

HGRN_CHUNK = 64
HGRN_SUB = 16
HGRN_STEP = 256
HGRN_HEADS_PER_STEP = 8
HGRN_MIN_SAFE_F = 2.0 ** -3.5
MOE_TILE = 256
NT_DIMS = (((1,), (1,)), ((), ()))
TN_DIMS = (((0,), (0,)), ((), ()))


def _cparams(*sem):
    return pltpu.CompilerParams(dimension_semantics=sem, vmem_limit_bytes=VMEM_LIMIT_BYTES)


def _inproj_kernel(x_ref, g_ref, w_ref, cs_ref, o_ref, hn_ref):
    @pl.when(pl.program_id(1) == 0)
    def _():
        x = x_ref[...]
        ms = jnp.mean(x * x, axis=-1, keepdims=True)
        hn_ref[...] = (x * lax.rsqrt(ms + RMS_EPS) * g_ref[...]).astype(BF16)

    acc = jnp.dot(hn_ref[...], w_ref[...], preferred_element_type=F32)
    o_ref[...] = (acc * cs_ref[...]).astype(o_ref.dtype)


def _inproj(x2, g, w, col0, colscale, out_dtype, tm=1024, tn=1792):
    n, d = x2.shape
    cols = colscale.shape[1]
    j0 = col0 // tn
    return pl.pallas_call(
        _inproj_kernel,
        grid=(n // tm, cols // tn),
        in_specs=[
            pl.BlockSpec((tm, d), lambda i, j: (i, 0)),
            pl.BlockSpec((1, d), lambda i, j: (0, 0)),
            pl.BlockSpec((d, tn), lambda i, j: (0, j0 + j)),
            pl.BlockSpec((1, tn), lambda i, j: (0, j)),
        ],
        out_specs=pl.BlockSpec((tm, tn), lambda i, j: (i, j)),
        out_shape=jax.ShapeDtypeStruct((n, cols), out_dtype),
        scratch_shapes=[pltpu.VMEM((tm, d), BF16)],
        compiler_params=_cparams("parallel", "arbitrary"),
        name="inproj",
    )(x2, g, w, colscale)


MOBA_HEADS_PER_STEP = 4
MOBA_EXTRA_POS = 0
MOBA_EXTRA_SEL = 8
MASK_BIG = 2.0 ** 60
LOG2E = 1.4426950408889634


def _moba_key_extras(t):
    nb = t // MOBA_BLOCK
    assert MOBA_EXTRA_SEL + nb <= LANES and MOBA_BLOCK <= 256
    pos = np.arange(t)
    kx = np.zeros((t, LANES), np.float32)
    kx[:, MOBA_EXTRA_POS:MOBA_EXTRA_POS + 3] = ((pos // MOBA_BLOCK) * MOBA_BLOCK)[:, None]
    kx[:, MOBA_EXTRA_POS + 3:MOBA_EXTRA_POS + 6] = (pos % MOBA_BLOCK)[:, None]
    kx[pos, MOBA_EXTRA_SEL + pos // MOBA_BLOCK] = -MASK_BIG
    return jnp.asarray(kx, dtype=BF16)


def _moba_kernel(sl_ref, q_ref, k_ref, v_ref, kx_ref, o_ref, kaug_ref, qaug_ref, *, nb):
    hg = pl.program_id(1)
    i = pl.program_id(2)
    bs = MOBA_BLOCK
    dh = HEAD_DIM
    t = q_ref.shape[1]
    nbp = -(-nb // 8) * 8
    nx = 8 + nbp

    @pl.when(i == 0)
    def _():
        r8 = lax.broadcasted_iota(jnp.int32, (8, t), 0)
        piece = r8 % 3
        er = lax.broadcasted_iota(jnp.int32, (nx, LANES), 0)
        ec = lax.broadcasted_iota(jnp.int32, (nx, LANES), 1)
        embed = jnp.where(er == ec, 1.0, 0.0).astype(BF16)
        blk = lax.broadcasted_iota(jnp.int32, (nbp, t), 0)
        qblk = lax.broadcasted_iota(jnp.int32, (nbp, t), 1) // bs
        for hh in range(MOBA_HEADS_PER_STEP):
            h = hg * MOBA_HEADS_PER_STEP + hh
            cs = slice(hh * dh, (hh + 1) * dh)
            kaug_ref[hh, :, 0:dh] = k_ref[0, :, cs]
            kaug_ref[hh, :, dh:] = kx_ref[...]
            rows = [jnp.mean(k_ref[0, j * bs:(j + 1) * bs, cs].astype(F32), axis=0, keepdims=True)
                    for j in range(nb)]
            if nbp > nb:
                rows.append(jnp.zeros((nbp - nb, dh), F32))
            km = jnp.concatenate(rows, axis=0)
            hi = km.astype(BF16)
            mid = (km - hi.astype(F32)).astype(BF16)
            lo = (km - hi.astype(F32) - mid.astype(F32)).astype(BF16)
            km4 = jnp.concatenate([hi, mid, lo, jnp.zeros_like(hi)], axis=0)
            q = q_ref[0, :, cs]
            g4 = lax.dot_general(km4, q, NT_DIMS, preferred_element_type=F32)
            gate_t = g4[0:nbp] + g4[nbp:2 * nbp] + g4[2 * nbp:3 * nbp]
            rank = jnp.zeros(gate_t.shape, F32)
            for jp in range(nb - 1):
                other = gate_t[jp:jp + 1, :]
                beats = (other > gate_t) | ((other == gate_t) & (blk > jp))
                rank = rank + jnp.where(beats & (qblk > jp), 1.0, 0.0)
            notsel_t = jnp.where((blk < qblk) & (rank >= MOBA_TOPK), 1.0, 0.0)
            slope_rows = jnp.where(r8 >= 6, 0.0,
                                   jnp.where(piece == 0, sl_ref[0, h],
                                             jnp.where(piece == 1, sl_ref[1, h], sl_ref[2, h])))
            qx_t = jnp.concatenate([slope_rows, notsel_t], axis=0).astype(BF16)
            qx = lax.dot_general(qx_t, embed, TN_DIMS, preferred_element_type=F32)
            qaug_ref[hh, :, 0:dh] = q
            qaug_ref[hh, :, dh:] = qx.astype(BF16)

    rowi = lax.broadcasted_iota(jnp.int32, (bs, bs), 0)
    coli = lax.broadcasted_iota(jnp.int32, (bs, bs), 1)

    for c in range(nb):
        @pl.when(i == c)
        def _(c=c):
            n = (c + 1) * bs
            for hh in range(MOBA_HEADS_PER_STEP):
                cs = slice(hh * dh, (hh + 1) * dh)
                s = lax.dot_general(qaug_ref[hh, c * bs:n, :], kaug_ref[hh, 0:n, :], NT_DIMS,
                                    preferred_element_type=F32)
                s_own = jnp.where(rowi >= coli, s[:, c * bs:], -jnp.inf)
                m = jnp.max(s_own, axis=1, keepdims=True)
                if c > 0:
                    s_past = s[:, :c * bs]
                    m = jnp.maximum(m, jnp.max(s_past, axis=1, keepdims=True))
                p_own = jnp.exp2(s_own - m)
                l = jnp.sum(p_own, axis=1, keepdims=True)
                acc = jnp.dot(p_own.astype(BF16), v_ref[0, c * bs:n, cs],
                              preferred_element_type=F32)
                if c > 0:
                    p_past = jnp.exp2(s_past - m)
                    l = l + jnp.sum(p_past, axis=1, keepdims=True)
                    acc = acc + jnp.dot(p_past.astype(BF16), v_ref[0, 0:c * bs, cs],
                                        preferred_element_type=F32)
                o_ref[0, :, cs] = (acc / l).astype(o_ref.dtype)


def _moba(qkv, slope_pieces, n_heads):
    b, t, _ = qkv.shape
    bs = MOBA_BLOCK
    nb = t // bs
    hps = MOBA_HEADS_PER_STEP
    assert n_heads % hps == 0
    ng = n_heads // hps
    w = hps * HEAD_DIM
    kern = functools.partial(_moba_kernel, nb=nb)
    return pl.pallas_call(
        kern,
        grid=(b, ng, nb),
        in_specs=[
            pl.BlockSpec(memory_space=pltpu.SMEM),
            pl.BlockSpec((1, t, w), lambda bi, hi, qi: (bi, 0, hi)),
            pl.BlockSpec((1, t, w), lambda bi, hi, qi: (bi, 0, ng + hi)),
            pl.BlockSpec((1, t, w), lambda bi, hi, qi: (bi, 0, 2 * ng + hi)),
            pl.BlockSpec((t, LANES), lambda bi, hi, qi: (0, 0)),
        ],
        out_specs=pl.BlockSpec((1, bs, w), lambda bi, hi, qi: (bi, qi, hi)),
        out_shape=jax.ShapeDtypeStruct((b, t, n_heads * HEAD_DIM), BF16),
        scratch_shapes=[
            pltpu.VMEM((hps, t, HEAD_DIM + LANES), BF16),
            pltpu.VMEM((hps, t, HEAD_DIM + LANES), BF16),
        ],
        compiler_params=_cparams("parallel", "parallel", "arbitrary"),
        name="moba",
    )(slope_pieces, qkv, qkv, qkv, _moba_key_extras(t))


def _sigmoid(x):
    return 1.0 / (1.0 + jnp.exp(-x))


def _hgrn_chunk(q, f, iv, g, gn, st, bounded_decay):
    c = HGRN_CHUNK
    sub = HGRN_SUB
    half = sub // 2
    nsub = c // sub
    qf = q * _sigmoid(q)
    kf = jnp.maximum(1.0 - f, 0.0)
    b2 = jnp.log2(f)
    rowc = lax.broadcasted_iota(jnp.int32, b2.shape, 0)
    shift = 1
    while shift < c:
        b2 = b2 + jnp.where(rowc >= shift, pltpu.roll(b2, shift, 0), 0.0)
        shift *= 2
    c2 = b2 - jnp.log2(kf)

    inter = lax.dot_general((qf * jnp.exp2(b2)).astype(BF16), st.astype(BF16), NT_DIMS,
                            preferred_element_type=F32)

    lane = lax.broadcasted_iota(jnp.int32, (sub, LANES), 1)
    tsub = lax.broadcasted_iota(jnp.int32, (sub, LANES), 0)
    colid = lax.broadcasted_iota(jnp.int32, (sub, c), 1)
    ones = jnp.ones((LANES, LANES), BF16)
    a_rows = []
    if bounded_decay:
        b_mid = b2[c // 2 - 1:c // 2, :]
        qt = qf * jnp.exp2(b2 - b_mid)
        kt = jnp.exp2(b_mid - c2)
        pair = lax.dot_general(qt.astype(BF16), kt.astype(BF16), NT_DIMS,
                               preferred_element_type=F32)
        rr = lax.broadcasted_iota(jnp.int32, (c, c), 0)
        rc = lax.broadcasted_iota(jnp.int32, (c, c), 1)
        a_rows.append(jnp.where(rr >= rc, pair, 0.0))
    for bi in range(0 if bounded_decay else nsub):
        lo = bi * sub
        b_i = b2[lo:lo + sub]
        c_i = c2[lo:lo + sub]
        q_i = qf[lo:lo + sub]
        pieces = []
        for s in range(sub):
            if s < half:
                pieces.append(q_i * jnp.exp2(b_i - c_i[s:s + 1, :]))
            else:
                pieces.append(q_i[half:] * jnp.exp2(b_i[half:] - c_i[s:s + 1, :]))
        pm = jnp.concatenate(pieces, axis=0).astype(BF16)
        rs = jnp.dot(pm, ones, preferred_element_type=F32)
        rel = lane - lo
        key = jnp.where((rel >= 0) & (rel <= tsub), rel, -1)
        key_lo, key_hi = key[:half], key[half:]
        a_lo = jnp.zeros((half, LANES), F32)
        a_hi = jnp.zeros((half, LANES), F32)
        off = 0
        for s in range(sub):
            if s < half:
                a_lo = jnp.where(key_lo == s, rs[off:off + half], a_lo)
                a_hi = jnp.where(key_hi == s, rs[off + half:off + sub], a_hi)
                off += sub
            else:
                a_hi = jnp.where(key_hi == s, rs[off:off + half], a_hi)
                off += half
        a_blk = jnp.concatenate([a_lo, a_hi], axis=0)[:, :c]
        if bi > 0:
            b0 = b2[lo - 1:lo, :]
            qt = q_i * jnp.exp2(b_i - b0)
            kt = jnp.exp2(jnp.minimum(b0 - c2, 0.0))
            cross = lax.dot_general(qt.astype(BF16), kt.astype(BF16), NT_DIMS,
                                    preferred_element_type=F32)
            a_blk = a_blk + jnp.where(colid < lo, cross, 0.0)
        a_rows.append(a_blk)
    a = jnp.concatenate(a_rows, axis=0)
    intra = jnp.dot(a.astype(BF16), iv.astype(BF16), preferred_element_type=F32)

    b_last = b2[c - 1:c, :]
    khat = jnp.exp2(b_last - c2)
    st_new = st * jnp.exp2(b_last) + lax.dot_general(
        iv.astype(BF16), khat.astype(BF16), TN_DIMS, preferred_element_type=F32)

    o = inter + intra
    y = o * lax.rsqrt(jnp.mean(o * o, axis=-1, keepdims=True) + RMS_EPS) * gn
    return y * (g * _sigmoid(g)), st_new


def _hgrn_kernel(q_ref, f_ref, i_ref, g_ref, lb_ref, gn_ref, o_ref, st_ref):
    @pl.when(pl.program_id(2) == 0)
    def _():
        st_ref[...] = jnp.zeros_like(st_ref)

    gn = gn_ref[...]
    d = HEAD_DIM
    lb = lb_ref[...]
    f_all = lb + (1.0 - lb) * _sigmoid(f_ref[0].astype(F32))
    f_min = jnp.min(f_all)

    def run(bounded_decay):
        sts = [st_ref[hh] for hh in range(HGRN_HEADS_PER_STEP)]
        for ci in range(HGRN_STEP // HGRN_CHUNK):
            sl = slice(ci * HGRN_CHUNK, (ci + 1) * HGRN_CHUNK)
            for hh in range(HGRN_HEADS_PER_STEP):
                cs = slice(hh * d, (hh + 1) * d)
                out, sts[hh] = _hgrn_chunk(q_ref[0, sl, cs].astype(F32), f_all[sl, cs],
                                           i_ref[0, sl, cs], g_ref[0, sl, cs].astype(F32), gn,
                                           sts[hh], bounded_decay)
                o_ref[0, sl, cs] = out.astype(o_ref.dtype)
        for hh in range(HGRN_HEADS_PER_STEP):
            st_ref[hh] = sts[hh]

    @pl.when(f_min >= HGRN_MIN_SAFE_F)
    def _():
        run(True)

    @pl.when(jnp.logical_not(f_min >= HGRN_MIN_SAFE_F))
    def _():
        run(False)


def _hgrn(hp, col0, lb, gn, n_heads):
    b, t, _ = hp.shape
    d = HEAD_DIM
    ts = HGRN_STEP
    hps = HGRN_HEADS_PER_STEP
    assert n_heads % hps == 0 and col0 % (hps * d) == 0
    ng = n_heads // hps
    blk0 = col0 // (hps * d)

    def col(group):
        return pl.BlockSpec((1, ts, hps * d), lambda bi, hi, ti: (bi, ti, blk0 + group * ng + hi))

    return pl.pallas_call(
        _hgrn_kernel,
        grid=(b, ng, t // ts),
        in_specs=[col(0), col(1), col(2), col(3),
                  pl.BlockSpec((1, hps * d), lambda bi, hi, ti: (0, hi)),
                  pl.BlockSpec((1, d), lambda bi, hi, ti: (0, 0))],
        out_specs=pl.BlockSpec((1, ts, hps * d), lambda bi, hi, ti: (bi, ti, hi)),
        out_shape=jax.ShapeDtypeStruct((b, t, n_heads * d), BF16),
        scratch_shapes=[pltpu.VMEM((hps, d, d), F32)],
        compiler_params=_cparams("parallel", "parallel", "arbitrary"),
        name="hgrn2",
    )(hp, hp, hp, hp, lb, gn)


def _outproj_kernel(oa_ref, or_ref, x_ref, w_ref, g_ref, wr_ref, br_ref,
                    x1_ref, h2_ref, rt_ref, cnt_out_ref, cnt_ref):
    wa = oa_ref.shape[1]
    x1 = (x_ref[...]
          + jnp.dot(oa_ref[...], w_ref[0:wa, :], preferred_element_type=F32)
          + jnp.dot(or_ref[...], w_ref[wa:, :], preferred_element_type=F32))
    x1_ref[...] = x1
    h2 = x1 * lax.rsqrt(jnp.mean(x1 * x1, axis=-1, keepdims=True) + RMS_EPS) * g_ref[...]
    h2_ref[...] = h2

    h_hi = h2.astype(BF16)
    h_mid = (h2 - h_hi.astype(F32)).astype(BF16)
    part = jnp.dot(h_hi, wr_ref[...], preferred_element_type=F32)
    logits = (part[:, :LANES] + part[:, LANES:] + br_ref[...]
              + jnp.dot(h_mid, wr_ref[:, :LANES], preferred_element_type=F32))
    lane = lax.broadcasted_iota(jnp.int32, logits.shape, 1)
    big = jnp.int32(4 * LANES)
    ninf = -jnp.inf

    lg = jnp.where(lane < N_GROUPS, logits, ninf)
    mg = jnp.max(lg, axis=1, keepdims=True)
    gidx = jnp.min(jnp.where(lg == mg, lane, big), axis=1, keepdims=True)
    grp_w = 1.0 / jnp.sum(jnp.exp(lg - mg), axis=1, keepdims=True)

    lo = N_GROUPS + EXPERTS_PER_GROUP * gidx
    le = jnp.where((lane >= lo) & (lane < lo + EXPERTS_PER_GROUP), logits, ninf)
    m1 = jnp.max(le, axis=1, keepdims=True)
    i1 = jnp.min(jnp.where(le == m1, lane, big), axis=1, keepdims=True)
    le2 = jnp.where(lane == i1, ninf, le)
    m2 = jnp.max(le2, axis=1, keepdims=True)
    i2 = jnp.min(jnp.where(le2 == m2, lane, big), axis=1, keepdims=True)
    r21 = jnp.exp(m2 - m1)
    w1 = grp_w / (1.0 + r21)
    w2 = grp_w * r21 / (1.0 + r21)
    e1 = i1 - N_GROUPS
    e2 = i2 - N_GROUPS

    @pl.when(pl.program_id(0) == 0)
    def _():
        cnt_ref[...] = jnp.zeros_like(cnt_ref)

    tm = logits.shape[0]
    onehot = jnp.where((lane == e1) | (lane == e2), 1.0, 0.0)
    rr = lax.broadcasted_iota(jnp.int32, (tm, tm), 0)
    rc = lax.broadcasted_iota(jnp.int32, (tm, tm), 1)
    before = jnp.where(rr > rc, 1.0, 0.0).astype(BF16)
    prefix = (jnp.dot(before, onehot.astype(BF16), preferred_element_type=F32) + cnt_ref[0:1, :])
    rank1 = jnp.sum(jnp.where(lane == e1, prefix, 0.0), axis=1, keepdims=True)
    rank2 = jnp.sum(jnp.where(lane == e2, prefix, 0.0), axis=1, keepdims=True)
    total = cnt_ref[0:1, :] + jnp.sum(onehot, axis=0, keepdims=True)
    cnt_ref[0:1, :] = total
    cnt_out_ref[...] = jnp.broadcast_to(total, cnt_out_ref.shape)

    cols = [e1.astype(F32), e2.astype(F32), w1, w2, rank1, rank2]
    rt = jnp.zeros(logits.shape, F32)
    for ci, cv in enumerate(cols):
        rt = jnp.where(lane == ci, cv, rt)
    rt_ref[...] = rt


def _outproj(oa, orec, x2, w_out, g, wr, br, tm=512):
    n, d = x2.shape
    wa = oa.shape[1]
    wrc = orec.shape[1]
    row = lambda i: (i, 0)
    const = lambda i: (0, 0)
    return pl.pallas_call(
        _outproj_kernel,
        grid=(n // tm,),
        in_specs=[
            pl.BlockSpec((tm, wa), row),
            pl.BlockSpec((tm, wrc), row),
            pl.BlockSpec((tm, d), row),
            pl.BlockSpec((wa + wrc, d), const),
            pl.BlockSpec((1, d), const),
            pl.BlockSpec((d, 2 * LANES), const),
            pl.BlockSpec((1, LANES), const),
        ],
        out_specs=[pl.BlockSpec((tm, d), row), pl.BlockSpec((tm, d), row),
                   pl.BlockSpec((tm, LANES), row), pl.BlockSpec((8, LANES), const)],
        out_shape=[jax.ShapeDtypeStruct((n, d), F32), jax.ShapeDtypeStruct((n, d), F32),
                   jax.ShapeDtypeStruct((n, LANES), F32), jax.ShapeDtypeStruct((8, LANES), F32)],
        scratch_shapes=[pltpu.VMEM((8, LANES), F32)],
        compiler_params=_cparams("arbitrary"),
        name="outproj_route",
    )(oa, orec, x2, w_out, g, wr, br)


WEIGHT_CAST_ROWS = 256


def _cast_rows(dst_ref, src_ref):
    rows = dst_ref.shape[0]
    step = min(rows, WEIGHT_CAST_ROWS)
    assert rows % step == 0

    def body(c, carry):
        sl = pl.ds(pl.multiple_of(c * step, step), step)
        dst_ref[sl, :] = src_ref[sl, :].astype(BF16)
        return carry

    lax.fori_loop(0, rows // step, body, 0)


def _moe_kernel(te_ref, nu_ref, first_ref, nxt_ref, src_ref, h2_hbm, wg_hbm, wu_hbm, wd_hbm, y_ref,
                xbuf0, xbuf1, xbuf2, sems, wsems, wg32, wu32, wd32, wgb, wub, wdb):
    s = pl.program_id(0)
    nu = nu_ref[0]
    tm = MOE_TILE
    xbufs = (xbuf0, xbuf1, xbuf2)
    nbuf = len(xbufs)
    prev_tile = jnp.maximum(s - 1, 0)

    def weight_copies(e):
        return (pltpu.make_async_copy(wg_hbm.at[e], wg32, wsems.at[0]),
                pltpu.make_async_copy(wu_hbm.at[e], wu32, wsems.at[1]),
                pltpu.make_async_copy(wd_hbm.at[e], wd32, wsems.at[2]))

    def start_gather(tile, slot):
        for r in range(tm):
            tok = src_ref[tile * tm + r]
            pltpu.make_async_copy(h2_hbm.at[pl.ds(tok, 1), :], xbufs[slot].at[pl.ds(r, 1), :],
                                  sems.at[slot]).start()

    def wait_gather(slot):
        pltpu.make_async_copy(h2_hbm.at[pl.ds(0, tm), :], xbufs[slot], sems.at[slot]).wait()

    def ffn(slot):
        x = xbufs[slot][...].astype(BF16)
        a = jnp.dot(x, wgb[...], preferred_element_type=F32)
        u = jnp.dot(x, wub[...], preferred_element_type=F32)
        hid = (a * _sigmoid(a) * u).astype(BF16)
        y_ref[...] = jnp.dot(hid, wdb[...], preferred_element_type=F32)

    @pl.when(s == 0)
    def _():
        for cp in weight_copies(te_ref[0]):
            cp.start()
        start_gather(0, 0)

        @pl.when(nu > 1)
        def _():
            start_gather(1, 1)

    @pl.when((s >= 1) & (s <= nu) & (first_ref[prev_tile] == 1))
    def _():
        for cp in weight_copies(te_ref[prev_tile]):
            cp.wait()
        _cast_rows(wgb, wg32)
        _cast_rows(wub, wu32)
        _cast_rows(wdb, wd32)

        @pl.when(nxt_ref[prev_tile] >= 0)
        def _():
            for cp in weight_copies(nxt_ref[prev_tile]):
                cp.start(priority=1)

    for ph in range(nbuf):
        cur, nxt_buf = (ph - 1) % nbuf, (ph + 1) % nbuf

        @pl.when((s >= 1) & (s + 1 < nu) & (s % nbuf == ph))
        def _(cur=cur, nxt_buf=nxt_buf):
            wait_gather(cur)
            start_gather(s + 1, nxt_buf)
            ffn(cur)

        @pl.when((s >= 1) & (s <= nu) & (s + 1 >= nu) & (s % nbuf == ph))
        def _(cur=cur):
            wait_gather(cur)
            ffn(cur)

    @pl.when(s > nu)
    def _():
        y_ref[...] = jnp.zeros_like(y_ref)


def _moe(tile_expert, n_used, first, nxt, src, h2, wg, wu, wd):
    p = src.shape[0]
    d, f = wg.shape[1:]
    tm = MOE_TILE
    n_tiles = p // tm
    any_spec = pl.BlockSpec(memory_space=pl.ANY)
    return pl.pallas_call(
        _moe_kernel,
        grid_spec=pltpu.PrefetchScalarGridSpec(
            num_scalar_prefetch=5,
            grid=(n_tiles + 1,),
            in_specs=[any_spec, any_spec, any_spec, any_spec],
            out_specs=pl.BlockSpec((tm, d), lambda s, *_: (jnp.maximum(s - 1, 0), 0)),
            scratch_shapes=[pltpu.VMEM((tm, d), F32)] * 3 + [
                            pltpu.SemaphoreType.DMA((3,)), pltpu.SemaphoreType.DMA((3,)),
                            pltpu.VMEM((d, f), F32), pltpu.VMEM((d, f), F32), pltpu.VMEM((f, d), F32),
                            pltpu.VMEM((d, f), BF16), pltpu.VMEM((d, f), BF16),
                            pltpu.VMEM((f, d), BF16)],
        ),
        out_shape=jax.ShapeDtypeStruct((p, d), F32),
        compiler_params=_cparams("arbitrary"),
        name="moe_ffn",
    )(tile_expert, n_used, first, nxt, src, h2, wg, wu, wd)


COMBINE_TILE = 256


def _final_kernel(pos_ref, x1_ref, ys_hbm, rt_ref, g_ref, o_ref, ya0, yb0, ya1, yb1, sems):
    s = pl.program_id(0)
    n_tiles = pl.num_programs(0) - 1
    tm = COMBINE_TILE
    n_tok = pos_ref.shape[0] // 2
    ybufs = ((ya0, yb0), (ya1, yb1))

    def start_gather(tile, slot):
        for k in range(2):
            for r in range(tm):
                row = pos_ref[k * n_tok + tile * tm + r]
                pltpu.make_async_copy(ys_hbm.at[pl.ds(row, 1), :],
                                      ybufs[slot][k].at[pl.ds(r, 1), :], sems.at[slot]).start()

    def wait_gather(slot):
        for k in range(2):
            pltpu.make_async_copy(ys_hbm.at[pl.ds(0, tm), :], ybufs[slot][k], sems.at[slot]).wait()

    def finish(slot):
        rt = rt_ref[...]
        x2 = (x1_ref[...] + rt[:, 2:3] * ybufs[slot][0][...] + rt[:, 3:4] * ybufs[slot][1][...])
        o_ref[...] = (x2 * lax.rsqrt(jnp.mean(x2 * x2, axis=-1, keepdims=True) + RMS_EPS)
                      * g_ref[...])

    @pl.when(s == 0)
    def _():
        start_gather(s, 0)

    for parity in range(2):
        @pl.when((s >= 1) & (s < n_tiles) & (s % 2 == parity))
        def _(parity=parity):
            wait_gather(1 - parity)
            start_gather(s, parity)
            finish(1 - parity)

        @pl.when((s == n_tiles) & (s % 2 == parity))
        def _(parity=parity):
            wait_gather(1 - parity)
            finish(1 - parity)


def _final(x1, ys, pos, rt, g):
    n, d = x1.shape
    tm = COMBINE_TILE
    prev = lambda s: jnp.maximum(s - 1, 0)
    return pl.pallas_call(
        _final_kernel,
        grid_spec=pltpu.PrefetchScalarGridSpec(
            num_scalar_prefetch=1,
            grid=(n // tm + 1,),
            in_specs=[pl.BlockSpec((tm, d), lambda s, p: (prev(s), 0)),
                      pl.BlockSpec(memory_space=pl.ANY),
                      pl.BlockSpec((tm, LANES), lambda s, p: (prev(s), 0)),
                      pl.BlockSpec((1, d), lambda s, p: (0, 0))],
            out_specs=pl.BlockSpec((tm, d), lambda s, p: (prev(s), 0)),
            scratch_shapes=[pltpu.VMEM((tm, d), F32)] * 4 + [pltpu.SemaphoreType.DMA((2,))],
        ),
        out_shape=jax.ShapeDtypeStruct((n, d), F32),
        compiler_params=_cparams("arbitrary"),
        name="combine_norm",
    )(pos, x1, ys, rt, g)


def _dispatch_plan(rt, counts, n_tokens):
    tm = MOE_TILE
    n_rows = 2 * n_tokens + N_EXPERTS * tm
    n_tiles = n_rows // tm
    ids = rt[:, 0:6].astype(jnp.int32)
    ef = jnp.concatenate([ids[:, 0], ids[:, 1]])
    rank = jnp.concatenate([ids[:, 4], ids[:, 5]])
    counts = counts[0, :N_EXPERTS].astype(jnp.int32)
    tiles_per = (counts + tm - 1) // tm
    tile_end = jnp.cumsum(tiles_per)
    row_start = (tile_end - tiles_per) * tm
    pos = row_start[ef] + rank
    tok = jnp.concatenate([jnp.arange(n_tokens, dtype=jnp.int32)] * 2)
    src = jnp.zeros((n_rows,), jnp.int32).at[pos].set(tok, unique_indices=True,
                                                      mode="promise_in_bounds")
    n_used = tile_end[-1]
    tile_ids = jnp.arange(n_tiles, dtype=jnp.int32)
    tile_expert = jnp.sum((tile_ids[:, None] >= tile_end[None, :]).astype(jnp.int32), axis=1)
    last_expert = jnp.sum((n_used - 1 >= tile_end).astype(jnp.int32))
    tile_expert = jnp.where(tile_ids < n_used, tile_expert, last_expert).astype(jnp.int32)
    first = jnp.concatenate([jnp.ones((1,), jnp.int32),
                             (tile_expert[1:] != tile_expert[:-1]).astype(jnp.int32)])
    eid = jnp.arange(N_EXPERTS, dtype=jnp.int32)
    later = (tiles_per[None, :] > 0) & (eid[None, :] > eid[:, None])
    next_expert = jnp.min(jnp.where(later, eid[None, :], N_EXPERTS), axis=1)
    next_expert = jnp.where(next_expert == N_EXPERTS, -1, next_expert).astype(jnp.int32)
    nxt = next_expert[tile_expert]
    return (src, pos[:n_tokens], pos[n_tokens:], tile_expert, first, nxt,
            n_used.reshape(1).astype(jnp.int32))


def kernel(x, norm_mix_g, w_in, hgrn_lb_logits, hgrn_out_norm_g, w_out, norm_ffn_g, w_group_router,
           b_group_router, w_expert_router, b_expert_router, w_gate, w_up, w_down, final_norm_g):
    b, t, d = x.shape
    n = b * t
    depth = w_in.shape[0]
    assert depth == 1, "the final norm is fused into the combine step of the only layer"
    attn_w = d // 2
    n_heads = attn_w // HEAD_DIM
    x2 = x.reshape(n, d)
    lb_all = jnp.cumsum(jax.nn.softmax(hgrn_lb_logits.astype(F32), axis=0), axis=0)[:depth]
    slope2 = jnp.asarray(2.0 ** (-8.0 * np.arange(1, n_heads + 1) / n_heads), dtype=F32) * F32(LOG2E)
    s_hi = slope2.astype(BF16).astype(F32)
    s_mid = (slope2 - s_hi).astype(BF16).astype(F32)
    s_lo = (slope2 - s_hi - s_mid).astype(BF16).astype(F32)
    slope_pieces = jnp.stack([s_hi, s_mid, s_lo])
    proj_scale = jnp.concatenate([jnp.full((attn_w,), HEAD_DIM ** -0.5 * LOG2E, F32),
                                  jnp.ones((w_in.shape[2] - attn_w,), F32)]).reshape(1, -1)

    for l in range(depth):
        g_mix = norm_mix_g[l].reshape(1, d)
        x2, w_in_l, w_out_l = lax.optimization_barrier(
            (x2, w_in[l].astype(BF16), w_out[l].astype(BF16)))
        proj = _inproj(x2, g_mix, w_in_l, 0, proj_scale, BF16).reshape(b, t, -1)
        o_attn = _moba(proj, slope_pieces, n_heads)
        o_rec = _hgrn(proj, 3 * attn_w, lb_all[l].reshape(1, -1),
                      hgrn_out_norm_g[l].reshape(1, HEAD_DIM), n_heads)

        wr = jnp.concatenate(
            [w_group_router[l],
             jnp.transpose(w_expert_router[l], (1, 0, 2)).reshape(d, N_EXPERTS)], axis=1)
        wr = jnp.pad(wr, ((0, 0), (0, LANES - wr.shape[1])))
        wr_hi = wr.astype(BF16)
        wr = jnp.concatenate([wr_hi, (wr - wr_hi.astype(F32)).astype(BF16)], axis=1)
        br =jnp.concatenate([b_group_router[l], b_expert_router[l].reshape(-1)])
        br = jnp.pad(br, (0, LANES - br.shape[0])).reshape(1, LANES)
        x1, h2, rt, counts = _outproj(o_attn.reshape(n, attn_w), o_rec.reshape(n, -1), x2,
                                      w_out_l, norm_ffn_g[l].reshape(1, d), wr, br)
        src, pos1, pos2, tile_expert, first, nxt, n_used = _dispatch_plan(rt, counts, n)
        ys = _moe(tile_expert, n_used, first, nxt, src, h2, w_gate[l], w_up[l], w_down[l])
        x2 = _final(x1, ys, jnp.concatenate([pos1, pos2]), rt, final_norm_g.reshape(1, d))
    return x2.reshape(b, t, d)
```

```python
import functools

import jax
import jax.numpy as jnp
import numpy as np
from jax import lax
from jax.experimental import pallas as pl
from jax.experimental.pallas import tpu as pltpu

F32 = jnp.float32
BF16 = jnp.bfloat16

HEAD_DIM = 128
MOBA_BLOCK = 256
MOBA_TOPK = 3
N_GROUPS = 4
EXPERTS_PER_GROUP = 4
N_EXPERTS = N_GROUPS * EXPERTS_PER_GROUP
RMS_EPS = 1e-6

LANES = 128
VMEM_LIMIT_BYTES = 56 * 1024 * 1024

HGRN_CHUNK = 64
HGRN_SUB = 16
HGRN_STEP = 256
HGRN_HEADS_PER_STEP = 8
HGRN_MIN_SAFE_F = 2.0 ** -3.5
MOE_TILE = 256
INPROJ_TILE = (1024, 1792)
OUTPROJ_TILE = 512
NT_DIMS = (((1,), (1,)), ((), ()))
TN_DIMS = (((0,), (0,)), ((), ()))


def _cparams(*sem):
    return pltpu.CompilerParams(dimension_semantics=sem, vmem_limit_bytes=VMEM_LIMIT_BYTES)


def _inproj_kernel(x_ref, g_ref, w_ref, cs_ref, o_ref, hn_ref):
    @pl.when(pl.program_id(1) == 0)
    def _():
        x = x_ref[...]
        ms = jnp.mean(x * x, axis=-1, keepdims=True)
        hn_ref[...] = (x * lax.rsqrt(ms + RMS_EPS) * g_ref[...]).astype(BF16)

    acc = jnp.dot(hn_ref[...], w_ref[...], preferred_element_type=F32)
    o_ref[...] = (acc * cs_ref[...]).astype(o_ref.dtype)


def _inproj(x2, g, w, col0, colscale, out_dtype):
    n, d = x2.shape
    tm, tn = INPROJ_TILE
    cols = colscale.shape[1]
    j0 = col0 // tn
    return pl.pallas_call(
        _inproj_kernel,
        grid=(n // tm, cols // tn),
        in_specs=[
            pl.BlockSpec((tm, d), lambda i, j: (i, 0)),
            pl.BlockSpec((1, d), lambda i, j: (0, 0)),
            pl.BlockSpec((d, tn), lambda i, j: (0, j0 + j)),
            pl.BlockSpec((1, tn), lambda i, j: (0, j)),
        ],
        out_specs=pl.BlockSpec((tm, tn), lambda i, j: (i, j)),
        out_shape=jax.ShapeDtypeStruct((n, cols), out_dtype),
        scratch_shapes=[pltpu.VMEM((tm, d), BF16)],
        compiler_params=_cparams("parallel", "arbitrary"),
        name="inproj",
    )(x2, g, w, colscale)


MOBA_HEADS_PER_STEP = 4
MOBA_EXTRA_POS = 0
MOBA_EXTRA_SEL = 8
MASK_BIG = 2.0 ** 60
LOG2E = 1.4426950408889634


def _moba_key_extras(t):
    nb = t // MOBA_BLOCK
    assert MOBA_EXTRA_SEL + nb <= LANES and MOBA_BLOCK <= 256
    pos = np.arange(t)
    kx = np.zeros((t, LANES), np.float32)
    kx[:, MOBA_EXTRA_POS:MOBA_EXTRA_POS + 3] = ((pos // MOBA_BLOCK) * MOBA_BLOCK)[:, None]
    kx[:, MOBA_EXTRA_POS + 3:MOBA_EXTRA_POS + 6] = (pos % MOBA_BLOCK)[:, None]
    kx[pos, MOBA_EXTRA_SEL + pos // MOBA_BLOCK] = -MASK_BIG
    return jnp.asarray(kx, dtype=BF16)


def _moba_kernel(sl_ref, q_ref, k_ref, v_ref, kx_ref, o_ref, kaug_ref, qaug_ref, *, nb):
    hg = pl.program_id(1)
    i = pl.program_id(2)
    bs = MOBA_BLOCK
    dh = HEAD_DIM
    t = q_ref.shape[1]
    nbp = -(-nb // 8) * 8
    nx = 8 + nbp

    @pl.when(i == 0)
    def _():
        r8 = lax.broadcasted_iota(jnp.int32, (8, t), 0)
        piece = r8 % 3
        er = lax.broadcasted_iota(jnp.int32, (nx, LANES), 0)
        ec = lax.broadcasted_iota(jnp.int32, (nx, LANES), 1)
        embed = jnp.where(er == ec, 1.0, 0.0).astype(BF16)
        blk = lax.broadcasted_iota(jnp.int32, (nbp, t), 0)
        qblk = lax.broadcasted_iota(jnp.int32, (nbp, t), 1) // bs
        for hh in range(MOBA_HEADS_PER_STEP):
            h = hg * MOBA_HEADS_PER_STEP + hh
            cs = slice(hh * dh, (hh + 1) * dh)
            kaug_ref[hh, :, 0:dh] = k_ref[0, :, cs]
            kaug_ref[hh, :, dh:] = kx_ref[...]
            rows = [jnp.mean(k_ref[0, j * bs:(j + 1) * bs, cs].astype(F32), axis=0, keepdims=True)
                    for j in range(nb)]
            if nbp > nb:
                rows.append(jnp.zeros((nbp - nb, dh), F32))
            km = jnp.concatenate(rows, axis=0)
            hi = km.astype(BF16)
            mid = (km - hi.astype(F32)).astype(BF16)
            lo = (km - hi.astype(F32) - mid.astype(F32)).astype(BF16)
            km4 = jnp.concatenate([hi, mid, lo, jnp.zeros_like(hi)], axis=0)
            q = q_ref[0, :, cs]
            g4 = lax.dot_general(km4, q, NT_DIMS, preferred_element_type=F32)
            gate_t = g4[0:nbp] + g4[nbp:2 * nbp] + g4[2 * nbp:3 * nbp]
            rank = jnp.zeros(gate_t.shape, F32)
            for jp in range(nb - 1):
                other = gate_t[jp:jp + 1, :]
                beats = (other > gate_t) | ((other == gate_t) & (blk > jp))
                rank = rank + jnp.where(beats & (qblk > jp), 1.0, 0.0)
            notsel_t = jnp.where((blk < qblk) & (rank >= MOBA_TOPK), 1.0, 0.0)
            slope_rows = jnp.where(r8 >= 6, 0.0,
                                   jnp.where(piece == 0, sl_ref[0, h],
                                             jnp.where(piece == 1, sl_ref[1, h], sl_ref[2, h])))
            qx_t = jnp.concatenate([slope_rows, notsel_t], axis=0).astype(BF16)
            qx = lax.dot_general(qx_t, embed, TN_DIMS, preferred_element_type=F32)
            qaug_ref[hh, :, 0:dh] = q
            qaug_ref[hh, :, dh:] = qx.astype(BF16)

    rowi = lax.broadcasted_iota(jnp.int32, (bs, bs), 0)
    coli = lax.broadcasted_iota(jnp.int32, (bs, bs), 1)

    for c in range(nb):
        @pl.when(i == c)
        def _(c=c):
            n = (c + 1) * bs
            for hh in range(MOBA_HEADS_PER_STEP):
                cs = slice(hh * dh, (hh + 1) * dh)
                s = lax.dot_general(qaug_ref[hh, c * bs:n, :], kaug_ref[hh, 0:n, :], NT_DIMS,
                                    preferred_element_type=F32)
                s_own = jnp.where(rowi >= coli, s[:, c * bs:], -jnp.inf)
                m = jnp.max(s_own, axis=1, keepdims=True)
                if c > 0:
                    s_past = s[:, :c * bs]
                    m = jnp.maximum(m, jnp.max(s_past, axis=1, keepdims=True))
                p_own = jnp.exp2(s_own - m)
                l = jnp.sum(p_own, axis=1, keepdims=True)
                acc = jnp.dot(p_own.astype(BF16), v_ref[0, c * bs:n, cs],
                              preferred_element_type=F32)
                if c > 0:
                    p_past = jnp.exp2(s_past - m)
                    l = l + jnp.sum(p_past, axis=1, keepdims=True)
                    acc = acc + jnp.dot(p_past.astype(BF16), v_ref[0, 0:c * bs, cs],
                                        preferred_element_type=F32)
                o_ref[0, :, cs] = (acc / l).astype(o_ref.dtype)


def _moba(qkv, slope_pieces, n_heads):
    b, t, _ = qkv.shape
    bs = MOBA_BLOCK
    nb = t // bs
    hps = MOBA_HEADS_PER_STEP
    assert n_heads % hps == 0
    ng = n_heads // hps
    w = hps * HEAD_DIM
    kern = functools.partial(_moba_kernel, nb=nb)
    return pl.pallas_call(
        kern,
        grid=(b, ng, nb),
        in_specs=[
            pl.BlockSpec(memory_space=pltpu.SMEM),
            pl.BlockSpec((1, t, w), lambda bi, hi, qi: (bi, 0, hi)),
            pl.BlockSpec((1, t, w), lambda bi, hi, qi: (bi, 0, ng + hi)),
            pl.BlockSpec((1, t, w), lambda bi, hi, qi: (bi, 0, 2 * ng + hi)),
            pl.BlockSpec((t, LANES), lambda bi, hi, qi: (0, 0)),
        ],
        out_specs=pl.BlockSpec((1, bs, w), lambda bi, hi, qi: (bi, qi, hi)),
        out_shape=jax.ShapeDtypeStruct((b, t, n_heads * HEAD_DIM), BF16),
        scratch_shapes=[
            pltpu.VMEM((hps, t, HEAD_DIM + LANES), BF16),
            pltpu.VMEM((hps, t, HEAD_DIM + LANES), BF16),
        ],
        compiler_params=_cparams("parallel", "parallel", "arbitrary"),
        name="moba",
    )(slope_pieces, qkv, qkv, qkv, _moba_key_extras(t))


def _sigmoid(x):
    return 1.0 / (1.0 + jnp.exp(-x))


def _hgrn_chunk(q, f, iv, g, gn, st, bounded_decay):
    c = HGRN_CHUNK
    sub = HGRN_SUB
    half = sub // 2
    nsub = c // sub
    qf = q * _sigmoid(q)
    kf = jnp.maximum(1.0 - f, 0.0)
    b2 = jnp.log2(f)
    rowc = lax.broadcasted_iota(jnp.int32, b2.shape, 0)
    shift = 1
    while shift < c:
        b2 = b2 + jnp.where(rowc >= shift, pltpu.roll(b2, shift, 0), 0.0)
        shift *= 2
    c2 = b2 - jnp.log2(kf)

    inter = lax.dot_general((qf * jnp.exp2(b2)).astype(BF16), st.astype(BF16), NT_DIMS,
                            preferred_element_type=F32)

    lane = lax.broadcasted_iota(jnp.int32, (sub, LANES), 1)
    tsub = lax.broadcasted_iota(jnp.int32, (sub, LANES), 0)
    colid = lax.broadcasted_iota(jnp.int32, (sub, c), 1)
    ones = jnp.ones((LANES, LANES), BF16)
    a_rows = []
    if bounded_decay:
        b_mid = b2[c // 2 - 1:c // 2, :]
        qt = qf * jnp.exp2(b2 - b_mid)
        kt = jnp.exp2(b_mid - c2)
        pair = lax.dot_general(qt.astype(BF16), kt.astype(BF16), NT_DIMS,
                               preferred_element_type=F32)
        rr = lax.broadcasted_iota(jnp.int32, (c, c), 0)
        rc = lax.broadcasted_iota(jnp.int32, (c, c), 1)
        a_rows.append(jnp.where(rr >= rc, pair, 0.0))
    for bi in range(0 if bounded_decay else nsub):
        lo = bi * sub
        b_i = b2[lo:lo + sub]
        c_i = c2[lo:lo + sub]
        q_i = qf[lo:lo + sub]
        pieces = []
        for s in range(sub):
            if s < half:
                pieces.append(q_i * jnp.exp2(b_i - c_i[s:s + 1, :]))
            else:
                pieces.append(q_i[half:] * jnp.exp2(b_i[half:] - c_i[s:s + 1, :]))
        pm = jnp.concatenate(pieces, axis=0).astype(BF16)
        rs = jnp.dot(pm, ones, preferred_element_type=F32)
        rel = lane - lo
        key = jnp.where((rel >= 0) & (rel <= tsub), rel, -1)
        key_lo, key_hi = key[:half], key[half:]
        a_lo = jnp.zeros((half, LANES), F32)
        a_hi = jnp.zeros((half, LANES), F32)
        off = 0
        for s in range(sub):
            if s < half:
                a_lo = jnp.where(key_lo == s, rs[off:off + half], a_lo)
                a_hi = jnp.where(key_hi == s, rs[off + half:off + sub], a_hi)
                off += sub
            else:
                a_hi = jnp.where(key_hi == s, rs[off:off + half], a_hi)
                off += half
        a_blk = jnp.concatenate([a_lo, a_hi], axis=0)[:, :c]
        if bi > 0:
            b0 = b2[lo - 1:lo, :]
            qt = q_i * jnp.exp2(b_i - b0)
            kt = jnp.exp2(jnp.minimum(b0 - c2, 0.0))
            cross = lax.dot_general(qt.astype(BF16), kt.astype(BF16), NT_DIMS,
                                    preferred_element_type=F32)
            a_blk = a_blk + jnp.where(colid < lo, cross, 0.0)
        a_rows.append(a_blk)
    a = jnp.concatenate(a_rows, axis=0)
    intra = jnp.dot(a.astype(BF16), iv.astype(BF16), preferred_element_type=F32)

    b_last = b2[c - 1:c, :]
    khat = jnp.exp2(b_last - c2)
    st_new = st * jnp.exp2(b_last) + lax.dot_general(
        iv.astype(BF16), khat.astype(BF16), TN_DIMS, preferred_element_type=F32)

    o = inter + intra
    y = o * lax.rsqrt(jnp.mean(o * o, axis=-1, keepdims=True) + RMS_EPS) * gn
    return y * (g * _sigmoid(g)), st_new


def _hgrn_kernel(q_ref, f_ref, i_ref, g_ref, lb_ref, gn_ref, o_ref, st_ref):
    @pl.when(pl.program_id(2) == 0)
    def _():
        st_ref[...] = jnp.zeros_like(st_ref)

    gn = gn_ref[...]
    d = HEAD_DIM
    lb = lb_ref[...]
    f_all = lb + (1.0 - lb) * _sigmoid(f_ref[0].astype(F32))
    f_min = jnp.min(f_all)

    def run(bounded_decay):
        sts = [st_ref[hh] for hh in range(HGRN_HEADS_PER_STEP)]
        for ci in range(HGRN_STEP // HGRN_CHUNK):
            sl = slice(ci * HGRN_CHUNK, (ci + 1) * HGRN_CHUNK)
            for hh in range(HGRN_HEADS_PER_STEP):
                cs = slice(hh * d, (hh + 1) * d)
                out, sts[hh] = _hgrn_chunk(q_ref[0, sl, cs].astype(F32), f_all[sl, cs],
                                           i_ref[0, sl, cs], g_ref[0, sl, cs].astype(F32), gn,
                                           sts[hh], bounded_decay)
                o_ref[0, sl, cs] = out.astype(o_ref.dtype)
        for hh in range(HGRN_HEADS_PER_STEP):
            st_ref[hh] = sts[hh]

    @pl.when(f_min >= HGRN_MIN_SAFE_F)
    def _():
        run(True)

    @pl.when(jnp.logical_not(f_min >= HGRN_MIN_SAFE_F))
    def _():
        run(False)


def _hgrn(hp, col0, lb, gn, n_heads):
    b, t, _ = hp.shape
    d = HEAD_DIM
    ts = HGRN_STEP
    hps = HGRN_HEADS_PER_STEP
    assert n_heads % hps == 0 and col0 % (hps * d) == 0
    ng = n_heads // hps
    blk0 = col0 // (hps * d)

    def col(group):
        return pl.BlockSpec((1, ts, hps * d), lambda bi, hi, ti: (bi, ti, blk0 + group * ng + hi))

    return pl.pallas_call(
        _hgrn_kernel,
        grid=(b, ng, t // ts),
        in_specs=[col(0), col(1), col(2), col(3),
                  pl.BlockSpec((1, hps * d), lambda bi, hi, ti: (0, hi)),
                  pl.BlockSpec((1, d), lambda bi, hi, ti: (0, 0))],
        out_specs=pl.BlockSpec((1, ts, hps * d), lambda bi, hi, ti: (bi, ti, hi)),
        out_shape=jax.ShapeDtypeStruct((b, t, n_heads * d), BF16),
        scratch_shapes=[pltpu.VMEM((hps, d, d), F32)],
        compiler_params=_cparams("parallel", "parallel", "arbitrary"),
        name="hgrn2",
    )(hp, hp, hp, hp, lb, gn)


def _outproj_kernel(oa_ref, or_ref, x_ref, w_ref, g_ref, wr_ref, br_ref,
                    x1_ref, h2_ref, rt_ref, cnt_out_ref, cnt_ref):
    wa = oa_ref.shape[1]
    x1 = (x_ref[...]
          + jnp.dot(oa_ref[...], w_ref[0:wa, :], preferred_element_type=F32)
          + jnp.dot(or_ref[...], w_ref[wa:, :], preferred_element_type=F32))
    x1_ref[...] = x1
    h2 = x1 * lax.rsqrt(jnp.mean(x1 * x1, axis=-1, keepdims=True) + RMS_EPS) * g_ref[...]
    h2_ref[...] = h2

    h_hi = h2.astype(BF16)
    h_mid = (h2 - h_hi.astype(F32)).astype(BF16)
    part = jnp.dot(h_hi, wr_ref[...], preferred_element_type=F32)
    logits = (part[:, :LANES] + part[:, LANES:] + br_ref[...]
              + jnp.dot(h_mid, wr_ref[:, :LANES], preferred_element_type=F32))
    lane = lax.broadcasted_iota(jnp.int32, logits.shape, 1)
    big = jnp.int32(4 * LANES)
    ninf = -jnp.inf

    lg = jnp.where(lane < N_GROUPS, logits, ninf)
    mg = jnp.max(lg, axis=1, keepdims=True)
    gidx = jnp.min(jnp.where(lg == mg, lane, big), axis=1, keepdims=True)
    grp_w = 1.0 / jnp.sum(jnp.exp(lg - mg), axis=1, keepdims=True)

    lo = N_GROUPS + EXPERTS_PER_GROUP * gidx
    le = jnp.where((lane >= lo) & (lane < lo + EXPERTS_PER_GROUP), logits, ninf)
    m1 = jnp.max(le, axis=1, keepdims=True)
    i1 = jnp.min(jnp.where(le == m1, lane, big), axis=1, keepdims=True)
    le2 = jnp.where(lane == i1, ninf, le)
    m2 = jnp.max(le2, axis=1, keepdims=True)
    i2 = jnp.min(jnp.where(le2 == m2, lane, big), axis=1, keepdims=True)
    r21 = jnp.exp(m2 - m1)
    w1 = grp_w / (1.0 + r21)
    w2 = grp_w * r21 / (1.0 + r21)
    e1 = i1 - N_GROUPS
    e2 = i2 - N_GROUPS

    @pl.when(pl.program_id(0) == 0)
    def _():
        cnt_ref[...] = jnp.zeros_like(cnt_ref)

    tm = logits.shape[0]
    onehot = jnp.where((lane == e1) | (lane == e2), 1.0, 0.0)
    rr = lax.broadcasted_iota(jnp.int32, (tm, tm), 0)
    rc = lax.broadcasted_iota(jnp.int32, (tm, tm), 1)
    before = jnp.where(rr > rc, 1.0, 0.0).astype(BF16)
    prefix = (jnp.dot(before, onehot.astype(BF16), preferred_element_type=F32) + cnt_ref[0:1, :])
    rank1 = jnp.sum(jnp.where(lane == e1, prefix, 0.0), axis=1, keepdims=True)
    rank2 = jnp.sum(jnp.where(lane == e2, prefix, 0.0), axis=1, keepdims=True)
    total = cnt_ref[0:1, :] + jnp.sum(onehot, axis=0, keepdims=True)
    cnt_ref[0:1, :] = total
    cnt_out_ref[...] = jnp.broadcast_to(total, cnt_out_ref.shape)

    cols = [e1.astype(F32), e2.astype(F32), w1, w2, rank1, rank2]
    rt = jnp.zeros(logits.shape, F32)
    for ci, cv in enumerate(cols):
        rt = jnp.where(lane == ci, cv, rt)
    rt_ref[...] = rt


def _outproj(oa, orec, x2, w_out, g, wr, br):
    n, d = x2.shape
    tm = OUTPROJ_TILE
    wa = oa.shape[1]
    wrc = orec.shape[1]
    row = lambda i: (i, 0)
    const = lambda i: (0, 0)
    return pl.pallas_call(
        _outproj_kernel,
        grid=(n // tm,),
        in_specs=[
            pl.BlockSpec((tm, wa), row),
            pl.BlockSpec((tm, wrc), row),
            pl.BlockSpec((tm, d), row),
            pl.BlockSpec((wa + wrc, d), const),
            pl.BlockSpec((1, d), const),
            pl.BlockSpec((d, 2 * LANES), const),
            pl.BlockSpec((1, LANES), const),
        ],
        out_specs=[pl.BlockSpec((tm, d), row), pl.BlockSpec((tm, d), row),
                   pl.BlockSpec((tm, LANES), row), pl.BlockSpec((8, LANES), const)],
        out_shape=[jax.ShapeDtypeStruct((n, d), F32), jax.ShapeDtypeStruct((n, d), F32),
                   jax.ShapeDtypeStruct((n, LANES), F32), jax.ShapeDtypeStruct((8, LANES), F32)],
        scratch_shapes=[pltpu.VMEM((8, LANES), F32)],
        compiler_params=_cparams("arbitrary"),
        name="outproj_route",
    )(oa, orec, x2, w_out, g, wr, br)


WEIGHT_CAST_ROWS = 256


def _cast_rows(dst_ref, src_ref):
    rows = dst_ref.shape[0]
    step = min(rows, WEIGHT_CAST_ROWS)
    assert rows % step == 0

    def body(c, carry):
        sl = pl.ds(pl.multiple_of(c * step, step), step)
        dst_ref[sl, :] = src_ref[sl, :].astype(BF16)
        return carry

    lax.fori_loop(0, rows // step, body, 0)


def _moe_kernel(te_ref, nu_ref, first_ref, nxt_ref, src_ref, h2_hbm, wg_hbm, wu_hbm, wd_hbm, y_ref,
                xbuf0, xbuf1, xbuf2, sems, wsems, wg32, wu32, wd32, wgb, wub, wdb):
    s = pl.program_id(0)
    nu = nu_ref[0]
    tm = MOE_TILE
    xbufs = (xbuf0, xbuf1, xbuf2)
    nbuf = len(xbufs)
    prev_tile = jnp.maximum(s - 1, 0)

    def weight_copies(e):
        return (pltpu.make_async_copy(wg_hbm.at[e], wg32, wsems.at[0]),
                pltpu.make_async_copy(wu_hbm.at[e], wu32, wsems.at[1]),
                pltpu.make_async_copy(wd_hbm.at[e], wd32, wsems.at[2]))

    def start_gather(tile, slot):
        for r in range(tm):
            tok = src_ref[tile * tm + r]
            pltpu.make_async_copy(h2_hbm.at[pl.ds(tok, 1), :], xbufs[slot].at[pl.ds(r, 1), :],
                                  sems.at[slot]).start()

    def wait_gather(slot):
        pltpu.make_async_copy(h2_hbm.at[pl.ds(0, tm), :], xbufs[slot], sems.at[slot]).wait()

    def ffn(slot):
        x = xbufs[slot][...].astype(BF16)
        a = jnp.dot(x, wgb[...], preferred_element_type=F32)
        u = jnp.dot(x, wub[...], preferred_element_type=F32)
        hid = (a * _sigmoid(a) * u).astype(BF16)
        y_ref[...] = jnp.dot(hid, wdb[...], preferred_element_type=F32)

    @pl.when(s == 0)
    def _():
        for cp in weight_copies(te_ref[0]):
            cp.start()
        start_gather(0, 0)

        @pl.when(nu > 1)
        def _():
            start_gather(1, 1)

    @pl.when((s >= 1) & (s <= nu) & (first_ref[prev_tile] == 1))
    def _():
        for cp in weight_copies(te_ref[prev_tile]):
            cp.wait()
        _cast_rows(wgb, wg32)
        _cast_rows(wub, wu32)
        _cast_rows(wdb, wd32)

        @pl.when(nxt_ref[prev_tile] >= 0)
        def _():
            for cp in weight_copies(nxt_ref[prev_tile]):
                cp.start(priority=1)

    for ph in range(nbuf):
        cur, nxt_buf = (ph - 1) % nbuf, (ph + 1) % nbuf

        @pl.when((s >= 1) & (s + 1 < nu) & (s % nbuf == ph))
        def _(cur=cur, nxt_buf=nxt_buf):
            wait_gather(cur)
            start_gather(s + 1, nxt_buf)
            ffn(cur)

        @pl.when((s >= 1) & (s <= nu) & (s + 1 >= nu) & (s % nbuf == ph))
        def _(cur=cur):
            wait_gather(cur)
            ffn(cur)

    @pl.when(s > nu)
    def _():
        y_ref[...] = jnp.zeros_like(y_ref)


def _moe(tile_expert, n_used, first, nxt, src, h2, wg, wu, wd):
    p = src.shape[0]
    d, f = wg.shape[1:]
    tm = MOE_TILE
    n_tiles = p // tm
    any_spec = pl.BlockSpec(memory_space=pl.ANY)
    return pl.pallas_call(
        _moe_kernel,
        grid_spec=pltpu.PrefetchScalarGridSpec(
            num_scalar_prefetch=5,
            grid=(n_tiles + 1,),
            in_specs=[any_spec, any_spec, any_spec, any_spec],
            out_specs=pl.BlockSpec((tm, d), lambda s, *_: (jnp.maximum(s - 1, 0), 0)),
            scratch_shapes=[pltpu.VMEM((tm, d), F32)] * 3 + [
                            pltpu.SemaphoreType.DMA((3,)), pltpu.SemaphoreType.DMA((3,)),
                            pltpu.VMEM((d, f), F32), pltpu.VMEM((d, f), F32), pltpu.VMEM((f, d), F32),
                            pltpu.VMEM((d, f), BF16), pltpu.VMEM((d, f), BF16),
                            pltpu.VMEM((f, d), BF16)],
        ),
        out_shape=jax.ShapeDtypeStruct((p, d), F32),
        compiler_params=_cparams("arbitrary"),
        name="moe_ffn",
    )(tile_expert, n_used, first, nxt, src, h2, wg, wu, wd)


COMBINE_TILE = 512


def _final_kernel(pos_ref, x1_ref, ys_hbm, rt_ref, g_ref, o_ref, ya0, yb0, ya1, yb1, sems):
    s = pl.program_id(0)
    n_tiles = pl.num_programs(0) - 1
    tm = COMBINE_TILE
    n_tok = pos_ref.shape[0] // 2
    ybufs = ((ya0, yb0), (ya1, yb1))

    def start_gather(tile, slot):
        for k in range(2):
            for r in range(tm):
                row = pos_ref[k * n_tok + tile * tm + r]
                pltpu.make_async_copy(ys_hbm.at[pl.ds(row, 1), :],
                                      ybufs[slot][k].at[pl.ds(r, 1), :], sems.at[slot]).start()

    def wait_gather(slot):
        for k in range(2):
            pltpu.make_async_copy(ys_hbm.at[pl.ds(0, tm), :], ybufs[slot][k], sems.at[slot]).wait()

    def finish(slot):
        rt = rt_ref[...]
        x2 = (x1_ref[...] + rt[:, 2:3] * ybufs[slot][0][...] + rt[:, 3:4] * ybufs[slot][1][...])
        o_ref[...] = (x2 * lax.rsqrt(jnp.mean(x2 * x2, axis=-1, keepdims=True) + RMS_EPS)
                      * g_ref[...])

    @pl.when(s == 0)
    def _():
        start_gather(s, 0)

    for parity in range(2):
        @pl.when((s >= 1) & (s < n_tiles) & (s % 2 == parity))
        def _(parity=parity):
            wait_gather(1 - parity)
            start_gather(s, parity)
            finish(1 - parity)

        @pl.when((s == n_tiles) & (s % 2 == parity))
        def _(parity=parity):
            wait_gather(1 - parity)
            finish(1 - parity)


def _final(x1, ys, pos, rt, g):
    n, d = x1.shape
    tm = COMBINE_TILE
    prev = lambda s: jnp.maximum(s - 1, 0)
    return pl.pallas_call(
        _final_kernel,
        grid_spec=pltpu.PrefetchScalarGridSpec(
            num_scalar_prefetch=1,
            grid=(n // tm + 1,),
            in_specs=[pl.BlockSpec((tm, d), lambda s, p: (prev(s), 0)),
                      pl.BlockSpec(memory_space=pl.ANY),
                      pl.BlockSpec((tm, LANES), lambda s, p: (prev(s), 0)),
                      pl.BlockSpec((1, d), lambda s, p: (0, 0))],
            out_specs=pl.BlockSpec((tm, d), lambda s, p: (prev(s), 0)),
            scratch_shapes=[pltpu.VMEM((tm, d), F32)] * 4 + [pltpu.SemaphoreType.DMA((2,))],
        ),
        out_shape=jax.ShapeDtypeStruct((n, d), F32),
        compiler_params=_cparams("arbitrary"),
        name="combine_norm",
    )(pos, x1, ys, rt, g)


def _dispatch_plan(rt, counts, n_tokens):
    tm = MOE_TILE
    n_rows = 2 * n_tokens + N_EXPERTS * tm
    n_tiles = n_rows // tm
    ids = rt[:, 0:6].astype(jnp.int32)
    ef = jnp.concatenate([ids[:, 0], ids[:, 1]])
    rank = jnp.concatenate([ids[:, 4], ids[:, 5]])
    counts = counts[0, :N_EXPERTS].astype(jnp.int32)
    tiles_per = (counts + tm - 1) // tm
    tile_end = jnp.cumsum(tiles_per)
    row_start = (tile_end - tiles_per) * tm
    pos = row_start[ef] + rank
    tok = jnp.concatenate([jnp.arange(n_tokens, dtype=jnp.int32)] * 2)
    src = jnp.zeros((n_rows,), jnp.int32).at[pos].set(tok, unique_indices=True,
                                                      mode="promise_in_bounds")
    n_used = tile_end[-1]
    tile_ids = jnp.arange(n_tiles, dtype=jnp.int32)
    tile_expert = jnp.sum((tile_ids[:, None] >= tile_end[None, :]).astype(jnp.int32), axis=1)
    last_expert = jnp.sum((n_used - 1 >= tile_end).astype(jnp.int32))
    tile_expert = jnp.where(tile_ids < n_used, tile_expert, last_expert).astype(jnp.int32)
    first = jnp.concatenate([jnp.ones((1,), jnp.int32),
                             (tile_expert[1:] != tile_expert[:-1]).astype(jnp.int32)])
    eid = jnp.arange(N_EXPERTS, dtype=jnp.int32)
    later = (tiles_per[None, :] > 0) & (eid[None, :] > eid[:, None])
    next_expert = jnp.min(jnp.where(later, eid[None, :], N_EXPERTS), axis=1)
    next_expert = jnp.where(next_expert == N_EXPERTS, -1, next_expert).astype(jnp.int32)
    nxt = next_expert[tile_expert]
    return (src, pos[:n_tokens], pos[n_tokens:], tile_expert, first, nxt,
            n_used.reshape(1).astype(jnp.int32))


def kernel(x, norm_mix_g, w_in, hgrn_lb_logits, hgrn_out_norm_g, w_out, norm_ffn_g, w_group_router,
           b_group_router, w_expert_router, b_expert_router, w_gate, w_up, w_down, final_norm_g):
    b, t, d = x.shape
    n = b * t
    depth = w_in.shape[0]
    assert depth == 1, "the final norm is fused into the combine step of the only layer"
    attn_w = d // 2
    n_heads = attn_w // HEAD_DIM
    x2 = x.reshape(n, d)
    lb_all = jnp.cumsum(jax.nn.softmax(hgrn_lb_logits.astype(F32), axis=0), axis=0)[:depth]
    slope2 = jnp.asarray(2.0 ** (-8.0 * np.arange(1, n_heads + 1) / n_heads), dtype=F32) * F32(LOG2E)
    s_hi = slope2.astype(BF16).astype(F32)
    s_mid = (slope2 - s_hi).astype(BF16).astype(F32)
    s_lo = (slope2 - s_hi - s_mid).astype(BF16).astype(F32)
    slope_pieces = jnp.stack([s_hi, s_mid, s_lo])
    proj_scale = jnp.concatenate([jnp.full((attn_w,), HEAD_DIM ** -0.5 * LOG2E, F32),
                                  jnp.ones((w_in.shape[2] - attn_w,), F32)]).reshape(1, -1)

    for l in range(depth):
        g_mix = norm_mix_g[l].reshape(1, d)
        x2, w_in_l, w_out_l = lax.optimization_barrier(
            (x2, w_in[l].astype(BF16), w_out[l].astype(BF16)))
        proj = _inproj(x2, g_mix, w_in_l, 0, proj_scale, BF16).reshape(b, t, -1)
        o_attn = _moba(proj, slope_pieces, n_heads)
        o_rec = _hgrn(proj, 3 * attn_w, lb_all[l].reshape(1, -1),
                      hgrn_out_norm_g[l].reshape(1, HEAD_DIM), n_heads)

        wr = jnp.concatenate(
            [w_group_router[l],
             jnp.transpose(w_expert_router[l], (1, 0, 2)).reshape(d, N_EXPERTS)], axis=1)
        wr = jnp.pad(wr, ((0, 0), (0, LANES - wr.shape[1])))
        wr_hi = wr.astype(BF16)
        wr = jnp.concatenate([wr_hi, (wr - wr_hi.astype(F32)).astype(BF16)], axis=1)
        br =jnp.concatenate([b_group_router[l], b_expert_router[l].reshape(-1)])
        br = jnp.pad(br, (0, LANES - br.shape[0])).reshape(1, LANES)
        x1, h2, rt, counts = _outproj(o_attn.reshape(n, attn_w), o_rec.reshape(n, -1), x2,
                                      w_out_l, norm_ffn_g[l].reshape(1, d), wr, br)
        src, pos1, pos2, tile_expert, first, nxt, n_used = _dispatch_plan(rt, counts, n)
        ys = _moe(tile_expert, n_used, first, nxt, src, h2, w_gate[l], w_up[l], w_down[l])
        x2 = _final(x1, ys, jnp.concatenate([pos1, pos2]), rt, final_norm_g.reshape(1, d))
    return x2.reshape(b, t, d)
```

```python
import functools

import jax
import jax.numpy as jnp
import numpy as np
from jax import lax
from jax.experimental import pallas as pl
from jax.experimental.pallas import tpu as pltpu

F32 = jnp.float32
BF16 = jnp.bfloat16

HEAD_DIM = 128
MOBA_BLOCK = 256
MOBA_TOPK = 3
N_GROUPS = 4
EXPERTS_PER_GROUP = 4
N_EXPERTS = N_GROUPS * EXPERTS_PER_GROUP
RMS_EPS = 1e-6

LANES = 128
VMEM_LIMIT_BYTES = 56 * 1024 * 1024

HGRN_CHUNK = 64
HGRN_SUB = 16
HGRN_STEP = 256
HGRN_HEADS_PER_STEP = 8
HGRN_MIN_SAFE_F = 2.0 ** -3.5
MOE_TILE = 256
INPROJ_TILE = (1024, 1792)
OUTPROJ_TILE = 512
NT_DIMS = (((1,), (1,)), ((), ()))
TN_DIMS = (((0,), (0,)), ((), ()))


def _cparams(*sem):
    return pltpu.CompilerParams(dimension_semantics=sem, vmem_limit_bytes=VMEM_LIMIT_BYTES)


def _inproj_kernel(x_ref, g_ref, w_ref, cs_ref, o_ref, hn_ref):
    @pl.when(pl.program_id(1) == 0)
    def _():
        x = x_ref[...]
        ms = jnp.mean(x * x, axis=-1, keepdims=True)
        hn_ref[...] = (x * lax.rsqrt(ms + RMS_EPS) * g_ref[...]).astype(BF16)

    acc = jnp.dot(hn_ref[...], w_ref[...], preferred_element_type=F32)
    o_ref[...] = (acc * cs_ref[...]).astype(o_ref.dtype)


def _inproj(x2, g, w, col0, colscale, out_dtype):
    n, d = x2.shape
    tm, tn = INPROJ_TILE
    cols = colscale.shape[1]
    j0 = col0 // tn
    return pl.pallas_call(
        _inproj_kernel,
        grid=(n // tm, cols // tn),
        in_specs=[
            pl.BlockSpec((tm, d), lambda i, j: (i, 0)),
            pl.BlockSpec((1, d), lambda i, j: (0, 0)),
            pl.BlockSpec((d, tn), lambda i, j: (0, j0 + j)),
            pl.BlockSpec((1, tn), lambda i, j: (0, j)),
        ],
        out_specs=pl.BlockSpec((tm, tn), lambda i, j: (i, j)),
        out_shape=jax.ShapeDtypeStruct((n, cols), out_dtype),
        scratch_shapes=[pltpu.VMEM((tm, d), BF16)],
        compiler_params=_cparams("parallel", "arbitrary"),
        name="inproj",
    )(x2, g, w, colscale)


MOBA_HEADS_PER_STEP = 4
MOBA_EXTRA_POS = 0
MOBA_EXTRA_SEL = 8
MASK_BIG = 2.0 ** 60
LOG2E = 1.4426950408889634


def _moba_key_extras(t):
    nb = t // MOBA_BLOCK
    assert MOBA_EXTRA_SEL + nb <= LANES and MOBA_BLOCK <= 256
    pos = np.arange(t)
    kx = np.zeros((t, LANES), np.float32)
    kx[:, MOBA_EXTRA_POS:MOBA_EXTRA_POS + 3] = ((pos // MOBA_BLOCK) * MOBA_BLOCK)[:, None]
    kx[:, MOBA_EXTRA_POS + 3:MOBA_EXTRA_POS + 6] = (pos % MOBA_BLOCK)[:, None]
    kx[pos, MOBA_EXTRA_SEL + pos // MOBA_BLOCK] = -MASK_BIG
    return jnp.asarray(kx, dtype=BF16)


def _moba_kernel(sl_ref, q_ref, k_ref, v_ref, kx_ref, o_ref, kaug_ref, qaug_ref, *, nb):
    hg = pl.program_id(1)
    i = pl.program_id(2)
    bs = MOBA_BLOCK
    dh = HEAD_DIM
    t = q_ref.shape[1]
    nbp = -(-nb // 8) * 8
    nx = 8 + nbp

    @pl.when(i == 0)
    def _():
        r8 = lax.broadcasted_iota(jnp.int32, (8, t), 0)
        piece = r8 % 3
        er = lax.broadcasted_iota(jnp.int32, (nx, LANES), 0)
        ec = lax.broadcasted_iota(jnp.int32, (nx, LANES), 1)
        embed = jnp.where(er == ec, 1.0, 0.0).astype(BF16)
        blk = lax.broadcasted_iota(jnp.int32, (nbp, t), 0)
        qblk = lax.broadcasted_iota(jnp.int32, (nbp, t), 1) // bs
        for hh in range(MOBA_HEADS_PER_STEP):
            h = hg * MOBA_HEADS_PER_STEP + hh
            cs = slice(hh * dh, (hh + 1) * dh)
            kaug_ref[hh, :, 0:dh] = k_ref[0, :, cs]
            kaug_ref[hh, :, dh:] = kx_ref[...]
            rows = [jnp.mean(k_ref[0, j * bs:(j + 1) * bs, cs].astype(F32), axis=0, keepdims=True)
                    for j in range(nb)]
            if nbp > nb:
                rows.append(jnp.zeros((nbp - nb, dh), F32))
            km = jnp.concatenate(rows, axis=0)
            hi = km.astype(BF16)
            mid = (km - hi.astype(F32)).astype(BF16)
            lo = (km - hi.astype(F32) - mid.astype(F32)).astype(BF16)
            km4 = jnp.concatenate([hi, mid, lo, jnp.zeros_like(hi)], axis=0)
            q = q_ref[0, :, cs]
            g4 = lax.dot_general(km4, q, NT_DIMS, preferred_element_type=F32)
            gate_t = g4[0:nbp] + g4[nbp:2 * nbp] + g4[2 * nbp:3 * nbp]
            rank = jnp.zeros(gate_t.shape, F32)
            for jp in range(nb - 1):
                other = gate_t[jp:jp + 1, :]
                beats = (other > gate_t) | ((other == gate_t) & (blk > jp))
                rank = rank + jnp.where(beats & (qblk > jp), 1.0, 0.0)
            notsel_t = jnp.where((blk < qblk) & (rank >= MOBA_TOPK), 1.0, 0.0)
            slope_rows = jnp.where(r8 >= 6, 0.0,
                                   jnp.where(piece == 0, sl_ref[0, h],
                                             jnp.where(piece == 1, sl_ref[1, h], sl_ref[2, h])))
            qx_t = jnp.concatenate([slope_rows, notsel_t], axis=0).astype(BF16)
            qx = lax.dot_general(qx_t, embed, TN_DIMS, preferred_element_type=F32)
            qaug_ref[hh, :, 0:dh] = q
            qaug_ref[hh, :, dh:] = qx.astype(BF16)

    rowi = lax.broadcasted_iota(jnp.int32, (bs, bs), 0)
    coli = lax.broadcasted_iota(jnp.int32, (bs, bs), 1)

    for c in range(nb):
        @pl.when(i == c)
        def _(c=c):
            n = (c + 1) * bs
            for hh in range(MOBA_HEADS_PER_STEP):
                cs = slice(hh * dh, (hh + 1) * dh)
                s = lax.dot_general(qaug_ref[hh, c * bs:n, :], kaug_ref[hh, 0:n, :], NT_DIMS,
                                    preferred_element_type=F32)
                s_own = jnp.where(rowi >= coli, s[:, c * bs:], -jnp.inf)
                m = jnp.max(s_own, axis=1, keepdims=True)
                if c > 0:
                    s_past = s[:, :c * bs]
                    m = jnp.maximum(m, jnp.max(s_past, axis=1, keepdims=True))
                p_own = jnp.exp2(s_own - m)
                l = jnp.sum(p_own, axis=1, keepdims=True)
                acc = jnp.dot(p_own.astype(BF16), v_ref[0, c * bs:n, cs],
                              preferred_element_type=F32)
                if c > 0:
                    p_past = jnp.exp2(s_past - m)
                    l = l + jnp.sum(p_past, axis=1, keepdims=True)
                    acc = acc + jnp.dot(p_past.astype(BF16), v_ref[0, 0:c * bs, cs],
                                        preferred_element_type=F32)
                o_ref[0, :, cs] = (acc / l).astype(o_ref.dtype)


def _moba(qkv, slope_pieces, n_heads):
    b, t, _ = qkv.shape
    bs = MOBA_BLOCK
    nb = t // bs
    hps = MOBA_HEADS_PER_STEP
    assert n_heads % hps == 0
    ng = n_heads // hps
    w = hps * HEAD_DIM
    kern = functools.partial(_moba_kernel, nb=nb)
    return pl.pallas_call(
        kern,
        grid=(b, ng, nb),
        in_specs=[
            pl.BlockSpec(memory_space=pltpu.SMEM),
            pl.BlockSpec((1, t, w), lambda bi, hi, qi: (bi, 0, hi)),
            pl.BlockSpec((1, t, w), lambda bi, hi, qi: (bi, 0, ng + hi)),
            pl.BlockSpec((1, t, w), lambda bi, hi, qi: (bi, 0, 2 * ng + hi)),
            pl.BlockSpec((t, LANES), lambda bi, hi, qi: (0, 0)),
        ],
        out_specs=pl.BlockSpec((1, bs, w), lambda bi, hi, qi: (bi, qi, hi)),
        out_shape=jax.ShapeDtypeStruct((b, t, n_heads * HEAD_DIM), BF16),
        scratch_shapes=[
            pltpu.VMEM((hps, t, HEAD_DIM + LANES), BF16),
            pltpu.VMEM((hps, t, HEAD_DIM + LANES), BF16),
        ],
        compiler_params=_cparams("parallel", "parallel", "arbitrary"),
        name="moba",
    )(slope_pieces, qkv, qkv, qkv, _moba_key_extras(t))


def _sigmoid(x):
    return 1.0 / (1.0 + jnp.exp(-x))


def _hgrn_chunk(q, f, iv, g, gn, st, bounded_decay):
    c = HGRN_CHUNK
    sub = HGRN_SUB
    half = sub // 2
    nsub = c // sub
    qf = q * _sigmoid(q)
    kf = jnp.maximum(1.0 - f, 0.0)
    b2 = jnp.log2(f)
    rowc = lax.broadcasted_iota(jnp.int32, b2.shape, 0)
    shift = 1
    while shift < c:
        b2 = b2 + jnp.where(rowc >= shift, pltpu.roll(b2, shift, 0), 0.0)
        shift *= 2
    c2 = b2 - jnp.log2(kf)

    inter = lax.dot_general((qf * jnp.exp2(b2)).astype(BF16), st.astype(BF16), NT_DIMS,
                            preferred_element_type=F32)

    lane = lax.broadcasted_iota(jnp.int32, (sub, LANES), 1)
    tsub = lax.broadcasted_iota(jnp.int32, (sub, LANES), 0)
    colid = lax.broadcasted_iota(jnp.int32, (sub, c), 1)
    ones = jnp.ones((LANES, LANES), BF16)
    a_rows = []
    if bounded_decay:
        b_mid = b2[c // 2 - 1:c // 2, :]
        qt = qf * jnp.exp2(b2 - b_mid)
        kt = jnp.exp2(b_mid - c2)
        pair = lax.dot_general(qt.astype(BF16), kt.astype(BF16), NT_DIMS,
                               preferred_element_type=F32)
        rr = lax.broadcasted_iota(jnp.int32, (c, c), 0)
        rc = lax.broadcasted_iota(jnp.int32, (c, c), 1)
        a_rows.append(jnp.where(rr >= rc, pair, 0.0))
    for bi in range(0 if bounded_decay else nsub):
        lo = bi * sub
        b_i = b2[lo:lo + sub]
        c_i = c2[lo:lo + sub]
        q_i = qf[lo:lo + sub]
        pieces = []
        for s in range(sub):
            if s < half:
                pieces.append(q_i * jnp.exp2(b_i - c_i[s:s + 1, :]))
            else:
                pieces.append(q_i[half:] * jnp.exp2(b_i[half:] - c_i[s:s + 1, :]))
        pm = jnp.concatenate(pieces, axis=0).astype(BF16)
        rs = jnp.dot(pm, ones, preferred_element_type=F32)
        rel = lane - lo
        key = jnp.where((rel >= 0) & (rel <= tsub), rel, -1)
        key_lo, key_hi = key[:half], key[half:]
        a_lo = jnp.zeros((half, LANES), F32)
        a_hi = jnp.zeros((half, LANES), F32)
        off = 0
        for s in range(sub):
            if s < half:
                a_lo = jnp.where(key_lo == s, rs[off:off + half], a_lo)
                a_hi = jnp.where(key_hi == s, rs[off + half:off + sub], a_hi)
                off += sub
            else:
                a_hi = jnp.where(key_hi == s, rs[off:off + half], a_hi)
                off += half
        a_blk = jnp.concatenate([a_lo, a_hi], axis=0)[:, :c]
        if bi > 0:
            b0 = b2[lo - 1:lo, :]
            qt = q_i * jnp.exp2(b_i - b0)
            kt = jnp.exp2(jnp.minimum(b0 - c2, 0.0))
            cross = lax.dot_general(qt.astype(BF16), kt.astype(BF16), NT_DIMS,
                                    preferred_element_type=F32)
            a_blk = a_blk + jnp.where(colid < lo, cross, 0.0)
        a_rows.append(a_blk)
    a = jnp.concatenate(a_rows, axis=0)
    intra = jnp.dot(a.astype(BF16), iv.astype(BF16), preferred_element_type=F32)

    b_last = b2[c - 1:c, :]
    khat = jnp.exp2(b_last - c2)
    st_new = st * jnp.exp2(b_last) + lax.dot_general(
        iv.astype(BF16), khat.astype(BF16), TN_DIMS, preferred_element_type=F32)

    o = inter + intra
    y = o * lax.rsqrt(jnp.mean(o * o, axis=-1, keepdims=True) + RMS_EPS) * gn
    return y * (g * _sigmoid(g)), st_new


def _hgrn_kernel(q_ref, f_ref, i_ref, g_ref, lb_ref, gn_ref, o_ref, st_ref):
    @pl.when(pl.program_id(2) == 0)
    def _():
        st_ref[...] = jnp.zeros_like(st_ref)

    gn = gn_ref[...]
    d = HEAD_DIM
    lb = lb_ref[...]
    f_all = lb + (1.0 - lb) * _sigmoid(f_ref[0].astype(F32))
    f_min = jnp.min(f_all)

    def run(bounded_decay):
        sts = [st_ref[hh] for hh in range(HGRN_HEADS_PER_STEP)]
        for ci in range(HGRN_STEP // HGRN_CHUNK):
            sl = slice(ci * HGRN_CHUNK, (ci + 1) * HGRN_CHUNK)
            for hh in range(HGRN_HEADS_PER_STEP):
                cs = slice(hh * d, (hh + 1) * d)
                out, sts[hh] = _hgrn_chunk(q_ref[0, sl, cs].astype(F32), f_all[sl, cs],
                                           i_ref[0, sl, cs], g_ref[0, sl, cs].astype(F32), gn,
                                           sts[hh], bounded_decay)
                o_ref[0, sl, cs] = out.astype(o_ref.dtype)
        for hh in range(HGRN_HEADS_PER_STEP):
            st_ref[hh] = sts[hh]

    @pl.when(f_min >= HGRN_MIN_SAFE_F)
    def _():
        run(True)

    @pl.when(jnp.logical_not(f_min >= HGRN_MIN_SAFE_F))
    def _():
        run(False)


def _hgrn(hp, col0, lb, gn, n_heads):
    b, t, _ = hp.shape
    d = HEAD_DIM
    ts = HGRN_STEP
    hps = HGRN_HEADS_PER_STEP
    assert n_heads % hps == 0 and col0 % (hps * d) == 0
    ng = n_heads // hps
    blk0 = col0 // (hps * d)

    def col(group):
        return pl.BlockSpec((1, ts, hps * d), lambda bi, hi, ti: (bi, ti, blk0 + group * ng + hi))

    return pl.pallas_call(
        _hgrn_kernel,
        grid=(b, ng, t // ts),
        in_specs=[col(0), col(1), col(2), col(3),
                  pl.BlockSpec((1, hps * d), lambda bi, hi, ti: (0, hi)),
                  pl.BlockSpec((1, d), lambda bi, hi, ti: (0, 0))],
        out_specs=pl.BlockSpec((1, ts, hps * d), lambda bi, hi, ti: (bi, ti, hi)),
        out_shape=jax.ShapeDtypeStruct((b, t, n_heads * d), BF16),
        scratch_shapes=[pltpu.VMEM((hps, d, d), F32)],
        compiler_params=_cparams("parallel", "parallel", "arbitrary"),
        name="hgrn2",
    )(hp, hp, hp, hp, lb, gn)


def _outproj_kernel(oa_ref, or_ref, x_ref, w_ref, g_ref, wr_ref, br_ref,
                    x1_ref, h2_ref, rt_ref, cnt_out_ref, cnt_ref):
    wa = oa_ref.shape[1]
    x1 = (x_ref[...]
          + jnp.dot(oa_ref[...], w_ref[0:wa, :], preferred_element_type=F32)
          + jnp.dot(or_ref[...], w_ref[wa:, :], preferred_element_type=F32))
    x1_ref[...] = x1
    h2 = x1 * lax.rsqrt(jnp.mean(x1 * x1, axis=-1, keepdims=True) + RMS_EPS) * g_ref[...]
    h2_ref[...] = h2

    h_hi = h2.astype(BF16)
    h_mid = (h2 - h_hi.astype(F32)).astype(BF16)
    part = jnp.dot(h_hi, wr_ref[...], preferred_element_type=F32)
    logits = (part[:, :LANES] + part[:, LANES:] + br_ref[...]
              + jnp.dot(h_mid, wr_ref[:, :LANES], preferred_element_type=F32))
    lane = lax.broadcasted_iota(jnp.int32, logits.shape, 1)
    big = jnp.int32(4 * LANES)
    ninf = -jnp.inf

    lg = jnp.where(lane < N_GROUPS, logits, ninf)
    mg = jnp.max(lg, axis=1, keepdims=True)
    gidx = jnp.min(jnp.where(lg == mg, lane, big), axis=1, keepdims=True)
    grp_w = 1.0 / jnp.sum(jnp.exp(lg - mg), axis=1, keepdims=True)

    lo = N_GROUPS + EXPERTS_PER_GROUP * gidx
    le = jnp.where((lane >= lo) & (lane < lo + EXPERTS_PER_GROUP), logits, ninf)
    m1 = jnp.max(le, axis=1, keepdims=True)
    i1 = jnp.min(jnp.where(le == m1, lane, big), axis=1, keepdims=True)
    le2 = jnp.where(lane == i1, ninf, le)
    m2 = jnp.max(le2, axis=1, keepdims=True)
    i2 = jnp.min(jnp.where(le2 == m2, lane, big), axis=1, keepdims=True)
    r21 = jnp.exp(m2 - m1)
    w1 = grp_w / (1.0 + r21)
    w2 = grp_w * r21 / (1.0 + r21)
    e1 = i1 - N_GROUPS
    e2 = i2 - N_GROUPS

    @pl.when(pl.program_id(0) == 0)
    def _():
        cnt_ref[...] = jnp.zeros_like(cnt_ref)

    tm = logits.shape[0]
    onehot = jnp.where((lane == e1) | (lane == e2), 1.0, 0.0)
    rr = lax.broadcasted_iota(jnp.int32, (tm, tm), 0)
    rc = lax.broadcasted_iota(jnp.int32, (tm, tm), 1)
    before = jnp.where(rr > rc, 1.0, 0.0).astype(BF16)
    prefix = (jnp.dot(before, onehot.astype(BF16), preferred_element_type=F32) + cnt_ref[0:1, :])
    rank1 = jnp.sum(jnp.where(lane == e1, prefix, 0.0), axis=1, keepdims=True)
    rank2 = jnp.sum(jnp.where(lane == e2, prefix, 0.0), axis=1, keepdims=True)
    total = cnt_ref[0:1, :] + jnp.sum(onehot, axis=0, keepdims=True)
    cnt_ref[0:1, :] = total
    cnt_out_ref[...] = jnp.broadcast_to(total, cnt_out_ref.shape)

    cols = [e1.astype(F32), e2.astype(F32), w1, w2, rank1, rank2]
    rt = jnp.zeros(logits.shape, F32)
    for ci, cv in enumerate(cols):
        rt = jnp.where(lane == ci, cv, rt)
    rt_ref[...] = rt


def _outproj(oa, orec, x2, w_out, g, wr, br):
    n, d = x2.shape
    tm = OUTPROJ_TILE
    wa = oa.shape[1]
    wrc = orec.shape[1]
    row = lambda i: (i, 0)
    const = lambda i: (0, 0)
    return pl.pallas_call(
        _outproj_kernel,
        grid=(n // tm,),
        in_specs=[
            pl.BlockSpec((tm, wa), row),
            pl.BlockSpec((tm, wrc), row),
            pl.BlockSpec((tm, d), row),
            pl.BlockSpec((wa + wrc, d), const),
            pl.BlockSpec((1, d), const),
            pl.BlockSpec((d, 2 * LANES), const),
            pl.BlockSpec((1, LANES), const),
        ],
        out_specs=[pl.BlockSpec((tm, d), row), pl.BlockSpec((tm, d), row),
                   pl.BlockSpec((tm, LANES), row), pl.BlockSpec((8, LANES), const)],
        out_shape=[jax.ShapeDtypeStruct((n, d), F32), jax.ShapeDtypeStruct((n, d), F32),
                   jax.ShapeDtypeStruct((n, LANES), F32), jax.ShapeDtypeStruct((8, LANES), F32)],
        scratch_shapes=[pltpu.VMEM((8, LANES), F32)],
        compiler_params=_cparams("arbitrary"),
        name="outproj_route",
    )(oa, orec, x2, w_out, g, wr, br)


WEIGHT_CAST_ROWS = 256


def _cast_rows(dst_ref, src_ref):
    rows = dst_ref.shape[0]
    step = min(rows, WEIGHT_CAST_ROWS)
    assert rows % step == 0

    def body(c, carry):
        sl = pl.ds(pl.multiple_of(c * step, step), step)
        dst_ref[sl, :] = src_ref[sl, :].astype(BF16)
        return carry

    lax.fori_loop(0, rows // step, body, 0)


def _moe_kernel(te_ref, nu_ref, first_ref, nxt_ref, src_ref, h2_hbm, wg_hbm, wu_hbm, wd_hbm, y_ref,
                xbuf0, xbuf1, xbuf2, xbuf3, sems, wsems, wg32, wu32, wd32, wgb, wub, wdb):
    s = pl.program_id(0)
    nu = nu_ref[0]
    tm = MOE_TILE
    xbufs = (xbuf0, xbuf1, xbuf2, xbuf3)
    nbuf = len(xbufs)
    ahead = nbuf - 1
    prev_tile = jnp.maximum(s - 1, 0)

    def weight_copies(e):
        return (pltpu.make_async_copy(wg_hbm.at[e], wg32, wsems.at[0]),
                pltpu.make_async_copy(wu_hbm.at[e], wu32, wsems.at[1]),
                pltpu.make_async_copy(wd_hbm.at[e], wd32, wsems.at[2]))

    def start_gather(tile, slot):
        for r in range(tm):
            tok = src_ref[tile * tm + r]
            pltpu.make_async_copy(h2_hbm.at[pl.ds(tok, 1), :], xbufs[slot].at[pl.ds(r, 1), :],
                                  sems.at[slot]).start()

    def wait_gather(slot):
        pltpu.make_async_copy(h2_hbm.at[pl.ds(0, tm), :], xbufs[slot], sems.at[slot]).wait()

    def ffn(slot):
        x = xbufs[slot][...].astype(BF16)
        a = jnp.dot(x, wgb[...], preferred_element_type=F32)
        u = jnp.dot(x, wub[...], preferred_element_type=F32)
        hid = (a * _sigmoid(a) * u).astype(BF16)
        y_ref[...] = jnp.dot(hid, wdb[...], preferred_element_type=F32)

    @pl.when(s == 0)
    def _():
        for cp in weight_copies(te_ref[0]):
            cp.start()
        start_gather(0, 0)
        for t0 in range(1, ahead):
            @pl.when(nu > t0)
            def _(t0=t0):
                start_gather(t0, t0)

    @pl.when((s >= 1) & (s <= nu) & (first_ref[prev_tile] == 1))
    def _():
        for cp in weight_copies(te_ref[prev_tile]):
            cp.wait()
        _cast_rows(wgb, wg32)
        _cast_rows(wub, wu32)
        _cast_rows(wdb, wd32)

        @pl.when(nxt_ref[prev_tile] >= 0)
        def _():
            for cp in weight_copies(nxt_ref[prev_tile]):
                cp.start(priority=1)

    for ph in range(nbuf):
        cur, nxt_buf = (ph - 1) % nbuf, (ph - 2) % nbuf
        fetch = s - 1 + ahead

        @pl.when((s >= 1) & (fetch < nu) & (s % nbuf == ph))
        def _(cur=cur, nxt_buf=nxt_buf):
            wait_gather(cur)
            start_gather(fetch, nxt_buf)
            ffn(cur)

        @pl.when((s >= 1) & (s <= nu) & (fetch >= nu) & (s % nbuf == ph))
        def _(cur=cur):
            wait_gather(cur)
            ffn(cur)

    @pl.when(s > nu)
    def _():
        y_ref[...] = jnp.zeros_like(y_ref)


def _moe(tile_expert, n_used, first, nxt, src, h2, wg, wu, wd):
    p = src.shape[0]
    d, f = wg.shape[1:]
    tm = MOE_TILE
    n_tiles = p // tm
    any_spec = pl.BlockSpec(memory_space=pl.ANY)
    return pl.pallas_call(
        _moe_kernel,
        grid_spec=pltpu.PrefetchScalarGridSpec(
            num_scalar_prefetch=5,
            grid=(n_tiles + 1,),
            in_specs=[any_spec, any_spec, any_spec, any_spec],
            out_specs=pl.BlockSpec((tm, d), lambda s, *_: (jnp.maximum(s - 1, 0), 0)),
            scratch_shapes=[pltpu.VMEM((tm, d), F32)] * 4 + [
                            pltpu.SemaphoreType.DMA((4,)), pltpu.SemaphoreType.DMA((3,)),
                            pltpu.VMEM((d, f), F32), pltpu.VMEM((d, f), F32), pltpu.VMEM((f, d), F32),
                            pltpu.VMEM((d, f), BF16), pltpu.VMEM((d, f), BF16),
                            pltpu.VMEM((f, d), BF16)],
        ),
        out_shape=jax.ShapeDtypeStruct((p, d), F32),
        compiler_params=_cparams("arbitrary"),
        name="moe_ffn",
    )(tile_expert, n_used, first, nxt, src, h2, wg, wu, wd)


COMBINE_TILE = 512


def _final_kernel(pos_ref, x1_ref, ys_hbm, rt_ref, g_ref, o_ref, ya0, yb0, ya1, yb1, sems):
    s = pl.program_id(0)
    n_tiles = pl.num_programs(0) - 1
    tm = COMBINE_TILE
    n_tok = pos_ref.shape[0] // 2
    ybufs = ((ya0, yb0), (ya1, yb1))

    def start_gather(tile, slot):
        for k in range(2):
            for r in range(tm):
                row = pos_ref[k * n_tok + tile * tm + r]
                pltpu.make_async_copy(ys_hbm.at[pl.ds(row, 1), :],
                                      ybufs[slot][k].at[pl.ds(r, 1), :], sems.at[slot]).start()

    def wait_gather(slot):
        for k in range(2):
            pltpu.make_async_copy(ys_hbm.at[pl.ds(0, tm), :], ybufs[slot][k], sems.at[slot]).wait()

    def finish(slot):
        rt = rt_ref[...]
        x2 = (x1_ref[...] + rt[:, 2:3] * ybufs[slot][0][...] + rt[:, 3:4] * ybufs[slot][1][...])
        o_ref[...] = (x2 * lax.rsqrt(jnp.mean(x2 * x2, axis=-1, keepdims=True) + RMS_EPS)
                      * g_ref[...])

    @pl.when(s == 0)
    def _():
        start_gather(s, 0)

    for parity in range(2):
        @pl.when((s >= 1) & (s < n_tiles) & (s % 2 == parity))
        def _(parity=parity):
            wait_gather(1 - parity)
            start_gather(s, parity)
            finish(1 - parity)

        @pl.when((s == n_tiles) & (s % 2 == parity))
        def _(parity=parity):
            wait_gather(1 - parity)
            finish(1 - parity)


def _final(x1, ys, pos, rt, g):
    n, d = x1.shape
    tm = COMBINE_TILE
    prev = lambda s: jnp.maximum(s - 1, 0)
    return pl.pallas_call(
        _final_kernel,
        grid_spec=pltpu.PrefetchScalarGridSpec(
            num_scalar_prefetch=1,
            grid=(n // tm + 1,),
            in_specs=[pl.BlockSpec((tm, d), lambda s, p: (prev(s), 0)),
                      pl.BlockSpec(memory_space=pl.ANY),
                      pl.BlockSpec((tm, LANES), lambda s, p: (prev(s), 0)),
                      pl.BlockSpec((1, d), lambda s, p: (0, 0))],
            out_specs=pl.BlockSpec((tm, d), lambda s, p: (prev(s), 0)),
            scratch_shapes=[pltpu.VMEM((tm, d), F32)] * 4 + [pltpu.SemaphoreType.DMA((2,))],
        ),
        out_shape=jax.ShapeDtypeStruct((n, d), F32),
        compiler_params=_cparams("arbitrary"),
        name="combine_norm",
    )(pos, x1, ys, rt, g)


def _dispatch_plan(rt, counts, n_tokens):
    tm = MOE_TILE
    n_rows = 2 * n_tokens + N_EXPERTS * tm
    n_tiles = n_rows // tm
    ids = rt[:, 0:6].astype(jnp.int32)
    ef = jnp.concatenate([ids[:, 0], ids[:, 1]])
    rank = jnp.concatenate([ids[:, 4], ids[:, 5]])
    counts = counts[0, :N_EXPERTS].astype(jnp.int32)
    tiles_per = (counts + tm - 1) // tm
    tile_end = jnp.cumsum(tiles_per)
    row_start = (tile_end - tiles_per) * tm
    pos = row_start[ef] + rank
    tok = jnp.concatenate([jnp.arange(n_tokens, dtype=jnp.int32)] * 2)
    src = jnp.zeros((n_rows,), jnp.int32).at[pos].set(tok, unique_indices=True,
                                                      mode="promise_in_bounds")
    n_used = tile_end[-1]
    tile_ids = jnp.arange(n_tiles, dtype=jnp.int32)
    tile_expert = jnp.sum((tile_ids[:, None] >= tile_end[None, :]).astype(jnp.int32), axis=1)
    last_expert = jnp.sum((n_used - 1 >= tile_end).astype(jnp.int32))
    tile_expert = jnp.where(tile_ids < n_used, tile_expert, last_expert).astype(jnp.int32)
    first = jnp.concatenate([jnp.ones((1,), jnp.int32),
                             (tile_expert[1:] != tile_expert[:-1]).astype(jnp.int32)])
    eid = jnp.arange(N_EXPERTS, dtype=jnp.int32)
    later = (tiles_per[None, :] > 0) & (eid[None, :] > eid[:, None])
    next_expert = jnp.min(jnp.where(later, eid[None, :], N_EXPERTS), axis=1)
    next_expert = jnp.where(next_expert == N_EXPERTS, -1, next_expert).astype(jnp.int32)
    nxt = next_expert[tile_expert]
    return (src, pos[:n_tokens], pos[n_tokens:], tile_expert, first, nxt,
            n_used.reshape(1).astype(jnp.int32))


def kernel(x, norm_mix_g, w_in, hgrn_lb_logits, hgrn_out_norm_g, w_out, norm_ffn_g, w_group_router,
           b_group_router, w_expert_router, b_expert_router, w_gate, w_up, w_down, final_norm_g):
    b, t, d = x.shape
    n = b * t
    depth = w_in.shape[0]
    assert depth == 1, "the final norm is fused into the combine step of the only layer"
    attn_w = d // 2
    n_heads = attn_w // HEAD_DIM
    x2 = x.reshape(n, d)
    lb_all = jnp.cumsum(jax.nn.softmax(hgrn_lb_logits.astype(F32), axis=0), axis=0)[:depth]
    slope2 = jnp.asarray(2.0 ** (-8.0 * np.arange(1, n_heads + 1) / n_heads), dtype=F32) * F32(LOG2E)
    s_hi = slope2.astype(BF16).astype(F32)
    s_mid = (slope2 - s_hi).astype(BF16).astype(F32)
    s_lo = (slope2 - s_hi - s_mid).astype(BF16).astype(F32)
    slope_pieces = jnp.stack([s_hi, s_mid, s_lo])
    proj_scale = jnp.concatenate([jnp.full((attn_w,), HEAD_DIM ** -0.5 * LOG2E, F32),
                                  jnp.ones((w_in.shape[2] - attn_w,), F32)]).reshape(1, -1)

    for l in range(depth):
        g_mix = norm_mix_g[l].reshape(1, d)
        x2, w_in_l, w_out_l = lax.optimization_barrier(
            (x2, w_in[l].astype(BF16), w_out[l].astype(BF16)))
        proj = _inproj(x2, g_mix, w_in_l, 0, proj_scale, BF16).reshape(b, t, -1)
        o_attn = _moba(proj, slope_pieces, n_heads)
        o_rec = _hgrn(proj, 3 * attn_w, lb_all[l].reshape(1, -1),
                      hgrn_out_norm_g[l].reshape(1, HEAD_DIM), n_heads)

        wr = jnp.concatenate(
            [w_group_router[l],
             jnp.transpose(w_expert_router[l], (1, 0, 2)).reshape(d, N_EXPERTS)], axis=1)
        wr = jnp.pad(wr, ((0, 0), (0, LANES - wr.shape[1])))
        wr_hi = wr.astype(BF16)
        wr = jnp.concatenate([wr_hi, (wr - wr_hi.astype(F32)).astype(BF16)], axis=1)
        br =jnp.concatenate([b_group_router[l], b_expert_router[l].reshape(-1)])
        br = jnp.pad(br, (0, LANES - br.shape[0])).reshape(1, LANES)
        x1, h2, rt, counts = _outproj(o_attn.reshape(n, attn_w), o_rec.reshape(n, -1), x2,
                                      w_out_l, norm_ffn_g[l].reshape(1, d), wr, br)
        src, pos1, pos2, tile_expert, first, nxt, n_used = _dispatch_plan(rt, counts, n)
        ys = _moe(tile_expert, n_used, first, nxt, src, h2, w_gate[l], w_up[l], w_down[l])
        x2 = _final(x1, ys, jnp.concatenate([pos1, pos2]), rt, final_norm_g.reshape(1, d))
    return x2.reshape(b, t, d)
```

```python
import functools

import jax
import jax.numpy as jnp
import numpy as np
from jax import lax
from jax.experimental import pallas as pl
from jax.experimental.pallas import tpu as pltpu

F32 = jnp.float32
BF16 = jnp.bfloat16

HEAD_DIM = 128
MOBA_BLOCK = 256
MOBA_TOPK = 3
N_GROUPS = 4
EXPERTS_PER_GROUP = 4
N_EXPERTS = N_GROUPS * EXPERTS_PER_GROUP
RMS_EPS = 1e-6

LANES = 128
VMEM_LIMIT_BYTES = 56 * 1024 * 1024

HGRN_CHUNK = 64
HGRN_SUB = 16
HGRN_STEP = 256
HGRN_HEADS_PER_STEP = 8
HGRN_MIN_SAFE_F = 2.0 ** -3.5
MOE_TILE = 256
INPROJ_TILE = (1024, 1792)
OUTPROJ_TILE = 512
NT_DIMS = (((1,), (1,)), ((), ()))
TN_DIMS = (((0,), (0,)), ((), ()))


def _cparams(*sem):
    return pltpu.CompilerParams(dimension_semantics=sem, vmem_limit_bytes=VMEM_LIMIT_BYTES)


def _inproj_kernel(x_ref, g_ref, w_ref, cs_ref, o_ref, hn_ref):
    @pl.when(pl.program_id(1) == 0)
    def _():
        x = x_ref[...]
        ms = jnp.mean(x * x, axis=-1, keepdims=True)
        hn_ref[...] = (x * lax.rsqrt(ms + RMS_EPS) * g_ref[...]).astype(BF16)

    acc = jnp.dot(hn_ref[...], w_ref[...], preferred_element_type=F32)
    o_ref[...] = (acc * cs_ref[...]).astype(o_ref.dtype)


def _inproj(x2, g, w, col0, colscale, out_dtype):
    n, d = x2.shape
    tm, tn = INPROJ_TILE
    cols = colscale.shape[1]
    j0 = col0 // tn
    return pl.pallas_call(
        _inproj_kernel,
        grid=(n // tm, cols // tn),
        in_specs=[
            pl.BlockSpec((tm, d), lambda i, j: (i, 0)),
            pl.BlockSpec((1, d), lambda i, j: (0, 0)),
            pl.BlockSpec((d, tn), lambda i, j: (0, j0 + j)),
            pl.BlockSpec((1, tn), lambda i, j: (0, j)),
        ],
        out_specs=pl.BlockSpec((tm, tn), lambda i, j: (i, j)),
        out_shape=jax.ShapeDtypeStruct((n, cols), out_dtype),
        scratch_shapes=[pltpu.VMEM((tm, d), BF16)],
        compiler_params=_cparams("parallel", "arbitrary"),
        name="inproj",
    )(x2, g, w, colscale)


MOBA_HEADS_PER_STEP = 4
MOBA_EXTRA_POS = 0
MOBA_EXTRA_SEL = 8
MASK_BIG = 2.0 ** 60
LOG2E = 1.4426950408889634


def _moba_key_extras(t):
    nb = t // MOBA_BLOCK
    assert MOBA_EXTRA_SEL + nb <= LANES and MOBA_BLOCK <= 256
    pos = np.arange(t)
    kx = np.zeros((t, LANES), np.float32)
    kx[:, MOBA_EXTRA_POS:MOBA_EXTRA_POS + 3] = ((pos // MOBA_BLOCK) * MOBA_BLOCK)[:, None]
    kx[:, MOBA_EXTRA_POS + 3:MOBA_EXTRA_POS + 6] = (pos % MOBA_BLOCK)[:, None]
    kx[pos, MOBA_EXTRA_SEL + pos // MOBA_BLOCK] = -MASK_BIG
    return jnp.asarray(kx, dtype=BF16)


def _moba_kernel(sl_ref, q_ref, k_ref, v_ref, kx_ref, o_ref, kaug_ref, qaug_ref, *, nb):
    hg = pl.program_id(1)
    i = pl.program_id(2)
    bs = MOBA_BLOCK
    dh = HEAD_DIM
    t = q_ref.shape[1]
    nbp = -(-nb // 8) * 8
    nx = 8 + nbp

    @pl.when(i == 0)
    def _():
        r8 = lax.broadcasted_iota(jnp.int32, (8, t), 0)
        piece = r8 % 3
        er = lax.broadcasted_iota(jnp.int32, (nx, LANES), 0)
        ec = lax.broadcasted_iota(jnp.int32, (nx, LANES), 1)
        embed = jnp.where(er == ec, 1.0, 0.0).astype(BF16)
        blk = lax.broadcasted_iota(jnp.int32, (nbp, t), 0)
        qblk = lax.broadcasted_iota(jnp.int32, (nbp, t), 1) // bs
        for hh in range(MOBA_HEADS_PER_STEP):
            h = hg * MOBA_HEADS_PER_STEP + hh
            cs = slice(hh * dh, (hh + 1) * dh)
            kaug_ref[hh, :, 0:dh] = k_ref[0, :, cs]
            kaug_ref[hh, :, dh:] = kx_ref[...]
            rows = [jnp.mean(k_ref[0, j * bs:(j + 1) * bs, cs].astype(F32), axis=0, keepdims=True)
                    for j in range(nb)]
            if nbp > nb:
                rows.append(jnp.zeros((nbp - nb, dh), F32))
            km = jnp.concatenate(rows, axis=0)
            hi = km.astype(BF16)
            mid = (km - hi.astype(F32)).astype(BF16)
            lo = (km - hi.astype(F32) - mid.astype(F32)).astype(BF16)
            km4 = jnp.concatenate([hi, mid, lo, jnp.zeros_like(hi)], axis=0)
            q = q_ref[0, :, cs]
            g4 = lax.dot_general(km4, q, NT_DIMS, preferred_element_type=F32)
            gate_t = g4[0:nbp] + g4[nbp:2 * nbp] + g4[2 * nbp:3 * nbp]
            rank = jnp.zeros(gate_t.shape, F32)
            for jp in range(nb - 1):
                other = gate_t[jp:jp + 1, :]
                beats = (other > gate_t) | ((other == gate_t) & (blk > jp))
                rank = rank + jnp.where(beats & (qblk > jp), 1.0, 0.0)
            notsel_t = jnp.where((blk < qblk) & (rank >= MOBA_TOPK), 1.0, 0.0)
            slope_rows = jnp.where(r8 >= 6, 0.0,
                                   jnp.where(piece == 0, sl_ref[0, h],
                                             jnp.where(piece == 1, sl_ref[1, h], sl_ref[2, h])))
            qx_t = jnp.concatenate([slope_rows, notsel_t], axis=0).astype(BF16)
            qx = lax.dot_general(qx_t, embed, TN_DIMS, preferred_element_type=F32)
            qaug_ref[hh, :, 0:dh] = q
            qaug_ref[hh, :, dh:] = qx.astype(BF16)

    rowi = lax.broadcasted_iota(jnp.int32, (bs, bs), 0)
    coli = lax.broadcasted_iota(jnp.int32, (bs, bs), 1)

    for c in range(nb):
        @pl.when(i == c)
        def _(c=c):
            n = (c + 1) * bs
            for hh in range(MOBA_HEADS_PER_STEP):
                cs = slice(hh * dh, (hh + 1) * dh)
                s = lax.dot_general(qaug_ref[hh, c * bs:n, :], kaug_ref[hh, 0:n, :], NT_DIMS,
                                    preferred_element_type=F32)
                s_own = jnp.where(rowi >= coli, s[:, c * bs:], -jnp.inf)
                m = jnp.max(s_own, axis=1, keepdims=True)
                if c > 0:
                    s_past = s[:, :c * bs]
                    m = jnp.maximum(m, jnp.max(s_past, axis=1, keepdims=True))
                p_own = jnp.exp2(s_own - m)
                l = jnp.sum(p_own, axis=1, keepdims=True)
                acc = jnp.dot(p_own.astype(BF16), v_ref[0, c * bs:n, cs],
                              preferred_element_type=F32)
                if c > 0:
                    p_past = jnp.exp2(s_past - m)
                    l = l + jnp.sum(p_past, axis=1, keepdims=True)
                    acc = acc + jnp.dot(p_past.astype(BF16), v_ref[0, 0:c * bs, cs],
                                        preferred_element_type=F32)
                o_ref[0, :, cs] = (acc / l).astype(o_ref.dtype)


def _moba(qkv, slope_pieces, n_heads):
    b, t, _ = qkv.shape
    bs = MOBA_BLOCK
    nb = t // bs
    hps = MOBA_HEADS_PER_STEP
    assert n_heads % hps == 0
    ng = n_heads // hps
    w = hps * HEAD_DIM
    kern = functools.partial(_moba_kernel, nb=nb)
    return pl.pallas_call(
        kern,
        grid=(b, ng, nb),
        in_specs=[
            pl.BlockSpec(memory_space=pltpu.SMEM),
            pl.BlockSpec((1, t, w), lambda bi, hi, qi: (bi, 0, hi)),
            pl.BlockSpec((1, t, w), lambda bi, hi, qi: (bi, 0, ng + hi)),
            pl.BlockSpec((1, t, w), lambda bi, hi, qi: (bi, 0, 2 * ng + hi)),
            pl.BlockSpec((t, LANES), lambda bi, hi, qi: (0, 0)),
        ],
        out_specs=pl.BlockSpec((1, bs, w), lambda bi, hi, qi: (bi, qi, hi)),
        out_shape=jax.ShapeDtypeStruct((b, t, n_heads * HEAD_DIM), BF16),
        scratch_shapes=[
            pltpu.VMEM((hps, t, HEAD_DIM + LANES), BF16),
            pltpu.VMEM((hps, t, HEAD_DIM + LANES), BF16),
        ],
        compiler_params=_cparams("parallel", "parallel", "arbitrary"),
        name="moba",
    )(slope_pieces, qkv, qkv, qkv, _moba_key_extras(t))


def _sigmoid(x):
    return 1.0 / (1.0 + jnp.exp(-x))


def _hgrn_chunk(q, f, iv, g, gn, st, bounded_decay):
    c = HGRN_CHUNK
    sub = HGRN_SUB
    half = sub // 2
    nsub = c // sub
    qf = q * _sigmoid(q)
    kf = jnp.maximum(1.0 - f, 0.0)
    b2 = jnp.log2(f)
    rowc = lax.broadcasted_iota(jnp.int32, b2.shape, 0)
    shift = 1
    while shift < c:
        b2 = b2 + jnp.where(rowc >= shift, pltpu.roll(b2, shift, 0), 0.0)
        shift *= 2
    c2 = b2 - jnp.log2(kf)

    inter = lax.dot_general((qf * jnp.exp2(b2)).astype(BF16), st.astype(BF16), NT_DIMS,
                            preferred_element_type=F32)

    lane = lax.broadcasted_iota(jnp.int32, (sub, LANES), 1)
    tsub = lax.broadcasted_iota(jnp.int32, (sub, LANES), 0)
    colid = lax.broadcasted_iota(jnp.int32, (sub, c), 1)
    ones = jnp.ones((LANES, LANES), BF16)
    a_rows = []
    if bounded_decay:
        b_mid = b2[c // 2 - 1:c // 2, :]
        qt = qf * jnp.exp2(b2 - b_mid)
        kt = jnp.exp2(b_mid - c2)
        pair = lax.dot_general(qt.astype(BF16), kt.astype(BF16), NT_DIMS,
                               preferred_element_type=F32)
        rr = lax.broadcasted_iota(jnp.int32, (c, c), 0)
        rc = lax.broadcasted_iota(jnp.int32, (c, c), 1)
        a_rows.append(jnp.where(rr >= rc, pair, 0.0))
    for bi in range(0 if bounded_decay else nsub):
        lo = bi * sub
        b_i = b2[lo:lo + sub]
        c_i = c2[lo:lo + sub]
        q_i = qf[lo:lo + sub]
        pieces = []
        for s in range(sub):
            if s < half:
                pieces.append(q_i * jnp.exp2(b_i - c_i[s:s + 1, :]))
            else:
                pieces.append(q_i[half:] * jnp.exp2(b_i[half:] - c_i[s:s + 1, :]))
        pm = jnp.concatenate(pieces, axis=0).astype(BF16)
        rs = jnp.dot(pm, ones, preferred_element_type=F32)
        rel = lane - lo
        key = jnp.where((rel >= 0) & (rel <= tsub), rel, -1)
        key_lo, key_hi = key[:half], key[half:]
        a_lo = jnp.zeros((half, LANES), F32)
        a_hi = jnp.zeros((half, LANES), F32)
        off = 0
        for s in range(sub):
            if s < half:
                a_lo = jnp.where(key_lo == s, rs[off:off + half], a_lo)
                a_hi = jnp.where(key_hi == s, rs[off + half:off + sub], a_hi)
                off += sub
            else:
                a_hi = jnp.where(key_hi == s, rs[off:off + half], a_hi)
                off += half
        a_blk = jnp.concatenate([a_lo, a_hi], axis=0)[:, :c]
        if bi > 0:
            b0 = b2[lo - 1:lo, :]
            qt = q_i * jnp.exp2(b_i - b0)
            kt = jnp.exp2(jnp.minimum(b0 - c2, 0.0))
            cross = lax.dot_general(qt.astype(BF16), kt.astype(BF16), NT_DIMS,
                                    preferred_element_type=F32)
            a_blk = a_blk + jnp.where(colid < lo, cross, 0.0)
        a_rows.append(a_blk)
    a = jnp.concatenate(a_rows, axis=0)
    intra = jnp.dot(a.astype(BF16), iv.astype(BF16), preferred_element_type=F32)

    b_last = b2[c - 1:c, :]
    khat = jnp.exp2(b_last - c2)
    st_new = st * jnp.exp2(b_last) + lax.dot_general(
        iv.astype(BF16), khat.astype(BF16), TN_DIMS, preferred_element_type=F32)

    o = inter + intra
    y = o * lax.rsqrt(jnp.mean(o * o, axis=-1, keepdims=True) + RMS_EPS) * gn
    return y * (g * _sigmoid(g)), st_new


def _hgrn_kernel(q_ref, f_ref, i_ref, g_ref, lb_ref, gn_ref, o_ref, st_ref):
    @pl.when(pl.program_id(2) == 0)
    def _():
        st_ref[...] = jnp.zeros_like(st_ref)

    gn = gn_ref[...]
    d = HEAD_DIM
    lb = lb_ref[...]
    f_all = lb + (1.0 - lb) * _sigmoid(f_ref[0].astype(F32))
    f_min = jnp.min(f_all)

    def run(bounded_decay):
        sts = [st_ref[hh] for hh in range(HGRN_HEADS_PER_STEP)]
        for ci in range(HGRN_STEP // HGRN_CHUNK):
            sl = slice(ci * HGRN_CHUNK, (ci + 1) * HGRN_CHUNK)
            for hh in range(HGRN_HEADS_PER_STEP):
                cs = slice(hh * d, (hh + 1) * d)
                out, sts[hh] = _hgrn_chunk(q_ref[0, sl, cs].astype(F32), f_all[sl, cs],
                                           i_ref[0, sl, cs], g_ref[0, sl, cs].astype(F32), gn,
                                           sts[hh], bounded_decay)
                o_ref[0, sl, cs] = out.astype(o_ref.dtype)
        for hh in range(HGRN_HEADS_PER_STEP):
            st_ref[hh] = sts[hh]

    @pl.when(f_min >= HGRN_MIN_SAFE_F)
    def _():
        run(True)

    @pl.when(jnp.logical_not(f_min >= HGRN_MIN_SAFE_F))
    def _():
        run(False)


def _hgrn(hp, col0, lb, gn, n_heads):
    b, t, _ = hp.shape
    d = HEAD_DIM
    ts = HGRN_STEP
    hps = HGRN_HEADS_PER_STEP
    assert n_heads % hps == 0 and col0 % (hps * d) == 0
    ng = n_heads // hps
    blk0 = col0 // (hps * d)

    def col(group):
        return pl.BlockSpec((1, ts, hps * d), lambda bi, hi, ti: (bi, ti, blk0 + group * ng + hi))

    return pl.pallas_call(
        _hgrn_kernel,
        grid=(b, ng, t // ts),
        in_specs=[col(0), col(1), col(2), col(3),
                  pl.BlockSpec((1, hps * d), lambda bi, hi, ti: (0, hi)),
                  pl.BlockSpec((1, d), lambda bi, hi, ti: (0, 0))],
        out_specs=pl.BlockSpec((1, ts, hps * d), lambda bi, hi, ti: (bi, ti, hi)),
        out_shape=jax.ShapeDtypeStruct((b, t, n_heads * d), BF16),
        scratch_shapes=[pltpu.VMEM((hps, d, d), F32)],
        compiler_params=_cparams("parallel", "parallel", "arbitrary"),
        name="hgrn2",
    )(hp, hp, hp, hp, lb, gn)


def _outproj_kernel(oa_ref, or_ref, x_ref, w_ref, g_ref, wr_ref, br_ref,
                    x1_ref, h2_ref, rt_ref, ri_ref, cnt_out_ref, cnt_ref):
    wa = oa_ref.shape[1]
    x1 = (x_ref[...]
          + jnp.dot(oa_ref[...], w_ref[0:wa, :], preferred_element_type=F32)
          + jnp.dot(or_ref[...], w_ref[wa:, :], preferred_element_type=F32))
    x1_ref[...] = x1
    h2 = x1 * lax.rsqrt(jnp.mean(x1 * x1, axis=-1, keepdims=True) + RMS_EPS) * g_ref[...]
    h2_ref[...] = h2

    h_hi = h2.astype(BF16)
    h_mid = (h2 - h_hi.astype(F32)).astype(BF16)
    part = jnp.dot(h_hi, wr_ref[...], preferred_element_type=F32)
    logits = (part[:, :LANES] + part[:, LANES:] + br_ref[...]
              + jnp.dot(h_mid, wr_ref[:, :LANES], preferred_element_type=F32))
    lane = lax.broadcasted_iota(jnp.int32, logits.shape, 1)
    big = jnp.int32(4 * LANES)
    ninf = -jnp.inf

    lg = jnp.where(lane < N_GROUPS, logits, ninf)
    mg = jnp.max(lg, axis=1, keepdims=True)
    gidx = jnp.min(jnp.where(lg == mg, lane, big), axis=1, keepdims=True)
    grp_w = 1.0 / jnp.sum(jnp.exp(lg - mg), axis=1, keepdims=True)

    lo = N_GROUPS + EXPERTS_PER_GROUP * gidx
    le = jnp.where((lane >= lo) & (lane < lo + EXPERTS_PER_GROUP), logits, ninf)
    m1 = jnp.max(le, axis=1, keepdims=True)
    i1 = jnp.min(jnp.where(le == m1, lane, big), axis=1, keepdims=True)
    le2 = jnp.where(lane == i1, ninf, le)
    m2 = jnp.max(le2, axis=1, keepdims=True)
    i2 = jnp.min(jnp.where(le2 == m2, lane, big), axis=1, keepdims=True)
    r21 = jnp.exp(m2 - m1)
    w1 = grp_w / (1.0 + r21)
    w2 = grp_w * r21 / (1.0 + r21)
    e1 = i1 - N_GROUPS
    e2 = i2 - N_GROUPS

    @pl.when(pl.program_id(0) == 0)
    def _():
        cnt_ref[...] = jnp.zeros_like(cnt_ref)

    tm = logits.shape[0]
    onehot = jnp.where((lane == e1) | (lane == e2), 1.0, 0.0)
    rr = lax.broadcasted_iota(jnp.int32, (tm, tm), 0)
    rc = lax.broadcasted_iota(jnp.int32, (tm, tm), 1)
    before = jnp.where(rr > rc, 1.0, 0.0).astype(BF16)
    prefix = (jnp.dot(before, onehot.astype(BF16), preferred_element_type=F32) + cnt_ref[0:1, :])
    rank1 = jnp.sum(jnp.where(lane == e1, prefix, 0.0), axis=1, keepdims=True)
    rank2 = jnp.sum(jnp.where(lane == e2, prefix, 0.0), axis=1, keepdims=True)
    total = cnt_ref[0:1, :] + jnp.sum(onehot, axis=0, keepdims=True)
    cnt_ref[0:1, :] = total
    cnt_out_ref[...] = jnp.broadcast_to(total, cnt_out_ref.shape)

    cols = [e1.astype(F32), e2.astype(F32), w1, w2, rank1, rank2]
    rt = jnp.zeros(logits.shape, F32)
    for ci, cv in enumerate(cols):
        rt = jnp.where(lane == ci, cv, rt)
    rt_ref[...] = rt
    icols = [e1, e2, rank1.astype(jnp.int32), rank2.astype(jnp.int32)]
    ri = jnp.zeros(logits.shape, jnp.int32)
    for ci, cv in enumerate(icols):
        ri = jnp.where(lane == ci, cv, ri)
    ri_ref[...] = ri[:, :ri_ref.shape[1]]


def _outproj(oa, orec, x2, w_out, g, wr, br):
    n, d = x2.shape
    tm = OUTPROJ_TILE
    wa = oa.shape[1]
    wrc = orec.shape[1]
    row = lambda i: (i, 0)
    const = lambda i: (0, 0)
    return pl.pallas_call(
        _outproj_kernel,
        grid=(n // tm,),
        in_specs=[
            pl.BlockSpec((tm, wa), row),
            pl.BlockSpec((tm, wrc), row),
            pl.BlockSpec((tm, d), row),
            pl.BlockSpec((wa + wrc, d), const),
            pl.BlockSpec((1, d), const),
            pl.BlockSpec((d, 2 * LANES), const),
            pl.BlockSpec((1, LANES), const),
        ],
        out_specs=[pl.BlockSpec((tm, d), row), pl.BlockSpec((tm, d), row),
                   pl.BlockSpec((tm, LANES), row), pl.BlockSpec((tm, 8), row),
                   pl.BlockSpec((8, LANES), const)],
        out_shape=[jax.ShapeDtypeStruct((n, d), F32), jax.ShapeDtypeStruct((n, d), F32),
                   jax.ShapeDtypeStruct((n, LANES), F32), jax.ShapeDtypeStruct((n, 8), jnp.int32),
                   jax.ShapeDtypeStruct((8, LANES), F32)],
        scratch_shapes=[pltpu.VMEM((8, LANES), F32)],
        compiler_params=_cparams("arbitrary"),
        name="outproj_route",
    )(oa, orec, x2, w_out, g, wr, br)


WEIGHT_CAST_ROWS = 256


def _cast_rows(dst_ref, src_ref):
    rows = dst_ref.shape[0]
    step = min(rows, WEIGHT_CAST_ROWS)
    assert rows % step == 0

    def body(c, carry):
        sl = pl.ds(pl.multiple_of(c * step, step), step)
        dst_ref[sl, :] = src_ref[sl, :].astype(BF16)
        return carry

    lax.fori_loop(0, rows // step, body, 0)


def _moe_kernel(te_ref, nu_ref, first_ref, nxt_ref, src_ref, h2_hbm, wg_hbm, wu_hbm, wd_hbm, y_ref,
                xbuf0, xbuf1, xbuf2, sems, wsems, wg32, wu32, wd32, wgb, wub, wdb):
    s = pl.program_id(0)
    nu = nu_ref[0]
    tm = MOE_TILE
    xbufs = (xbuf0, xbuf1, xbuf2)
    nbuf = len(xbufs)
    prev_tile = jnp.maximum(s - 1, 0)

    def weight_copies(e):
        return (pltpu.make_async_copy(wg_hbm.at[e], wg32, wsems.at[0]),
                pltpu.make_async_copy(wu_hbm.at[e], wu32, wsems.at[1]),
                pltpu.make_async_copy(wd_hbm.at[e], wd32, wsems.at[2]))

    def start_gather(tile, slot):
        for r in range(tm):
            tok = src_ref[tile * tm + r]
            pltpu.make_async_copy(h2_hbm.at[pl.ds(tok, 1), :], xbufs[slot].at[pl.ds(r, 1), :],
                                  sems.at[slot]).start()

    def wait_gather(slot):
        pltpu.make_async_copy(h2_hbm.at[pl.ds(0, tm), :], xbufs[slot], sems.at[slot]).wait()

    def ffn(slot):
        x = xbufs[slot][...].astype(BF16)
        a = jnp.dot(x, wgb[...], preferred_element_type=F32)
        u = jnp.dot(x, wub[...], preferred_element_type=F32)
        hid = (a * _sigmoid(a) * u).astype(BF16)
        y_ref[...] = jnp.dot(hid, wdb[...], preferred_element_type=F32)

    @pl.when(s == 0)
    def _():
        for cp in weight_copies(te_ref[0]):
            cp.start()
        start_gather(0, 0)

        @pl.when(nu > 1)
        def _():
            start_gather(1, 1)

    @pl.when((s >= 1) & (s <= nu) & (first_ref[prev_tile] == 1))
    def _():
        for cp in weight_copies(te_ref[prev_tile]):
            cp.wait()
        _cast_rows(wgb, wg32)
        _cast_rows(wub, wu32)
        _cast_rows(wdb, wd32)

        @pl.when(nxt_ref[prev_tile] >= 0)
        def _():
            for cp in weight_copies(nxt_ref[prev_tile]):
                cp.start(priority=1)

    for ph in range(nbuf):
        cur, nxt_buf = (ph - 1) % nbuf, (ph + 1) % nbuf

        @pl.when((s >= 1) & (s + 1 < nu) & (s % nbuf == ph))
        def _(cur=cur, nxt_buf=nxt_buf):
            wait_gather(cur)
            start_gather(s + 1, nxt_buf)
            ffn(cur)

        @pl.when((s >= 1) & (s <= nu) & (s + 1 >= nu) & (s % nbuf == ph))
        def _(cur=cur):
            wait_gather(cur)
            ffn(cur)

    @pl.when(s > nu)
    def _():
        y_ref[...] = jnp.zeros_like(y_ref)


def _moe(tile_expert, n_used, first, nxt, src, h2, wg, wu, wd):
    p = src.shape[0]
    d, f = wg.shape[1:]
    tm = MOE_TILE
    n_tiles = p // tm
    any_spec = pl.BlockSpec(memory_space=pl.ANY)
    return pl.pallas_call(
        _moe_kernel,
        grid_spec=pltpu.PrefetchScalarGridSpec(
            num_scalar_prefetch=5,
            grid=(n_tiles + 1,),
            in_specs=[any_spec, any_spec, any_spec, any_spec],
            out_specs=pl.BlockSpec((tm, d), lambda s, *_: (jnp.maximum(s - 1, 0), 0)),
            scratch_shapes=[pltpu.VMEM((tm, d), F32)] * 3 + [
                            pltpu.SemaphoreType.DMA((3,)), pltpu.SemaphoreType.DMA((3,)),
                            pltpu.VMEM((d, f), F32), pltpu.VMEM((d, f), F32), pltpu.VMEM((f, d), F32),
                            pltpu.VMEM((d, f), BF16), pltpu.VMEM((d, f), BF16),
                            pltpu.VMEM((f, d), BF16)],
        ),
        out_shape=jax.ShapeDtypeStruct((p, d), F32),
        compiler_params=_cparams("arbitrary"),
        name="moe_ffn",
    )(tile_expert, n_used, first, nxt, src, h2, wg, wu, wd)


COMBINE_TILE = 512


def _final_kernel(pos_ref, x1_ref, ys_hbm, rt_ref, g_ref, o_ref, ya0, yb0, ya1, yb1, sems):
    s = pl.program_id(0)
    n_tiles = pl.num_programs(0) - 1
    tm = COMBINE_TILE
    n_tok = pos_ref.shape[0] // 2
    ybufs = ((ya0, yb0), (ya1, yb1))

    def start_gather(tile, slot):
        for k in range(2):
            for r in range(tm):
                row = pos_ref[k * n_tok + tile * tm + r]
                pltpu.make_async_copy(ys_hbm.at[pl.ds(row, 1), :],
                                      ybufs[slot][k].at[pl.ds(r, 1), :], sems.at[slot]).start()

    def wait_gather(slot):
        for k in range(2):
            pltpu.make_async_copy(ys_hbm.at[pl.ds(0, tm), :], ybufs[slot][k], sems.at[slot]).wait()

    def finish(slot):
        rt = rt_ref[...]
        x2 = (x1_ref[...] + rt[:, 2:3] * ybufs[slot][0][...] + rt[:, 3:4] * ybufs[slot][1][...])
        o_ref[...] = (x2 * lax.rsqrt(jnp.mean(x2 * x2, axis=-1, keepdims=True) + RMS_EPS)
                      * g_ref[...])

    @pl.when(s == 0)
    def _():
        start_gather(s, 0)

    for parity in range(2):
        @pl.when((s >= 1) & (s < n_tiles) & (s % 2 == parity))
        def _(parity=parity):
            wait_gather(1 - parity)
            start_gather(s, parity)
            finish(1 - parity)

        @pl.when((s == n_tiles) & (s % 2 == parity))
        def _(parity=parity):
            wait_gather(1 - parity)
            finish(1 - parity)


def _final(x1, ys, pos, rt, g):
    n, d = x1.shape
    tm = COMBINE_TILE
    prev = lambda s: jnp.maximum(s - 1, 0)
    return pl.pallas_call(
        _final_kernel,
        grid_spec=pltpu.PrefetchScalarGridSpec(
            num_scalar_prefetch=1,
            grid=(n // tm + 1,),
            in_specs=[pl.BlockSpec((tm, d), lambda s, p: (prev(s), 0)),
                      pl.BlockSpec(memory_space=pl.ANY),
                      pl.BlockSpec((tm, LANES), lambda s, p: (prev(s), 0)),
                      pl.BlockSpec((1, d), lambda s, p: (0, 0))],
            out_specs=pl.BlockSpec((tm, d), lambda s, p: (prev(s), 0)),
            scratch_shapes=[pltpu.VMEM((tm, d), F32)] * 4 + [pltpu.SemaphoreType.DMA((2,))],
        ),
        out_shape=jax.ShapeDtypeStruct((n, d), F32),
        compiler_params=_cparams("arbitrary"),
        name="combine_norm",
    )(pos, x1, ys, rt, g)


def _dispatch_plan(ids, counts, n_tokens):
    tm = MOE_TILE
    n_rows = 2 * n_tokens + N_EXPERTS * tm
    n_tiles = n_rows // tm
    ef = jnp.concatenate([ids[:, 0], ids[:, 1]])
    rank = jnp.concatenate([ids[:, 2], ids[:, 3]])
    counts = counts[0, :N_EXPERTS].astype(jnp.int32)
    tiles_per = (counts + tm - 1) // tm
    tile_end = jnp.cumsum(tiles_per)
    row_start = (tile_end - tiles_per) * tm
    pos = row_start[ef] + rank
    tok = jnp.concatenate([jnp.arange(n_tokens, dtype=jnp.int32)] * 2)
    src = jnp.zeros((n_rows,), jnp.int32).at[pos].set(tok, unique_indices=True,
                                                      mode="promise_in_bounds")
    n_used = tile_end[-1]
    tile_ids = jnp.arange(n_tiles, dtype=jnp.int32)
    tile_expert = jnp.sum((tile_ids[:, None] >= tile_end[None, :]).astype(jnp.int32), axis=1)
    last_expert = jnp.sum((n_used - 1 >= tile_end).astype(jnp.int32))
    tile_expert = jnp.where(tile_ids < n_used, tile_expert, last_expert).astype(jnp.int32)
    first = jnp.concatenate([jnp.ones((1,), jnp.int32),
                             (tile_expert[1:] != tile_expert[:-1]).astype(jnp.int32)])
    eid = jnp.arange(N_EXPERTS, dtype=jnp.int32)
    later = (tiles_per[None, :] > 0) & (eid[None, :] > eid[:, None])
    next_expert = jnp.min(jnp.where(later, eid[None, :], N_EXPERTS), axis=1)
    next_expert = jnp.where(next_expert == N_EXPERTS, -1, next_expert).astype(jnp.int32)
    nxt = next_expert[tile_expert]
    return (src, pos[:n_tokens], pos[n_tokens:], tile_expert, first, nxt,
            n_used.reshape(1).astype(jnp.int32))


def kernel(x, norm_mix_g, w_in, hgrn_lb_logits, hgrn_out_norm_g, w_out, norm_ffn_g, w_group_router,
           b_group_router, w_expert_router, b_expert_router, w_gate, w_up, w_down, final_norm_g):
    b, t, d = x.shape
    n = b * t
    depth = w_in.shape[0]
    assert depth == 1, "the final norm is fused into the combine step of the only layer"
    attn_w = d // 2
    n_heads = attn_w // HEAD_DIM
    x2 = x.reshape(n, d)
    lb_all = jnp.cumsum(jax.nn.softmax(hgrn_lb_logits.astype(F32), axis=0), axis=0)[:depth]
    slope2 = jnp.asarray(2.0 ** (-8.0 * np.arange(1, n_heads + 1) / n_heads), dtype=F32) * F32(LOG2E)
    s_hi = slope2.astype(BF16).astype(F32)
    s_mid = (slope2 - s_hi).astype(BF16).astype(F32)
    s_lo = (slope2 - s_hi - s_mid).astype(BF16).astype(F32)
    slope_pieces = jnp.stack([s_hi, s_mid, s_lo])
    proj_scale = jnp.concatenate([jnp.full((attn_w,), HEAD_DIM ** -0.5 * LOG2E, F32),
                                  jnp.ones((w_in.shape[2] - attn_w,), F32)]).reshape(1, -1)

    for l in range(depth):
        g_mix = norm_mix_g[l].reshape(1, d)
        x2, w_in_l, w_out_l = lax.optimization_barrier(
            (x2, w_in[l].astype(BF16), w_out[l].astype(BF16)))
        proj = _inproj(x2, g_mix, w_in_l, 0, proj_scale, BF16).reshape(b, t, -1)
        o_attn = _moba(proj, slope_pieces, n_heads)
        o_rec = _hgrn(proj, 3 * attn_w, lb_all[l].reshape(1, -1),
                      hgrn_out_norm_g[l].reshape(1, HEAD_DIM), n_heads)

        wr = jnp.concatenate(
            [w_group_router[l],
             jnp.transpose(w_expert_router[l], (1, 0, 2)).reshape(d, N_EXPERTS)], axis=1)
        wr = jnp.pad(wr, ((0, 0), (0, LANES - wr.shape[1])))
        wr_hi = wr.astype(BF16)
        wr = jnp.concatenate([wr_hi, (wr - wr_hi.astype(F32)).astype(BF16)], axis=1)
        br =jnp.concatenate([b_group_router[l], b_expert_router[l].reshape(-1)])
        br = jnp.pad(br, (0, LANES - br.shape[0])).reshape(1, LANES)
        x1, h2, rt, ri, counts = _outproj(o_attn.reshape(n, attn_w), o_rec.reshape(n, -1), x2,
                                          w_out_l, norm_ffn_g[l].reshape(1, d), wr, br)
        src, pos1, pos2, tile_expert, first, nxt, n_used = _dispatch_plan(ri, counts, n)
        ys = _moe(tile_expert, n_used, first, nxt, src, h2, w_gate[l], w_up[l], w_down[l])
        x2 = _final(x1, ys, jnp.concatenate([pos1, pos2]), rt, final_norm_g.reshape(1, d))
    return x2.reshape(b, t, d)
```

```python
import functools

import jax
import jax.numpy as jnp
import numpy as np
from jax import lax
from jax.experimental import pallas as pl
from jax.experimental.pallas import tpu as pltpu

F32 = jnp.float32
BF16 = jnp.bfloat16

HEAD_DIM = 128
MOBA_BLOCK = 256
MOBA_TOPK = 3
N_GROUPS = 4
EXPERTS_PER_GROUP = 4
N_EXPERTS = N_GROUPS * EXPERTS_PER_GROUP
RMS_EPS = 1e-6

LANES = 128
VMEM_LIMIT_BYTES = 56 * 1024 * 1024

HGRN_CHUNK = 64
HGRN_SUB = 16
HGRN_STEP = 256
HGRN_HEADS_PER_STEP = 8
HGRN_MIN_SAFE_F = 2.0 ** -3.5
MOE_TILE = 256
INPROJ_TILE = (1024, 1792)
OUTPROJ_TILE = 512
NT_DIMS = (((1,), (1,)), ((), ()))
TN_DIMS = (((0,), (0,)), ((), ()))


def _cparams(*sem):
    return pltpu.CompilerParams(dimension_semantics=sem, vmem_limit_bytes=VMEM_LIMIT_BYTES)


def _inproj_kernel(x_ref, g_ref, w_ref, cs_ref, o_ref, hn_ref):
    @pl.when(pl.program_id(1) == 0)
    def _():
        x = x_ref[...]
        ms = jnp.mean(x * x, axis=-1, keepdims=True)
        hn_ref[...] = (x * lax.rsqrt(ms + RMS_EPS) * g_ref[...]).astype(BF16)

    acc = jnp.dot(hn_ref[...], w_ref[...], preferred_element_type=F32)
    o_ref[...] = (acc * cs_ref[...]).astype(o_ref.dtype)


def _inproj(x2, g, w, col0, colscale, out_dtype):
    n, d = x2.shape
    tm, tn = INPROJ_TILE
    cols = colscale.shape[1]
    j0 = col0 // tn
    return pl.pallas_call(
        _inproj_kernel,
        grid=(n // tm, cols // tn),
        in_specs=[
            pl.BlockSpec((tm, d), lambda i, j: (i, 0)),
            pl.BlockSpec((1, d), lambda i, j: (0, 0)),
            pl.BlockSpec((d, tn), lambda i, j: (0, j0 + j)),
            pl.BlockSpec((1, tn), lambda i, j: (0, j)),
        ],
        out_specs=pl.BlockSpec((tm, tn), lambda i, j: (i, j)),
        out_shape=jax.ShapeDtypeStruct((n, cols), out_dtype),
        scratch_shapes=[pltpu.VMEM((tm, d), BF16)],
        compiler_params=_cparams("parallel", "arbitrary"),
        name="inproj",
    )(x2, g, w, colscale)


MOBA_HEADS_PER_STEP = 4
MOBA_EXTRA_POS = 0
MOBA_EXTRA_SEL = 8
MASK_BIG = 2.0 ** 60
LOG2E = 1.4426950408889634


def _moba_key_extras(t):
    nb = t // MOBA_BLOCK
    assert MOBA_EXTRA_SEL + nb <= LANES and MOBA_BLOCK <= 256
    pos = np.arange(t)
    kx = np.zeros((t, LANES), np.float32)
    kx[:, MOBA_EXTRA_POS:MOBA_EXTRA_POS + 3] = ((pos // MOBA_BLOCK) * MOBA_BLOCK)[:, None]
    kx[:, MOBA_EXTRA_POS + 3:MOBA_EXTRA_POS + 6] = (pos % MOBA_BLOCK)[:, None]
    kx[pos, MOBA_EXTRA_SEL + pos // MOBA_BLOCK] = -MASK_BIG
    return jnp.asarray(kx, dtype=BF16)


def _moba_kernel(sl_ref, q_ref, k_ref, v_ref, kx_ref, o_ref, kaug_ref, qaug_ref, *, nb):
    hg = pl.program_id(1)
    i = pl.program_id(2)
    bs = MOBA_BLOCK
    dh = HEAD_DIM
    t = q_ref.shape[1]
    nbp = -(-nb // 8) * 8
    nx = 8 + nbp

    @pl.when(i == 0)
    def _():
        r8 = lax.broadcasted_iota(jnp.int32, (8, t), 0)
        piece = r8 % 3
        er = lax.broadcasted_iota(jnp.int32, (nx, LANES), 0)
        ec = lax.broadcasted_iota(jnp.int32, (nx, LANES), 1)
        embed = jnp.where(er == ec, 1.0, 0.0).astype(BF16)
        blk = lax.broadcasted_iota(jnp.int32, (nbp, t), 0)
        qblk = lax.broadcasted_iota(jnp.int32, (nbp, t), 1) // bs
        for hh in range(MOBA_HEADS_PER_STEP):
            h = hg * MOBA_HEADS_PER_STEP + hh
            cs = slice(hh * dh, (hh + 1) * dh)
            kaug_ref[hh, :, 0:dh] = k_ref[0, :, cs]
            kaug_ref[hh, :, dh:] = kx_ref[...]
            rows = [jnp.mean(k_ref[0, j * bs:(j + 1) * bs, cs].astype(F32), axis=0, keepdims=True)
                    for j in range(nb)]
            if nbp > nb:
                rows.append(jnp.zeros((nbp - nb, dh), F32))
            km = jnp.concatenate(rows, axis=0)
            hi = km.astype(BF16)
            mid = (km - hi.astype(F32)).astype(BF16)
            lo = (km - hi.astype(F32) - mid.astype(F32)).astype(BF16)
            km4 = jnp.concatenate([hi, mid, lo, jnp.zeros_like(hi)], axis=0)
            q = q_ref[0, :, cs]
            g4 = lax.dot_general(km4, q, NT_DIMS, preferred_element_type=F32)
            gate_t = g4[0:nbp] + g4[nbp:2 * nbp] + g4[2 * nbp:3 * nbp]
            rank = jnp.zeros(gate_t.shape, F32)
            for jp in range(nb - 1):
                other = gate_t[jp:jp + 1, :]
                beats = (other > gate_t) | ((other == gate_t) & (blk > jp))
                rank = rank + jnp.where(beats & (qblk > jp), 1.0, 0.0)
            notsel_t = jnp.where((blk < qblk) & (rank >= MOBA_TOPK), 1.0, 0.0)
            slope_rows = jnp.where(r8 >= 6, 0.0,
                                   jnp.where(piece == 0, sl_ref[0, h],
                                             jnp.where(piece == 1, sl_ref[1, h], sl_ref[2, h])))
            qx_t = jnp.concatenate([slope_rows, notsel_t], axis=0).astype(BF16)
            qx = lax.dot_general(qx_t, embed, TN_DIMS, preferred_element_type=F32)
            qaug_ref[hh, :, 0:dh] = q
            qaug_ref[hh, :, dh:] = qx.astype(BF16)

    rowi = lax.broadcasted_iota(jnp.int32, (bs, bs), 0)
    coli = lax.broadcasted_iota(jnp.int32, (bs, bs), 1)

    for c in range(nb):
        @pl.when(i == c)
        def _(c=c):
            n = (c + 1) * bs
            for hh in range(MOBA_HEADS_PER_STEP):
                cs = slice(hh * dh, (hh + 1) * dh)
                s = lax.dot_general(qaug_ref[hh, c * bs:n, :], kaug_ref[hh, 0:n, :], NT_DIMS,
                                    preferred_element_type=F32)
                s_own = jnp.where(rowi >= coli, s[:, c * bs:], -jnp.inf)
                m = jnp.max(s_own, axis=1, keepdims=True)
                if c > 0:
                    s_past = s[:, :c * bs]
                    m = jnp.maximum(m, jnp.max(s_past, axis=1, keepdims=True))
                p_own = jnp.exp2(s_own - m)
                l = jnp.sum(p_own, axis=1, keepdims=True)
                acc = jnp.dot(p_own.astype(BF16), v_ref[0, c * bs:n, cs],
                              preferred_element_type=F32)
                if c > 0:
                    p_past = jnp.exp2(s_past - m)
                    l = l + jnp.sum(p_past, axis=1, keepdims=True)
                    acc = acc + jnp.dot(p_past.astype(BF16), v_ref[0, 0:c * bs, cs],
                                        preferred_element_type=F32)
                o_ref[0, :, cs] = (acc / l).astype(o_ref.dtype)


def _moba(qkv, slope_pieces, n_heads):
    b, t, _ = qkv.shape
    bs = MOBA_BLOCK
    nb = t // bs
    hps = MOBA_HEADS_PER_STEP
    assert n_heads % hps == 0
    ng = n_heads // hps
    w = hps * HEAD_DIM
    kern = functools.partial(_moba_kernel, nb=nb)
    return pl.pallas_call(
        kern,
        grid=(b, ng, nb),
        in_specs=[
            pl.BlockSpec(memory_space=pltpu.SMEM),
            pl.BlockSpec((1, t, w), lambda bi, hi, qi: (bi, 0, hi)),
            pl.BlockSpec((1, t, w), lambda bi, hi, qi: (bi, 0, ng + hi)),
            pl.BlockSpec((1, t, w), lambda bi, hi, qi: (bi, 0, 2 * ng + hi)),
            pl.BlockSpec((t, LANES), lambda bi, hi, qi: (0, 0)),
        ],
        out_specs=pl.BlockSpec((1, bs, w), lambda bi, hi, qi: (bi, qi, hi)),
        out_shape=jax.ShapeDtypeStruct((b, t, n_heads * HEAD_DIM), BF16),
        scratch_shapes=[
            pltpu.VMEM((hps, t, HEAD_DIM + LANES), BF16),
            pltpu.VMEM((hps, t, HEAD_DIM + LANES), BF16),
        ],
        compiler_params=_cparams("parallel", "parallel", "arbitrary"),
        name="moba",
    )(slope_pieces, qkv, qkv, qkv, _moba_key_extras(t))


def _sigmoid(x):
    return 1.0 / (1.0 + jnp.exp(-x))


def _hgrn_chunk(q, f, iv, g, gn, st, bounded_decay):
    c = HGRN_CHUNK
    sub = HGRN_SUB
    half = sub // 2
    nsub = c // sub
    qf = q * _sigmoid(q)
    kf = jnp.maximum(1.0 - f, 0.0)
    b2 = jnp.log2(f)
    rowc = lax.broadcasted_iota(jnp.int32, b2.shape, 0)
    shift = 1
    while shift < c:
        b2 = b2 + jnp.where(rowc >= shift, pltpu.roll(b2, shift, 0), 0.0)
        shift *= 2
    c2 = b2 - jnp.log2(kf)

    inter = lax.dot_general((qf * jnp.exp2(b2)).astype(BF16), st.astype(BF16), NT_DIMS,
                            preferred_element_type=F32)

    lane = lax.broadcasted_iota(jnp.int32, (sub, LANES), 1)
    tsub = lax.broadcasted_iota(jnp.int32, (sub, LANES), 0)
    colid = lax.broadcasted_iota(jnp.int32, (sub, c), 1)
    ones = jnp.ones((LANES, LANES), BF16)
    a_rows = []
    if bounded_decay:
        b_mid = b2[c // 2 - 1:c // 2, :]
        qt = qf * jnp.exp2(b2 - b_mid)
        kt = jnp.exp2(b_mid - c2)
        pair = lax.dot_general(qt.astype(BF16), kt.astype(BF16), NT_DIMS,
                               preferred_element_type=F32)
        rr = lax.broadcasted_iota(jnp.int32, (c, c), 0)
        rc = lax.broadcasted_iota(jnp.int32, (c, c), 1)
        a_rows.append(jnp.where(rr >= rc, pair, 0.0))
    for bi in range(0 if bounded_decay else nsub):
        lo = bi * sub
        b_i = b2[lo:lo + sub]
        c_i = c2[lo:lo + sub]
        q_i = qf[lo:lo + sub]
        pieces = []
        for s in range(sub):
            if s < half:
                pieces.append(q_i * jnp.exp2(b_i - c_i[s:s + 1, :]))
            else:
                pieces.append(q_i[half:] * jnp.exp2(b_i[half:] - c_i[s:s + 1, :]))
        pm = jnp.concatenate(pieces, axis=0).astype(BF16)
        rs = jnp.dot(pm, ones, preferred_element_type=F32)
        rel = lane - lo
        key = jnp.where((rel >= 0) & (rel <= tsub), rel, -1)
        key_lo, key_hi = key[:half], key[half:]
        a_lo = jnp.zeros((half, LANES), F32)
        a_hi = jnp.zeros((half, LANES), F32)
        off = 0
        for s in range(sub):
            if s < half:
                a_lo = jnp.where(key_lo == s, rs[off:off + half], a_lo)
                a_hi = jnp.where(key_hi == s, rs[off + half:off + sub], a_hi)
                off += sub
            else:
                a_hi = jnp.where(key_hi == s, rs[off:off + half], a_hi)
                off += half
        a_blk = jnp.concatenate([a_lo, a_hi], axis=0)[:, :c]
        if bi > 0:
            b0 = b2[lo - 1:lo, :]
            qt = q_i * jnp.exp2(b_i - b0)
            kt = jnp.exp2(jnp.minimum(b0 - c2, 0.0))
            cross = lax.dot_general(qt.astype(BF16), kt.astype(BF16), NT_DIMS,
                                    preferred_element_type=F32)
            a_blk = a_blk + jnp.where(colid < lo, cross, 0.0)
        a_rows.append(a_blk)
    a = jnp.concatenate(a_rows, axis=0)
    intra = jnp.dot(a.astype(BF16), iv.astype(BF16), preferred_element_type=F32)

    b_last = b2[c - 1:c, :]
    khat = jnp.exp2(b_last - c2)
    st_new = st * jnp.exp2(b_last) + lax.dot_general(
        iv.astype(BF16), khat.astype(BF16), TN_DIMS, preferred_element_type=F32)

    o = inter + intra
    y = o * lax.rsqrt(jnp.mean(o * o, axis=-1, keepdims=True) + RMS_EPS) * gn
    return y * (g * _sigmoid(g)), st_new


def _hgrn_kernel(q_ref, f_ref, i_ref, g_ref, lb_ref, gn_ref, o_ref, st_ref):
    @pl.when(pl.program_id(2) == 0)
    def _():
        st_ref[...] = jnp.zeros_like(st_ref)

    gn = gn_ref[...]
    d = HEAD_DIM
    lb = lb_ref[...]
    f_all = lb + (1.0 - lb) * _sigmoid(f_ref[0].astype(F32))
    f_min = jnp.min(f_all)

    def run(bounded_decay):
        sts = [st_ref[hh] for hh in range(HGRN_HEADS_PER_STEP)]
        for ci in range(HGRN_STEP // HGRN_CHUNK):
            sl = slice(ci * HGRN_CHUNK, (ci + 1) * HGRN_CHUNK)
            for hh in range(HGRN_HEADS_PER_STEP):
                cs = slice(hh * d, (hh + 1) * d)
                out, sts[hh] = _hgrn_chunk(q_ref[0, sl, cs].astype(F32), f_all[sl, cs],
                                           i_ref[0, sl, cs], g_ref[0, sl, cs].astype(F32), gn,
                                           sts[hh], bounded_decay)
                o_ref[0, sl, cs] = out.astype(o_ref.dtype)
        for hh in range(HGRN_HEADS_PER_STEP):
            st_ref[hh] = sts[hh]

    @pl.when(f_min >= HGRN_MIN_SAFE_F)
    def _():
        run(True)

    @pl.when(jnp.logical_not(f_min >= HGRN_MIN_SAFE_F))
    def _():
        run(False)


def _hgrn(hp, col0, lb, gn, n_heads):
    b, t, _ = hp.shape
    d = HEAD_DIM
    ts = HGRN_STEP
    hps = HGRN_HEADS_PER_STEP
    assert n_heads % hps == 0 and col0 % (hps * d) == 0
    ng = n_heads // hps
    blk0 = col0 // (hps * d)

    def col(group):
        return pl.BlockSpec((1, ts, hps * d), lambda bi, hi, ti: (bi, ti, blk0 + group * ng + hi))

    return pl.pallas_call(
        _hgrn_kernel,
        grid=(b, ng, t // ts),
        in_specs=[col(0), col(1), col(2), col(3),
                  pl.BlockSpec((1, hps * d), lambda bi, hi, ti: (0, hi)),
                  pl.BlockSpec((1, d), lambda bi, hi, ti: (0, 0))],
        out_specs=pl.BlockSpec((1, ts, hps * d), lambda bi, hi, ti: (bi, ti, hi)),
        out_shape=jax.ShapeDtypeStruct((b, t, n_heads * d), BF16),
        scratch_shapes=[pltpu.VMEM((hps, d, d), F32)],
        compiler_params=_cparams("parallel", "parallel", "arbitrary"),
        name="hgrn2",
    )(hp, hp, hp, hp, lb, gn)


def _outproj_kernel(oa_ref, or_ref, x_ref, w_ref, g_ref, wr_ref, br_ref,
                    x1_ref, h2_ref, rt_ref, cnt_out_ref, cnt_ref):
    wa = oa_ref.shape[1]
    x1 = (x_ref[...]
          + jnp.dot(oa_ref[...], w_ref[0:wa, :], preferred_element_type=F32)
          + jnp.dot(or_ref[...], w_ref[wa:, :], preferred_element_type=F32))
    x1_ref[...] = x1
    h2 = x1 * lax.rsqrt(jnp.mean(x1 * x1, axis=-1, keepdims=True) + RMS_EPS) * g_ref[...]
    h2_ref[...] = h2

    h_hi = h2.astype(BF16)
    h_mid = (h2 - h_hi.astype(F32)).astype(BF16)
    part = jnp.dot(h_hi, wr_ref[...], preferred_element_type=F32)
    logits = (part[:, :LANES] + part[:, LANES:] + br_ref[...]
              + jnp.dot(h_mid, wr_ref[:, :LANES], preferred_element_type=F32))
    lane = lax.broadcasted_iota(jnp.int32, logits.shape, 1)
    big = jnp.int32(4 * LANES)
    ninf = -jnp.inf

    lg = jnp.where(lane < N_GROUPS, logits, ninf)
    mg = jnp.max(lg, axis=1, keepdims=True)
    gidx = jnp.min(jnp.where(lg == mg, lane, big), axis=1, keepdims=True)
    grp_w = 1.0 / jnp.sum(jnp.exp(lg - mg), axis=1, keepdims=True)

    lo = N_GROUPS + EXPERTS_PER_GROUP * gidx
    le = jnp.where((lane >= lo) & (lane < lo + EXPERTS_PER_GROUP), logits, ninf)
    m1 = jnp.max(le, axis=1, keepdims=True)
    i1 = jnp.min(jnp.where(le == m1, lane, big), axis=1, keepdims=True)
    le2 = jnp.where(lane == i1, ninf, le)
    m2 = jnp.max(le2, axis=1, keepdims=True)
    i2 = jnp.min(jnp.where(le2 == m2, lane, big), axis=1, keepdims=True)
    r21 = jnp.exp(m2 - m1)
    w1 = grp_w / (1.0 + r21)
    w2 = grp_w * r21 / (1.0 + r21)
    e1 = i1 - N_GROUPS
    e2 = i2 - N_GROUPS

    @pl.when(pl.program_id(0) == 0)
    def _():
        cnt_ref[...] = jnp.zeros_like(cnt_ref)

    tm = logits.shape[0]
    onehot = jnp.where((lane == e1) | (lane == e2), 1.0, 0.0)
    rr = lax.broadcasted_iota(jnp.int32, (tm, tm), 0)
    rc = lax.broadcasted_iota(jnp.int32, (tm, tm), 1)
    before = jnp.where(rr > rc, 1.0, 0.0).astype(BF16)
    prefix = (jnp.dot(before, onehot.astype(BF16), preferred_element_type=F32) + cnt_ref[0:1, :])
    rank1 = jnp.sum(jnp.where(lane == e1, prefix, 0.0), axis=1, keepdims=True)
    rank2 = jnp.sum(jnp.where(lane == e2, prefix, 0.0), axis=1, keepdims=True)
    total = cnt_ref[0:1, :] + jnp.sum(onehot, axis=0, keepdims=True)
    cnt_ref[0:1, :] = total
    cnt_out_ref[...] = jnp.broadcast_to(total, cnt_out_ref.shape)

    cols = [e1.astype(F32), e2.astype(F32), w1, w2, rank1, rank2]
    rt = jnp.zeros(logits.shape, F32)
    for ci, cv in enumerate(cols):
        rt = jnp.where(lane == ci, cv, rt)
    rt_ref[...] = rt


def _outproj(oa, orec, x2, w_out, g, wr, br):
    n, d = x2.shape
    tm = OUTPROJ_TILE
    wa = oa.shape[1]
    wrc = orec.shape[1]
    row = lambda i: (i, 0)
    const = lambda i: (0, 0)
    return pl.pallas_call(
        _outproj_kernel,
        grid=(n // tm,),
        in_specs=[
            pl.BlockSpec((tm, wa), row),
            pl.BlockSpec((tm, wrc), row),
            pl.BlockSpec((tm, d), row),
            pl.BlockSpec((wa + wrc, d), const),
            pl.BlockSpec((1, d), const),
            pl.BlockSpec((d, 2 * LANES), const),
            pl.BlockSpec((1, LANES), const),
        ],
        out_specs=[pl.BlockSpec((tm, d), row), pl.BlockSpec((tm, d), row),
                   pl.BlockSpec((tm, LANES), row), pl.BlockSpec((8, LANES), const)],
        out_shape=[jax.ShapeDtypeStruct((n, d), F32), jax.ShapeDtypeStruct((n, d), F32),
                   jax.ShapeDtypeStruct((n, LANES), F32), jax.ShapeDtypeStruct((8, LANES), F32)],
        scratch_shapes=[pltpu.VMEM((8, LANES), F32)],
        compiler_params=_cparams("arbitrary"),
        name="outproj_route",
    )(oa, orec, x2, w_out, g, wr, br)


WEIGHT_CAST_ROWS = 256


def _cast_rows(dst_ref, src_ref):
    rows = dst_ref.shape[0]
    step = min(rows, WEIGHT_CAST_ROWS)
    assert rows % step == 0

    def body(c, carry):
        sl = pl.ds(pl.multiple_of(c * step, step), step)
        dst_ref[sl, :] = src_ref[sl, :].astype(BF16)
        return carry

    lax.fori_loop(0, rows // step, body, 0)


def _moe_kernel(te_ref, nu_ref, first_ref, nxt_ref, src_ref, h2_hbm, wg_hbm, wu_hbm, wd_hbm, y_ref,
                xbuf0, xbuf1, xbuf2, sems, wsems, wg32, wu32, wd32, wgb, wub, wdb):
    s = pl.program_id(0)
    nu = nu_ref[0]
    tm = MOE_TILE
    xbufs = (xbuf0, xbuf1, xbuf2)
    nbuf = len(xbufs)
    prev_tile = jnp.maximum(s - 1, 0)

    def weight_copies(e):
        return (pltpu.make_async_copy(wg_hbm.at[e], wg32, wsems.at[0]),
                pltpu.make_async_copy(wu_hbm.at[e], wu32, wsems.at[1]),
                pltpu.make_async_copy(wd_hbm.at[e], wd32, wsems.at[2]))

    def start_gather(tile, slot):
        for r in range(tm):
            tok = src_ref[tile * tm + r]
            pltpu.make_async_copy(h2_hbm.at[pl.ds(tok, 1), :], xbufs[slot].at[pl.ds(r, 1), :],
                                  sems.at[slot]).start()

    def wait_gather(slot):
        pltpu.make_async_copy(h2_hbm.at[pl.ds(0, tm), :], xbufs[slot], sems.at[slot]).wait()

    def ffn(slot):
        x = xbufs[slot][...].astype(BF16)
        a = jnp.dot(x, wgb[...], preferred_element_type=F32)
        u = jnp.dot(x, wub[...], preferred_element_type=F32)
        hid = (a * _sigmoid(a) * u).astype(BF16)
        y_ref[...] = jnp.dot(hid, wdb[...], preferred_element_type=F32)

    @pl.when(s == 0)
    def _():
        for cp in weight_copies(te_ref[0]):
            cp.start()
        start_gather(0, 0)

        @pl.when(nu > 1)
        def _():
            start_gather(1, 1)

    @pl.when((s >= 1) & (s <= nu) & (first_ref[prev_tile] == 1))
    def _():
        for cp in weight_copies(te_ref[prev_tile]):
            cp.wait()
        _cast_rows(wgb, wg32)
        _cast_rows(wub, wu32)
        _cast_rows(wdb, wd32)

        @pl.when(nxt_ref[prev_tile] >= 0)
        def _():
            for cp in weight_copies(nxt_ref[prev_tile]):
                cp.start(priority=1)

    for ph in range(nbuf):
        cur, nxt_buf = (ph - 1) % nbuf, (ph + 1) % nbuf

        @pl.when((s >= 1) & (s + 1 < nu) & (s % nbuf == ph))
        def _(cur=cur, nxt_buf=nxt_buf):
            wait_gather(cur)
            start_gather(s + 1, nxt_buf)
            ffn(cur)

        @pl.when((s >= 1) & (s <= nu) & (s + 1 >= nu) & (s % nbuf == ph))
        def _(cur=cur):
            wait_gather(cur)
            ffn(cur)

    @pl.when(s > nu)
    def _():
        y_ref[...] = jnp.zeros_like(y_ref)


def _moe(tile_expert, n_used, first, nxt, src, h2, wg, wu, wd):
    p = src.shape[0]
    d, f = wg.shape[1:]
    tm = MOE_TILE
    n_tiles = p // tm
    any_spec = pl.BlockSpec(memory_space=pl.ANY)
    return pl.pallas_call(
        _moe_kernel,
        grid_spec=pltpu.PrefetchScalarGridSpec(
            num_scalar_prefetch=5,
            grid=(n_tiles + 1,),
            in_specs=[any_spec, any_spec, any_spec, any_spec],
            out_specs=pl.BlockSpec((tm, d), lambda s, *_: (jnp.maximum(s - 1, 0), 0)),
            scratch_shapes=[pltpu.VMEM((tm, d), F32)] * 3 + [
                            pltpu.SemaphoreType.DMA((3,)), pltpu.SemaphoreType.DMA((3,)),
                            pltpu.VMEM((d, f), F32), pltpu.VMEM((d, f), F32), pltpu.VMEM((f, d), F32),
                            pltpu.VMEM((d, f), BF16), pltpu.VMEM((d, f), BF16),
                            pltpu.VMEM((f, d), BF16)],
        ),
        out_shape=jax.ShapeDtypeStruct((p, d), F32),
        compiler_params=_cparams("arbitrary"),
        name="moe_ffn",
    )(tile_expert, n_used, first, nxt, src, h2, wg, wu, wd)


COMBINE_TILE = 512


def _final_kernel(pos_ref, x1_ref, ys_hbm, rt_ref, g_ref, o_ref, ya0, yb0, ya1, yb1, ya2, yb2,
                  sems):
    s = pl.program_id(0)
    n_tiles = pl.num_programs(0) - 1
    tm = COMBINE_TILE
    n_tok = pos_ref.shape[0] // 2
    ybufs = ((ya0, yb0), (ya1, yb1), (ya2, yb2))
    nbuf = len(ybufs)

    def start_gather(tile, slot):
        for k in range(2):
            for r in range(tm):
                row = pos_ref[k * n_tok + tile * tm + r]
                pltpu.make_async_copy(ys_hbm.at[pl.ds(row, 1), :],
                                      ybufs[slot][k].at[pl.ds(r, 1), :], sems.at[slot]).start()

    def wait_gather(slot):
        for k in range(2):
            pltpu.make_async_copy(ys_hbm.at[pl.ds(0, tm), :], ybufs[slot][k], sems.at[slot]).wait()

    def finish(slot):
        rt = rt_ref[...]
        x2 = (x1_ref[...] + rt[:, 2:3] * ybufs[slot][0][...] + rt[:, 3:4] * ybufs[slot][1][...])
        o_ref[...] = (x2 * lax.rsqrt(jnp.mean(x2 * x2, axis=-1, keepdims=True) + RMS_EPS)
                      * g_ref[...])

    @pl.when(s == 0)
    def _():
        start_gather(0, 0)

        @pl.when(n_tiles > 1)
        def _():
            start_gather(1, 1)

    for ph in range(nbuf):
        cur, nxt_buf = (ph - 1) % nbuf, (ph + 1) % nbuf

        @pl.when((s >= 1) & (s + 1 < n_tiles) & (s % nbuf == ph))
        def _(cur=cur, nxt_buf=nxt_buf):
            wait_gather(cur)
            start_gather(s + 1, nxt_buf)
            finish(cur)

        @pl.when((s >= 1) & (s + 1 >= n_tiles) & (s % nbuf == ph))
        def _(cur=cur):
            wait_gather(cur)
            finish(cur)


def _final(x1, ys, pos, rt, g):
    n, d = x1.shape
    tm = COMBINE_TILE
    prev = lambda s: jnp.maximum(s - 1, 0)
    return pl.pallas_call(
        _final_kernel,
        grid_spec=pltpu.PrefetchScalarGridSpec(
            num_scalar_prefetch=1,
            grid=(n // tm + 1,),
            in_specs=[pl.BlockSpec((tm, d), lambda s, p: (prev(s), 0)),
                      pl.BlockSpec(memory_space=pl.ANY),
                      pl.BlockSpec((tm, LANES), lambda s, p: (prev(s), 0)),
                      pl.BlockSpec((1, d), lambda s, p: (0, 0))],
            out_specs=pl.BlockSpec((tm, d), lambda s, p: (prev(s), 0)),
            scratch_shapes=[pltpu.VMEM((tm, d), F32)] * 6 + [pltpu.SemaphoreType.DMA((3,))],
        ),
        out_shape=jax.ShapeDtypeStruct((n, d), F32),
        compiler_params=_cparams("arbitrary"),
        name="combine_norm",
    )(pos, x1, ys, rt, g)


def _dispatch_plan(rt, counts, n_tokens):
    tm = MOE_TILE
    n_rows = 2 * n_tokens + N_EXPERTS * tm
    n_tiles = n_rows // tm
    ids = rt[:, 0:6].astype(jnp.int32)
    ef = jnp.concatenate([ids[:, 0], ids[:, 1]])
    rank = jnp.concatenate([ids[:, 4], ids[:, 5]])
    counts = counts[0, :N_EXPERTS].astype(jnp.int32)
    tiles_per = (counts + tm - 1) // tm
    tile_end = jnp.cumsum(tiles_per)
    row_start = (tile_end - tiles_per) * tm
    pos = row_start[ef] + rank
    tok = jnp.concatenate([jnp.arange(n_tokens, dtype=jnp.int32)] * 2)
    src = jnp.zeros((n_rows,), jnp.int32).at[pos].set(tok, unique_indices=True,
                                                      mode="promise_in_bounds")
    n_used = tile_end[-1]
    tile_ids = jnp.arange(n_tiles, dtype=jnp.int32)
    tile_expert = jnp.sum((tile_ids[:, None] >= tile_end[None, :]).astype(jnp.int32), axis=1)
    last_expert = jnp.sum((n_used - 1 >= tile_end).astype(jnp.int32))
    tile_expert = jnp.where(tile_ids < n_used, tile_expert, last_expert).astype(jnp.int32)
    first = jnp.concatenate([jnp.ones((1,), jnp.int32),
                             (tile_expert[1:] != tile_expert[:-1]).astype(jnp.int32)])
    eid = jnp.arange(N_EXPERTS, dtype=jnp.int32)
    later = (tiles_per[None, :] > 0) & (eid[None, :] > eid[:, None])
    next_expert = jnp.min(jnp.where(later, eid[None, :], N_EXPERTS), axis=1)
    next_expert = jnp.where(next_expert == N_EXPERTS, -1, next_expert).astype(jnp.int32)
    nxt = next_expert[tile_expert]
    return (src, pos[:n_tokens], pos[n_tokens:], tile_expert, first, nxt,
            n_used.reshape(1).astype(jnp.int32))


def kernel(x, norm_mix_g, w_in, hgrn_lb_logits, hgrn_out_norm_g, w_out, norm_ffn_g, w_group_router,
           b_group_router, w_expert_router, b_expert_router, w_gate, w_up, w_down, final_norm_g):
    b, t, d = x.shape
    n = b * t
    depth = w_in.shape[0]
    assert depth == 1, "the final norm is fused into the combine step of the only layer"
    attn_w = d // 2
    n_heads = attn_w // HEAD_DIM
    x2 = x.reshape(n, d)
    lb_all = jnp.cumsum(jax.nn.softmax(hgrn_lb_logits.astype(F32), axis=0), axis=0)[:depth]
    slope2 = jnp.asarray(2.0 ** (-8.0 * np.arange(1, n_heads + 1) / n_heads), dtype=F32) * F32(LOG2E)
    s_hi = slope2.astype(BF16).astype(F32)
    s_mid = (slope2 - s_hi).astype(BF16).astype(F32)
    s_lo = (slope2 - s_hi - s_mid).astype(BF16).astype(F32)
    slope_pieces = jnp.stack([s_hi, s_mid, s_lo])
    proj_scale = jnp.concatenate([jnp.full((attn_w,), HEAD_DIM ** -0.5 * LOG2E, F32),
                                  jnp.ones((w_in.shape[2] - attn_w,), F32)]).reshape(1, -1)

    for l in range(depth):
        g_mix = norm_mix_g[l].reshape(1, d)
        x2, w_in_l, w_out_l = lax.optimization_barrier(
            (x2, w_in[l].astype(BF16), w_out[l].astype(BF16)))
        proj = _inproj(x2, g_mix, w_in_l, 0, proj_scale, BF16).reshape(b, t, -1)
        o_attn = _moba(proj, slope_pieces, n_heads)
        o_rec = _hgrn(proj, 3 * attn_w, lb_all[l].reshape(1, -1),
                      hgrn_out_norm_g[l].reshape(1, HEAD_DIM), n_heads)

        wr = jnp.concatenate(
            [w_group_router[l],
             jnp.transpose(w_expert_router[l], (1, 0, 2)).reshape(d, N_EXPERTS)], axis=1)
        wr = jnp.pad(wr, ((0, 0), (0, LANES - wr.shape[1])))
        wr_hi = wr.astype(BF16)
        wr = jnp.concatenate([wr_hi, (wr - wr_hi.astype(F32)).astype(BF16)], axis=1)
        br =jnp.concatenate([b_group_router[l], b_expert_router[l].reshape(-1)])
        br = jnp.pad(br, (0, LANES - br.shape[0])).reshape(1, LANES)
        x1, h2, rt, counts = _outproj(o_attn.reshape(n, attn_w), o_rec.reshape(n, -1), x2,
                                      w_out_l, norm_ffn_g[l].reshape(1, d), wr, br)
        src, pos1, pos2, tile_expert, first, nxt, n_used = _dispatch_plan(rt, counts, n)
        ys = _moe(tile_expert, n_used, first, nxt, src, h2, w_gate[l], w_up[l], w_down[l])
        x2 = _final(x1, ys, jnp.concatenate([pos1, pos2]), rt, final_norm_g.reshape(1, d))
    return x2.reshape(b, t, d)
```
